```python
import jax, jax.numpy as jnp
from jax import lax
import numpy as np

D_MODEL = 1024
BATCH = 4
SEQ = 8192
DEPTH = 1

GRID_W = 64
CTX_LEN = 256
HEAD_DIM = 64
N_HEADS_TOTAL = D_MODEL // HEAD_DIM
ATT_HEADS = N_HEADS_TOTAL // 2
ATT_KV_HEADS = ATT_HEADS // 4
WINDOW = 128
BLOCK = 128
ROPE_BASE = 10000.0
GLA_HEADS = N_HEADS_TOTAL - ATT_HEADS
GLA_DV = HEAD_DIM
GLA_DK = HEAD_DIM // 2
GLA_CHUNK = 64
GATE_RANK = 16
GATE_TAU = 16.0
MIX_WIDTH = ATT_HEADS * HEAD_DIM + GLA_HEADS * GLA_DV
FFN_HIDDEN = -(-8 * D_MODEL // (3 * 256)) * 256
SPLIT_SIZES = (ATT_HEADS * HEAD_DIM, ATT_KV_HEADS * HEAD_DIM, ATT_KV_HEADS * HEAD_DIM,
               GLA_HEADS * GLA_DK, GLA_HEADS * GLA_DK, GLA_HEADS * GLA_DV, GLA_HEADS * GLA_DV,
               GATE_RANK, GATE_RANK)
IN_COLS = sum(SPLIT_SIZES)
NEG_INF = -1e30

kernel_name = 'hybrid_window_gqa_gla_dit_block'


def rmsnorm(x, gain, eps=1e-6):
    x32 = x.astype(jnp.float32)
    y = x32 * lax.rsqrt(jnp.mean(x32 * x32, axis=-1, keepdims=True) + eps)
    return y.astype(x.dtype) * gain


def modulate(h, shift, scale):
    return h * (1.0 + scale) + shift


def heads(t, n):
    return t.reshape(t.shape[:-1] + (n, t.shape[-1] // n))


def flip(t):
    return jnp.flip(t, axis=1)


def split_columns(p):
    idx = np.cumsum(SPLIT_SIZES)[:-1].tolist()
    return jnp.split(p, idx, axis=-1)


def axial_rope_tables(n_tokens):
    ROWS = n_tokens // GRID_W
    row = jnp.repeat(jnp.arange(ROWS), GRID_W).astype(jnp.float32)
    col = jnp.tile(jnp.arange(GRID_W), ROWS).astype(jnp.float32)
    half = HEAD_DIM // 2
    inv_freq = ROPE_BASE ** (-jnp.arange(0, half, 2, dtype=jnp.float32) / half)
    ang_r = row[:, None] * inv_freq[None, :]
    ang_c = col[:, None] * inv_freq[None, :]
    ang = jnp.concatenate([ang_r, ang_r, ang_c, ang_c], axis=-1)
    return jnp.cos(ang), jnp.sin(ang)


def apply_rope(x, cos, sin):
    shp = x.shape
    xr = x.reshape(shp[:-1] + (2, 2, HEAD_DIM // 4))
    rot = jnp.concatenate([-xr[..., 1:2, :], xr[..., 0:1, :]], axis=-2).reshape(shp)
    return x * cos[:, None, :].astype(x.dtype) + rot * sin[:, None, :].astype(x.dtype)


def softmax_with_sink(logits, sink):
    sink_col = jnp.broadcast_to(sink, logits.shape[:-1] + (1,))
    p = jax.nn.softmax(jnp.concatenate([logits, sink_col], axis=-1), axis=-1)
    return p[..., :-1]


def window_attention(q, k, v, k_ctx, v_ctx, sink):
    B, S, H, dh = q.shape
    G = H // ATT_KV_HEADS
    nb = S // BLOCK
    scale = dh ** -0.5
    qb = q.reshape(B, nb, BLOCK, ATT_KV_HEADS, G, dh)
    pad = ((0, 0), (BLOCK, BLOCK), (0, 0), (0, 0))
    kp = jnp.pad(k, pad).reshape(B, nb + 2, BLOCK, ATT_KV_HEADS, dh)
    vp = jnp.pad(v, pad).reshape(B, nb + 2, BLOCK, ATT_KV_HEADS, dh)
    kw = jnp.concatenate([kp[:, :-2], kp[:, 1:-1], kp[:, 2:]], axis=2)
    vw = jnp.concatenate([vp[:, :-2], vp[:, 1:-1], vp[:, 2:]], axis=2)
    s_win = jnp.einsum('bnqhgd,bnkhd->bhgnqk', qb, kw).astype(jnp.float32) * scale
    qpos = jnp.arange(nb)[:, None] * BLOCK + jnp.arange(BLOCK)[None, :]
    kpos = (jnp.arange(nb)[:, None] - 1) * BLOCK + jnp.arange(3 * BLOCK)[None, :]
    rel = kpos[:, None, :] - qpos[:, :, None]
    mask = (jnp.abs(rel) <= WINDOW) & (kpos[:, None, :] >= 0) & (kpos[:, None, :] < S)
    s_win = jnp.where(mask, s_win, NEG_INF)
    s_ctx = jnp.einsum('bnqhgd,bchd->bhgnqc', qb, k_ctx).astype(jnp.float32) * scale
    sink_b = sink.astype(jnp.float32).reshape(ATT_KV_HEADS, G)[None, :, :, None, None, None]
    p = softmax_with_sink(jnp.concatenate([s_win, s_ctx], axis=-1), sink_b).astype(v.dtype)
    n_win = 3 * BLOCK
    o = (jnp.einsum('bhgnqk,bnkhd->bnqhgd', p[..., :n_win], vw)
         + jnp.einsum('bhgnqc,bchd->bnqhgd', p[..., n_win:], v_ctx))
    return o.reshape(B, S, H * dh)


def context_attention(q, k, v, sink):
    B, C, H, dh = q.shape
    G = H // ATT_KV_HEADS
    qg = q.reshape(B, C, ATT_KV_HEADS, G, dh)
    s = jnp.einsum('bqhgd,bkhd->bhgqk', qg, k).astype(jnp.float32) * dh ** -0.5
    sink_b = sink.astype(jnp.float32).reshape(ATT_KV_HEADS, G)[None, :, :, None, None]
    p = softmax_with_sink(s, sink_b).astype(v.dtype)
    return jnp.einsum('bhgqk,bkhd->bqhgd', p, v).reshape(B, C, H * dh)


def log_decay(z, w_gate, b_gate):
    logits = (z @ w_gate + b_gate).astype(jnp.float32)
    return heads(jax.nn.log_sigmoid(logits) / GATE_TAU, GLA_HEADS)


def gla_chunked(q, k, v, log_a, s0):
    B, T, H, DK = q.shape
    DV = v.shape[-1]
    n = T // GLA_CHUNK
    f32 = jnp.float32
    qc = q.astype(f32).reshape(B, n, GLA_CHUNK, H, DK)
    kc = k.astype(f32).reshape(B, n, GLA_CHUNK, H, DK)
    vc = v.astype(f32).reshape(B, n, GLA_CHUNK, H, DV)
    b = jnp.cumsum(log_a.astype(f32).reshape(B, n, GLA_CHUNK, H, DK), axis=2)
    b_last = b[:, :, -1:]
    q_dec = qc * jnp.exp(b)
    k_inv = kc * jnp.exp(-b)
    lower = jnp.tril(jnp.ones((GLA_CHUNK, GLA_CHUNK), dtype=bool))
    A = jnp.where(lower, jnp.einsum('bnihk,bnjhk->bnhij', q_dec, k_inv), 0.0)
    intra = jnp.einsum('bnhij,bnjhv->bnihv', A, vc)
    dS = jnp.einsum('bnjhk,bnjhv->bnhkv', kc * jnp.exp(b_last - b), vc)
    decay = jnp.exp(b_last[:, :, 0])

    def step(state, inp):
        d, ds = inp
        return d[..., None] * state + ds, state

    s_final, s_before = lax.scan(step, s0, (jnp.moveaxis(decay, 1, 0), jnp.moveaxis(dS, 1, 0)))
    inter = jnp.einsum('bnihk,nbhkv->bnihv', q_dec, s_before)
    o = (intra + inter).reshape(B, T, H, DV).astype(v.dtype)
    return o, s_final


def gla_final_state(k, v, log_a):
    b = jnp.cumsum(log_a.astype(jnp.float32), axis=1)
    w = jnp.exp(b[:, -1:] - b)
    return jnp.einsum('bthk,bthv->bhkv', k.astype(jnp.float32) * w, v.astype(jnp.float32))


def gla_output(o, gate, g_norm):
    return rmsnorm(o, g_norm).reshape(gate.shape) * jax.nn.silu(gate)


def swiglu(h, w_in, w_out):
    g, u = jnp.split(h @ w_in, 2, axis=-1)
    return (jax.nn.silu(g) * u) @ w_out


def setup_inputs(seed: int = 0) -> dict:
    key = jax.random.key(seed)
    ks = jax.random.split(key, 24)
    L, D = DEPTH, D_MODEL
    nrm = jax.random.normal
    f32 = jnp.float32
    return {
        'x': nrm(ks[0], (BATCH, SEQ, D), f32),
        'c': nrm(ks[1], (BATCH, D), f32),
        'ctx': nrm(ks[2], (BATCH, CTX_LEN, D), f32),
        'c_ctx': nrm(ks[3], (D,), f32),
        'w_ada': nrm(ks[4], (L, D, 6 * D), f32) * (0.5 * D ** -0.5),
        'b_ada': nrm(ks[5], (L, 6 * D), f32) * 0.02,
        'g_pre_mix': 1.0 + 0.05 * nrm(ks[6], (L, D), f32),
        'g_post_mix': 1.0 + 0.05 * nrm(ks[7], (L, D), f32),
        'g_pre_ffn': 1.0 + 0.05 * nrm(ks[8], (L, D), f32),
        'g_post_ffn': 1.0 + 0.05 * nrm(ks[9], (L, D), f32),
        'w_in': nrm(ks[10], (L, D, IN_COLS), f32) * D ** -0.5,
        'attn_sink': 0.5 * nrm(ks[11], (L, ATT_HEADS), f32),
        'w_gate_fwd': nrm(ks[12], (L, GATE_RANK, GLA_HEADS * GLA_DK), f32) * GATE_RANK ** -0.5,
        'b_gate_fwd': 0.5 * nrm(ks[13], (L, GLA_HEADS * GLA_DK), f32),
        'w_gate_bwd': nrm(ks[14], (L, GATE_RANK, GLA_HEADS * GLA_DK), f32) * GATE_RANK ** -0.5,
        'b_gate_bwd': 0.5 * nrm(ks[15], (L, GLA_HEADS * GLA_DK), f32),
        'g_gla_norm': 1.0 + 0.05 * nrm(ks[16], (L, GLA_DV), f32),
        'w_out': nrm(ks[17], (L, MIX_WIDTH, D), f32) * MIX_WIDTH ** -0.5,
        'w_ffn_in': nrm(ks[18], (L, D, 2 * FFN_HIDDEN), f32) * D ** -0.5,
        'w_ffn_out': nrm(ks[19], (L, FFN_HIDDEN, D), f32) * FFN_HIDDEN ** -0.5,
    }


def reference(x, c, ctx, c_ctx, w_ada, b_ada, g_pre_mix, g_post_mix, g_pre_ffn, g_post_ffn,
              w_in, attn_sink, w_gate_fwd, b_gate_fwd, w_gate_bwd, b_gate_bwd, g_gla_norm,
              w_out, w_ffn_in, w_ffn_out):
    B, S, _ = x.shape
    cos, sin = axial_rope_tables(S)
    zero_state = jnp.zeros((B, GLA_HEADS, GLA_DK, GLA_DV), jnp.float32)
    for l in range(DEPTH):
        need_ctx_out = l < DEPTH - 1
        ada = jax.nn.silu(c) @ w_ada[l] + b_ada[l]
        ada_c = jax.nn.silu(c_ctx) @ w_ada[l] + b_ada[l]
        sh1, sc1, gt1, sh2, sc2, gt2 = jnp.split(ada[:, None, :], 6, axis=-1)
        sh1c, sc1c, gt1c, sh2c, sc2c, gt2c = jnp.split(ada_c, 6, axis=-1)

        h = modulate(rmsnorm(x, g_pre_mix[l]), sh1, sc1)
        hc = modulate(rmsnorm(ctx, g_pre_mix[l]), sh1c, sc1c)
        q, k, v, gq, gk, gv, gg, zf, zb = split_columns(h @ w_in[l])
        qc, kc, vc, gqc, gkc, gvc, ggc, zfc, zbc = split_columns(hc @ w_in[l])

        q_h = apply_rope(heads(q, ATT_HEADS), cos, sin)
        k_h = apply_rope(heads(k, ATT_KV_HEADS), cos, sin)
        v_h = heads(v, ATT_KV_HEADS)
        kc_h = heads(kc, ATT_KV_HEADS)
        vc_h = heads(vc, ATT_KV_HEADS)
        attn_lat = window_attention(q_h, k_h, v_h, kc_h, vc_h, attn_sink[l])

        gkc_h = heads(gkc, GLA_HEADS)
        gvc_h = heads(gvc, GLA_HEADS)
        la_fc = log_decay(zfc, w_gate_fwd[l], b_gate_fwd[l])
        la_bc = log_decay(zbc, w_gate_bwd[l], b_gate_bwd[l])
        if need_ctx_out:
            gqc_h = heads(gqc, GLA_HEADS) * GLA_DK ** -0.5
            oc_f, s_f = gla_chunked(gqc_h, gkc_h, gvc_h, la_fc, zero_state)
            oc_b, s_b = gla_chunked(flip(gqc_h), flip(gkc_h), flip(gvc_h), flip(la_bc), zero_state)
            gla_ctx = gla_output(oc_f + flip(oc_b), ggc, g_gla_norm[l])
            attn_ctx = context_attention(heads(qc, ATT_HEADS), kc_h, vc_h, attn_sink[l])
        else:
            s_f = gla_final_state(gkc_h, gvc_h, la_fc)
            s_b = gla_final_state(flip(gkc_h), flip(gvc_h), flip(la_bc))

        gq_h = heads(gq, GLA_HEADS) * GLA_DK ** -0.5
        gk_h = heads(gk, GLA_HEADS)
        gv_h = heads(gv, GLA_HEADS)
        la_f = log_decay(zf, w_gate_fwd[l], b_gate_fwd[l])
        la_b = log_decay(zb, w_gate_bwd[l], b_gate_bwd[l])
        o_f, _ = gla_chunked(gq_h, gk_h, gv_h, la_f, s_f)
        o_b, _ = gla_chunked(flip(gq_h), flip(gk_h), flip(gv_h), flip(la_b), s_b)
        gla_lat = gla_output(o_f + flip(o_b), gg, g_gla_norm[l])

        y = jnp.concatenate([attn_lat, gla_lat], axis=-1) @ w_out[l]
        x = x + gt1 * rmsnorm(y, g_post_mix[l])

        f = swiglu(modulate(rmsnorm(x, g_pre_ffn[l]), sh2, sc2), w_ffn_in[l], w_ffn_out[l])
        x = x + gt2 * rmsnorm(f, g_post_ffn[l])

        if need_ctx_out:
            yc = jnp.concatenate([attn_ctx, gla_ctx], axis=-1) @ w_out[l]
            ctx = ctx + gt1c * rmsnorm(yc, g_post_mix[l])
            fc = swiglu(modulate(rmsnorm(ctx, g_pre_ffn[l]), sh2c, sc2c), w_ffn_in[l], w_ffn_out[l])
            ctx = ctx + gt2c * rmsnorm(fc, g_post_ffn[l])
    return x
```

```python
import functools

import jax
import jax.numpy as jnp
import numpy as np
from jax import lax
from jax.experimental import pallas as pl
from jax.experimental.pallas import tpu as pltpu

D = 1024
HEAD_DIM = 64
ATT_HEADS = 8
ATT_KV_HEADS = 2
GROUP = ATT_HEADS // ATT_KV_HEADS
WINDOW = 128
GRID_W = 64
ROPE_BASE = 10000.0
GLA_HEADS = 8
GLA_DK = 32
GLA_DV = 64
CHUNK = 64
GATE_RANK = 16
GATE_TAU = 16.0
FFN_HIDDEN = 2816
NEG_INF = -1e30
EPS = 1e-6

NQ = ATT_HEADS * HEAD_DIM
NKV = ATT_KV_HEADS * HEAD_DIM
GQ = GLA_HEADS * GLA_DK
GV = GLA_HEADS * GLA_DV
QKV_W = NQ + 2 * NKV
GIN_W = 2 * GQ + 2 * GV
Z_W = 128
LA_W = 2 * GQ

LANES = 128
VMEM_LIMIT = 56 * 1024 * 1024

BF16 = jnp.bfloat16
F32 = jnp.float32


def _dot(a, b):
    return jnp.dot(a, b, preferred_element_type=F32)


def _dot_nt(a, b):
    return lax.dot_general(a, b, (((1,), (1,)), ((), ())), preferred_element_type=F32)


def _dot_tn(a, b):
    return lax.dot_general(a, b, (((0,), (0,)), ((), ())), preferred_element_type=F32)


def _rms(x):
    return x * lax.rsqrt(jnp.mean(x * x, axis=-1, keepdims=True) + EPS)


def _silu(x):
    return x * (1.0 / (1.0 + jnp.exp(-x)))


def _hi_lo(x):
    hi = x.astype(BF16)
    lo = (x - hi.astype(F32)).astype(BF16)
    return hi, lo


def _ada_kernel(c_ref, w_ref, b_ref, o_ref):
    a = _silu(c_ref[...]).astype(BF16)
    o_ref[...] = _dot(a, w_ref[...].astype(BF16)) + b_ref[...]


def _ada(c8, w_ada, b_ada):
    n = w_ada.shape[1]
    bn = 1024
    return pl.pallas_call(
        _ada_kernel,
        out_shape=jax.ShapeDtypeStruct((8, n), F32),
        grid=(n // bn,),
        in_specs=[pl.BlockSpec((8, D), lambda j: (0, 0)),
                  pl.BlockSpec((D, bn), lambda j: (0, j)),
                  pl.BlockSpec((1, bn), lambda j: (0, j))],
        out_specs=pl.BlockSpec((8, bn), lambda j: (0, j)),
        compiler_params=pltpu.CompilerParams(dimension_semantics=("arbitrary",)),
        name="ada",
    )(c8, w_ada, b_ada)


def _inproj_kernel(*refs, tiles_per_batch, rope):
    if rope:
        (x_ref, ada_ref, g_ref, w_ref, wg_ref, bg_ref, cos_ref, sa_ref, sb_ref,
         qkv_ref, gin_ref, la_ref) = refs
    else:
        (x_ref, ada_ref, g_ref, w_ref, wg_ref, bg_ref, qkv_ref, gin_ref, la_ref) = refs
    if tiles_per_batch is None:
        row = 4
    else:
        row = pl.program_id(0) // tiles_per_batch
    ada = ada_ref[pl.ds(row, 1), :]
    sh1 = ada[:, 0:D]
    sc1 = ada[:, D:2 * D]
    h = (_rms(x_ref[...]) * g_ref[...]) * (1.0 + sc1) + sh1
    hb = h.astype(BF16)

    qkv = _dot(hb, w_ref[:, 0:QKV_W])
    if rope:
        cos = cos_ref[...]
        sa = sa_ref[...]
        sb = sb_ref[...]
        for g in range((NQ + NKV) // LANES):
            xg = qkv[:, g * LANES:(g + 1) * LANES]
            rg = (xg * cos + pltpu.roll(xg, LANES - 16, 1) * sa
                  + pltpu.roll(xg, 16, 1) * sb)
            if g < NQ // LANES:
                rg = rg * (HEAD_DIM ** -0.5)
            qkv_ref[:, g * LANES:(g + 1) * LANES] = rg.astype(BF16)
        qkv_ref[:, NQ + NKV:QKV_W] = qkv[:, NQ + NKV:QKV_W].astype(BF16)
    else:
        qkv_ref[...] = qkv.astype(BF16)

    gin = _dot(hb, w_ref[:, QKV_W:QKV_W + GIN_W])
    gin_ref[:, 0:GQ] = (gin[:, 0:GQ] * (GLA_DK ** -0.5)).astype(BF16)
    gin_ref[:, GQ:GIN_W] = gin[:, GQ:GIN_W].astype(BF16)

    z = _dot(hb, w_ref[:, QKV_W + GIN_W:QKV_W + GIN_W + Z_W])
    logits = _dot(z.astype(BF16), wg_ref[...]) + bg_ref[...]
    log_sig = jnp.minimum(logits, 0.0) - jnp.log(1.0 + jnp.exp(-jnp.abs(logits)))
    la_ref[...] = log_sig * (1.0 / GATE_TAU)


def _inproj(x2d, ada, g_pre, w_in_p, w_gate_p, b_gate_p, tables, *, tm, tiles_per_batch):
    n = x2d.shape[0]
    rope = tables is not None
    wcols = w_in_p.shape[1]
    const = lambda i: (0, 0)
    in_specs = [pl.BlockSpec((tm, D), lambda i: (i, 0)),
                pl.BlockSpec((8, 6 * D), const),
                pl.BlockSpec((1, D), const),
                pl.BlockSpec((D, wcols), const),
                pl.BlockSpec((Z_W, LA_W), const),
                pl.BlockSpec((1, LA_W), const)]
    args = [x2d, ada, g_pre, w_in_p, w_gate_p, b_gate_p]
    if rope:
        tmap = lambda i: (i % tiles_per_batch, 0)
        in_specs += [pl.BlockSpec((tm, LANES), tmap)] * 3
        args += list(tables)
    return pl.pallas_call(
        functools.partial(_inproj_kernel, tiles_per_batch=tiles_per_batch, rope=rope),
        out_shape=(jax.ShapeDtypeStruct((n, QKV_W), BF16),
                   jax.ShapeDtypeStruct((n, GIN_W), BF16),
                   jax.ShapeDtypeStruct((n, LA_W), F32)),
        grid=(n // tm,),
        in_specs=in_specs,
        out_specs=(pl.BlockSpec((tm, QKV_W), lambda i: (i, 0)),
                   pl.BlockSpec((tm, GIN_W), lambda i: (i, 0)),
                   pl.BlockSpec((tm, LA_W), lambda i: (i, 0))),
        compiler_params=pltpu.CompilerParams(dimension_semantics=("arbitrary",),
                                             vmem_limit_bytes=VMEM_LIMIT),
        name="inproj_lat" if rope else "inproj_ctx",
    )(*args)


def _attn_kernel(q_ref, kp_ref, kc_ref, kn_ref, kx_ref, sink_ref, o_ref, *, bq, n_blocks, n_ctx):
    i = pl.program_id(1)
    q = q_ref[...]
    kcat = jnp.concatenate([kp_ref[:, 0:NKV], kc_ref[:, 0:NKV], kn_ref[:, 0:NKV],
                            kx_ref[:, 0:NKV]], axis=0)
    vcat = jnp.concatenate([kp_ref[:, NKV:2 * NKV], kc_ref[:, NKV:2 * NKV],
                            kn_ref[:, NKV:2 * NKV], kx_ref[:, NKV:2 * NKV]], axis=0)
    nk = 3 * bq + n_ctx
    r = lax.broadcasted_iota(jnp.int32, (bq, nk), 0)
    c = lax.broadcasted_iota(jnp.int32, (bq, nk), 1)
    rel = c - bq - r
    in_win = (rel <= WINDOW) & (rel >= -WINDOW)
    in_win = in_win & ((c >= bq) | (i > 0)) & ((c < 2 * bq) | (i < n_blocks - 1))
    valid = in_win | (c >= 3 * bq)
    keep = jnp.where(valid, 1.0, 0.0)
    keep = jnp.concatenate([keep] * GROUP, axis=0) > 0.5
    lane = lax.broadcasted_iota(jnp.int32, (bq, LANES), 1)
    lo = lane < HEAD_DIM

    outs = []
    for kvh in range(ATT_KV_HEADS):
        sel = lo if kvh == 0 else jnp.logical_not(lo)
        zero = jnp.zeros((bq, LANES), BF16)
        lhs = jnp.concatenate(
            [jnp.where(sel, q[:, g * LANES:(g + 1) * LANES], zero) for g in range(GROUP)],
            axis=0)
        s = _dot_nt(lhs, kcat)
        s = jnp.where(keep, s, NEG_INF)
        sink = sink_ref[kvh]
        m = jnp.maximum(jnp.max(s, axis=-1, keepdims=True), sink)
        e = jnp.exp(s - m)
        denom = jnp.sum(e, axis=-1, keepdims=True) + jnp.exp(sink - m)
        p = (e * (1.0 / denom)).astype(BF16)
        outs.append(_dot(p, vcat))
    for g in range(GROUP):
        og = jnp.where(lo, outs[0][g * bq:(g + 1) * bq], outs[1][g * bq:(g + 1) * bq])
        o_ref[:, g * LANES:(g + 1) * LANES] = og.astype(BF16)


def _attn(qkv, qkv_ctx, sink_cols, *, batch, seq, n_ctx):
    bq = WINDOW
    nb = seq // bq
    qkv3 = qkv.reshape(batch, seq, QKV_W)
    ctx3 = qkv_ctx.reshape(batch, n_ctx, QKV_W)
    kv_blk = NQ // (2 * NKV)
    return pl.pallas_call(
        functools.partial(_attn_kernel, bq=bq, n_blocks=nb, n_ctx=n_ctx),
        out_shape=jax.ShapeDtypeStruct((batch, seq, NQ), BF16),
        grid=(batch, nb),
        in_specs=[
            pl.BlockSpec((None, bq, NQ), lambda b, i: (b, i, 0)),
            pl.BlockSpec((None, bq, 2 * NKV), lambda b, i: (b, jnp.maximum(i - 1, 0), kv_blk)),
            pl.BlockSpec((None, bq, 2 * NKV), lambda b, i: (b, i, kv_blk)),
            pl.BlockSpec((None, bq, 2 * NKV), lambda b, i: (b, jnp.minimum(i + 1, nb - 1), kv_blk)),
            pl.BlockSpec((None, n_ctx, 2 * NKV), lambda b, i: (b, 0, kv_blk)),
            pl.BlockSpec((ATT_KV_HEADS, GROUP * bq, 1), lambda b, i: (0, 0, 0)),
        ],
        out_specs=pl.BlockSpec((None, bq, NQ), lambda b, i: (b, i, 0)),
        compiler_params=pltpu.CompilerParams(dimension_semantics=("arbitrary", "arbitrary"),
                                             vmem_limit_bytes=VMEM_LIMIT),
        name="attn",
    )(qkv3, qkv3, qkv3, qkv3, ctx3, sink_cols)


GRP_K = 4 * GLA_DK
GRP_V = 4 * GLA_DV
N_GRP = GLA_HEADS // 4


def _block_ones(n, blk, kind):
    r = lax.broadcasted_iota(jnp.int32, (n, n), 0)
    c = lax.broadcasted_iota(jnp.int32, (n, n), 1)
    same = (r // blk) == (c // blk)
    if kind == "lower":
        same = same & (c <= r)
    elif kind == "upper":
        same = same & (c >= r)
    return jnp.where(same, 1.0, 0.0).astype(BF16)


def _cum_and_total(la, tri, ones):
    hi, lo = _hi_lo(la)
    cum = _dot(tri, hi) + _dot(tri, lo)
    tot = _dot(ones, hi) + _dot(ones, lo)
    return cum, tot


def _state_mask():
    r = lax.broadcasted_iota(jnp.int32, (GRP_V, GRP_K), 0) // GLA_DV
    c = lax.broadcasted_iota(jnp.int32, (GRP_V, GRP_K), 1) // GLA_DK
    return r == c


def _state_step(st_ref, kd_c, v_c, d_row, smask):
    for g in range(N_GRP):
        ds = _dot_tn(v_c[:, g * GRP_V:(g + 1) * GRP_V], kd_c[:, g * GRP_K:(g + 1) * GRP_K])
        ds = jnp.where(smask, ds, 0.0)
        st_ref[g] = st_ref[g] * d_row[:, g * GRP_K:(g + 1) * GRP_K] + ds


def _scan_states(st_ref, k, v, la, *, reverse, store_ref=None, store_base=None):
    t = k.shape[0]
    nch = t // CHUNK
    tri = _block_ones(t, CHUNK, "upper" if reverse else "lower")
    ones = _block_ones(t, CHUNK, "all")
    cum, tot = _cum_and_total(la, tri, ones)
    kd = (k.astype(F32) * jnp.exp(tot - cum)).astype(BF16)
    dec = jnp.exp(tot)
    smask = _state_mask()
    order = range(nch - 1, -1, -1) if reverse else range(nch)
    for ci in order:
        sl = slice(ci * CHUNK, (ci + 1) * CHUNK)
        if store_ref is not None:
            for g in range(N_GRP):
                store_ref[store_base + ci, g] = st_ref[g].astype(BF16)
        _state_step(st_ref, kd[sl], v[sl], dec[ci * CHUNK:ci * CHUNK + 1], smask)


def _chunk_outputs(q, k, v, la, states, *, reverse, st_ref=None):
    t = q.shape[0]
    nch = t // CHUNK
    tri = _block_ones(t, CHUNK, "upper" if reverse else "lower")
    ones = _block_ones(t, CHUNK, "all")
    cum, tot = _cum_and_total(la, tri, ones)
    qd = (q.astype(F32) * jnp.exp(cum)).astype(BF16)
    ki = (k.astype(F32) * jnp.exp(-cum)).astype(BF16)
    if st_ref is not None:
        kd = (k.astype(F32) * jnp.exp(tot - cum)).astype(BF16)
        dec = jnp.exp(tot)
        smask = _state_mask()

    hr = lax.broadcasted_iota(jnp.int32, (GLA_HEADS * CHUNK, GQ), 0) // CHUNK
    hc = lax.broadcasted_iota(jnp.int32, (GLA_HEADS * CHUNK, GQ), 1) // GLA_DK
    kmask = hr == hc
    vr = lax.broadcasted_iota(jnp.int32, (4 * CHUNK, GRP_V), 0) // CHUNK
    vc = lax.broadcasted_iota(jnp.int32, (4 * CHUNK, GRP_V), 1) // GLA_DV
    vmask = vr == vc
    ai = lax.broadcasted_iota(jnp.int32, (CHUNK, GLA_HEADS * CHUNK), 0)
    aj = lax.broadcasted_iota(jnp.int32, (CHUNK, GLA_HEADS * CHUNK), 1) % CHUNK
    amask = (aj >= ai) if reverse else (aj <= ai)

    outs = []
    for ci in range(nch):
        sl = slice(ci * CHUNK, (ci + 1) * CHUNK)
        qd_c, ki_c, v_c = qd[sl], ki[sl], v[sl]
        ki_bd = jnp.where(kmask, jnp.concatenate([ki_c] * GLA_HEADS, axis=0),
                          jnp.zeros((), BF16))
        a = _dot_nt(qd_c, ki_bd)
        a = jnp.where(amask, a, 0.0).astype(BF16)
        parts = []
        for g in range(N_GRP):
            v_g = v_c[:, g * GRP_V:(g + 1) * GRP_V]
            v_bd = jnp.where(vmask, jnp.concatenate([v_g] * 4, axis=0), jnp.zeros((), BF16))
            o_g = _dot(a[:, g * 4 * CHUNK:(g + 1) * 4 * CHUNK], v_bd)
            if st_ref is not None:
                s_g = st_ref[g].astype(BF16)
            else:
                s_g = states(ci, g)
            o_g = o_g + _dot_nt(qd_c[:, g * GRP_K:(g + 1) * GRP_K], s_g)
            parts.append(o_g)
        outs.append(jnp.concatenate(parts, axis=1))
        if st_ref is not None:
            _state_step(st_ref, kd[sl], v_c, dec[ci * CHUNK:ci * CHUNK + 1], smask)
    return jnp.concatenate(outs, axis=0)


def _gla_kernel(gin_ref, la_ref, ginc_ref, lac_ref, gn_ref, o_ref, st_ref, sb_ref, *,
                tb, n_blocks):
    phase = pl.program_id(1)
    j = pl.program_id(2)
    ch_per_blk = tb // CHUNK

    @pl.when(phase == 0)
    def _backward_states():
        @pl.when(j == 0)
        def _():
            st_ref[...] = jnp.zeros_like(st_ref)
            _scan_states(st_ref, ginc_ref[:, GQ:2 * GQ], ginc_ref[:, 2 * GQ:2 * GQ + GV],
                         lac_ref[:, GQ:2 * GQ], reverse=True)

        blk = n_blocks - 1 - j
        _scan_states(st_ref, gin_ref[:, GQ:2 * GQ], gin_ref[:, 2 * GQ:2 * GQ + GV],
                     la_ref[:, GQ:2 * GQ], reverse=True,
                     store_ref=sb_ref, store_base=blk * ch_per_blk)

    @pl.when(phase == 1)
    def _outputs():
        @pl.when(j == 0)
        def _():
            st_ref[...] = jnp.zeros_like(st_ref)
            _scan_states(st_ref, ginc_ref[:, GQ:2 * GQ], ginc_ref[:, 2 * GQ:2 * GQ + GV],
                         lac_ref[:, 0:GQ], reverse=False)

        q = gin_ref[:, 0:GQ]
        k = gin_ref[:, GQ:2 * GQ]
        v = gin_ref[:, 2 * GQ:2 * GQ + GV]
        o_f = _chunk_outputs(q, k, v, la_ref[:, 0:GQ], None, reverse=False, st_ref=st_ref)
        base = j * ch_per_blk
        o_b = _chunk_outputs(q, k, v, la_ref[:, GQ:2 * GQ],
                             lambda ci, g: sb_ref[base + ci, g], reverse=True)
        o = o_f + o_b
        hm = _block_ones(GV, GLA_DV, "all")
        ms = _dot((o * o).astype(BF16), hm) * (1.0 / GLA_DV)
        y = o * lax.rsqrt(ms + EPS) * gn_ref[...]
        gate = gin_ref[:, 2 * GQ + GV:GIN_W].astype(F32)
        o_ref[...] = (y * _silu(gate)).astype(BF16)


def _gla(gin, la, gin_ctx, la_ctx, gn_tiled, *, batch, seq, n_ctx):
    tb = 256
    nb = seq // tb
    nch = seq // CHUNK
    gin3 = gin.reshape(batch, seq, GIN_W)
    la3 = la.reshape(batch, seq, LA_W)
    ginc3 = gin_ctx.reshape(batch, n_ctx, GIN_W)
    lac3 = la_ctx.reshape(batch, n_ctx, LA_W)

    def blk_map(b, p, j):
        return (b, jnp.where(p == 0, nb - 1 - j, j), 0)

    return pl.pallas_call(
        functools.partial(_gla_kernel, tb=tb, n_blocks=nb),
        out_shape=jax.ShapeDtypeStruct((batch, seq, GV), BF16),
        grid=(batch, 2, nb),
        in_specs=[pl.BlockSpec((None, tb, GIN_W), blk_map),
                  pl.BlockSpec((None, tb, LA_W), blk_map),
                  pl.BlockSpec((None, n_ctx, GIN_W), lambda b, p, j: (b, 0, 0)),
                  pl.BlockSpec((None, n_ctx, LA_W), lambda b, p, j: (b, 0, 0)),
                  pl.BlockSpec((1, GV), lambda b, p, j: (0, 0))],
        out_specs=pl.BlockSpec((None, tb, GV), lambda b, p, j: (b, jnp.where(p == 0, 0, j), 0)),
        scratch_shapes=[pltpu.VMEM((N_GRP, GRP_V, GRP_K), F32),
                        pltpu.VMEM((nch, N_GRP, GRP_V, GRP_K), BF16)],
        compiler_params=pltpu.CompilerParams(
            dimension_semantics=("arbitrary", "arbitrary", "arbitrary"),
            vmem_limit_bytes=VMEM_LIMIT),
        name="gla",
    )(gin3, la3, ginc3, lac3, gn_tiled)


FFN_CHUNK = 256


def _mix_ffn_kernel(att_ref, gla_ref, x_ref, ada_ref, gpm_ref, gpf_ref, gqf_ref,
                    woa_ref, wog_ref, wfi_ref, wfo_ref, o_ref, acc_ref, *, tiles_per_batch):
    row = pl.program_id(0) // tiles_per_batch
    ada = ada_ref[pl.ds(row, 1), :]
    gt1 = ada[:, 2 * D:3 * D]
    sh2 = ada[:, 3 * D:4 * D]
    sc2 = ada[:, 4 * D:5 * D]
    gt2 = ada[:, 5 * D:6 * D]

    y = _dot(att_ref[...], woa_ref[...]) + _dot(gla_ref[...], wog_ref[...])
    x1 = x_ref[...] + gt1 * (_rms(y) * gpm_ref[...])
    h = ((_rms(x1) * gpf_ref[...]) * (1.0 + sc2) + sh2).astype(BF16)

    n_chunks = FFN_HIDDEN // FFN_CHUNK
    for ci in range(n_chunks):
        c0 = ci * FFN_CHUNK
        g = _dot(h, wfi_ref[:, c0:c0 + FFN_CHUNK])
        u = _dot(h, wfi_ref[:, FFN_HIDDEN + c0:FFN_HIDDEN + c0 + FFN_CHUNK])
        a = (_silu(g) * u).astype(BF16)
        part = _dot(a, wfo_ref[c0:c0 + FFN_CHUNK, :])
        if ci == 0:
            acc_ref[...] = part
        else:
            acc_ref[...] += part
    o_ref[...] = x1 + gt2 * (_rms(acc_ref[...]) * gqf_ref[...])


def _mix_ffn(att, gla, x2d, ada, g_post_mix, g_pre_ffn, g_post_ffn,
             w_out_a, w_out_g, w_ffn_in, w_ffn_out, *, tm, tiles_per_batch):
    n = x2d.shape[0]
    const = lambda i: (0, 0)
    resident = functools.partial(pl.BlockSpec, index_map=const, pipeline_mode=pl.Buffered(1))
    return pl.pallas_call(
        functools.partial(_mix_ffn_kernel, tiles_per_batch=tiles_per_batch),
        out_shape=jax.ShapeDtypeStruct((n, D), F32),
        grid=(n // tm,),
        in_specs=[pl.BlockSpec((tm, NQ), lambda i: (i, 0)),
                  pl.BlockSpec((tm, GV), lambda i: (i, 0)),
                  pl.BlockSpec((tm, D), lambda i: (i, 0)),
                  pl.BlockSpec((8, 6 * D), const),
                  pl.BlockSpec((1, D), const),
                  pl.BlockSpec((1, D), const),
                  pl.BlockSpec((1, D), const),
                  resident((NQ, D)),
                  resident((GV, D)),
                  resident((D, 2 * FFN_HIDDEN)),
                  resident((FFN_HIDDEN, D))],
        out_specs=pl.BlockSpec((tm, D), lambda i: (i, 0)),
        scratch_shapes=[pltpu.VMEM((tm, D), F32)],
        compiler_params=pltpu.CompilerParams(dimension_semantics=("arbitrary",),
                                             vmem_limit_bytes=VMEM_LIMIT),
        name="mix_ffn",
    )(att, gla, x2d, ada, g_post_mix, g_pre_ffn, g_post_ffn,
      w_out_a, w_out_g, w_ffn_in, w_ffn_out)


def _head_pair_perm():
    idx = []
    for g in range(GROUP):
        idx += list(range(g * HEAD_DIM, (g + 1) * HEAD_DIM))
        idx += list(range((g + GROUP) * HEAD_DIM, (g + GROUP + 1) * HEAD_DIM))
    return np.asarray(idx, dtype=np.int32)


def _rope_tables(seq):
    rows = seq // GRID_W
    row = jnp.repeat(jnp.arange(rows), GRID_W).astype(F32)
    col = jnp.tile(jnp.arange(GRID_W), rows).astype(F32)
    half = HEAD_DIM // 2
    inv_freq = ROPE_BASE ** (-jnp.arange(0, half, 2, dtype=F32) / half)
    ang_r = row[:, None] * inv_freq[None, :]
    ang_c = col[:, None] * inv_freq[None, :]
    ang = jnp.concatenate([ang_r, ang_r, ang_c, ang_c], axis=-1)
    cos, sin = jnp.cos(ang), jnp.sin(ang)
    even = ((np.arange(HEAD_DIM) // 16) % 2 == 0)[None, :]
    sa = jnp.where(even, -sin, 0.0)
    sb = jnp.where(even, 0.0, sin)
    tile2 = lambda t: jnp.concatenate([t, t], axis=-1)
    return tile2(cos), tile2(sa), tile2(sb)


def kernel(x, c, ctx, c_ctx, w_ada, b_ada, g_pre_mix, g_post_mix, g_pre_ffn, g_post_ffn,
           w_in, attn_sink, w_gate_fwd, b_gate_fwd, w_gate_bwd, b_gate_bwd, g_gla_norm,
           w_out, w_ffn_in, w_ffn_out):
    batch, seq, _ = x.shape
    n_ctx = ctx.shape[1]
    depth = w_ada.shape[0]
    assert depth == 1
    l = 0
    perm = _head_pair_perm()

    c8 = jnp.zeros((8, D), F32).at[0:batch].set(c).at[4].set(c_ctx)
    ada = _ada(c8, w_ada[l], b_ada[l][None, :])

    wi = w_in[l]
    n_main = NQ + 2 * NKV + GIN_W
    w_in_p = jnp.concatenate(
        [wi[:, :NQ][:, perm], wi[:, NQ:n_main], wi[:, n_main:],
         jnp.zeros((D, Z_W - 2 * GATE_RANK), F32)], axis=1).astype(BF16)
    w_gate_p = jnp.zeros((Z_W, LA_W), F32)
    w_gate_p = w_gate_p.at[0:GATE_RANK, 0:GQ].set(w_gate_fwd[l])
    w_gate_p = w_gate_p.at[GATE_RANK:2 * GATE_RANK, GQ:].set(w_gate_bwd[l]).astype(BF16)
    b_gate_p = jnp.concatenate([b_gate_fwd[l], b_gate_bwd[l]])[None, :]

    tm = 512
    tables = _rope_tables(seq)
    x2d = x.reshape(batch * seq, D)
    qkv, gin, la = _inproj(x2d, ada, g_pre_mix[l][None, :], w_in_p, w_gate_p, b_gate_p,
                           tables, tm=tm, tiles_per_batch=seq // tm)
    qkv_c, gin_c, la_c = _inproj(ctx.reshape(batch * n_ctx, D), ada, g_pre_mix[l][None, :],
                                 w_in_p, w_gate_p, b_gate_p, None, tm=n_ctx,
                                 tiles_per_batch=None)

    sink_cols = jnp.repeat(attn_sink[l].reshape(ATT_KV_HEADS, GROUP), WINDOW, axis=1)
    sink_cols = sink_cols.reshape(ATT_KV_HEADS, GROUP * WINDOW, 1)
    att = _attn(qkv, qkv_c, sink_cols, batch=batch, seq=seq, n_ctx=n_ctx)

    gn_tiled = jnp.tile(g_gla_norm[l], GLA_HEADS)[None, :]
    gla = _gla(gin, la, gin_c, la_c, gn_tiled, batch=batch, seq=seq, n_ctx=n_ctx)

    wo = w_out[l]
    w_out_a = wo[:NQ][perm].astype(BF16)
    w_out_g = wo[NQ:].astype(BF16)
    out = _mix_ffn(att.reshape(batch * seq, NQ), gla.reshape(batch * seq, GV), x2d, ada,
                   g_post_mix[l][None, :], g_pre_ffn[l][None, :], g_post_ffn[l][None, :],
                   w_out_a, w_out_g, w_ffn_in[l].astype(BF16), w_ffn_out[l].astype(BF16),
                   tm=tm, tiles_per_batch=seq // tm)
    return out.reshape(batch, seq, D)
```

```python
import functools

import jax
import jax.numpy as jnp
import numpy as np
from jax import lax
from jax.experimental import pallas as pl
from jax.experimental.pallas import tpu as pltpu

D = 1024
HEAD_DIM = 64
ATT_HEADS = 8
ATT_KV_HEADS = 2
GROUP = ATT_HEADS // ATT_KV_HEADS
WINDOW = 128
GRID_W = 64
ROPE_BASE = 10000.0
ROPE_BLK = HEAD_DIM // 4
GLA_HEADS = 8
GLA_DK = 32
GLA_DV = 64
CHUNK = 64
GATE_RANK = 16
GATE_TAU = 16.0
FFN_HIDDEN = 2816
NEG_INF = -1e30
EPS = 1e-6
LOG2E = 1.4426950408889634
Q_SCALE = LOG2E * HEAD_DIM ** -0.5

NQ = ATT_HEADS * HEAD_DIM
NKV = ATT_KV_HEADS * HEAD_DIM
GQ = GLA_HEADS * GLA_DK
GV = GLA_HEADS * GLA_DV
GIN_W = 2 * GQ + 2 * GV
Z_W = 128
LA_W = 2 * GQ
REST_W = NKV + GIN_W + Z_W

LANES = 128
VMEM_LIMIT = 56 * 1024 * 1024

BF16 = jnp.bfloat16
F32 = jnp.float32


def _dot(a, b):
    return jnp.dot(a, b, preferred_element_type=F32)


def _dot_nt(a, b):
    return lax.dot_general(a, b, (((1,), (1,)), ((), ())), preferred_element_type=F32)


def _dot_tn(a, b):
    return lax.dot_general(a, b, (((0,), (0,)), ((), ())), preferred_element_type=F32)


def _rms(x):
    return x * lax.rsqrt(jnp.mean(x * x, axis=-1, keepdims=True) + EPS)


def _silu(x):
    return x * (1.0 / (1.0 + jnp.exp(-x)))


def _hi_lo(x):
    hi = x.astype(BF16)
    lo = (x - hi.astype(F32)).astype(BF16)
    return hi, lo


def _ada_kernel(c_ref, w_ref, b_ref, o_ref):
    a = _silu(c_ref[...]).astype(BF16)
    o_ref[...] = _dot(a, w_ref[...].astype(BF16)) + b_ref[...]


def _ada(c8, w_ada, b_ada):
    n = w_ada.shape[1]
    bn = 1024
    return pl.pallas_call(
        _ada_kernel,
        out_shape=jax.ShapeDtypeStruct((8, n), F32),
        grid=(n // bn,),
        in_specs=[pl.BlockSpec((8, D), lambda j: (0, 0)),
                  pl.BlockSpec((D, bn), lambda j: (0, j)),
                  pl.BlockSpec((1, bn), lambda j: (0, j))],
        out_specs=pl.BlockSpec((8, bn), lambda j: (0, j)),
        compiler_params=pltpu.CompilerParams(dimension_semantics=("arbitrary",)),
        name="ada",
    )(c8, w_ada, b_ada)


def _inproj_kernel(*refs, tiles_per_batch, rope):
    if rope:
        (x_ref, ada_ref, g_ref, wqv_ref, w_ref, wg_ref, bg_ref,
         cost_ref, sat_ref, sbt_ref, cos_ref, sa_ref, sb_ref,
         qt_ref, k_ref, vt_ref, gin_ref, la_ref) = refs
        row = pl.program_id(0) // tiles_per_batch
    else:
        (x_ref, ada_ref, g_ref, wqv_ref, w_ref, wg_ref, bg_ref,
         qt_ref, k_ref, vt_ref, gin_ref, la_ref) = refs
        row = 4
    ada = ada_ref[pl.ds(row, 1), :]
    sh1 = ada[:, 0:D]
    sc1 = ada[:, D:2 * D]
    h = (_rms(x_ref[...]) * g_ref[...]) * (1.0 + sc1) + sh1
    hb = h.astype(BF16)

    qv = _dot_nt(wqv_ref[...], hb)
    if rope:
        cost = cost_ref[...]
        sat = sat_ref[...]
        sbt = sbt_ref[...]
        for hd in range(ATT_HEADS):
            xh = qv[hd * HEAD_DIM:(hd + 1) * HEAD_DIM]
            up = jnp.concatenate([xh[ROPE_BLK:], xh[:ROPE_BLK]], axis=0)
            dn = jnp.concatenate([xh[HEAD_DIM - ROPE_BLK:], xh[:HEAD_DIM - ROPE_BLK]], axis=0)
            rh = (xh * cost + up * sat + dn * sbt) * Q_SCALE
            qt_ref[hd * HEAD_DIM:(hd + 1) * HEAD_DIM, :] = rh.astype(BF16)
    else:
        qt_ref[...] = qv[0:NQ].astype(BF16)
    vt_ref[...] = qv[NQ:NQ + NKV].astype(BF16)

    k = _dot(hb, w_ref[:, 0:NKV])
    if rope:
        k = (k * cos_ref[...] + pltpu.roll(k, LANES - ROPE_BLK, 1) * sa_ref[...]
             + pltpu.roll(k, ROPE_BLK, 1) * sb_ref[...])
    k_ref[...] = k.astype(BF16)

    gin = _dot(hb, w_ref[:, NKV:NKV + GIN_W])
    gin_ref[:, 0:GQ] = (gin[:, 0:GQ] * (GLA_DK ** -0.5)).astype(BF16)
    gin_ref[:, GQ:GIN_W] = gin[:, GQ:GIN_W].astype(BF16)

    z = _dot(hb, w_ref[:, NKV + GIN_W:REST_W])
    logits = _dot(z.astype(BF16), wg_ref[...]) + bg_ref[...]
    log_sig = jnp.minimum(logits, 0.0) - jnp.log(1.0 + jnp.exp(-jnp.abs(logits)))
    la_ref[...] = log_sig * (1.0 / GATE_TAU)


def _inproj(x2d, ada, g_pre, w_qv_t, w_rest, w_gate_p, b_gate_p, tables, *,
            batch, tokens, tm):
    n = x2d.shape[0]
    tpb = tokens // tm
    rope = tables is not None
    const = lambda i: (0, 0)
    in_specs = [pl.BlockSpec((tm, D), lambda i: (i, 0)),
                pl.BlockSpec((8, 6 * D), const),
                pl.BlockSpec((1, D), const),
                pl.BlockSpec((NQ + NKV, D), const),
                pl.BlockSpec((D, REST_W), const),
                pl.BlockSpec((Z_W, LA_W), const),
                pl.BlockSpec((1, LA_W), const)]
    args = [x2d, ada, g_pre, w_qv_t, w_rest, w_gate_p, b_gate_p]
    if rope:
        in_specs += [pl.BlockSpec((HEAD_DIM, tm), lambda i: (0, i % tpb))] * 3
        in_specs += [pl.BlockSpec((tm, LANES), lambda i: (i % tpb, 0))] * 3
        args += list(tables)
    return pl.pallas_call(
        functools.partial(_inproj_kernel, tiles_per_batch=tpb, rope=rope),
        out_shape=(jax.ShapeDtypeStruct((batch, NQ, tokens), BF16),
                   jax.ShapeDtypeStruct((n, NKV), BF16),
                   jax.ShapeDtypeStruct((batch, NKV, tokens), BF16),
                   jax.ShapeDtypeStruct((n, GIN_W), BF16),
                   jax.ShapeDtypeStruct((n, LA_W), F32)),
        grid=(n // tm,),
        in_specs=in_specs,
        out_specs=(pl.BlockSpec((None, NQ, tm), lambda i: (i // tpb, 0, i % tpb)),
                   pl.BlockSpec((tm, NKV), lambda i: (i, 0)),
                   pl.BlockSpec((None, NKV, tm), lambda i: (i // tpb, 0, i % tpb)),
                   pl.BlockSpec((tm, GIN_W), lambda i: (i, 0)),
                   pl.BlockSpec((tm, LA_W), lambda i: (i, 0))),
        compiler_params=pltpu.CompilerParams(dimension_semantics=("arbitrary",),
                                             vmem_limit_bytes=VMEM_LIMIT),
        name="inproj_lat" if rope else "inproj_ctx",
    )(*args)


HEADS_PER_DOT = 2


def _attn_kernel(sink_ref, qt_ref, kp_ref, kc_ref, kn_ref, kx_ref,
                 vp_ref, vc_ref, vn_ref, vx_ref, o_ref, *, bq, n_blocks):
    i = pl.program_id(1)
    kcat = jnp.concatenate([kp_ref[...], kc_ref[...], kn_ref[...], kx_ref[...]], axis=0)
    vtcat = jnp.concatenate([vp_ref[...], vc_ref[...], vn_ref[...], vx_ref[...]], axis=1)
    nw = 3 * bq
    nq = HEADS_PER_DOT * bq
    c = lax.broadcasted_iota(jnp.int32, (nw, nq), 0)
    r = lax.broadcasted_iota(jnp.int32, (nw, nq), 1) % bq
    rel = c - bq - r
    keep = (rel <= WINDOW) & (rel >= -WINDOW)
    keep = keep & ((c >= bq) | (i > 0)) & ((c < 2 * bq) | (i < n_blocks - 1))
    lane = lax.broadcasted_iota(jnp.int32, (1, nq), 1)
    zeros = jnp.zeros((HEAD_DIM, nq), BF16)

    tiles = [(kvh, kvh * GROUP + pr * HEADS_PER_DOT)
             for kvh in range(ATT_KV_HEADS) for pr in range(GROUP // HEADS_PER_DOT)]

    scores = []
    for kvh, h0 in tiles:
        qt = jnp.concatenate(
            [qt_ref[(h0 + u) * HEAD_DIM:(h0 + u + 1) * HEAD_DIM, :]
             for u in range(HEADS_PER_DOT)], axis=1)
        rhs = jnp.concatenate([qt, zeros] if kvh == 0 else [zeros, qt], axis=0)
        scores.append(_dot(kcat, rhs))

    probs = []
    for (kvh, h0), st in zip(tiles, scores):
        st_w = jnp.where(keep, st[0:nw], NEG_INF)
        st_c = st[nw:]
        sink = jnp.where(lane < bq, sink_ref[h0], sink_ref[h0 + 1]) * LOG2E
        m = jnp.maximum(jnp.maximum(jnp.max(st_w, axis=0, keepdims=True),
                                    jnp.max(st_c, axis=0, keepdims=True)), sink)
        e_w = jnp.exp2(st_w - m)
        e_c = jnp.exp2(st_c - m)
        denom = (jnp.sum(e_w, axis=0, keepdims=True) + jnp.sum(e_c, axis=0, keepdims=True)
                 + jnp.exp2(sink - m))
        probs.append((jnp.concatenate([e_w.astype(BF16), e_c.astype(BF16)], axis=0), denom))

    for (kvh, h0), (et, denom) in zip(tiles, probs):
        vt = vtcat[kvh * HEAD_DIM:(kvh + 1) * HEAD_DIM]
        ot = _dot(vt, et) * (1.0 / denom)
        for u in range(HEADS_PER_DOT):
            o_ref[(h0 + u) * HEAD_DIM:(h0 + u + 1) * HEAD_DIM, :] = (
                ot[:, u * bq:(u + 1) * bq].astype(BF16))


def _attn(sink, qt, k, vt, k_ctx, vt_ctx, *, batch, seq, n_ctx):
    bq = WINDOW
    nb = seq // bq
    k3 = k.reshape(batch, seq, NKV)
    kx3 = k_ctx.reshape(batch, n_ctx, NKV)
    prev = lambda i: jnp.maximum(i - 1, 0)
    nxt = lambda i: jnp.minimum(i + 1, nb - 1)
    return pl.pallas_call(
        functools.partial(_attn_kernel, bq=bq, n_blocks=nb),
        out_shape=jax.ShapeDtypeStruct((batch, NQ, seq), BF16),
        grid=(batch, nb),
        in_specs=[
            pl.BlockSpec(memory_space=pltpu.SMEM),
            pl.BlockSpec((None, NQ, bq), lambda b, i: (b, 0, i)),
            pl.BlockSpec((None, bq, NKV), lambda b, i: (b, prev(i), 0)),
            pl.BlockSpec((None, bq, NKV), lambda b, i: (b, i, 0)),
            pl.BlockSpec((None, bq, NKV), lambda b, i: (b, nxt(i), 0)),
            pl.BlockSpec((None, n_ctx, NKV), lambda b, i: (b, 0, 0)),
            pl.BlockSpec((None, NKV, bq), lambda b, i: (b, 0, prev(i))),
            pl.BlockSpec((None, NKV, bq), lambda b, i: (b, 0, i)),
            pl.BlockSpec((None, NKV, bq), lambda b, i: (b, 0, nxt(i))),
            pl.BlockSpec((None, NKV, n_ctx), lambda b, i: (b, 0, 0)),
        ],
        out_specs=pl.BlockSpec((None, NQ, bq), lambda b, i: (b, 0, i)),
        compiler_params=pltpu.CompilerParams(dimension_semantics=("arbitrary", "arbitrary"),
                                             vmem_limit_bytes=VMEM_LIMIT),
        name="attn",
    )(sink, qt, k3, k3, k3, kx3, vt, vt, vt, vt_ctx)


GRP_K = 4 * GLA_DK
GRP_V = 4 * GLA_DV
N_GRP = GLA_HEADS // 4


def _block_ones(n, blk, kind):
    r = lax.broadcasted_iota(jnp.int32, (n, n), 0)
    c = lax.broadcasted_iota(jnp.int32, (n, n), 1)
    same = (r // blk) == (c // blk)
    if kind == "lower":
        same = same & (c <= r)
    elif kind == "upper":
        same = same & (c >= r)
    return jnp.where(same, 1.0, 0.0).astype(BF16)


def _cum_and_total(la, tri, ones):
    hi, lo = _hi_lo(la)
    cum = _dot(tri, hi) + _dot(tri, lo)
    tot = _dot(ones, hi) + _dot(ones, lo)
    return cum, tot


def _state_mask():
    r = lax.broadcasted_iota(jnp.int32, (GRP_V, GRP_K), 0) // GLA_DV
    c = lax.broadcasted_iota(jnp.int32, (GRP_V, GRP_K), 1) // GLA_DK
    return r == c


def _state_step(st_ref, kd_c, v_c, d_row, smask):
    for g in range(N_GRP):
        ds = _dot_tn(v_c[:, g * GRP_V:(g + 1) * GRP_V], kd_c[:, g * GRP_K:(g + 1) * GRP_K])
        ds = jnp.where(smask, ds, 0.0)
        st_ref[g] = st_ref[g] * d_row[:, g * GRP_K:(g + 1) * GRP_K] + ds


def _scan_states(st_ref, k, v, la, *, reverse, store_ref=None, store_base=None):
    t = k.shape[0]
    nch = t // CHUNK
    tri = _block_ones(t, CHUNK, "upper" if reverse else "lower")
    ones = _block_ones(t, CHUNK, "all")
    cum, tot = _cum_and_total(la, tri, ones)
    kd = (k.astype(F32) * jnp.exp(tot - cum)).astype(BF16)
    dec = jnp.exp(tot)
    smask = _state_mask()
    order = range(nch - 1, -1, -1) if reverse else range(nch)
    for ci in order:
        sl = slice(ci * CHUNK, (ci + 1) * CHUNK)
        if store_ref is not None:
            for g in range(N_GRP):
                store_ref[store_base + ci, g] = st_ref[g].astype(BF16)
        _state_step(st_ref, kd[sl], v[sl], dec[ci * CHUNK:ci * CHUNK + 1], smask)


def _chunk_outputs(q, k, v, la, states, *, reverse, st_ref=None):
    t = q.shape[0]
    nch = t // CHUNK
    tri = _block_ones(t, CHUNK, "upper" if reverse else "lower")
    ones = _block_ones(t, CHUNK, "all")
    cum, tot = _cum_and_total(la, tri, ones)
    qd = (q.astype(F32) * jnp.exp(cum)).astype(BF16)
    ki = (k.astype(F32) * jnp.exp(-cum)).astype(BF16)
    if st_ref is not None:
        kd = (k.astype(F32) * jnp.exp(tot - cum)).astype(BF16)
        dec = jnp.exp(tot)
        smask = _state_mask()

    hr = lax.broadcasted_iota(jnp.int32, (GLA_HEADS * CHUNK, GQ), 0) // CHUNK
    hc = lax.broadcasted_iota(jnp.int32, (GLA_HEADS * CHUNK, GQ), 1) // GLA_DK
    kmask = hr == hc
    vr = lax.broadcasted_iota(jnp.int32, (4 * CHUNK, GRP_V), 0) // CHUNK
    vc = lax.broadcasted_iota(jnp.int32, (4 * CHUNK, GRP_V), 1) // GLA_DV
    vmask = vr == vc
    ai = lax.broadcasted_iota(jnp.int32, (CHUNK, GLA_HEADS * CHUNK), 0)
    aj = lax.broadcasted_iota(jnp.int32, (CHUNK, GLA_HEADS * CHUNK), 1) % CHUNK
    amask = (aj >= ai) if reverse else (aj <= ai)

    outs = []
    for ci in range(nch):
        sl = slice(ci * CHUNK, (ci + 1) * CHUNK)
        qd_c, ki_c, v_c = qd[sl], ki[sl], v[sl]
        ki_bd = jnp.where(kmask, jnp.concatenate([ki_c] * GLA_HEADS, axis=0),
                          jnp.zeros((), BF16))
        a = _dot_nt(qd_c, ki_bd)
        a = jnp.where(amask, a, 0.0).astype(BF16)
        parts = []
        for g in range(N_GRP):
            v_g = v_c[:, g * GRP_V:(g + 1) * GRP_V]
            v_bd = jnp.where(vmask, jnp.concatenate([v_g] * 4, axis=0), jnp.zeros((), BF16))
            o_g = _dot(a[:, g * 4 * CHUNK:(g + 1) * 4 * CHUNK], v_bd)
            if st_ref is not None:
                s_g = st_ref[g].astype(BF16)
            else:
                s_g = states(ci, g)
            o_g = o_g + _dot_nt(qd_c[:, g * GRP_K:(g + 1) * GRP_K], s_g)
            parts.append(o_g)
        outs.append(jnp.concatenate(parts, axis=1))
        if st_ref is not None:
            _state_step(st_ref, kd[sl], v_c, dec[ci * CHUNK:ci * CHUNK + 1], smask)
    return jnp.concatenate(outs, axis=0)


def _gla_kernel(gin_ref, la_ref, ginc_ref, lac_ref, gn_ref, o_ref, st_ref, sb_ref, *,
                tb, n_blocks):
    phase = pl.program_id(1)
    j = pl.program_id(2)
    ch_per_blk = tb // CHUNK

    @pl.when(phase == 0)
    def _backward_states():
        @pl.when(j == 0)
        def _():
            st_ref[...] = jnp.zeros_like(st_ref)
            _scan_states(st_ref, ginc_ref[:, GQ:2 * GQ], ginc_ref[:, 2 * GQ:2 * GQ + GV],
                         lac_ref[:, GQ:2 * GQ], reverse=True)

        blk = n_blocks - 1 - j
        _scan_states(st_ref, gin_ref[:, GQ:2 * GQ], gin_ref[:, 2 * GQ:2 * GQ + GV],
                     la_ref[:, GQ:2 * GQ], reverse=True,
                     store_ref=sb_ref, store_base=blk * ch_per_blk)

    @pl.when(phase == 1)
    def _outputs():
        @pl.when(j == 0)
        def _():
            st_ref[...] = jnp.zeros_like(st_ref)
            _scan_states(st_ref, ginc_ref[:, GQ:2 * GQ], ginc_ref[:, 2 * GQ:2 * GQ + GV],
                         lac_ref[:, 0:GQ], reverse=False)

        q = gin_ref[:, 0:GQ]
        k = gin_ref[:, GQ:2 * GQ]
        v = gin_ref[:, 2 * GQ:2 * GQ + GV]
        o_f = _chunk_outputs(q, k, v, la_ref[:, 0:GQ], None, reverse=False, st_ref=st_ref)
        base = j * ch_per_blk
        o_b = _chunk_outputs(q, k, v, la_ref[:, GQ:2 * GQ],
                             lambda ci, g: sb_ref[base + ci, g], reverse=True)
        o = o_f + o_b
        hm = _block_ones(GV, GLA_DV, "all")
        ms = _dot((o * o).astype(BF16), hm) * (1.0 / GLA_DV)
        y = o * lax.rsqrt(ms + EPS) * gn_ref[...]
        gate = gin_ref[:, 2 * GQ + GV:GIN_W].astype(F32)
        o_ref[...] = (y * _silu(gate)).astype(BF16)


def _gla(gin, la, gin_ctx, la_ctx, gn_tiled, *, batch, seq, n_ctx):
    tb = 256
    nb = seq // tb
    nch = seq // CHUNK
    gin3 = gin.reshape(batch, seq, GIN_W)
    la3 = la.reshape(batch, seq, LA_W)
    ginc3 = gin_ctx.reshape(batch, n_ctx, GIN_W)
    lac3 = la_ctx.reshape(batch, n_ctx, LA_W)

    def blk_map(b, p, j):
        return (b, jnp.where(p == 0, nb - 1 - j, j), 0)

    return pl.pallas_call(
        functools.partial(_gla_kernel, tb=tb, n_blocks=nb),
        out_shape=jax.ShapeDtypeStruct((batch, seq, GV), BF16),
        grid=(batch, 2, nb),
        in_specs=[pl.BlockSpec((None, tb, GIN_W), blk_map),
                  pl.BlockSpec((None, tb, LA_W), blk_map),
                  pl.BlockSpec((None, n_ctx, GIN_W), lambda b, p, j: (b, 0, 0)),
                  pl.BlockSpec((None, n_ctx, LA_W), lambda b, p, j: (b, 0, 0)),
                  pl.BlockSpec((1, GV), lambda b, p, j: (0, 0))],
        out_specs=pl.BlockSpec((None, tb, GV), lambda b, p, j: (b, jnp.where(p == 0, 0, j), 0)),
        scratch_shapes=[pltpu.VMEM((N_GRP, GRP_V, GRP_K), F32),
                        pltpu.VMEM((nch, N_GRP, GRP_V, GRP_K), BF16)],
        compiler_params=pltpu.CompilerParams(
            dimension_semantics=("arbitrary", "arbitrary", "arbitrary"),
            vmem_limit_bytes=VMEM_LIMIT),
        name="gla",
    )(gin3, la3, ginc3, lac3, gn_tiled)


FFN_CHUNK = 256


def _mix_ffn_kernel(att_ref, gla_ref, x_ref, ada_ref, gpm_ref, gpf_ref, gqf_ref,
                    woa_ref, wog_ref, wfi_ref, wfo_ref, o_ref, acc_ref, *, tiles_per_batch):
    row = pl.program_id(0) // tiles_per_batch
    ada = ada_ref[pl.ds(row, 1), :]
    gt1 = ada[:, 2 * D:3 * D]
    sh2 = ada[:, 3 * D:4 * D]
    sc2 = ada[:, 4 * D:5 * D]
    gt2 = ada[:, 5 * D:6 * D]

    y = _dot_tn(att_ref[...], woa_ref[...]) + _dot(gla_ref[...], wog_ref[...])
    x1 = x_ref[...] + gt1 * (_rms(y) * gpm_ref[...])
    h = ((_rms(x1) * gpf_ref[...]) * (1.0 + sc2) + sh2).astype(BF16)

    n_chunks = FFN_HIDDEN // FFN_CHUNK
    for ci in range(n_chunks):
        c0 = ci * FFN_CHUNK
        g = _dot(h, wfi_ref[:, c0:c0 + FFN_CHUNK])
        u = _dot(h, wfi_ref[:, FFN_HIDDEN + c0:FFN_HIDDEN + c0 + FFN_CHUNK])
        a = (_silu(g) * u).astype(BF16)
        part = _dot(a, wfo_ref[c0:c0 + FFN_CHUNK, :])
        if ci == 0:
            acc_ref[...] = part
        else:
            acc_ref[...] += part
    o_ref[...] = x1 + gt2 * (_rms(acc_ref[...]) * gqf_ref[...])


def _mix_ffn(att_t, gla, x2d, ada, g_post_mix, g_pre_ffn, g_post_ffn,
             w_out_a, w_out_g, w_ffn_in, w_ffn_out, *, tm, tiles_per_batch):
    n = x2d.shape[0]
    tpb = tiles_per_batch
    const = lambda i: (0, 0)
    resident = functools.partial(pl.BlockSpec, index_map=const, pipeline_mode=pl.Buffered(1))
    return pl.pallas_call(
        functools.partial(_mix_ffn_kernel, tiles_per_batch=tpb),
        out_shape=jax.ShapeDtypeStruct((n, D), F32),
        grid=(n // tm,),
        in_specs=[pl.BlockSpec((None, NQ, tm), lambda i: (i // tpb, 0, i % tpb)),
                  pl.BlockSpec((tm, GV), lambda i: (i, 0)),
                  pl.BlockSpec((tm, D), lambda i: (i, 0)),
                  pl.BlockSpec((8, 6 * D), const),
                  pl.BlockSpec((1, D), const),
                  pl.BlockSpec((1, D), const),
                  pl.BlockSpec((1, D), const),
                  resident((NQ, D)),
                  resident((GV, D)),
                  resident((D, 2 * FFN_HIDDEN)),
                  resident((FFN_HIDDEN, D))],
        out_specs=pl.BlockSpec((tm, D), lambda i: (i, 0)),
        scratch_shapes=[pltpu.VMEM((tm, D), F32)],
        compiler_params=pltpu.CompilerParams(dimension_semantics=("arbitrary",),
                                             vmem_limit_bytes=VMEM_LIMIT),
        name="mix_ffn",
    )(att_t, gla, x2d, ada, g_post_mix, g_pre_ffn, g_post_ffn,
      w_out_a, w_out_g, w_ffn_in, w_ffn_out)


def _rope_tables(seq):
    rows = seq // GRID_W
    row = jnp.repeat(jnp.arange(rows), GRID_W).astype(F32)
    col = jnp.tile(jnp.arange(GRID_W), rows).astype(F32)
    half = HEAD_DIM // 2
    inv_freq = ROPE_BASE ** (-jnp.arange(0, half, 2, dtype=F32) / half)
    ang_r = row[:, None] * inv_freq[None, :]
    ang_c = col[:, None] * inv_freq[None, :]
    ang = jnp.concatenate([ang_r, ang_r, ang_c, ang_c], axis=-1)
    cos, sin = jnp.cos(ang), jnp.sin(ang)
    even = ((np.arange(HEAD_DIM) // ROPE_BLK) % 2 == 0)[None, :]
    sa = jnp.where(even, -sin, 0.0)
    sb = jnp.where(even, 0.0, sin)
    tile2 = lambda t: jnp.concatenate([t, t], axis=-1)
    return (cos.T, sa.T, sb.T, tile2(cos), tile2(sa), tile2(sb))


def kernel(x, c, ctx, c_ctx, w_ada, b_ada, g_pre_mix, g_post_mix, g_pre_ffn, g_post_ffn,
           w_in, attn_sink, w_gate_fwd, b_gate_fwd, w_gate_bwd, b_gate_bwd, g_gla_norm,
           w_out, w_ffn_in, w_ffn_out):
    batch, seq, _ = x.shape
    n_ctx = ctx.shape[1]
    depth = w_ada.shape[0]
    assert depth == 1
    l = 0

    c8 = jnp.zeros((8, D), F32).at[0:batch].set(c).at[4].set(c_ctx)
    ada = _ada(c8, w_ada[l], b_ada[l][None, :])

    wi = w_in[l]
    v0 = NQ + NKV
    g0 = NQ + 2 * NKV
    z0 = g0 + GIN_W
    w_qv_t = jnp.concatenate([wi[:, 0:NQ], wi[:, v0:g0]], axis=1).T.astype(BF16)
    w_rest = jnp.concatenate(
        [wi[:, NQ:v0], wi[:, g0:z0], wi[:, z0:],
         jnp.zeros((D, Z_W - 2 * GATE_RANK), F32)], axis=1).astype(BF16)
    w_gate_p = jnp.zeros((Z_W, LA_W), F32)
    w_gate_p = w_gate_p.at[0:GATE_RANK, 0:GQ].set(w_gate_fwd[l])
    w_gate_p = w_gate_p.at[GATE_RANK:2 * GATE_RANK, GQ:].set(w_gate_bwd[l]).astype(BF16)
    b_gate_p = jnp.concatenate([b_gate_fwd[l], b_gate_bwd[l]])[None, :]

    tm = 512
    tables = _rope_tables(seq)
    x2d = x.reshape(batch * seq, D)
    g_pre = g_pre_mix[l][None, :]
    qt, k, vt, gin, la = _inproj(x2d, ada, g_pre, w_qv_t, w_rest, w_gate_p, b_gate_p,
                                 tables, batch=batch, tokens=seq, tm=tm)
    _, k_c, vt_c, gin_c, la_c = _inproj(ctx.reshape(batch * n_ctx, D), ada, g_pre, w_qv_t,
                                        w_rest, w_gate_p, b_gate_p, None,
                                        batch=batch, tokens=n_ctx, tm=n_ctx)

    att_t = _attn(attn_sink[l], qt, k, vt, k_c, vt_c, batch=batch, seq=seq, n_ctx=n_ctx)

    gn_tiled = jnp.tile(g_gla_norm[l], GLA_HEADS)[None, :]
    gla = _gla(gin, la, gin_c, la_c, gn_tiled, batch=batch, seq=seq, n_ctx=n_ctx)

    wo = w_out[l]
    out = _mix_ffn(att_t, gla.reshape(batch * seq, GV), x2d, ada,
                   g_post_mix[l][None, :], g_pre_ffn[l][None, :], g_post_ffn[l][None, :],
                   wo[:NQ].astype(BF16), wo[NQ:].astype(BF16),
                   w_ffn_in[l].astype(BF16), w_ffn_out[l].astype(BF16),
                   tm=tm, tiles_per_batch=seq // tm)
    return out.reshape(batch, seq, D)
```

```python
import functools

import jax
import jax.numpy as jnp
import numpy as np
from jax import lax
from jax.experimental import pallas as pl
from jax.experimental.pallas import tpu as pltpu

D = 1024
HEAD_DIM = 64
ATT_HEADS = 8
ATT_KV_HEADS = 2
GROUP = ATT_HEADS // ATT_KV_HEADS
WINDOW = 128
GRID_W = 64
ROPE_BASE = 10000.0
ROPE_BLK = HEAD_DIM // 4
GLA_HEADS = 8
GLA_DK = 32
GLA_DV = 64
CHUNK = 64
GATE_RANK = 16
GATE_TAU = 16.0
FFN_HIDDEN = 2816
NEG_INF = -1e30
EPS = 1e-6
LOG2E = 1.4426950408889634
Q_SCALE = LOG2E * HEAD_DIM ** -0.5
LA_SCALE = LOG2E / GATE_TAU

NQ = ATT_HEADS * HEAD_DIM
NKV = ATT_KV_HEADS * HEAD_DIM
GQ = GLA_HEADS * GLA_DK
GV = GLA_HEADS * GLA_DV
GIN_W = 2 * GQ + 2 * GV
Z_W = 128
LA_W = 2 * GQ
REST_W = NKV + GIN_W + Z_W

LANES = 128
VMEM_LIMIT = 56 * 1024 * 1024

BF16 = jnp.bfloat16
F32 = jnp.float32


def _dot(a, b):
    return jnp.dot(a, b, preferred_element_type=F32)


def _dot_nt(a, b):
    return lax.dot_general(a, b, (((1,), (1,)), ((), ())), preferred_element_type=F32)


def _dot_tn(a, b):
    return lax.dot_general(a, b, (((0,), (0,)), ((), ())), preferred_element_type=F32)


def _rms(x):
    return x * lax.rsqrt(jnp.mean(x * x, axis=-1, keepdims=True) + EPS)


def _silu(x):
    return x * (1.0 / (1.0 + jnp.exp(-x)))


def _ada_kernel(c_ref, w_ref, b_ref, o_ref):
    a = _silu(c_ref[...]).astype(BF16)
    o_ref[...] = _dot(a, w_ref[...].astype(BF16)) + b_ref[...]


def _ada(c8, w_ada, b_ada):
    n = w_ada.shape[1]
    bn = 1024
    return pl.pallas_call(
        _ada_kernel,
        out_shape=jax.ShapeDtypeStruct((8, n), F32),
        grid=(n // bn,),
        in_specs=[pl.BlockSpec((8, D), lambda j: (0, 0)),
                  pl.BlockSpec((D, bn), lambda j: (0, j)),
                  pl.BlockSpec((1, bn), lambda j: (0, j))],
        out_specs=pl.BlockSpec((8, bn), lambda j: (0, j)),
        compiler_params=pltpu.CompilerParams(dimension_semantics=("arbitrary",)),
        name="ada",
    )(c8, w_ada, b_ada)


def _inproj_kernel(*refs, tiles_per_batch, rope):
    if rope:
        (x_ref, ada_ref, g_ref, wqv_ref, w_ref, wg_ref, bg_ref,
         cost_ref, sat_ref, sbt_ref, cos_ref, sa_ref, sb_ref,
         qt_ref, k_ref, vt_ref, gin_ref, lah_ref, lal_ref) = refs
        row = pl.program_id(0) // tiles_per_batch
    else:
        (x_ref, ada_ref, g_ref, wqv_ref, w_ref, wg_ref, bg_ref,
         qt_ref, k_ref, vt_ref, gin_ref, lah_ref, lal_ref) = refs
        row = 4
    ada = ada_ref[pl.ds(row, 1), :]
    sh1 = ada[:, 0:D]
    sc1 = ada[:, D:2 * D]
    h = (_rms(x_ref[...]) * g_ref[...]) * (1.0 + sc1) + sh1
    hb = h.astype(BF16)

    qv = _dot_nt(wqv_ref[...], hb)
    if rope:
        cost = cost_ref[...]
        sat = sat_ref[...]
        sbt = sbt_ref[...]
        for hd in range(ATT_HEADS):
            xh = qv[hd * HEAD_DIM:(hd + 1) * HEAD_DIM]
            up = jnp.concatenate([xh[ROPE_BLK:], xh[:ROPE_BLK]], axis=0)
            dn = jnp.concatenate([xh[HEAD_DIM - ROPE_BLK:], xh[:HEAD_DIM - ROPE_BLK]], axis=0)
            rh = (xh * cost + up * sat + dn * sbt) * Q_SCALE
            qt_ref[hd * HEAD_DIM:(hd + 1) * HEAD_DIM, :] = rh.astype(BF16)
    else:
        qt_ref[...] = qv[0:NQ].astype(BF16)
    vt_ref[...] = qv[NQ:NQ + NKV].astype(BF16)

    k = _dot(hb, w_ref[:, 0:NKV])
    if rope:
        k = (k * cos_ref[...] + pltpu.roll(k, LANES - ROPE_BLK, 1) * sa_ref[...]
             + pltpu.roll(k, ROPE_BLK, 1) * sb_ref[...])
    k_ref[...] = k.astype(BF16)

    gin = _dot(hb, w_ref[:, NKV:NKV + GIN_W])
    gin_ref[:, 0:GQ] = (gin[:, 0:GQ] * (GLA_DK ** -0.5)).astype(BF16)
    gin_ref[:, GQ:2 * GQ + GV] = gin[:, GQ:2 * GQ + GV].astype(BF16)
    gin_ref[:, 2 * GQ + GV:GIN_W] = _silu(gin[:, 2 * GQ + GV:GIN_W]).astype(BF16)

    z = _dot(hb, w_ref[:, NKV + GIN_W:REST_W])
    logits = _dot(z.astype(BF16), wg_ref[...]) + bg_ref[...]
    log_sig = jnp.minimum(logits, 0.0) - jnp.log(1.0 + jnp.exp(-jnp.abs(logits)))
    la = log_sig * LA_SCALE
    hi = la.astype(BF16)
    lah_ref[...] = hi
    lal_ref[...] = (la - hi.astype(F32)).astype(BF16)


def _inproj(x2d, ada, g_pre, w_qv_t, w_rest, w_gate_p, b_gate_p, tables, *,
            batch, tokens, tm):
    n = x2d.shape[0]
    tpb = tokens // tm
    rope = tables is not None
    const = lambda i: (0, 0)
    in_specs = [pl.BlockSpec((tm, D), lambda i: (i, 0)),
                pl.BlockSpec((8, 6 * D), const),
                pl.BlockSpec((1, D), const),
                pl.BlockSpec((NQ + NKV, D), const),
                pl.BlockSpec((D, REST_W), const),
                pl.BlockSpec((Z_W, LA_W), const),
                pl.BlockSpec((1, LA_W), const)]
    args = [x2d, ada, g_pre, w_qv_t, w_rest, w_gate_p, b_gate_p]
    if rope:
        in_specs += [pl.BlockSpec((HEAD_DIM, tm), lambda i: (0, i % tpb))] * 3
        in_specs += [pl.BlockSpec((tm, LANES), lambda i: (i % tpb, 0))] * 3
        args += list(tables)
    row_blk = lambda w: pl.BlockSpec((tm, w), lambda i: (i, 0))
    return pl.pallas_call(
        functools.partial(_inproj_kernel, tiles_per_batch=tpb, rope=rope),
        out_shape=(jax.ShapeDtypeStruct((batch, NQ, tokens), BF16),
                   jax.ShapeDtypeStruct((n, NKV), BF16),
                   jax.ShapeDtypeStruct((batch, NKV, tokens), BF16),
                   jax.ShapeDtypeStruct((n, GIN_W), BF16),
                   jax.ShapeDtypeStruct((n, LA_W), BF16),
                   jax.ShapeDtypeStruct((n, LA_W), BF16)),
        grid=(n // tm,),
        in_specs=in_specs,
        out_specs=(pl.BlockSpec((None, NQ, tm), lambda i: (i // tpb, 0, i % tpb)),
                   row_blk(NKV),
                   pl.BlockSpec((None, NKV, tm), lambda i: (i // tpb, 0, i % tpb)),
                   row_blk(GIN_W), row_blk(LA_W), row_blk(LA_W)),
        compiler_params=pltpu.CompilerParams(dimension_semantics=("arbitrary",),
                                             vmem_limit_bytes=VMEM_LIMIT),
        name="inproj_lat" if rope else "inproj_ctx",
    )(*args)


HEADS_PER_DOT = 2


def _attn_kernel(sink_ref, qt_ref, kp_ref, kc_ref, kn_ref, kx_ref,
                 vp_ref, vc_ref, vn_ref, vx_ref, o_ref, *, bq, n_blocks):
    i = pl.program_id(1)
    kcat = jnp.concatenate([kp_ref[...], kc_ref[...], kn_ref[...], kx_ref[...]], axis=0)
    vtcat = jnp.concatenate([vp_ref[...], vc_ref[...], vn_ref[...], vx_ref[...]], axis=1)
    nw = 3 * bq
    nq = HEADS_PER_DOT * bq
    c = lax.broadcasted_iota(jnp.int32, (nw, nq), 0)
    r = lax.broadcasted_iota(jnp.int32, (nw, nq), 1) % bq
    rel = c - bq - r
    keep = (rel <= WINDOW) & (rel >= -WINDOW)
    keep = keep & ((c >= bq) | (i > 0)) & ((c < 2 * bq) | (i < n_blocks - 1))
    lane = lax.broadcasted_iota(jnp.int32, (1, nq), 1)
    zeros = jnp.zeros((HEAD_DIM, nq), BF16)

    tiles = [(kvh, kvh * GROUP + pr * HEADS_PER_DOT)
             for kvh in range(ATT_KV_HEADS) for pr in range(GROUP // HEADS_PER_DOT)]

    scores = []
    for kvh, h0 in tiles:
        qt = jnp.concatenate(
            [qt_ref[(h0 + u) * HEAD_DIM:(h0 + u + 1) * HEAD_DIM, :]
             for u in range(HEADS_PER_DOT)], axis=1)
        rhs = jnp.concatenate([qt, zeros] if kvh == 0 else [zeros, qt], axis=0)
        scores.append(_dot(kcat, rhs))

    probs = []
    for (kvh, h0), st in zip(tiles, scores):
        st_w = jnp.where(keep, st[0:nw], NEG_INF)
        st_c = st[nw:]
        sink = jnp.where(lane < bq, sink_ref[h0], sink_ref[h0 + 1]) * LOG2E
        m = jnp.maximum(jnp.maximum(jnp.max(st_w, axis=0, keepdims=True),
                                    jnp.max(st_c, axis=0, keepdims=True)), sink)
        e_w = jnp.exp2(st_w - m)
        e_c = jnp.exp2(st_c - m)
        denom = (jnp.sum(e_w, axis=0, keepdims=True) + jnp.sum(e_c, axis=0, keepdims=True)
                 + jnp.exp2(sink - m))
        probs.append((jnp.concatenate([e_w.astype(BF16), e_c.astype(BF16)], axis=0), denom))

    for (kvh, h0), (et, denom) in zip(tiles, probs):
        vt = vtcat[kvh * HEAD_DIM:(kvh + 1) * HEAD_DIM]
        ot = _dot(vt, et) * (1.0 / denom)
        for u in range(HEADS_PER_DOT):
            o_ref[(h0 + u) * HEAD_DIM:(h0 + u + 1) * HEAD_DIM, :] = (
                ot[:, u * bq:(u + 1) * bq].astype(BF16))


def _attn(sink, qt, k, vt, k_ctx, vt_ctx, *, batch, seq, n_ctx):
    bq = WINDOW
    nb = seq // bq
    k3 = k.reshape(batch, seq, NKV)
    kx3 = k_ctx.reshape(batch, n_ctx, NKV)
    prev = lambda i: jnp.maximum(i - 1, 0)
    nxt = lambda i: jnp.minimum(i + 1, nb - 1)
    return pl.pallas_call(
        functools.partial(_attn_kernel, bq=bq, n_blocks=nb),
        out_shape=jax.ShapeDtypeStruct((batch, NQ, seq), BF16),
        grid=(batch, nb),
        in_specs=[
            pl.BlockSpec(memory_space=pltpu.SMEM),
            pl.BlockSpec((None, NQ, bq), lambda b, i: (b, 0, i)),
            pl.BlockSpec((None, bq, NKV), lambda b, i: (b, prev(i), 0)),
            pl.BlockSpec((None, bq, NKV), lambda b, i: (b, i, 0)),
            pl.BlockSpec((None, bq, NKV), lambda b, i: (b, nxt(i), 0)),
            pl.BlockSpec((None, n_ctx, NKV), lambda b, i: (b, 0, 0)),
            pl.BlockSpec((None, NKV, bq), lambda b, i: (b, 0, prev(i))),
            pl.BlockSpec((None, NKV, bq), lambda b, i: (b, 0, i)),
            pl.BlockSpec((None, NKV, bq), lambda b, i: (b, 0, nxt(i))),
            pl.BlockSpec((None, NKV, n_ctx), lambda b, i: (b, 0, 0)),
        ],
        out_specs=pl.BlockSpec((None, NQ, bq), lambda b, i: (b, 0, i)),
        compiler_params=pltpu.CompilerParams(dimension_semantics=("arbitrary", "arbitrary"),
                                             vmem_limit_bytes=VMEM_LIMIT),
        name="attn",
    )(sink, qt, k3, k3, k3, kx3, vt, vt, vt, vt_ctx)


GRP_K = 4 * GLA_DK
GRP_V = 4 * GLA_DV
N_GRP = GLA_HEADS // 4
GLA_TB = 512
CUM_BLK = 256


def _gla_constants():
    t = CUM_BLK
    r = np.arange(t)[:, None]
    c = np.arange(t)[None, :]
    same_chunk = (r // CHUNK) == (c // CHUNK)
    tri_l = (same_chunk & (c <= r)).astype(np.float32)
    tri_u = (same_chunk & (c >= r)).astype(np.float32)
    rk = np.arange(4 * CHUNK)[:, None] // CHUNK
    kmask = (rk == (np.arange(GRP_K)[None, :] // GLA_DK)).astype(np.float32)
    vmask = (rk == (np.arange(GRP_V)[None, :] // GLA_DV)).astype(np.float32)
    smask = ((np.arange(GRP_K)[:, None] // GLA_DK)
             == (np.arange(GRP_V)[None, :] // GLA_DV)).astype(np.float32)
    ai = np.arange(CHUNK)[:, None]
    aj = np.arange(4 * CHUNK)[None, :] % CHUNK
    causal = (aj <= ai).astype(np.float32)
    anti = (aj >= ai).astype(np.float32)
    hmean = ((np.arange(GRP_V)[:, None] // GLA_DV)
             == (np.arange(GRP_V)[None, :] // GLA_DV)).astype(np.float32) / GLA_DV
    return (jnp.asarray(tri_l, BF16), jnp.asarray(tri_u, BF16), jnp.asarray(kmask, BF16),
            jnp.asarray(vmask, BF16), jnp.asarray(smask, F32), jnp.asarray(causal, F32),
            jnp.asarray(anti, F32), jnp.asarray(hmean, BF16))


def _gk(g):
    return slice(g * GRP_K, (g + 1) * GRP_K)


def _gv(g):
    return slice(g * GRP_V, (g + 1) * GRP_V)


def _chunk_slices(t):
    return [slice(ci * CHUNK, (ci + 1) * CHUNK) for ci in range(t // CHUNK)]


def _cum_and_totals(la_hi, la_lo, tri, *, reverse):
    parts = [slice(p, p + CUM_BLK) for p in range(0, la_hi.shape[0], CUM_BLK)]
    cum = jnp.concatenate([_dot(tri, la_hi[p]) + _dot(tri, la_lo[p]) for p in parts], axis=0)
    edge = 0 if reverse else CHUNK - 1
    tots = [cum[sl.start + edge:sl.start + edge + 1] for sl in _chunk_slices(cum.shape[0])]
    return cum, tots


def _bcast_chunks(rows, width):
    return jnp.concatenate([jnp.broadcast_to(r, (CHUNK, width)) for r in rows], axis=0)


def _decay(x, log2_factor):
    return x * jnp.exp2(log2_factor).astype(BF16)


def _state_increments(kd, v, smask):
    return [[_dot_tn(kd[sl, _gk(g)], v[sl, _gv(g)]) * smask for g in range(N_GRP)]
            for sl in _chunk_slices(kd.shape[0])]


def _decay_columns(tots):
    width = tots[0].shape[1]
    row = lax.broadcasted_iota(jnp.int32, (8, width), 0)
    rows = jnp.zeros((8, width), F32)
    for ci, t in enumerate(tots):
        rows = jnp.where(row == ci, jnp.broadcast_to(t, (8, width)), rows)
    return jnp.exp2(rows).T


def _scan_states(st_ref, k, v, la_hi, la_lo, tri, smask, *, reverse,
                 store_ref=None, store_base=None):
    nch = k.shape[0] // CHUNK
    cum, tots = _cum_and_totals(la_hi, la_lo, tri, reverse=reverse)
    kd = _decay(k, _bcast_chunks(tots, GQ) - cum)
    ds = _state_increments(kd, v, smask)
    dcol = _decay_columns(tots)
    state = [st_ref[g] for g in range(N_GRP)]
    for ci in (range(nch - 1, -1, -1) if reverse else range(nch)):
        for g in range(N_GRP):
            if store_ref is not None:
                store_ref[store_base + ci, g] = state[g].astype(BF16)
            state[g] = state[g] * dcol[_gk(g), ci:ci + 1] + ds[ci][g]
    for g in range(N_GRP):
        st_ref[g] = state[g]


def _block_outputs(q, k, v, la_f, la_b, st_ref, bwd_state, consts):
    tri_l, tri_u, kmask, vmask, smask, causal, anti = consts
    chunks = _chunk_slices(q.shape[0])

    cum_f, tots_f = _cum_and_totals(la_f[0], la_f[1], tri_l, reverse=False)
    cum_b, tots_b = _cum_and_totals(la_b[0], la_b[1], tri_u, reverse=True)

    qd_f = _decay(q, cum_f)
    qd_b = _decay(q, cum_b)
    ki_f = _decay(k, -cum_f)
    ki_b = _decay(k, -cum_b)
    kd_f = _decay(k, _bcast_chunks(tots_f, GQ) - cum_f)
    dcol_f = _decay_columns(tots_f)

    def scores(qd, ki, sl, g):
        ki_bd = jnp.concatenate([ki[sl, _gk(g)]] * 4, axis=0) * kmask
        return _dot_nt(qd[sl, _gk(g)], ki_bd)

    a_f = [[scores(qd_f, ki_f, sl, g) for g in range(N_GRP)] for sl in chunks]
    a_b = [[scores(qd_b, ki_b, sl, g) for g in range(N_GRP)] for sl in chunks]
    ds = _state_increments(kd_f, v, smask)
    a = [[(af[g] * causal + ab[g] * anti).astype(BF16) for g in range(N_GRP)]
         for af, ab in zip(a_f, a_b)]

    state = [st_ref[g] for g in range(N_GRP)]
    st_in = []
    for ci in range(len(chunks)):
        st_in.append([s.astype(BF16) for s in state])
        state = [state[g] * dcol_f[_gk(g), ci:ci + 1] + ds[ci][g] for g in range(N_GRP)]
    for g in range(N_GRP):
        st_ref[g] = state[g]

    outs = []
    for ci, sl in enumerate(chunks):
        parts = []
        for g in range(N_GRP):
            v_bd = jnp.concatenate([v[sl, _gv(g)]] * 4, axis=0) * vmask
            intra = _dot(a[ci][g], v_bd)
            qd2 = jnp.concatenate([qd_f[sl, _gk(g)], qd_b[sl, _gk(g)]], axis=1)
            st2 = jnp.concatenate([st_in[ci][g], bwd_state(ci, g)], axis=0)
            parts.append(intra + _dot(qd2, st2))
        outs.append(jnp.concatenate(parts, axis=1))
    return jnp.concatenate(outs, axis=0)


def _gla_kernel(gin_ref, lah_ref, lal_ref, ginc_ref, lahc_ref, lalc_ref, gn_ref,
                tril_ref, triu_ref, kmask_ref, vmask_ref, smask_ref, causal_ref, anti_ref,
                hmean_ref, o_ref, st_ref, sb_ref, *, n_blocks):
    phase = pl.program_id(1)
    j = pl.program_id(2)
    ch_per_blk = GLA_TB // CHUNK
    fwd = slice(0, GQ)
    bwd = slice(GQ, 2 * GQ)
    kcols = slice(GQ, 2 * GQ)
    vcols = slice(2 * GQ, 2 * GQ + GV)

    def ctx_state(cols, tri_ref, reverse):
        st_ref[...] = jnp.zeros_like(st_ref)
        _scan_states(st_ref, ginc_ref[:, kcols], ginc_ref[:, vcols],
                     lahc_ref[:, cols], lalc_ref[:, cols], tri_ref[...], smask_ref[...],
                     reverse=reverse)

    @pl.when(phase == 0)
    def _backward_states():
        @pl.when(j == 0)
        def _():
            ctx_state(bwd, triu_ref, True)

        blk = n_blocks - 1 - j
        _scan_states(st_ref, gin_ref[:, kcols], gin_ref[:, vcols],
                     lah_ref[:, bwd], lal_ref[:, bwd], triu_ref[...], smask_ref[...],
                     reverse=True, store_ref=sb_ref, store_base=blk * ch_per_blk)

    @pl.when(phase == 1)
    def _outputs():
        @pl.when(j == 0)
        def _():
            ctx_state(fwd, tril_ref, False)

        base = j * ch_per_blk
        consts = (tril_ref[...], triu_ref[...], kmask_ref[...], vmask_ref[...],
                  smask_ref[...], causal_ref[...], anti_ref[...])
        o = _block_outputs(gin_ref[:, 0:GQ], gin_ref[:, kcols], gin_ref[:, vcols],
                           (lah_ref[:, fwd], lal_ref[:, fwd]),
                           (lah_ref[:, bwd], lal_ref[:, bwd]), st_ref,
                           lambda ci, g: sb_ref[base + ci, g], consts)
        o2 = (o * o).astype(BF16)
        hmean = hmean_ref[...]
        ms = jnp.concatenate([_dot(o2[:, _gv(g)], hmean) for g in range(N_GRP)], axis=1)
        y = o * lax.rsqrt(ms + EPS) * gn_ref[...]
        o_ref[...] = (y * gin_ref[:, 2 * GQ + GV:GIN_W].astype(F32)).astype(BF16)


def _gla(gin, la_hi, la_lo, gin_ctx, la_hi_ctx, la_lo_ctx, gn_tiled, *, batch, seq, n_ctx):
    assert n_ctx % CUM_BLK == 0 and n_ctx // CHUNK <= 8 and GLA_TB // CHUNK <= 8
    tb = GLA_TB
    nb = seq // tb
    nch = seq // CHUNK
    r3 = lambda a, t: a.reshape(batch, t, a.shape[-1])
    consts = _gla_constants()

    def blk_map(b, p, j):
        return (b, jnp.where(p == 0, nb - 1 - j, j), 0)

    ctx_map = lambda b, p, j: (b, 0, 0)
    const_map = lambda b, p, j: (0, 0)
    return pl.pallas_call(
        functools.partial(_gla_kernel, n_blocks=nb),
        out_shape=jax.ShapeDtypeStruct((batch, seq, GV), BF16),
        grid=(batch, 2, nb),
        in_specs=[pl.BlockSpec((None, tb, GIN_W), blk_map),
                  pl.BlockSpec((None, tb, LA_W), blk_map),
                  pl.BlockSpec((None, tb, LA_W), blk_map),
                  pl.BlockSpec((None, n_ctx, GIN_W), ctx_map),
                  pl.BlockSpec((None, n_ctx, LA_W), ctx_map),
                  pl.BlockSpec((None, n_ctx, LA_W), ctx_map),
                  pl.BlockSpec((1, GV), const_map)]
                 + [pl.BlockSpec(cst.shape, const_map) for cst in consts],
        out_specs=pl.BlockSpec((None, tb, GV), lambda b, p, j: (b, jnp.where(p == 0, 0, j), 0)),
        scratch_shapes=[pltpu.VMEM((N_GRP, GRP_K, GRP_V), F32),
                        pltpu.VMEM((nch, N_GRP, GRP_K, GRP_V), BF16)],
        compiler_params=pltpu.CompilerParams(
            dimension_semantics=("arbitrary", "arbitrary", "arbitrary"),
            vmem_limit_bytes=VMEM_LIMIT),
        name="gla",
    )(r3(gin, seq), r3(la_hi, seq), r3(la_lo, seq), r3(gin_ctx, n_ctx), r3(la_hi_ctx, n_ctx),
      r3(la_lo_ctx, n_ctx), gn_tiled, *consts)


FFN_CHUNK = 256


def _mix_ffn_kernel(att_ref, gla_ref, x_ref, ada_ref, gpm_ref, gpf_ref, gqf_ref,
                    woa_ref, wog_ref, wfi_ref, wfo_ref, o_ref, acc_ref, *, tiles_per_batch):
    row = pl.program_id(0) // tiles_per_batch
    ada = ada_ref[pl.ds(row, 1), :]
    gt1 = ada[:, 2 * D:3 * D]
    sh2 = ada[:, 3 * D:4 * D]
    sc2 = ada[:, 4 * D:5 * D]
    gt2 = ada[:, 5 * D:6 * D]

    y = _dot_tn(att_ref[...], woa_ref[...]) + _dot(gla_ref[...], wog_ref[...])
    x1 = x_ref[...] + gt1 * (_rms(y) * gpm_ref[...])
    h = ((_rms(x1) * gpf_ref[...]) * (1.0 + sc2) + sh2).astype(BF16)

    n_chunks = FFN_HIDDEN // FFN_CHUNK
    for ci in range(n_chunks):
        c0 = ci * FFN_CHUNK
        g = _dot(h, wfi_ref[:, c0:c0 + FFN_CHUNK])
        u = _dot(h, wfi_ref[:, FFN_HIDDEN + c0:FFN_HIDDEN + c0 + FFN_CHUNK])
        a = (_silu(g) * u).astype(BF16)
        part = _dot(a, wfo_ref[c0:c0 + FFN_CHUNK, :])
        if ci == 0:
            acc_ref[...] = part
        else:
            acc_ref[...] += part
    o_ref[...] = x1 + gt2 * (_rms(acc_ref[...]) * gqf_ref[...])


def _mix_ffn(att_t, gla, x2d, ada, g_post_mix, g_pre_ffn, g_post_ffn,
             w_out_a, w_out_g, w_ffn_in, w_ffn_out, *, tm, tiles_per_batch):
    n = x2d.shape[0]
    tpb = tiles_per_batch
    const = lambda i: (0, 0)
    resident = functools.partial(pl.BlockSpec, index_map=const, pipeline_mode=pl.Buffered(1))
    return pl.pallas_call(
        functools.partial(_mix_ffn_kernel, tiles_per_batch=tpb),
        out_shape=jax.ShapeDtypeStruct((n, D), F32),
        grid=(n // tm,),
        in_specs=[pl.BlockSpec((None, NQ, tm), lambda i: (i // tpb, 0, i % tpb)),
                  pl.BlockSpec((tm, GV), lambda i: (i, 0)),
                  pl.BlockSpec((tm, D), lambda i: (i, 0)),
                  pl.BlockSpec((8, 6 * D), const),
                  pl.BlockSpec((1, D), const),
                  pl.BlockSpec((1, D), const),
                  pl.BlockSpec((1, D), const),
                  resident((NQ, D)),
                  resident((GV, D)),
                  resident((D, 2 * FFN_HIDDEN)),
                  resident((FFN_HIDDEN, D))],
        out_specs=pl.BlockSpec((tm, D), lambda i: (i, 0)),
        scratch_shapes=[pltpu.VMEM((tm, D), F32)],
        compiler_params=pltpu.CompilerParams(dimension_semantics=("arbitrary",),
                                             vmem_limit_bytes=VMEM_LIMIT),
        name="mix_ffn",
    )(att_t, gla, x2d, ada, g_post_mix, g_pre_ffn, g_post_ffn,
      w_out_a, w_out_g, w_ffn_in, w_ffn_out)


def _rope_tables(seq):
    rows = seq // GRID_W
    row = jnp.repeat(jnp.arange(rows), GRID_W).astype(F32)
    col = jnp.tile(jnp.arange(GRID_W), rows).astype(F32)
    half = HEAD_DIM // 2
    inv_freq = ROPE_BASE ** (-jnp.arange(0, half, 2, dtype=F32) / half)
    ang_r = row[:, None] * inv_freq[None, :]
    ang_c = col[:, None] * inv_freq[None, :]
    ang = jnp.concatenate([ang_r, ang_r, ang_c, ang_c], axis=-1)
    cos, sin = jnp.cos(ang), jnp.sin(ang)
    even = ((np.arange(HEAD_DIM) // ROPE_BLK) % 2 == 0)[None, :]
    sa = jnp.where(even, -sin, 0.0)
    sb = jnp.where(even, 0.0, sin)
    tile2 = lambda t: jnp.concatenate([t, t], axis=-1)
    return (cos.T, sa.T, sb.T, tile2(cos), tile2(sa), tile2(sb))


def kernel(x, c, ctx, c_ctx, w_ada, b_ada, g_pre_mix, g_post_mix, g_pre_ffn, g_post_ffn,
           w_in, attn_sink, w_gate_fwd, b_gate_fwd, w_gate_bwd, b_gate_bwd, g_gla_norm,
           w_out, w_ffn_in, w_ffn_out):
    batch, seq, _ = x.shape
    n_ctx = ctx.shape[1]
    depth = w_ada.shape[0]
    assert depth == 1
    l = 0

    c8 = jnp.zeros((8, D), F32).at[0:batch].set(c).at[4].set(c_ctx)
    ada = _ada(c8, w_ada[l], b_ada[l][None, :])

    wi = w_in[l]
    v0 = NQ + NKV
    g0 = NQ + 2 * NKV
    z0 = g0 + GIN_W
    w_qv_t = jnp.concatenate([wi[:, 0:NQ], wi[:, v0:g0]], axis=1).T.astype(BF16)
    w_rest = jnp.concatenate(
        [wi[:, NQ:v0], wi[:, g0:z0], wi[:, z0:],
         jnp.zeros((D, Z_W - 2 * GATE_RANK), F32)], axis=1).astype(BF16)
    w_gate_p = jnp.zeros((Z_W, LA_W), F32)
    w_gate_p = w_gate_p.at[0:GATE_RANK, 0:GQ].set(w_gate_fwd[l])
    w_gate_p = w_gate_p.at[GATE_RANK:2 * GATE_RANK, GQ:].set(w_gate_bwd[l]).astype(BF16)
    b_gate_p = jnp.concatenate([b_gate_fwd[l], b_gate_bwd[l]])[None, :]

    tm = 512
    tables = _rope_tables(seq)
    x2d = x.reshape(batch * seq, D)
    g_pre = g_pre_mix[l][None, :]
    qt, k, vt, gin, lah, lal = _inproj(x2d, ada, g_pre, w_qv_t, w_rest, w_gate_p, b_gate_p,
                                      tables, batch=batch, tokens=seq, tm=tm)
    _, k_c, vt_c, gin_c, lah_c, lal_c = _inproj(
        ctx.reshape(batch * n_ctx, D), ada, g_pre, w_qv_t, w_rest, w_gate_p, b_gate_p, None,
        batch=batch, tokens=n_ctx, tm=n_ctx)

    att_t = _attn(attn_sink[l], qt, k, vt, k_c, vt_c, batch=batch, seq=seq, n_ctx=n_ctx)

    gn_tiled = jnp.tile(g_gla_norm[l], GLA_HEADS)[None, :]
    gla = _gla(gin, lah, lal, gin_c, lah_c, lal_c, gn_tiled, batch=batch, seq=seq, n_ctx=n_ctx)

    wo = w_out[l]
    out = _mix_ffn(att_t, gla.reshape(batch * seq, GV), x2d, ada,
                   g_post_mix[l][None, :], g_pre_ffn[l][None, :], g_post_ffn[l][None, :],
                   wo[:NQ].astype(BF16), wo[NQ:].astype(BF16),
                   w_ffn_in[l].astype(BF16), w_ffn_out[l].astype(BF16),
                   tm=tm, tiles_per_batch=seq // tm)
    return out.reshape(batch, seq, D)
```

```python
import functools

import jax
import jax.numpy as jnp
import numpy as np
from jax import lax
from jax.experimental import pallas as pl
from jax.experimental.pallas import tpu as pltpu

D = 1024
HEAD_DIM = 64
ATT_HEADS = 8
ATT_KV_HEADS = 2
GROUP = ATT_HEADS // ATT_KV_HEADS
WINDOW = 128
GRID_W = 64
ROPE_BASE = 10000.0
ROPE_BLK = HEAD_DIM // 4
GLA_HEADS = 8
GLA_DK = 32
GLA_DV = 64
CHUNK = 64
GATE_RANK = 16
GATE_TAU = 16.0
FFN_HIDDEN = 2816
NEG_INF = -1e30
EPS = 1e-6
LOG2E = 1.4426950408889634
Q_SCALE = LOG2E * HEAD_DIM ** -0.5
LA_SCALE = LOG2E / GATE_TAU

NQ = ATT_HEADS * HEAD_DIM
NKV = ATT_KV_HEADS * HEAD_DIM
GQ = GLA_HEADS * GLA_DK
GV = GLA_HEADS * GLA_DV
GIN_W = 2 * GQ + 2 * GV
Z_W = 128
LA_W = 2 * GQ
REST_W = GIN_W + Z_W

LANES = 128
VMEM_LIMIT = 56 * 1024 * 1024

BF16 = jnp.bfloat16
F32 = jnp.float32


def _dot(a, b):
    return jnp.dot(a, b, preferred_element_type=F32)


def _dot_nt(a, b):
    return lax.dot_general(a, b, (((1,), (1,)), ((), ())), preferred_element_type=F32)


def _dot_tn(a, b):
    return lax.dot_general(a, b, (((0,), (0,)), ((), ())), preferred_element_type=F32)


def _rms(x):
    return x * lax.rsqrt(jnp.mean(x * x, axis=-1, keepdims=True) + EPS)


def _silu(x):
    return x * (1.0 / (1.0 + jnp.exp(-x)))


def _ada_kernel(c_ref, w_ref, b_ref, o_ref):
    a = _silu(c_ref[...]).astype(BF16)
    o_ref[...] = _dot(a, w_ref[...].astype(BF16)) + b_ref[...]


def _ada(c8, w_ada, b_ada):
    n = w_ada.shape[1]
    bn = 1024
    return pl.pallas_call(
        _ada_kernel,
        out_shape=jax.ShapeDtypeStruct((8, n), F32),
        grid=(n // bn,),
        in_specs=[pl.BlockSpec((8, D), lambda j: (0, 0)),
                  pl.BlockSpec((D, bn), lambda j: (0, j)),
                  pl.BlockSpec((1, bn), lambda j: (0, j))],
        out_specs=pl.BlockSpec((8, bn), lambda j: (0, j)),
        compiler_params=pltpu.CompilerParams(dimension_semantics=("arbitrary",)),
        name="ada",
    )(c8, w_ada, b_ada)


def _inproj_kernel(*refs, tiles_per_batch, rope):
    if rope:
        (x_ref, ada_ref, g_ref, wqkv_ref, w_ref, wg_ref, bg_ref,
         cost_ref, sat_ref, sbt_ref,
         qt_ref, k_ref, vt_ref, gin_ref, lah_ref, lal_ref) = refs
        row = pl.program_id(0) // tiles_per_batch
    else:
        (x_ref, ada_ref, g_ref, wqkv_ref, w_ref, wg_ref, bg_ref,
         qt_ref, k_ref, vt_ref, gin_ref, lah_ref, lal_ref) = refs
        row = 4
    ada = ada_ref[pl.ds(row, 1), :]
    sh1 = ada[:, 0:D]
    sc1 = ada[:, D:2 * D]
    h = (_rms(x_ref[...]) * g_ref[...]) * (1.0 + sc1) + sh1
    hb = h.astype(BF16)

    qkv = _dot_nt(wqkv_ref[...], hb)

    def head_t(hd):
        xh = qkv[hd * HEAD_DIM:(hd + 1) * HEAD_DIM]
        if not rope:
            return xh
        up = jnp.concatenate([xh[ROPE_BLK:], xh[:ROPE_BLK]], axis=0)
        dn = jnp.concatenate([xh[HEAD_DIM - ROPE_BLK:], xh[:HEAD_DIM - ROPE_BLK]], axis=0)
        return xh * cost_ref[...] + up * sat_ref[...] + dn * sbt_ref[...]

    for hd in range(ATT_HEADS):
        qh = head_t(hd)
        if rope:
            qh = qh * Q_SCALE
        qt_ref[hd * HEAD_DIM:(hd + 1) * HEAD_DIM, :] = qh.astype(BF16)
    kt = jnp.concatenate([head_t(ATT_HEADS + hd) for hd in range(ATT_KV_HEADS)], axis=0)
    k_ref[...] = kt.T.astype(BF16)
    vt_ref[...] = qkv[NQ + NKV:NQ + 2 * NKV].astype(BF16)

    gin = _dot(hb, w_ref[:, 0:GIN_W])
    gin_ref[:, 0:GQ] = (gin[:, 0:GQ] * (GLA_DK ** -0.5)).astype(BF16)
    gin_ref[:, GQ:2 * GQ + GV] = gin[:, GQ:2 * GQ + GV].astype(BF16)
    gin_ref[:, 2 * GQ + GV:GIN_W] = _silu(gin[:, 2 * GQ + GV:GIN_W]).astype(BF16)

    z = _dot(hb, w_ref[:, GIN_W:REST_W])
    logits = _dot(z.astype(BF16), wg_ref[...]) + bg_ref[...]
    log_sig = jnp.minimum(logits, 0.0) - jnp.log(1.0 + jnp.exp(-jnp.abs(logits)))
    la = log_sig * LA_SCALE
    hi = la.astype(BF16)
    lah_ref[...] = hi
    lal_ref[...] = (la - hi.astype(F32)).astype(BF16)


def _inproj(x2d, ada, g_pre, w_qkv_t, w_rest, w_gate_p, b_gate_p, tables, *,
            batch, tokens, tm):
    n = x2d.shape[0]
    tpb = tokens // tm
    rope = tables is not None
    const = lambda i: (0, 0)
    in_specs = [pl.BlockSpec((tm, D), lambda i: (i, 0)),
                pl.BlockSpec((8, 6 * D), const),
                pl.BlockSpec((1, D), const),
                pl.BlockSpec((NQ + 2 * NKV, D), const),
                pl.BlockSpec((D, REST_W), const),
                pl.BlockSpec((Z_W, LA_W), const),
                pl.BlockSpec((1, LA_W), const)]
    args = [x2d, ada, g_pre, w_qkv_t, w_rest, w_gate_p, b_gate_p]
    if rope:
        in_specs += [pl.BlockSpec((HEAD_DIM, tm), lambda i: (0, i % tpb))] * 3
        args += list(tables)
    row_blk = lambda w: pl.BlockSpec((tm, w), lambda i: (i, 0))
    return pl.pallas_call(
        functools.partial(_inproj_kernel, tiles_per_batch=tpb, rope=rope),
        out_shape=(jax.ShapeDtypeStruct((batch, NQ, tokens), BF16),
                   jax.ShapeDtypeStruct((n, NKV), BF16),
                   jax.ShapeDtypeStruct((batch, NKV, tokens), BF16),
                   jax.ShapeDtypeStruct((n, GIN_W), BF16),
                   jax.ShapeDtypeStruct((n, LA_W), BF16),
                   jax.ShapeDtypeStruct((n, LA_W), BF16)),
        grid=(n // tm,),
        in_specs=in_specs,
        out_specs=(pl.BlockSpec((None, NQ, tm), lambda i: (i // tpb, 0, i % tpb)),
                   row_blk(NKV),
                   pl.BlockSpec((None, NKV, tm), lambda i: (i // tpb, 0, i % tpb)),
                   row_blk(GIN_W), row_blk(LA_W), row_blk(LA_W)),
        compiler_params=pltpu.CompilerParams(dimension_semantics=("arbitrary",),
                                             vmem_limit_bytes=VMEM_LIMIT),
        name="inproj_lat" if rope else "inproj_ctx",
    )(*args)


HEADS_PER_DOT = 2


ATT_SUB = 4
ONES_ROWS = 16


def _attn_kernel(sink_ref, qt_ref, kp_ref, kc_ref, kn_ref, kx_ref,
                 vp_ref, vc_ref, vn_ref, vx_ref, o_ref, *, n_steps):
    i = pl.program_id(1)
    bq = WINDOW
    nw = 3 * bq
    nq = HEADS_PER_DOT * bq
    c = lax.broadcasted_iota(jnp.int32, (nw, nq), 0)
    r = lax.broadcasted_iota(jnp.int32, (nw, nq), 1) % bq
    rel = c - bq - r
    in_win = (rel <= WINDOW) & (rel >= -WINDOW)
    lane = lax.broadcasted_iota(jnp.int32, (1, nq), 1)
    zeros = jnp.zeros((HEAD_DIM, nq), BF16)

    k_blocks = [kp_ref[...]] + [kc_ref[s * bq:(s + 1) * bq] for s in range(ATT_SUB)] + [kn_ref[...]]
    v_blocks = ([vp_ref[...]] + [vc_ref[:, s * bq:(s + 1) * bq] for s in range(ATT_SUB)]
                + [vn_ref[...]])
    kx = kx_ref[...]
    vx = vx_ref[...]

    subs = []
    for s in range(ATT_SUB):
        keep = in_win
        if s == 0:
            keep = keep & ((c >= bq) | (i > 0))
        if s == ATT_SUB - 1:
            keep = keep & ((c < 2 * bq) | (i < n_steps - 1))
        kcat = jnp.concatenate(k_blocks[s:s + 3] + [kx], axis=0)
        vtcat = jnp.concatenate(v_blocks[s:s + 3] + [vx], axis=1)
        ones = jnp.ones((ONES_ROWS, vtcat.shape[1]), BF16)
        vaug = [jnp.concatenate([vtcat[kvh * HEAD_DIM:(kvh + 1) * HEAD_DIM], ones], axis=0)
                for kvh in range(ATT_KV_HEADS)]
        subs.append((keep, kcat, vaug))

    tiles = [(s, kvh, kvh * GROUP + pr * HEADS_PER_DOT)
             for s in range(ATT_SUB) for kvh in range(ATT_KV_HEADS)
             for pr in range(GROUP // HEADS_PER_DOT)]

    scores = []
    for s, kvh, h0 in tiles:
        qt = jnp.concatenate(
            [qt_ref[(h0 + u) * HEAD_DIM:(h0 + u + 1) * HEAD_DIM, s * bq:(s + 1) * bq]
             for u in range(HEADS_PER_DOT)], axis=1)
        rhs = jnp.concatenate([qt, zeros] if kvh == 0 else [zeros, qt], axis=0)
        scores.append(_dot(subs[s][1], rhs))

    probs = []
    for (s, kvh, h0), st in zip(tiles, scores):
        keep = subs[s][0]
        parts = [jnp.where(keep[0:bq], st[0:bq], NEG_INF), st[bq:2 * bq],
                 jnp.where(keep[2 * bq:nw], st[2 * bq:nw], NEG_INF), st[nw:]]
        sink = jnp.where(lane < bq, sink_ref[h0], sink_ref[h0 + 1]) * LOG2E
        m = sink
        for p in parts:
            m = jnp.maximum(m, jnp.max(p, axis=0, keepdims=True))
        et = jnp.concatenate([jnp.exp2(p - m).astype(BF16) for p in parts], axis=0)
        probs.append((et, jnp.exp2(sink - m)))

    for (s, kvh, h0), (et, e_sink) in zip(tiles, probs):
        ot = _dot(subs[s][2][kvh], et)
        denom = ot[HEAD_DIM:HEAD_DIM + 1] + e_sink
        on = ot[0:HEAD_DIM] * (1.0 / denom)
        for u in range(HEADS_PER_DOT):
            o_ref[(h0 + u) * HEAD_DIM:(h0 + u + 1) * HEAD_DIM, s * bq:(s + 1) * bq] = (
                on[:, u * bq:(u + 1) * bq].astype(BF16))


def _attn(sink, qt, k, vt, k_ctx, vt_ctx, *, batch, seq, n_ctx):
    bq = WINDOW
    bs = ATT_SUB * bq
    nb = seq // bq
    ns = seq // bs
    k3 = k.reshape(batch, seq, NKV)
    kx3 = k_ctx.reshape(batch, n_ctx, NKV)
    prev = lambda i: jnp.maximum(ATT_SUB * i - 1, 0)
    nxt = lambda i: jnp.minimum(ATT_SUB * (i + 1), nb - 1)
    return pl.pallas_call(
        functools.partial(_attn_kernel, n_steps=ns),
        out_shape=jax.ShapeDtypeStruct((batch, NQ, seq), BF16),
        grid=(batch, ns),
        in_specs=[
            pl.BlockSpec(memory_space=pltpu.SMEM),
            pl.BlockSpec((None, NQ, bs), lambda b, i: (b, 0, i)),
            pl.BlockSpec((None, bq, NKV), lambda b, i: (b, prev(i), 0)),
            pl.BlockSpec((None, bs, NKV), lambda b, i: (b, i, 0)),
            pl.BlockSpec((None, bq, NKV), lambda b, i: (b, nxt(i), 0)),
            pl.BlockSpec((None, n_ctx, NKV), lambda b, i: (b, 0, 0)),
            pl.BlockSpec((None, NKV, bq), lambda b, i: (b, 0, prev(i))),
            pl.BlockSpec((None, NKV, bs), lambda b, i: (b, 0, i)),
            pl.BlockSpec((None, NKV, bq), lambda b, i: (b, 0, nxt(i))),
            pl.BlockSpec((None, NKV, n_ctx), lambda b, i: (b, 0, 0)),
        ],
        out_specs=pl.BlockSpec((None, NQ, bs), lambda b, i: (b, 0, i)),
        compiler_params=pltpu.CompilerParams(dimension_semantics=("arbitrary", "arbitrary"),
                                             vmem_limit_bytes=VMEM_LIMIT),
        name="attn",
    )(sink, qt, k3, k3, k3, kx3, vt, vt, vt, vt_ctx)


GRP_K = 4 * GLA_DK
GRP_V = 4 * GLA_DV
N_GRP = GLA_HEADS // 4
GLA_TB = 512
CUM_BLK = 256


def _gla_constants():
    t = CUM_BLK
    r = np.arange(t)[:, None]
    c = np.arange(t)[None, :]
    same_chunk = (r // CHUNK) == (c // CHUNK)
    tri_l = (same_chunk & (c <= r)).astype(np.float32)
    tri_u = (same_chunk & (c >= r)).astype(np.float32)
    rk = np.arange(4 * CHUNK)[:, None] // CHUNK
    kmask = (rk == (np.arange(GRP_K)[None, :] // GLA_DK)).astype(np.float32)
    vmask = (rk == (np.arange(GRP_V)[None, :] // GLA_DV)).astype(np.float32)
    smask = ((np.arange(GRP_K)[:, None] // GLA_DK)
             == (np.arange(GRP_V)[None, :] // GLA_DV)).astype(np.float32)
    ai = np.arange(CHUNK)[:, None]
    aj = np.arange(4 * CHUNK)[None, :] % CHUNK
    causal = (aj <= ai).astype(np.float32)
    anti = (aj >= ai).astype(np.float32)
    hmean = ((np.arange(GRP_V)[:, None] // GLA_DV)
             == (np.arange(GRP_V)[None, :] // GLA_DV)).astype(np.float32) / GLA_DV
    return (jnp.asarray(tri_l, BF16), jnp.asarray(tri_u, BF16), jnp.asarray(kmask, BF16),
            jnp.asarray(vmask, BF16), jnp.asarray(smask, F32), jnp.asarray(causal, F32),
            jnp.asarray(anti, F32), jnp.asarray(hmean, BF16))


def _gk(g):
    return slice(g * GRP_K, (g + 1) * GRP_K)


def _gv(g):
    return slice(g * GRP_V, (g + 1) * GRP_V)


def _chunk_slices(t):
    return [slice(ci * CHUNK, (ci + 1) * CHUNK) for ci in range(t // CHUNK)]


def _cum_and_totals(la_hi, la_lo, tri, *, reverse):
    parts = [slice(p, p + CUM_BLK) for p in range(0, la_hi.shape[0], CUM_BLK)]
    cum = jnp.concatenate([_dot(tri, la_hi[p]) + _dot(tri, la_lo[p]) for p in parts], axis=0)
    edge = 0 if reverse else CHUNK - 1
    tots = [cum[sl.start + edge:sl.start + edge + 1] for sl in _chunk_slices(cum.shape[0])]
    return cum, tots


def _bcast_chunks(rows, width):
    return jnp.concatenate([jnp.broadcast_to(r, (CHUNK, width)) for r in rows], axis=0)


def _decay(x, log2_factor):
    return x * jnp.exp2(log2_factor).astype(BF16)


def _state_increments(kd, v, smask):
    return [[_dot_tn(kd[sl, _gk(g)], v[sl, _gv(g)]) * smask for g in range(N_GRP)]
            for sl in _chunk_slices(kd.shape[0])]


def _decay_columns(tots):
    width = tots[0].shape[1]
    row = lax.broadcasted_iota(jnp.int32, (8, width), 0)
    rows = jnp.zeros((8, width), F32)
    for ci, t in enumerate(tots):
        rows = jnp.where(row == ci, jnp.broadcast_to(t, (8, width)), rows)
    return jnp.exp2(rows).T


def _scan_states(st_ref, k, v, la_hi, la_lo, tri, smask, *, reverse,
                 store_ref=None, store_base=None):
    nch = k.shape[0] // CHUNK
    cum, tots = _cum_and_totals(la_hi, la_lo, tri, reverse=reverse)
    kd = _decay(k, _bcast_chunks(tots, GQ) - cum)
    ds = _state_increments(kd, v, smask)
    dcol = _decay_columns(tots)
    state = [st_ref[g] for g in range(N_GRP)]
    for ci in (range(nch - 1, -1, -1) if reverse else range(nch)):
        for g in range(N_GRP):
            if store_ref is not None:
                store_ref[store_base + ci, g] = state[g].astype(BF16)
            state[g] = state[g] * dcol[_gk(g), ci:ci + 1] + ds[ci][g]
    for g in range(N_GRP):
        st_ref[g] = state[g]


def _block_outputs(q, k, v, la_f, la_b, st_ref, bwd_state, consts):
    tri_l, tri_u, kmask, vmask, smask, causal, anti = consts
    chunks = _chunk_slices(q.shape[0])

    cum_f, tots_f = _cum_and_totals(la_f[0], la_f[1], tri_l, reverse=False)
    cum_b, tots_b = _cum_and_totals(la_b[0], la_b[1], tri_u, reverse=True)

    qd_f = _decay(q, cum_f)
    qd_b = _decay(q, cum_b)
    ki_f = _decay(k, -cum_f)
    ki_b = _decay(k, -cum_b)
    kd_f = _decay(k, _bcast_chunks(tots_f, GQ) - cum_f)
    dcol_f = _decay_columns(tots_f)

    def scores(qd, ki, sl, g):
        ki_bd = jnp.concatenate([ki[sl, _gk(g)]] * 4, axis=0) * kmask
        return _dot_nt(qd[sl, _gk(g)], ki_bd)

    a_f = [[scores(qd_f, ki_f, sl, g) for g in range(N_GRP)] for sl in chunks]
    a_b = [[scores(qd_b, ki_b, sl, g) for g in range(N_GRP)] for sl in chunks]
    ds = _state_increments(kd_f, v, smask)
    a = [[(af[g] * causal + ab[g] * anti).astype(BF16) for g in range(N_GRP)]
         for af, ab in zip(a_f, a_b)]

    state = [st_ref[g] for g in range(N_GRP)]
    st_in = []
    for ci in range(len(chunks)):
        st_in.append([s.astype(BF16) for s in state])
        state = [state[g] * dcol_f[_gk(g), ci:ci + 1] + ds[ci][g] for g in range(N_GRP)]
    for g in range(N_GRP):
        st_ref[g] = state[g]

    outs = []
    for ci, sl in enumerate(chunks):
        parts = []
        for g in range(N_GRP):
            v_bd = jnp.concatenate([v[sl, _gv(g)]] * 4, axis=0) * vmask
            intra = _dot(a[ci][g], v_bd)
            qd2 = jnp.concatenate([qd_f[sl, _gk(g)], qd_b[sl, _gk(g)]], axis=1)
            st2 = jnp.concatenate([st_in[ci][g], bwd_state(ci, g)], axis=0)
            parts.append(intra + _dot(qd2, st2))
        outs.append(jnp.concatenate(parts, axis=1))
    return jnp.concatenate(outs, axis=0)


def _gla_kernel(gin_ref, lah_ref, lal_ref, ginc_ref, lahc_ref, lalc_ref, gn_ref,
                tril_ref, triu_ref, kmask_ref, vmask_ref, smask_ref, causal_ref, anti_ref,
                hmean_ref, o_ref, st_ref, sb_ref, *, n_blocks):
    phase = pl.program_id(1)
    j = pl.program_id(2)
    ch_per_blk = GLA_TB // CHUNK
    fwd = slice(0, GQ)
    bwd = slice(GQ, 2 * GQ)
    kcols = slice(GQ, 2 * GQ)
    vcols = slice(2 * GQ, 2 * GQ + GV)

    def ctx_state(cols, tri_ref, reverse):
        st_ref[...] = jnp.zeros_like(st_ref)
        _scan_states(st_ref, ginc_ref[:, kcols], ginc_ref[:, vcols],
                     lahc_ref[:, cols], lalc_ref[:, cols], tri_ref[...], smask_ref[...],
                     reverse=reverse)

    @pl.when(phase == 0)
    def _backward_states():
        @pl.when(j == 0)
        def _():
            ctx_state(bwd, triu_ref, True)

        blk = n_blocks - 1 - j
        _scan_states(st_ref, gin_ref[:, kcols], gin_ref[:, vcols],
                     lah_ref[:, bwd], lal_ref[:, bwd], triu_ref[...], smask_ref[...],
                     reverse=True, store_ref=sb_ref, store_base=blk * ch_per_blk)

    @pl.when(phase == 1)
    def _outputs():
        @pl.when(j == 0)
        def _():
            ctx_state(fwd, tril_ref, False)

        base = j * ch_per_blk
        consts = (tril_ref[...], triu_ref[...], kmask_ref[...], vmask_ref[...],
                  smask_ref[...], causal_ref[...], anti_ref[...])
        o = _block_outputs(gin_ref[:, 0:GQ], gin_ref[:, kcols], gin_ref[:, vcols],
                           (lah_ref[:, fwd], lal_ref[:, fwd]),
                           (lah_ref[:, bwd], lal_ref[:, bwd]), st_ref,
                           lambda ci, g: sb_ref[base + ci, g], consts)
        o2 = (o * o).astype(BF16)
        hmean = hmean_ref[...]
        ms = jnp.concatenate([_dot(o2[:, _gv(g)], hmean) for g in range(N_GRP)], axis=1)
        y = o * lax.rsqrt(ms + EPS) * gn_ref[...]
        o_ref[...] = (y * gin_ref[:, 2 * GQ + GV:GIN_W].astype(F32)).astype(BF16)


def _gla(gin, la_hi, la_lo, gin_ctx, la_hi_ctx, la_lo_ctx, gn_tiled, *, batch, seq, n_ctx):
    assert n_ctx % CUM_BLK == 0 and n_ctx // CHUNK <= 8 and GLA_TB // CHUNK <= 8
    tb = GLA_TB
    nb = seq // tb
    nch = seq // CHUNK
    r3 = lambda a, t: a.reshape(batch, t, a.shape[-1])
    consts = _gla_constants()

    def blk_map(b, p, j):
        return (b, jnp.where(p == 0, nb - 1 - j, j), 0)

    ctx_map = lambda b, p, j: (b, 0, 0)
    const_map = lambda b, p, j: (0, 0)
    return pl.pallas_call(
        functools.partial(_gla_kernel, n_blocks=nb),
        out_shape=jax.ShapeDtypeStruct((batch, seq, GV), BF16),
        grid=(batch, 2, nb),
        in_specs=[pl.BlockSpec((None, tb, GIN_W), blk_map),
                  pl.BlockSpec((None, tb, LA_W), blk_map),
                  pl.BlockSpec((None, tb, LA_W), blk_map),
                  pl.BlockSpec((None, n_ctx, GIN_W), ctx_map),
                  pl.BlockSpec((None, n_ctx, LA_W), ctx_map),
                  pl.BlockSpec((None, n_ctx, LA_W), ctx_map),
                  pl.BlockSpec((1, GV), const_map)]
                 + [pl.BlockSpec(cst.shape, const_map) for cst in consts],
        out_specs=pl.BlockSpec((None, tb, GV), lambda b, p, j: (b, jnp.where(p == 0, 0, j), 0)),
        scratch_shapes=[pltpu.VMEM((N_GRP, GRP_K, GRP_V), F32),
                        pltpu.VMEM((nch, N_GRP, GRP_K, GRP_V), BF16)],
        compiler_params=pltpu.CompilerParams(
            dimension_semantics=("arbitrary", "arbitrary", "arbitrary"),
            vmem_limit_bytes=VMEM_LIMIT),
        name="gla",
    )(r3(gin, seq), r3(la_hi, seq), r3(la_lo, seq), r3(gin_ctx, n_ctx), r3(la_hi_ctx, n_ctx),
      r3(la_lo_ctx, n_ctx), gn_tiled, *consts)


FFN_CHUNK = 256


def _mix_ffn_kernel(att_ref, gla_ref, x_ref, ada_ref, gpm_ref, gpf_ref, gqf_ref,
                    woa_ref, wog_ref, wfi_ref, wfo_ref, o_ref, acc_ref, *, tiles_per_batch):
    row = pl.program_id(0) // tiles_per_batch
    ada = ada_ref[pl.ds(row, 1), :]
    gt1 = ada[:, 2 * D:3 * D]
    sh2 = ada[:, 3 * D:4 * D]
    sc2 = ada[:, 4 * D:5 * D]
    gt2 = ada[:, 5 * D:6 * D]

    y = _dot_tn(att_ref[...], woa_ref[...]) + _dot(gla_ref[...], wog_ref[...])
    x1 = x_ref[...] + gt1 * (_rms(y) * gpm_ref[...])
    h = ((_rms(x1) * gpf_ref[...]) * (1.0 + sc2) + sh2).astype(BF16)

    n_chunks = FFN_HIDDEN // FFN_CHUNK
    for ci in range(n_chunks):
        c0 = ci * FFN_CHUNK
        g = _dot(h, wfi_ref[:, c0:c0 + FFN_CHUNK])
        u = _dot(h, wfi_ref[:, FFN_HIDDEN + c0:FFN_HIDDEN + c0 + FFN_CHUNK])
        a = (_silu(g) * u).astype(BF16)
        part = _dot(a, wfo_ref[c0:c0 + FFN_CHUNK, :])
        if ci == 0:
            acc_ref[...] = part
        else:
            acc_ref[...] += part
    o_ref[...] = x1 + gt2 * (_rms(acc_ref[...]) * gqf_ref[...])


def _mix_ffn(att_t, gla, x2d, ada, g_post_mix, g_pre_ffn, g_post_ffn,
             w_out_a, w_out_g, w_ffn_in, w_ffn_out, *, tm, tiles_per_batch):
    n = x2d.shape[0]
    tpb = tiles_per_batch
    const = lambda i: (0, 0)
    resident = functools.partial(pl.BlockSpec, index_map=const, pipeline_mode=pl.Buffered(1))
    return pl.pallas_call(
        functools.partial(_mix_ffn_kernel, tiles_per_batch=tpb),
        out_shape=jax.ShapeDtypeStruct((n, D), F32),
        grid=(n // tm,),
        in_specs=[pl.BlockSpec((None, NQ, tm), lambda i: (i // tpb, 0, i % tpb)),
                  pl.BlockSpec((tm, GV), lambda i: (i, 0)),
                  pl.BlockSpec((tm, D), lambda i: (i, 0)),
                  pl.BlockSpec((8, 6 * D), const),
                  pl.BlockSpec((1, D), const),
                  pl.BlockSpec((1, D), const),
                  pl.BlockSpec((1, D), const),
                  resident((NQ, D)),
                  resident((GV, D)),
                  resident((D, 2 * FFN_HIDDEN)),
                  resident((FFN_HIDDEN, D))],
        out_specs=pl.BlockSpec((tm, D), lambda i: (i, 0)),
        scratch_shapes=[pltpu.VMEM((tm, D), F32)],
        compiler_params=pltpu.CompilerParams(dimension_semantics=("arbitrary",),
                                             vmem_limit_bytes=VMEM_LIMIT),
        name="mix_ffn",
    )(att_t, gla, x2d, ada, g_post_mix, g_pre_ffn, g_post_ffn,
      w_out_a, w_out_g, w_ffn_in, w_ffn_out)


def _rope_tables(seq):
    half = HEAD_DIM // 2
    inv_freq = ROPE_BASE ** (-jnp.arange(0, half, 2, dtype=F32) / half)
    dim = np.arange(HEAD_DIM)
    freq = jnp.tile(inv_freq, HEAD_DIM // ROPE_BLK)[:, None]
    pos = jnp.arange(seq)[None, :]
    row = (pos // GRID_W).astype(F32)
    col = (pos % GRID_W).astype(F32)
    ang = jnp.where((dim < half)[:, None], row, col) * freq
    cos, sin = jnp.cos(ang), jnp.sin(ang)
    even = ((dim // ROPE_BLK) % 2 == 0)[:, None]
    sa = jnp.where(even, -sin, 0.0)
    sb = jnp.where(even, 0.0, sin)
    return cos, sa, sb


def kernel(x, c, ctx, c_ctx, w_ada, b_ada, g_pre_mix, g_post_mix, g_pre_ffn, g_post_ffn,
           w_in, attn_sink, w_gate_fwd, b_gate_fwd, w_gate_bwd, b_gate_bwd, g_gla_norm,
           w_out, w_ffn_in, w_ffn_out):
    batch, seq, _ = x.shape
    n_ctx = ctx.shape[1]
    depth = w_ada.shape[0]
    assert depth == 1
    l = 0

    c8 = jnp.zeros((8, D), F32).at[0:batch].set(c).at[4].set(c_ctx)
    ada = _ada(c8, w_ada[l], b_ada[l][None, :])

    wi = w_in[l]
    g0 = NQ + 2 * NKV
    w_qkv_t = wi[:, 0:g0].T.astype(BF16)
    w_rest = jnp.concatenate(
        [wi[:, g0:], jnp.zeros((D, Z_W - 2 * GATE_RANK), F32)], axis=1).astype(BF16)
    w_gate_p = jnp.zeros((Z_W, LA_W), F32)
    w_gate_p = w_gate_p.at[0:GATE_RANK, 0:GQ].set(w_gate_fwd[l])
    w_gate_p = w_gate_p.at[GATE_RANK:2 * GATE_RANK, GQ:].set(w_gate_bwd[l]).astype(BF16)
    b_gate_p = jnp.concatenate([b_gate_fwd[l], b_gate_bwd[l]])[None, :]

    tm = 512
    tables = _rope_tables(seq)
    x2d = x.reshape(batch * seq, D)
    g_pre = g_pre_mix[l][None, :]
    qt, k, vt, gin, lah, lal = _inproj(x2d, ada, g_pre, w_qkv_t, w_rest, w_gate_p, b_gate_p,
                                      tables, batch=batch, tokens=seq, tm=tm)
    _, k_c, vt_c, gin_c, lah_c, lal_c = _inproj(
        ctx.reshape(batch * n_ctx, D), ada, g_pre, w_qkv_t, w_rest, w_gate_p, b_gate_p, None,
        batch=batch, tokens=n_ctx, tm=n_ctx)

    att_t = _attn(attn_sink[l], qt, k, vt, k_c, vt_c, batch=batch, seq=seq, n_ctx=n_ctx)

    gn_tiled = jnp.tile(g_gla_norm[l], GLA_HEADS)[None, :]
    gla = _gla(gin, lah, lal, gin_c, lah_c, lal_c, gn_tiled, batch=batch, seq=seq, n_ctx=n_ctx)

    wo = w_out[l]
    out = _mix_ffn(att_t, gla.reshape(batch * seq, GV), x2d, ada,
                   g_post_mix[l][None, :], g_pre_ffn[l][None, :], g_post_ffn[l][None, :],
                   wo[:NQ].astype(BF16), wo[NQ:].astype(BF16),
                   w_ffn_in[l].astype(BF16), w_ffn_out[l].astype(BF16),
                   tm=tm, tiles_per_batch=seq // tm)
    return out.reshape(batch, seq, D)
```

```python
import functools

import jax
import jax.numpy as jnp
import numpy as np
from jax import lax
from jax.experimental import pallas as pl
from jax.experimental.pallas import tpu as pltpu

D = 1024
HEAD_DIM = 64
ATT_HEADS = 8
ATT_KV_HEADS = 2
GROUP = ATT_HEADS // ATT_KV_HEADS
WINDOW = 128
GRID_W = 64
ROPE_BASE = 10000.0
ROPE_BLK = HEAD_DIM // 4
GLA_HEADS = 8
GLA_DK = 32
GLA_DV = 64
CHUNK = 64
GATE_RANK = 16
GATE_TAU = 16.0
FFN_HIDDEN = 2816
NEG_INF = -1e30
EPS = 1e-6
LOG2E = 1.4426950408889634
Q_SCALE = LOG2E * HEAD_DIM ** -0.5
LA_SCALE = LOG2E / GATE_TAU

NQ = ATT_HEADS * HEAD_DIM
NKV = ATT_KV_HEADS * HEAD_DIM
GQ = GLA_HEADS * GLA_DK
GV = GLA_HEADS * GLA_DV
GIN_W = 2 * GQ + 2 * GV
Z_W = 128
LA_W = 2 * GQ
REST_W = GIN_W + Z_W

LANES = 128
VMEM_LIMIT = 56 * 1024 * 1024

BF16 = jnp.bfloat16
F32 = jnp.float32


def _dot(a, b):
    return jnp.dot(a, b, preferred_element_type=F32)


def _dot_nt(a, b):
    return lax.dot_general(a, b, (((1,), (1,)), ((), ())), preferred_element_type=F32)


def _dot_tn(a, b):
    return lax.dot_general(a, b, (((0,), (0,)), ((), ())), preferred_element_type=F32)


def _rms(x):
    return x * lax.rsqrt(jnp.mean(x * x, axis=-1, keepdims=True) + EPS)


def _silu(x):
    return x * (1.0 / (1.0 + jnp.exp(-x)))


def _ada_kernel(c_ref, w_ref, b_ref, o_ref):
    a = _silu(c_ref[...]).astype(BF16)
    o_ref[...] = _dot(a, w_ref[...].astype(BF16)) + b_ref[...]


def _ada(c8, w_ada, b_ada):
    n = w_ada.shape[1]
    bn = 1024
    return pl.pallas_call(
        _ada_kernel,
        out_shape=jax.ShapeDtypeStruct((8, n), F32),
        grid=(n // bn,),
        in_specs=[pl.BlockSpec((8, D), lambda j: (0, 0)),
                  pl.BlockSpec((D, bn), lambda j: (0, j)),
                  pl.BlockSpec((1, bn), lambda j: (0, j))],
        out_specs=pl.BlockSpec((8, bn), lambda j: (0, j)),
        compiler_params=pltpu.CompilerParams(dimension_semantics=("arbitrary",)),
        name="ada",
    )(c8, w_ada, b_ada)


def _inproj_kernel(*refs, tiles_per_batch, rope):
    if rope:
        (x_ref, ada_ref, g_ref, wqkv_ref, w_ref, wg_ref, bg_ref,
         cost_ref, sat_ref, sbt_ref,
         qt_ref, k_ref, vt_ref, gin_ref, lah_ref, lal_ref) = refs
        row = pl.program_id(0) // tiles_per_batch
    else:
        (x_ref, ada_ref, g_ref, wqkv_ref, w_ref, wg_ref, bg_ref,
         qt_ref, k_ref, vt_ref, gin_ref, lah_ref, lal_ref) = refs
        row = 4
    ada = ada_ref[pl.ds(row, 1), :]
    sh1 = ada[:, 0:D]
    sc1 = ada[:, D:2 * D]
    h = (_rms(x_ref[...]) * g_ref[...]) * (1.0 + sc1) + sh1
    hb = h.astype(BF16)

    qkv = _dot_nt(wqkv_ref[...], hb)

    def head_t(hd):
        xh = qkv[hd * HEAD_DIM:(hd + 1) * HEAD_DIM]
        if not rope:
            return xh
        up = jnp.concatenate([xh[ROPE_BLK:], xh[:ROPE_BLK]], axis=0)
        dn = jnp.concatenate([xh[HEAD_DIM - ROPE_BLK:], xh[:HEAD_DIM - ROPE_BLK]], axis=0)
        return xh * cost_ref[...] + up * sat_ref[...] + dn * sbt_ref[...]

    for hd in range(ATT_HEADS):
        qh = head_t(hd)
        if rope:
            qh = qh * Q_SCALE
        qt_ref[hd * HEAD_DIM:(hd + 1) * HEAD_DIM, :] = qh.astype(BF16)
    kt = jnp.concatenate([head_t(ATT_HEADS + hd) for hd in range(ATT_KV_HEADS)], axis=0)
    k_ref[...] = kt.T.astype(BF16)
    vt_ref[...] = qkv[NQ + NKV:NQ + 2 * NKV].astype(BF16)

    gin = _dot(hb, w_ref[:, 0:GIN_W])
    gin_ref[:, 0:GQ] = (gin[:, 0:GQ] * (GLA_DK ** -0.5)).astype(BF16)
    gin_ref[:, GQ:2 * GQ + GV] = gin[:, GQ:2 * GQ + GV].astype(BF16)
    gin_ref[:, 2 * GQ + GV:GIN_W] = _silu(gin[:, 2 * GQ + GV:GIN_W]).astype(BF16)

    z = _dot(hb, w_ref[:, GIN_W:REST_W])
    logits = _dot(z.astype(BF16), wg_ref[...]) + bg_ref[...]
    log_sig = jnp.minimum(logits, 0.0) - jnp.log(1.0 + jnp.exp(-jnp.abs(logits)))
    la = log_sig * LA_SCALE
    hi = la.astype(BF16)
    lah_ref[...] = hi
    lal_ref[...] = (la - hi.astype(F32)).astype(BF16)


def _inproj(x2d, ada, g_pre, w_qkv_t, w_rest, w_gate_p, b_gate_p, tables, *,
            batch, tokens, tm):
    n = x2d.shape[0]
    tpb = tokens // tm
    rope = tables is not None
    const = lambda i: (0, 0)
    in_specs = [pl.BlockSpec((tm, D), lambda i: (i, 0)),
                pl.BlockSpec((8, 6 * D), const),
                pl.BlockSpec((1, D), const),
                pl.BlockSpec((NQ + 2 * NKV, D), const),
                pl.BlockSpec((D, REST_W), const),
                pl.BlockSpec((Z_W, LA_W), const),
                pl.BlockSpec((1, LA_W), const)]
    args = [x2d, ada, g_pre, w_qkv_t, w_rest, w_gate_p, b_gate_p]
    if rope:
        in_specs += [pl.BlockSpec((HEAD_DIM, tm), lambda i: (0, i % tpb))] * 3
        args += list(tables)
    row_blk = lambda w: pl.BlockSpec((tm, w), lambda i: (i, 0))
    return pl.pallas_call(
        functools.partial(_inproj_kernel, tiles_per_batch=tpb, rope=rope),
        out_shape=(jax.ShapeDtypeStruct((batch, NQ, tokens), BF16),
                   jax.ShapeDtypeStruct((n, NKV), BF16),
                   jax.ShapeDtypeStruct((batch, NKV, tokens), BF16),
                   jax.ShapeDtypeStruct((n, GIN_W), BF16),
                   jax.ShapeDtypeStruct((n, LA_W), BF16),
                   jax.ShapeDtypeStruct((n, LA_W), BF16)),
        grid=(n // tm,),
        in_specs=in_specs,
        out_specs=(pl.BlockSpec((None, NQ, tm), lambda i: (i // tpb, 0, i % tpb)),
                   row_blk(NKV),
                   pl.BlockSpec((None, NKV, tm), lambda i: (i // tpb, 0, i % tpb)),
                   row_blk(GIN_W), row_blk(LA_W), row_blk(LA_W)),
        compiler_params=pltpu.CompilerParams(dimension_semantics=("arbitrary",),
                                             vmem_limit_bytes=VMEM_LIMIT),
        name="inproj_lat" if rope else "inproj_ctx",
    )(*args)


HEADS_PER_DOT = 2


ATT_SUB = 4
ONES_ROWS = 16


def _attn_kernel(sink_ref, qt_ref, kp_ref, kc_ref, kn_ref, kx_ref,
                 vp_ref, vc_ref, vn_ref, vx_ref, o_ref, *, n_steps):
    i = pl.program_id(1)
    bq = WINDOW
    nw = 3 * bq
    nq = HEADS_PER_DOT * bq
    c = lax.broadcasted_iota(jnp.int32, (nw, nq), 0)
    r = lax.broadcasted_iota(jnp.int32, (nw, nq), 1) % bq
    rel = c - bq - r
    in_win = (rel <= WINDOW) & (rel >= -WINDOW)
    lane = lax.broadcasted_iota(jnp.int32, (1, nq), 1)
    zeros = jnp.zeros((HEAD_DIM, nq), BF16)

    k_blocks = [kp_ref[...]] + [kc_ref[s * bq:(s + 1) * bq] for s in range(ATT_SUB)] + [kn_ref[...]]
    v_blocks = ([vp_ref[...]] + [vc_ref[:, s * bq:(s + 1) * bq] for s in range(ATT_SUB)]
                + [vn_ref[...]])
    kx = kx_ref[...]
    vx = vx_ref[...]

    subs = []
    for s in range(ATT_SUB):
        keep = in_win
        if s == 0:
            keep = keep & ((c >= bq) | (i > 0))
        if s == ATT_SUB - 1:
            keep = keep & ((c < 2 * bq) | (i < n_steps - 1))
        kcat = jnp.concatenate(k_blocks[s:s + 3] + [kx], axis=0)
        vtcat = jnp.concatenate(v_blocks[s:s + 3] + [vx], axis=1)
        ones = jnp.ones((ONES_ROWS, vtcat.shape[1]), BF16)
        vaug = [jnp.concatenate([vtcat[kvh * HEAD_DIM:(kvh + 1) * HEAD_DIM], ones], axis=0)
                for kvh in range(ATT_KV_HEADS)]
        subs.append((keep, kcat, vaug))

    tiles = [(s, kvh, kvh * GROUP + pr * HEADS_PER_DOT)
             for s in range(ATT_SUB) for kvh in range(ATT_KV_HEADS)
             for pr in range(GROUP // HEADS_PER_DOT)]

    scores = []
    for s, kvh, h0 in tiles:
        qt = jnp.concatenate(
            [qt_ref[(h0 + u) * HEAD_DIM:(h0 + u + 1) * HEAD_DIM, s * bq:(s + 1) * bq]
             for u in range(HEADS_PER_DOT)], axis=1)
        rhs = jnp.concatenate([qt, zeros] if kvh == 0 else [zeros, qt], axis=0)
        scores.append(_dot(subs[s][1], rhs))

    probs = []
    for (s, kvh, h0), st in zip(tiles, scores):
        keep = subs[s][0]
        parts = [jnp.where(keep[0:bq], st[0:bq], NEG_INF), st[bq:2 * bq],
                 jnp.where(keep[2 * bq:nw], st[2 * bq:nw], NEG_INF), st[nw:]]
        sink = jnp.where(lane < bq, sink_ref[h0], sink_ref[h0 + 1]) * LOG2E
        m = sink
        for p in parts:
            m = jnp.maximum(m, jnp.max(p, axis=0, keepdims=True))
        et = jnp.concatenate([jnp.exp2(p - m).astype(BF16) for p in parts], axis=0)
        probs.append((et, jnp.exp2(sink - m)))

    for (s, kvh, h0), (et, e_sink) in zip(tiles, probs):
        ot = _dot(subs[s][2][kvh], et)
        denom = ot[HEAD_DIM:HEAD_DIM + 1] + e_sink
        on = ot[0:HEAD_DIM] * (1.0 / denom)
        for u in range(HEADS_PER_DOT):
            o_ref[(h0 + u) * HEAD_DIM:(h0 + u + 1) * HEAD_DIM, s * bq:(s + 1) * bq] = (
                on[:, u * bq:(u + 1) * bq].astype(BF16))


def _attn(sink, qt, k, vt, k_ctx, vt_ctx, *, batch, seq, n_ctx):
    bq = WINDOW
    bs = ATT_SUB * bq
    nb = seq // bq
    ns = seq // bs
    k3 = k.reshape(batch, seq, NKV)
    kx3 = k_ctx.reshape(batch, n_ctx, NKV)
    prev = lambda i: jnp.maximum(ATT_SUB * i - 1, 0)
    nxt = lambda i: jnp.minimum(ATT_SUB * (i + 1), nb - 1)
    return pl.pallas_call(
        functools.partial(_attn_kernel, n_steps=ns),
        out_shape=jax.ShapeDtypeStruct((batch, NQ, seq), BF16),
        grid=(batch, ns),
        in_specs=[
            pl.BlockSpec(memory_space=pltpu.SMEM),
            pl.BlockSpec((None, NQ, bs), lambda b, i: (b, 0, i)),
            pl.BlockSpec((None, bq, NKV), lambda b, i: (b, prev(i), 0)),
            pl.BlockSpec((None, bs, NKV), lambda b, i: (b, i, 0)),
            pl.BlockSpec((None, bq, NKV), lambda b, i: (b, nxt(i), 0)),
            pl.BlockSpec((None, n_ctx, NKV), lambda b, i: (b, 0, 0)),
            pl.BlockSpec((None, NKV, bq), lambda b, i: (b, 0, prev(i))),
            pl.BlockSpec((None, NKV, bs), lambda b, i: (b, 0, i)),
            pl.BlockSpec((None, NKV, bq), lambda b, i: (b, 0, nxt(i))),
            pl.BlockSpec((None, NKV, n_ctx), lambda b, i: (b, 0, 0)),
        ],
        out_specs=pl.BlockSpec((None, NQ, bs), lambda b, i: (b, 0, i)),
        compiler_params=pltpu.CompilerParams(dimension_semantics=("arbitrary", "arbitrary"),
                                             vmem_limit_bytes=VMEM_LIMIT),
        name="attn",
    )(sink, qt, k3, k3, k3, kx3, vt, vt, vt, vt_ctx)


GRP_K = 4 * GLA_DK
GRP_V = 4 * GLA_DV
N_GRP = GLA_HEADS // 4
GLA_TB = 512
CUM_BLK = 256


def _gla_constants():
    t = CUM_BLK
    r = np.arange(t)[:, None]
    c = np.arange(t)[None, :]
    same_chunk = (r // CHUNK) == (c // CHUNK)
    tri_l = (same_chunk & (c <= r)).astype(np.float32)
    tri_u = (same_chunk & (c >= r)).astype(np.float32)
    rk = np.arange(4 * CHUNK)[:, None] // CHUNK
    kmask = (rk == (np.arange(GRP_K)[None, :] // GLA_DK)).astype(np.float32)
    vmask = (rk == (np.arange(GRP_V)[None, :] // GLA_DV)).astype(np.float32)
    smask = ((np.arange(GRP_K)[:, None] // GLA_DK)
             == (np.arange(GRP_V)[None, :] // GLA_DV)).astype(np.float32)
    ai = np.arange(CHUNK)[:, None]
    aj = np.arange(4 * CHUNK)[None, :] % CHUNK
    causal = (aj <= ai).astype(np.float32)
    anti = (aj >= ai).astype(np.float32)
    hmean = ((np.arange(GRP_V)[:, None] // GLA_DV)
             == (np.arange(GRP_V)[None, :] // GLA_DV)).astype(np.float32) / GLA_DV
    return (jnp.asarray(tri_l, BF16), jnp.asarray(tri_u, BF16), jnp.asarray(kmask, BF16),
            jnp.asarray(vmask, BF16), jnp.asarray(smask, F32), jnp.asarray(causal, F32),
            jnp.asarray(anti, F32), jnp.asarray(hmean, BF16))


def _gk(g):
    return slice(g * GRP_K, (g + 1) * GRP_K)


def _gv(g):
    return slice(g * GRP_V, (g + 1) * GRP_V)


def _chunk_slices(t):
    return [slice(ci * CHUNK, (ci + 1) * CHUNK) for ci in range(t // CHUNK)]


def _cum_and_totals(la_hi, la_lo, tri, *, reverse):
    parts = [slice(p, p + CUM_BLK) for p in range(0, la_hi.shape[0], CUM_BLK)]
    cum = jnp.concatenate([_dot(tri, la_hi[p]) + _dot(tri, la_lo[p]) for p in parts], axis=0)
    edge = 0 if reverse else CHUNK - 1
    tots = [cum[sl.start + edge:sl.start + edge + 1] for sl in _chunk_slices(cum.shape[0])]
    return cum, tots


def _bcast_chunks(rows, width):
    return jnp.concatenate([jnp.broadcast_to(r, (CHUNK, width)) for r in rows], axis=0)


def _decay(x, log2_factor):
    return x * jnp.exp2(log2_factor).astype(BF16)


def _state_increments(kd, v, smask):
    return [[_dot_tn(kd[sl, _gk(g)], v[sl, _gv(g)]) * smask for g in range(N_GRP)]
            for sl in _chunk_slices(kd.shape[0])]


def _decay_columns(tots):
    width = tots[0].shape[1]
    row = lax.broadcasted_iota(jnp.int32, (8, width), 0)
    rows = jnp.zeros((8, width), F32)
    for ci, t in enumerate(tots):
        rows = jnp.where(row == ci, jnp.broadcast_to(t, (8, width)), rows)
    return jnp.exp2(rows).T


def _interleave(stage_gens):
    live = list(stage_gens)
    next(live[0], None)
    while live:
        for gen in list(live):
            try:
                next(gen)
            except StopIteration:
                live.remove(gen)


def _state_stages(st_ref, k, v, la_hi, la_lo, tri, smask, *, reverse,
                  store_ref=None, store_base=None):
    nch = k.shape[0] // CHUNK
    cum, tots = _cum_and_totals(la_hi, la_lo, tri, reverse=reverse)
    yield
    kd = _decay(k, _bcast_chunks(tots, GQ) - cum)
    dcol = _decay_columns(tots)
    yield
    ds = _state_increments(kd, v, smask)
    yield
    state = [st_ref[g] for g in range(N_GRP)]
    for ci in (range(nch - 1, -1, -1) if reverse else range(nch)):
        for g in range(N_GRP):
            if store_ref is not None:
                store_ref[store_base + ci, g] = state[g].astype(BF16)
            state[g] = state[g] * dcol[_gk(g), ci:ci + 1] + ds[ci][g]
    for g in range(N_GRP):
        st_ref[g] = state[g]


def _output_stages(q, k, v, la_f, la_b, gate, st_ref, bwd_state, consts, finish):
    tri_l, tri_u, kmask, vmask, smask, causal, anti, hmean, gn = consts
    chunks = _chunk_slices(q.shape[0])

    cum_f, tots_f = _cum_and_totals(la_f[0], la_f[1], tri_l, reverse=False)
    cum_b, tots_b = _cum_and_totals(la_b[0], la_b[1], tri_u, reverse=True)
    yield

    qd_f = _decay(q, cum_f)
    qd_b = _decay(q, cum_b)
    ki_f = _decay(k, -cum_f)
    ki_b = _decay(k, -cum_b)
    kd_f = _decay(k, _bcast_chunks(tots_f, GQ) - cum_f)
    dcol_f = _decay_columns(tots_f)
    v_bd = [[jnp.concatenate([v[sl, _gv(g)]] * 4, axis=0) * vmask for g in range(N_GRP)]
            for sl in chunks]
    yield

    def scores(qd, ki, sl, g):
        ki_bd = jnp.concatenate([ki[sl, _gk(g)]] * 4, axis=0) * kmask
        return _dot_nt(qd[sl, _gk(g)], ki_bd)

    a_f = [[scores(qd_f, ki_f, sl, g) for g in range(N_GRP)] for sl in chunks]
    a_b = [[scores(qd_b, ki_b, sl, g) for g in range(N_GRP)] for sl in chunks]
    ds = _state_increments(kd_f, v, smask)
    yield

    a = [[(af[g] * causal + ab[g] * anti).astype(BF16) for g in range(N_GRP)]
         for af, ab in zip(a_f, a_b)]
    state = [st_ref[g] for g in range(N_GRP)]
    st_in = []
    for ci in range(len(chunks)):
        st_in.append([s.astype(BF16) for s in state])
        state = [state[g] * dcol_f[_gk(g), ci:ci + 1] + ds[ci][g] for g in range(N_GRP)]
    for g in range(N_GRP):
        st_ref[g] = state[g]
    yield

    outs = []
    for ci, sl in enumerate(chunks):
        parts = []
        for g in range(N_GRP):
            intra = _dot(a[ci][g], v_bd[ci][g])
            qd2 = jnp.concatenate([qd_f[sl, _gk(g)], qd_b[sl, _gk(g)]], axis=1)
            st2 = jnp.concatenate([st_in[ci][g], bwd_state(ci, g)], axis=0)
            parts.append(intra + _dot(qd2, st2))
        outs.append(jnp.concatenate(parts, axis=1))
    o = jnp.concatenate(outs, axis=0)
    yield

    o2 = (o * o).astype(BF16)
    ms = jnp.concatenate([_dot(o2[:, _gv(g)], hmean) for g in range(N_GRP)], axis=1)
    yield
    finish((o * lax.rsqrt(ms + EPS) * gn * gate.astype(F32)).astype(BF16))


def _gla_kernel(gin_ref, lah_ref, lal_ref, ginc_ref, lahc_ref, lalc_ref, gn_ref,
                tril_ref, triu_ref, kmask_ref, vmask_ref, smask_ref, causal_ref, anti_ref,
                hmean_ref, o_ref, st_ref, sb_ref, *, n_blocks):
    phase = pl.program_id(1)
    j = pl.program_id(2)
    ch_per_blk = GLA_TB // CHUNK
    fwd = slice(0, GQ)
    bwd = slice(GQ, 2 * GQ)
    kcols = slice(GQ, 2 * GQ)
    vcols = slice(2 * GQ, 2 * GQ + GV)

    gcols = slice(2 * GQ + GV, GIN_W)
    halves = [slice(r0, r0 + CUM_BLK) for r0 in range(0, GLA_TB, CUM_BLK)]
    ch_per_half = CUM_BLK // CHUNK

    def ctx_state(cols, tri_ref, reverse):
        st_ref[...] = jnp.zeros_like(st_ref)
        ctx_blocks = [slice(r0, r0 + CUM_BLK) for r0 in range(0, ginc_ref.shape[0], CUM_BLK)]
        for rs in (ctx_blocks[::-1] if reverse else ctx_blocks):
            for _ in _state_stages(st_ref, ginc_ref[rs, kcols], ginc_ref[rs, vcols],
                                   lahc_ref[rs, cols], lalc_ref[rs, cols], tri_ref[...],
                                   smask_ref[...], reverse=reverse):
                pass

    @pl.when(phase == 0)
    def _backward_states():
        @pl.when(j == 0)
        def _():
            ctx_state(bwd, triu_ref, True)

        blk = n_blocks - 1 - j
        _interleave([
            _state_stages(st_ref, gin_ref[rs, kcols], gin_ref[rs, vcols],
                          lah_ref[rs, bwd], lal_ref[rs, bwd], triu_ref[...], smask_ref[...],
                          reverse=True, store_ref=sb_ref,
                          store_base=blk * ch_per_blk + hi * ch_per_half)
            for hi, rs in reversed(list(enumerate(halves)))])

    @pl.when(phase == 1)
    def _outputs():
        @pl.when(j == 0)
        def _():
            ctx_state(fwd, tril_ref, False)

        consts = (tril_ref[...], triu_ref[...], kmask_ref[...], vmask_ref[...],
                  smask_ref[...], causal_ref[...], anti_ref[...], hmean_ref[...], gn_ref[...])

        def half(hi, rs):
            base = j * ch_per_blk + hi * ch_per_half

            def finish(val):
                o_ref[rs, :] = val

            return _output_stages(
                gin_ref[rs, 0:GQ], gin_ref[rs, kcols], gin_ref[rs, vcols],
                (lah_ref[rs, fwd], lal_ref[rs, fwd]), (lah_ref[rs, bwd], lal_ref[rs, bwd]),
                gin_ref[rs, gcols], st_ref, lambda ci, g: sb_ref[base + ci, g], consts, finish)

        _interleave([half(hi, rs) for hi, rs in enumerate(halves)])


def _gla(gin, la_hi, la_lo, gin_ctx, la_hi_ctx, la_lo_ctx, gn_tiled, *, batch, seq, n_ctx):
    assert n_ctx % CUM_BLK == 0 and n_ctx // CHUNK <= 8 and GLA_TB // CHUNK <= 8
    tb = GLA_TB
    nb = seq // tb
    nch = seq // CHUNK
    r3 = lambda a, t: a.reshape(batch, t, a.shape[-1])
    consts = _gla_constants()

    def blk_map(b, p, j):
        return (b, jnp.where(p == 0, nb - 1 - j, j), 0)

    ctx_map = lambda b, p, j: (b, 0, 0)
    const_map = lambda b, p, j: (0, 0)
    return pl.pallas_call(
        functools.partial(_gla_kernel, n_blocks=nb),
        out_shape=jax.ShapeDtypeStruct((batch, seq, GV), BF16),
        grid=(batch, 2, nb),
        in_specs=[pl.BlockSpec((None, tb, GIN_W), blk_map),
                  pl.BlockSpec((None, tb, LA_W), blk_map),
                  pl.BlockSpec((None, tb, LA_W), blk_map),
                  pl.BlockSpec((None, n_ctx, GIN_W), ctx_map),
                  pl.BlockSpec((None, n_ctx, LA_W), ctx_map),
                  pl.BlockSpec((None, n_ctx, LA_W), ctx_map),
                  pl.BlockSpec((1, GV), const_map)]
                 + [pl.BlockSpec(cst.shape, const_map) for cst in consts],
        out_specs=pl.BlockSpec((None, tb, GV), lambda b, p, j: (b, jnp.where(p == 0, 0, j), 0)),
        scratch_shapes=[pltpu.VMEM((N_GRP, GRP_K, GRP_V), F32),
                        pltpu.VMEM((nch, N_GRP, GRP_K, GRP_V), BF16)],
        compiler_params=pltpu.CompilerParams(
            dimension_semantics=("arbitrary", "arbitrary", "arbitrary"),
            vmem_limit_bytes=VMEM_LIMIT),
        name="gla",
    )(r3(gin, seq), r3(la_hi, seq), r3(la_lo, seq), r3(gin_ctx, n_ctx), r3(la_hi_ctx, n_ctx),
      r3(la_lo_ctx, n_ctx), gn_tiled, *consts)


FFN_CHUNK = 256
MIX_SLICES = 8


def _mix_ffn_kernel(att_ref, gla_ref, x_ref, ada_ref, gpm_ref, gpf_ref, gqf_ref,
                    woa_ref, wog_ref, wfi_ref, wfo_ref, o_ref, x1_ref, h_ref, y_ref, acc_ref, *,
                    tiles_per_batch, n_tiles):
    i = pl.program_id(0)
    tm = x_ref.shape[0]
    n_chunks = FFN_HIDDEN // FFN_CHUNK
    mix_rows = tm // MIX_SLICES

    def mix_matmul():
        y_ref[...] = _dot_tn(att_ref[...], woa_ref[...]) + _dot(gla_ref[...], wog_ref[...])

    def mix_rows_slice(slot, si, anchor):
        rs = slice(si * mix_rows, (si + 1) * mix_rows)
        row = jnp.minimum(i, n_tiles - 1) // tiles_per_batch
        ada = ada_ref[pl.ds(row, 1), :]
        bits = pltpu.bitcast(anchor, jnp.uint32)
        sixteen = jnp.uint32(16)
        zero = pltpu.bitcast(
            lax.shift_right_logical(lax.shift_right_logical(bits, sixteen), sixteen), F32)
        gt1 = ada[:, 2 * D:3 * D] + zero
        sh2 = ada[:, 3 * D:4 * D]
        sc2 = ada[:, 4 * D:5 * D]
        x1 = x_ref[rs, :] + gt1 * (_rms(y_ref[rs, :]) * gpm_ref[...])
        x1_ref[slot, rs, :] = x1
        h_ref[slot, rs, :] = ((_rms(x1) * gpf_ref[...]) * (1.0 + sc2) + sh2).astype(BF16)

    def ffn_chunk(slot, ci):
        c0 = ci * FFN_CHUNK
        h = h_ref[slot]
        g = _dot(h, wfi_ref[:, c0:c0 + FFN_CHUNK])
        u = _dot(h, wfi_ref[:, FFN_HIDDEN + c0:FFN_HIDDEN + c0 + FFN_CHUNK])
        a = (_silu(g) * u).astype(BF16)
        part = _dot(a, wfo_ref[c0:c0 + FFN_CHUNK, :])
        if ci == 0:
            acc_ref[...] = part
        else:
            acc_ref[...] += part
        return part[0:8, :]

    def ffn_finish(slot):
        row = jnp.maximum(i - 1, 0) // tiles_per_batch
        gt2 = ada_ref[pl.ds(row, 1), 5 * D:6 * D]
        o_ref[...] = x1_ref[slot] + gt2 * (_rms(acc_ref[...]) * gqf_ref[...])

    @pl.when(i == 0)
    def _():
        x1_ref[1] = jnp.zeros(x1_ref.shape[1:], F32)
        h_ref[1] = jnp.zeros(h_ref.shape[1:], BF16)

    for parity in range(2):
        @pl.when(i % 2 == parity)
        def _():
            for ci in range(n_chunks):
                anchor = ffn_chunk(1 - parity, ci)
                if ci == 0:
                    mix_matmul()
                elif ci - 1 < MIX_SLICES:
                    mix_rows_slice(parity, ci - 1, anchor[0:1, :])
            ffn_finish(1 - parity)


def _mix_ffn(att_t, gla, x2d, ada, g_post_mix, g_pre_ffn, g_post_ffn,
             w_out_a, w_out_g, w_ffn_in, w_ffn_out, *, tm, tiles_per_batch):
    n = x2d.shape[0]
    tpb = tiles_per_batch
    nt = n // tm
    const = lambda i: (0, 0)
    resident = functools.partial(pl.BlockSpec, index_map=const, pipeline_mode=pl.Buffered(1))
    cur = lambda i: jnp.minimum(i, nt - 1)
    done = lambda i: jnp.maximum(i - 1, 0)
    return pl.pallas_call(
        functools.partial(_mix_ffn_kernel, tiles_per_batch=tpb, n_tiles=nt),
        out_shape=jax.ShapeDtypeStruct((n, D), F32),
        grid=(nt + 1,),
        in_specs=[pl.BlockSpec((None, NQ, tm), lambda i: (cur(i) // tpb, 0, cur(i) % tpb)),
                  pl.BlockSpec((tm, GV), lambda i: (cur(i), 0)),
                  pl.BlockSpec((tm, D), lambda i: (cur(i), 0)),
                  pl.BlockSpec((8, 6 * D), const),
                  pl.BlockSpec((1, D), const),
                  pl.BlockSpec((1, D), const),
                  pl.BlockSpec((1, D), const),
                  resident((NQ, D)),
                  resident((GV, D)),
                  resident((D, 2 * FFN_HIDDEN)),
                  resident((FFN_HIDDEN, D))],
        out_specs=pl.BlockSpec((tm, D), lambda i: (done(i), 0)),
        scratch_shapes=[pltpu.VMEM((2, tm, D), F32),
                        pltpu.VMEM((2, tm, D), BF16),
                        pltpu.VMEM((tm, D), F32),
                        pltpu.VMEM((tm, D), F32)],
        compiler_params=pltpu.CompilerParams(dimension_semantics=("arbitrary",),
                                             vmem_limit_bytes=VMEM_LIMIT),
        name="mix_ffn",
    )(att_t, gla, x2d, ada, g_post_mix, g_pre_ffn, g_post_ffn,
      w_out_a, w_out_g, w_ffn_in, w_ffn_out)


def _rope_tables(seq):
    half = HEAD_DIM // 2
    inv_freq = ROPE_BASE ** (-jnp.arange(0, half, 2, dtype=F32) / half)
    dim = np.arange(HEAD_DIM)
    freq = jnp.tile(inv_freq, HEAD_DIM // ROPE_BLK)[:, None]
    pos = jnp.arange(seq)[None, :]
    row = (pos // GRID_W).astype(F32)
    col = (pos % GRID_W).astype(F32)
    ang = jnp.where((dim < half)[:, None], row, col) * freq
    cos, sin = jnp.cos(ang), jnp.sin(ang)
    even = ((dim // ROPE_BLK) % 2 == 0)[:, None]
    sa = jnp.where(even, -sin, 0.0)
    sb = jnp.where(even, 0.0, sin)
    return cos, sa, sb


def kernel(x, c, ctx, c_ctx, w_ada, b_ada, g_pre_mix, g_post_mix, g_pre_ffn, g_post_ffn,
           w_in, attn_sink, w_gate_fwd, b_gate_fwd, w_gate_bwd, b_gate_bwd, g_gla_norm,
           w_out, w_ffn_in, w_ffn_out):
    batch, seq, _ = x.shape
    n_ctx = ctx.shape[1]
    depth = w_ada.shape[0]
    assert depth == 1
    l = 0

    c8 = jnp.zeros((8, D), F32).at[0:batch].set(c).at[4].set(c_ctx)
    ada = _ada(c8, w_ada[l], b_ada[l][None, :])

    wi = w_in[l]
    g0 = NQ + 2 * NKV
    w_qkv_t = wi[:, 0:g0].T.astype(BF16)
    w_rest = jnp.concatenate(
        [wi[:, g0:], jnp.zeros((D, Z_W - 2 * GATE_RANK), F32)], axis=1).astype(BF16)
    w_gate_p = jnp.zeros((Z_W, LA_W), F32)
    w_gate_p = w_gate_p.at[0:GATE_RANK, 0:GQ].set(w_gate_fwd[l])
    w_gate_p = w_gate_p.at[GATE_RANK:2 * GATE_RANK, GQ:].set(w_gate_bwd[l]).astype(BF16)
    b_gate_p = jnp.concatenate([b_gate_fwd[l], b_gate_bwd[l]])[None, :]

    tm = 512
    tables = _rope_tables(seq)
    x2d = x.reshape(batch * seq, D)
    g_pre = g_pre_mix[l][None, :]
    qt, k, vt, gin, lah, lal = _inproj(x2d, ada, g_pre, w_qkv_t, w_rest, w_gate_p, b_gate_p,
                                      tables, batch=batch, tokens=seq, tm=tm)
    _, k_c, vt_c, gin_c, lah_c, lal_c = _inproj(
        ctx.reshape(batch * n_ctx, D), ada, g_pre, w_qkv_t, w_rest, w_gate_p, b_gate_p, None,
        batch=batch, tokens=n_ctx, tm=n_ctx)

    att_t = _attn(attn_sink[l], qt, k, vt, k_c, vt_c, batch=batch, seq=seq, n_ctx=n_ctx)

    gn_tiled = jnp.tile(g_gla_norm[l], GLA_HEADS)[None, :]
    gla = _gla(gin, lah, lal, gin_c, lah_c, lal_c, gn_tiled, batch=batch, seq=seq, n_ctx=n_ctx)

    wo = w_out[l]
    out = _mix_ffn(att_t, gla.reshape(batch * seq, GV), x2d, ada,
                   g_post_mix[l][None, :], g_pre_ffn[l][None, :], g_post_ffn[l][None, :],
                   wo[:NQ].astype(BF16), wo[NQ:].astype(BF16),
                   w_ffn_in[l].astype(BF16), w_ffn_out[l].astype(BF16),
                   tm=tm, tiles_per_batch=seq // tm)
    return out.reshape(batch, seq, D)
```

```python
import functools

import jax
import jax.numpy as jnp
import numpy as np
from jax import lax
from jax.experimental import pallas as pl
from jax.experimental.pallas import tpu as pltpu

D = 1024
HEAD_DIM = 64
ATT_HEADS = 8
ATT_KV_HEADS = 2
GROUP = ATT_HEADS // ATT_KV_HEADS
WINDOW = 128
GRID_W = 64
ROPE_BASE = 10000.0
ROPE_BLK = HEAD_DIM // 4
GLA_HEADS = 8
GLA_DK = 32
GLA_DV = 64
CHUNK = 64
GATE_RANK = 16
GATE_TAU = 16.0
FFN_HIDDEN = 2816
NEG_INF = -1e30
EPS = 1e-6
LOG2E = 1.4426950408889634
Q_SCALE = LOG2E * HEAD_DIM ** -0.5
LA_SCALE = LOG2E / GATE_TAU

NQ = ATT_HEADS * HEAD_DIM
NKV = ATT_KV_HEADS * HEAD_DIM
GQ = GLA_HEADS * GLA_DK
GV = GLA_HEADS * GLA_DV
GIN_W = 2 * GQ + 2 * GV
KV_W = GQ + GV
K_COLS = slice(0, GQ)
V_COLS = slice(GQ, KV_W)
Q_COLS = slice(KV_W, KV_W + GQ)
G_COLS = slice(KV_W + GQ, GIN_W)
Z_W = 128
LA_W = 2 * GQ
REST_W = GIN_W + Z_W

LANES = 128
VMEM_LIMIT = 56 * 1024 * 1024

BF16 = jnp.bfloat16
F32 = jnp.float32


def _dot(a, b):
    return jnp.dot(a, b, preferred_element_type=F32)


def _dot_nt(a, b):
    return lax.dot_general(a, b, (((1,), (1,)), ((), ())), preferred_element_type=F32)


def _dot_tn(a, b):
    return lax.dot_general(a, b, (((0,), (0,)), ((), ())), preferred_element_type=F32)


def _rms(x):
    return x * lax.rsqrt(jnp.mean(x * x, axis=-1, keepdims=True) + EPS)


def _silu(x):
    return x * (1.0 / (1.0 + jnp.exp(-x)))


def _ada_kernel(c_ref, w_ref, b_ref, o_ref):
    a = _silu(c_ref[...]).astype(BF16)
    o_ref[...] = _dot(a, w_ref[...].astype(BF16)) + b_ref[...]


def _ada(c8, w_ada, b_ada):
    n = w_ada.shape[1]
    bn = 1024
    return pl.pallas_call(
        _ada_kernel,
        out_shape=jax.ShapeDtypeStruct((8, n), F32),
        grid=(n // bn,),
        in_specs=[pl.BlockSpec((8, D), lambda j: (0, 0)),
                  pl.BlockSpec((D, bn), lambda j: (0, j)),
                  pl.BlockSpec((1, bn), lambda j: (0, j))],
        out_specs=pl.BlockSpec((8, bn), lambda j: (0, j)),
        compiler_params=pltpu.CompilerParams(dimension_semantics=("arbitrary",)),
        name="ada",
    )(c8, w_ada, b_ada)


NORM_ROWS = 128


def _inproj_kernel(*refs, tiles_per_batch, rope):
    if rope:
        (x_ref, ada_ref, g_ref, wqkv_ref, w_ref, wg_ref, bg_ref,
         cost_ref, sat_ref, sbt_ref,
         qt_ref, k_ref, vt_ref, gin_ref, lah_ref, lal_ref) = refs
        row = pl.program_id(0) // tiles_per_batch
    else:
        (x_ref, ada_ref, g_ref, wqkv_ref, w_ref, wg_ref, bg_ref,
         qt_ref, k_ref, vt_ref, gin_ref, lah_ref, lal_ref) = refs
        row = 4
    ada = ada_ref[pl.ds(row, 1), :]
    sh1 = ada[:, 0:D]
    sc1 = ada[:, D:2 * D]
    tm = x_ref.shape[0]
    hb = jnp.concatenate(
        [((_rms(x_ref[r0:r0 + NORM_ROWS, :]) * g_ref[...]) * (1.0 + sc1) + sh1).astype(BF16)
         for r0 in range(0, tm, NORM_ROWS)], axis=0)

    qkv = _dot_nt(wqkv_ref[...], hb)
    z = _dot(hb, w_ref[:, GIN_W:REST_W])
    logits = _dot(z.astype(BF16), wg_ref[...]) + bg_ref[...]
    gin = _dot(hb, w_ref[:, 0:GIN_W])

    def head_t(hd):
        xh = qkv[hd * HEAD_DIM:(hd + 1) * HEAD_DIM]
        if not rope:
            return xh
        up = jnp.concatenate([xh[ROPE_BLK:], xh[:ROPE_BLK]], axis=0)
        dn = jnp.concatenate([xh[HEAD_DIM - ROPE_BLK:], xh[:HEAD_DIM - ROPE_BLK]], axis=0)
        return xh * cost_ref[...] + up * sat_ref[...] + dn * sbt_ref[...]

    for hd in range(ATT_HEADS):
        qh = head_t(hd)
        if rope:
            qh = qh * Q_SCALE
        qt_ref[hd * HEAD_DIM:(hd + 1) * HEAD_DIM, :] = qh.astype(BF16)
    kt = jnp.concatenate([head_t(ATT_HEADS + hd) for hd in range(ATT_KV_HEADS)], axis=0)
    k_ref[...] = kt.T.astype(BF16)
    vt_ref[...] = qkv[NQ + NKV:NQ + 2 * NKV].astype(BF16)

    log_sig = jnp.minimum(logits, 0.0) - jnp.log(1.0 + jnp.exp(-jnp.abs(logits)))
    la = log_sig * LA_SCALE
    hi = la.astype(BF16)
    lah_ref[...] = hi
    lal_ref[...] = (la - hi.astype(F32)).astype(BF16)

    gin_ref[:, 0:KV_W] = gin[:, 0:KV_W].astype(BF16)
    gin_ref[:, Q_COLS] = (gin[:, Q_COLS] * (GLA_DK ** -0.5)).astype(BF16)
    gin_ref[:, G_COLS] = _silu(gin[:, G_COLS]).astype(BF16)


def _inproj(x2d, ada, g_pre, w_qkv_t, w_rest, w_gate_p, b_gate_p, tables, *,
            batch, tokens, tm):
    n = x2d.shape[0]
    tpb = tokens // tm
    rope = tables is not None
    const = lambda i: (0, 0)
    in_specs = [pl.BlockSpec((tm, D), lambda i: (i, 0)),
                pl.BlockSpec((8, 6 * D), const),
                pl.BlockSpec((1, D), const),
                pl.BlockSpec((NQ + 2 * NKV, D), const),
                pl.BlockSpec((D, REST_W), const),
                pl.BlockSpec((Z_W, LA_W), const),
                pl.BlockSpec((1, LA_W), const)]
    args = [x2d, ada, g_pre, w_qkv_t, w_rest, w_gate_p, b_gate_p]
    if rope:
        in_specs += [pl.BlockSpec((HEAD_DIM, tm), lambda i: (0, i % tpb))] * 3
        args += list(tables)
    row_blk = lambda w: pl.BlockSpec((tm, w), lambda i: (i, 0))
    return pl.pallas_call(
        functools.partial(_inproj_kernel, tiles_per_batch=tpb, rope=rope),
        out_shape=(jax.ShapeDtypeStruct((batch, NQ, tokens), BF16),
                   jax.ShapeDtypeStruct((n, NKV), BF16),
                   jax.ShapeDtypeStruct((batch, NKV, tokens), BF16),
                   jax.ShapeDtypeStruct((n, GIN_W), BF16),
                   jax.ShapeDtypeStruct((n, LA_W), BF16),
                   jax.ShapeDtypeStruct((n, LA_W), BF16)),
        grid=(n // tm,),
        in_specs=in_specs,
        out_specs=(pl.BlockSpec((None, NQ, tm), lambda i: (i // tpb, 0, i % tpb)),
                   row_blk(NKV),
                   pl.BlockSpec((None, NKV, tm), lambda i: (i // tpb, 0, i % tpb)),
                   row_blk(GIN_W), row_blk(LA_W), row_blk(LA_W)),
        compiler_params=pltpu.CompilerParams(dimension_semantics=("arbitrary",),
                                             vmem_limit_bytes=VMEM_LIMIT),
        name="inproj_lat" if rope else "inproj_ctx",
    )(*args)


HEADS_PER_DOT = 2


ATT_SUB = 4
ONES_ROWS = 16


def _attn_kernel(sink_ref, qt_ref, kp_ref, kc_ref, kn_ref, kx_ref,
                 vp_ref, vc_ref, vn_ref, vx_ref, o_ref, *, n_steps):
    i = pl.program_id(1)
    bq = WINDOW
    nw = 3 * bq
    nq = HEADS_PER_DOT * bq
    c = lax.broadcasted_iota(jnp.int32, (nw, nq), 0)
    r = lax.broadcasted_iota(jnp.int32, (nw, nq), 1) % bq
    rel = c - bq - r
    in_win = (rel <= WINDOW) & (rel >= -WINDOW)
    lane = lax.broadcasted_iota(jnp.int32, (1, nq), 1)
    zeros = jnp.zeros((HEAD_DIM, nq), BF16)

    k_blocks = [kp_ref[...]] + [kc_ref[s * bq:(s + 1) * bq] for s in range(ATT_SUB)] + [kn_ref[...]]
    v_blocks = ([vp_ref[...]] + [vc_ref[:, s * bq:(s + 1) * bq] for s in range(ATT_SUB)]
                + [vn_ref[...]])
    kx = kx_ref[...]
    vx = vx_ref[...]

    subs = []
    for s in range(ATT_SUB):
        keep = in_win
        if s == 0:
            keep = keep & ((c >= bq) | (i > 0))
        if s == ATT_SUB - 1:
            keep = keep & ((c < 2 * bq) | (i < n_steps - 1))
        kcat = jnp.concatenate(k_blocks[s:s + 3] + [kx], axis=0)
        vtcat = jnp.concatenate(v_blocks[s:s + 3] + [vx], axis=1)
        ones = jnp.ones((ONES_ROWS, vtcat.shape[1]), BF16)
        vaug = [jnp.concatenate([vtcat[kvh * HEAD_DIM:(kvh + 1) * HEAD_DIM], ones], axis=0)
                for kvh in range(ATT_KV_HEADS)]
        subs.append((keep, kcat, vaug))

    tiles = [(s, kvh, kvh * GROUP + pr * HEADS_PER_DOT)
             for s in range(ATT_SUB) for kvh in range(ATT_KV_HEADS)
             for pr in range(GROUP // HEADS_PER_DOT)]

    scores = []
    for s, kvh, h0 in tiles:
        qt = jnp.concatenate(
            [qt_ref[(h0 + u) * HEAD_DIM:(h0 + u + 1) * HEAD_DIM, s * bq:(s + 1) * bq]
             for u in range(HEADS_PER_DOT)], axis=1)
        rhs = jnp.concatenate([qt, zeros] if kvh == 0 else [zeros, qt], axis=0)
        scores.append(_dot(subs[s][1], rhs))

    probs = []
    for (s, kvh, h0), st in zip(tiles, scores):
        keep = subs[s][0]
        parts = [jnp.where(keep[0:bq], st[0:bq], NEG_INF), st[bq:2 * bq],
                 jnp.where(keep[2 * bq:nw], st[2 * bq:nw], NEG_INF), st[nw:]]
        sink = jnp.where(lane < bq, sink_ref[h0], sink_ref[h0 + 1]) * LOG2E
        m = sink
        for p in parts:
            m = jnp.maximum(m, jnp.max(p, axis=0, keepdims=True))
        et = jnp.concatenate([jnp.exp2(p - m).astype(BF16) for p in parts], axis=0)
        probs.append((et, jnp.exp2(sink - m)))

    for (s, kvh, h0), (et, e_sink) in zip(tiles, probs):
        ot = _dot(subs[s][2][kvh], et)
        denom = ot[HEAD_DIM:HEAD_DIM + 1] + e_sink
        on = ot[0:HEAD_DIM] * (1.0 / denom)
        for u in range(HEADS_PER_DOT):
            o_ref[(h0 + u) * HEAD_DIM:(h0 + u + 1) * HEAD_DIM, s * bq:(s + 1) * bq] = (
                on[:, u * bq:(u + 1) * bq].astype(BF16))


def _attn(sink, qt, k, vt, k_ctx, vt_ctx, *, batch, seq, n_ctx):
    bq = WINDOW
    bs = ATT_SUB * bq
    nb = seq // bq
    ns = seq // bs
    k3 = k.reshape(batch, seq, NKV)
    kx3 = k_ctx.reshape(batch, n_ctx, NKV)
    prev = lambda i: jnp.maximum(ATT_SUB * i - 1, 0)
    nxt = lambda i: jnp.minimum(ATT_SUB * (i + 1), nb - 1)
    return pl.pallas_call(
        functools.partial(_attn_kernel, n_steps=ns),
        out_shape=jax.ShapeDtypeStruct((batch, NQ, seq), BF16),
        grid=(batch, ns),
        in_specs=[
            pl.BlockSpec(memory_space=pltpu.SMEM),
            pl.BlockSpec((None, NQ, bs), lambda b, i: (b, 0, i)),
            pl.BlockSpec((None, bq, NKV), lambda b, i: (b, prev(i), 0)),
            pl.BlockSpec((None, bs, NKV), lambda b, i: (b, i, 0)),
            pl.BlockSpec((None, bq, NKV), lambda b, i: (b, nxt(i), 0)),
            pl.BlockSpec((None, n_ctx, NKV), lambda b, i: (b, 0, 0)),
            pl.BlockSpec((None, NKV, bq), lambda b, i: (b, 0, prev(i))),
            pl.BlockSpec((None, NKV, bs), lambda b, i: (b, 0, i)),
            pl.BlockSpec((None, NKV, bq), lambda b, i: (b, 0, nxt(i))),
            pl.BlockSpec((None, NKV, n_ctx), lambda b, i: (b, 0, 0)),
        ],
        out_specs=pl.BlockSpec((None, NQ, bs), lambda b, i: (b, 0, i)),
        compiler_params=pltpu.CompilerParams(dimension_semantics=("arbitrary", "arbitrary"),
                                             vmem_limit_bytes=VMEM_LIMIT),
        name="attn",
    )(sink, qt, k3, k3, k3, kx3, vt, vt, vt, vt_ctx)


GRP_K = 4 * GLA_DK
GRP_V = 4 * GLA_DV
N_GRP = GLA_HEADS // 4
GLA_TB = 512
CUM_BLK = 256


def _gla_constants():
    t = CUM_BLK
    r = np.arange(t)[:, None]
    c = np.arange(t)[None, :]
    same_chunk = (r // CHUNK) == (c // CHUNK)
    tri_l = (same_chunk & (c <= r)).astype(np.float32)
    tri_u = (same_chunk & (c >= r)).astype(np.float32)
    rk = np.arange(4 * CHUNK)[:, None] // CHUNK
    kmask = (rk == (np.arange(GRP_K)[None, :] // GLA_DK)).astype(np.float32)
    vmask = (rk == (np.arange(GRP_V)[None, :] // GLA_DV)).astype(np.float32)
    smask = ((np.arange(GRP_K)[:, None] // GLA_DK)
             == (np.arange(GRP_V)[None, :] // GLA_DV)).astype(np.float32)
    ai = np.arange(CHUNK)[:, None]
    aj = np.arange(4 * CHUNK)[None, :] % CHUNK
    causal = (aj <= ai).astype(np.float32)
    anti = (aj >= ai).astype(np.float32)
    hmean = ((np.arange(GRP_V)[:, None] // GLA_DV)
             == (np.arange(GRP_V)[None, :] // GLA_DV)).astype(np.float32) / GLA_DV
    return (jnp.asarray(tri_l, BF16), jnp.asarray(tri_u, BF16), jnp.asarray(kmask, BF16),
            jnp.asarray(vmask, BF16), jnp.asarray(smask, F32), jnp.asarray(causal, F32),
            jnp.asarray(anti, F32), jnp.asarray(hmean, BF16))


def _gk(g):
    return slice(g * GRP_K, (g + 1) * GRP_K)


def _gv(g):
    return slice(g * GRP_V, (g + 1) * GRP_V)


def _chunk_slices(t):
    return [slice(ci * CHUNK, (ci + 1) * CHUNK) for ci in range(t // CHUNK)]


def _cum_and_totals(la_hi, la_lo, tri, *, reverse):
    parts = [slice(p, p + CUM_BLK) for p in range(0, la_hi.shape[0], CUM_BLK)]
    cum = jnp.concatenate([_dot(tri, la_hi[p]) + _dot(tri, la_lo[p]) for p in parts], axis=0)
    edge = 0 if reverse else CHUNK - 1
    tots = [cum[sl.start + edge:sl.start + edge + 1] for sl in _chunk_slices(cum.shape[0])]
    return cum, tots


def _bcast_chunks(rows, width):
    return jnp.concatenate([jnp.broadcast_to(r, (CHUNK, width)) for r in rows], axis=0)


def _decay(x, log2_factor):
    return x * jnp.exp2(log2_factor).astype(BF16)


def _state_increments(kd, v, smask):
    return [[_dot_tn(kd[sl, _gk(g)], v[sl, _gv(g)]) * smask for g in range(N_GRP)]
            for sl in _chunk_slices(kd.shape[0])]


def _decay_columns(tots):
    width = tots[0].shape[1]
    row = lax.broadcasted_iota(jnp.int32, (8, width), 0)
    rows = jnp.zeros((8, width), F32)
    for ci, t in enumerate(tots):
        rows = jnp.where(row == ci, jnp.broadcast_to(t, (8, width)), rows)
    return jnp.exp2(rows).T


def _interleave(stage_gens):
    live = list(stage_gens)
    next(live[0], None)
    while live:
        for gen in list(live):
            try:
                next(gen)
            except StopIteration:
                live.remove(gen)


def _state_stages(st_ref, k, v, la_hi, la_lo, tri, smask, *, reverse,
                  store_ref=None, store_base=None):
    nch = k.shape[0] // CHUNK
    cum, tots = _cum_and_totals(la_hi, la_lo, tri, reverse=reverse)
    yield
    kd = _decay(k, _bcast_chunks(tots, GQ) - cum)
    dcol = _decay_columns(tots)
    yield
    ds = _state_increments(kd, v, smask)
    yield
    state = [st_ref[g] for g in range(N_GRP)]
    for ci in (range(nch - 1, -1, -1) if reverse else range(nch)):
        for g in range(N_GRP):
            if store_ref is not None:
                store_ref[store_base + ci, g] = state[g].astype(BF16)
            state[g] = state[g] * dcol[_gk(g), ci:ci + 1] + ds[ci][g]
    for g in range(N_GRP):
        st_ref[g] = state[g]


def _output_stages(q, k, v, la_f, la_b, gate, st_ref, bwd_state, consts, finish):
    tri_l, tri_u, kmask, vmask, smask, causal, anti, hmean, gn = consts
    chunks = _chunk_slices(q.shape[0])

    cum_f, tots_f = _cum_and_totals(la_f[0], la_f[1], tri_l, reverse=False)
    cum_b, tots_b = _cum_and_totals(la_b[0], la_b[1], tri_u, reverse=True)
    yield

    qd_f = _decay(q, cum_f)
    qd_b = _decay(q, cum_b)
    ki_f = _decay(k, -cum_f)
    ki_b = _decay(k, -cum_b)
    kd_f = _decay(k, _bcast_chunks(tots_f, GQ) - cum_f)
    dcol_f = _decay_columns(tots_f)
    v_bd = [[jnp.concatenate([v[sl, _gv(g)]] * 4, axis=0) * vmask for g in range(N_GRP)]
            for sl in chunks]
    yield

    def scores(qd, ki, sl, g):
        ki_bd = jnp.concatenate([ki[sl, _gk(g)]] * 4, axis=0) * kmask
        return _dot_nt(qd[sl, _gk(g)], ki_bd)

    a_f = [[scores(qd_f, ki_f, sl, g) for g in range(N_GRP)] for sl in chunks]
    a_b = [[scores(qd_b, ki_b, sl, g) for g in range(N_GRP)] for sl in chunks]
    ds = _state_increments(kd_f, v, smask)
    yield

    a = [[(af[g] * causal + ab[g] * anti).astype(BF16) for g in range(N_GRP)]
         for af, ab in zip(a_f, a_b)]
    state = [st_ref[g] for g in range(N_GRP)]
    st_in = []
    for ci in range(len(chunks)):
        st_in.append([s.astype(BF16) for s in state])
        state = [state[g] * dcol_f[_gk(g), ci:ci + 1] + ds[ci][g] for g in range(N_GRP)]
    for g in range(N_GRP):
        st_ref[g] = state[g]
    yield

    outs = []
    for ci, sl in enumerate(chunks):
        parts = []
        for g in range(N_GRP):
            intra = _dot(a[ci][g], v_bd[ci][g])
            qd2 = jnp.concatenate([qd_f[sl, _gk(g)], qd_b[sl, _gk(g)]], axis=1)
            st2 = jnp.concatenate([st_in[ci][g], bwd_state(ci, g)], axis=0)
            parts.append(intra + _dot(qd2, st2))
        outs.append(jnp.concatenate(parts, axis=1))
    o = jnp.concatenate(outs, axis=0)
    yield

    o2 = (o * o).astype(BF16)
    ms = jnp.concatenate([_dot(o2[:, _gv(g)], hmean) for g in range(N_GRP)], axis=1)
    yield
    finish((o * lax.rsqrt(ms + EPS) * gn * gate.astype(F32)).astype(BF16))


def _gla_kernel(kv_ref, qg_ref, lahf_ref, lahb_ref, lalf_ref, lalb_ref,
                ginc_ref, lahc_ref, lalc_ref, gn_ref,
                tril_ref, triu_ref, kmask_ref, vmask_ref, smask_ref, causal_ref, anti_ref,
                hmean_ref, o_ref, st_ref, sb_ref, *, n_blocks):
    phase = pl.program_id(1)
    j = pl.program_id(2)
    ch_per_blk = GLA_TB // CHUNK
    fwd = slice(0, GQ)
    bwd = slice(GQ, 2 * GQ)
    halves = [slice(r0, r0 + CUM_BLK) for r0 in range(0, GLA_TB, CUM_BLK)]
    ch_per_half = CUM_BLK // CHUNK

    def ctx_state(cols, tri_ref, reverse):
        st_ref[...] = jnp.zeros_like(st_ref)
        ctx_blocks = [slice(r0, r0 + CUM_BLK) for r0 in range(0, ginc_ref.shape[0], CUM_BLK)]
        for rs in (ctx_blocks[::-1] if reverse else ctx_blocks):
            for _ in _state_stages(st_ref, ginc_ref[rs, K_COLS], ginc_ref[rs, V_COLS],
                                   lahc_ref[rs, cols], lalc_ref[rs, cols], tri_ref[...],
                                   smask_ref[...], reverse=reverse):
                pass

    @pl.when(phase == 0)
    def _backward_states():
        @pl.when(j == 0)
        def _():
            ctx_state(bwd, triu_ref, True)

        blk = n_blocks - 1 - j
        _interleave([
            _state_stages(st_ref, kv_ref[rs, K_COLS], kv_ref[rs, V_COLS],
                          lahb_ref[rs, :], lalb_ref[rs, :], triu_ref[...], smask_ref[...],
                          reverse=True, store_ref=sb_ref,
                          store_base=blk * ch_per_blk + hi * ch_per_half)
            for hi, rs in reversed(list(enumerate(halves)))])

    @pl.when(phase == 1)
    def _outputs():
        @pl.when(j == 0)
        def _():
            ctx_state(fwd, tril_ref, False)

        consts = (tril_ref[...], triu_ref[...], kmask_ref[...], vmask_ref[...],
                  smask_ref[...], causal_ref[...], anti_ref[...], hmean_ref[...], gn_ref[...])

        def half(hi, rs):
            base = j * ch_per_blk + hi * ch_per_half

            def finish(val):
                o_ref[rs, :] = val

            return _output_stages(
                qg_ref[rs, 0:GQ], kv_ref[rs, K_COLS], kv_ref[rs, V_COLS],
                (lahf_ref[rs, :], lalf_ref[rs, :]), (lahb_ref[rs, :], lalb_ref[rs, :]),
                qg_ref[rs, GQ:KV_W], st_ref, lambda ci, g: sb_ref[base + ci, g], consts, finish)

        _interleave([half(hi, rs) for hi, rs in enumerate(halves)])


def _gla(gin, la_hi, la_lo, gin_ctx, la_hi_ctx, la_lo_ctx, gn_tiled, *, batch, seq, n_ctx):
    assert n_ctx % CUM_BLK == 0 and n_ctx // CHUNK <= 8 and GLA_TB // CHUNK <= 8
    tb = GLA_TB
    nb = seq // tb
    nch = seq // CHUNK
    r3 = lambda a, t: a.reshape(batch, t, a.shape[-1])
    consts = _gla_constants()

    def both(col):
        return lambda b, p, j: (b, jnp.where(p == 0, nb - 1 - j, j), col)

    def out_only(col):
        return lambda b, p, j: (b, jnp.where(p == 0, 0, j), col)

    ctx_map = lambda b, p, j: (b, 0, 0)
    const_map = lambda b, p, j: (0, 0)
    gin3, lah3, lal3 = r3(gin, seq), r3(la_hi, seq), r3(la_lo, seq)
    return pl.pallas_call(
        functools.partial(_gla_kernel, n_blocks=nb),
        out_shape=jax.ShapeDtypeStruct((batch, seq, GV), BF16),
        grid=(batch, 2, nb),
        in_specs=[pl.BlockSpec((None, tb, KV_W), both(0)),
                  pl.BlockSpec((None, tb, KV_W), out_only(1)),
                  pl.BlockSpec((None, tb, GQ), out_only(0)),
                  pl.BlockSpec((None, tb, GQ), both(1)),
                  pl.BlockSpec((None, tb, GQ), out_only(0)),
                  pl.BlockSpec((None, tb, GQ), both(1)),
                  pl.BlockSpec((None, n_ctx, GIN_W), ctx_map),
                  pl.BlockSpec((None, n_ctx, LA_W), ctx_map),
                  pl.BlockSpec((None, n_ctx, LA_W), ctx_map),
                  pl.BlockSpec((1, GV), const_map)]
                 + [pl.BlockSpec(cst.shape, const_map) for cst in consts],
        out_specs=pl.BlockSpec((None, tb, GV), lambda b, p, j: (b, jnp.where(p == 0, 0, j), 0)),
        scratch_shapes=[pltpu.VMEM((N_GRP, GRP_K, GRP_V), F32),
                        pltpu.VMEM((nch, N_GRP, GRP_K, GRP_V), BF16)],
        compiler_params=pltpu.CompilerParams(
            dimension_semantics=("arbitrary", "arbitrary", "arbitrary"),
            vmem_limit_bytes=VMEM_LIMIT),
        name="gla",
    )(gin3, gin3, lah3, lah3, lal3, lal3, r3(gin_ctx, n_ctx), r3(la_hi_ctx, n_ctx),
      r3(la_lo_ctx, n_ctx), gn_tiled, *consts)


FFN_CHUNK = 256
MIX_SLICES = 8


def _mix_ffn_kernel(att_ref, gla_ref, x_ref, ada_ref, gpm_ref, gpf_ref, gqf_ref,
                    woa_ref, wog_ref, wfi_ref, wfo_ref, o_ref, x1_ref, h_ref, y_ref, acc_ref, *,
                    tiles_per_batch, n_tiles):
    i = pl.program_id(0)
    tm = x_ref.shape[0]
    n_chunks = FFN_HIDDEN // FFN_CHUNK
    mix_rows = tm // MIX_SLICES

    def mix_matmul():
        y_ref[...] = _dot_tn(att_ref[...], woa_ref[...]) + _dot(gla_ref[...], wog_ref[...])

    def mix_rows_slice(slot, si, anchor):
        rs = slice(si * mix_rows, (si + 1) * mix_rows)
        row = jnp.minimum(i, n_tiles - 1) // tiles_per_batch
        ada = ada_ref[pl.ds(row, 1), :]
        bits = pltpu.bitcast(anchor, jnp.uint32)
        sixteen = jnp.uint32(16)
        zero = pltpu.bitcast(
            lax.shift_right_logical(lax.shift_right_logical(bits, sixteen), sixteen), F32)
        gt1 = ada[:, 2 * D:3 * D] + zero
        sh2 = ada[:, 3 * D:4 * D]
        sc2 = ada[:, 4 * D:5 * D]
        x1 = x_ref[rs, :] + gt1 * (_rms(y_ref[rs, :]) * gpm_ref[...])
        x1_ref[slot, rs, :] = x1
        h_ref[slot, rs, :] = ((_rms(x1) * gpf_ref[...]) * (1.0 + sc2) + sh2).astype(BF16)

    def ffn_chunk(slot, ci):
        c0 = ci * FFN_CHUNK
        h = h_ref[slot]
        g = _dot(h, wfi_ref[:, c0:c0 + FFN_CHUNK])
        u = _dot(h, wfi_ref[:, FFN_HIDDEN + c0:FFN_HIDDEN + c0 + FFN_CHUNK])
        a = (_silu(g) * u).astype(BF16)
        part = _dot(a, wfo_ref[c0:c0 + FFN_CHUNK, :])
        if ci == 0:
            acc_ref[...] = part
        else:
            acc_ref[...] += part
        return part[0:8, :]

    def ffn_finish(slot):
        row = jnp.maximum(i - 1, 0) // tiles_per_batch
        gt2 = ada_ref[pl.ds(row, 1), 5 * D:6 * D]
        o_ref[...] = x1_ref[slot] + gt2 * (_rms(acc_ref[...]) * gqf_ref[...])

    @pl.when(i == 0)
    def _():
        x1_ref[1] = jnp.zeros(x1_ref.shape[1:], F32)
        h_ref[1] = jnp.zeros(h_ref.shape[1:], BF16)

    for parity in range(2):
        @pl.when(i % 2 == parity)
        def _():
            for ci in range(n_chunks):
                anchor = ffn_chunk(1 - parity, ci)
                if ci == 0:
                    mix_matmul()
                elif ci - 1 < MIX_SLICES:
                    mix_rows_slice(parity, ci - 1, anchor[0:1, :])
            ffn_finish(1 - parity)


def _mix_ffn(att_t, gla, x2d, ada, g_post_mix, g_pre_ffn, g_post_ffn,
             w_out_a, w_out_g, w_ffn_in, w_ffn_out, *, tm, tiles_per_batch):
    n = x2d.shape[0]
    tpb = tiles_per_batch
    nt = n // tm
    const = lambda i: (0, 0)
    resident = functools.partial(pl.BlockSpec, index_map=const, pipeline_mode=pl.Buffered(1))
    cur = lambda i: jnp.minimum(i, nt - 1)
    done = lambda i: jnp.maximum(i - 1, 0)
    return pl.pallas_call(
        functools.partial(_mix_ffn_kernel, tiles_per_batch=tpb, n_tiles=nt),
        out_shape=jax.ShapeDtypeStruct((n, D), F32),
        grid=(nt + 1,),
        in_specs=[pl.BlockSpec((None, NQ, tm), lambda i: (cur(i) // tpb, 0, cur(i) % tpb)),
                  pl.BlockSpec((tm, GV), lambda i: (cur(i), 0)),
                  pl.BlockSpec((tm, D), lambda i: (cur(i), 0)),
                  pl.BlockSpec((8, 6 * D), const),
                  pl.BlockSpec((1, D), const),
                  pl.BlockSpec((1, D), const),
                  pl.BlockSpec((1, D), const),
                  resident((NQ, D)),
                  resident((GV, D)),
                  resident((D, 2 * FFN_HIDDEN)),
                  resident((FFN_HIDDEN, D))],
        out_specs=pl.BlockSpec((tm, D), lambda i: (done(i), 0)),
        scratch_shapes=[pltpu.VMEM((2, tm, D), F32),
                        pltpu.VMEM((2, tm, D), BF16),
                        pltpu.VMEM((tm, D), F32),
                        pltpu.VMEM((tm, D), F32)],
        compiler_params=pltpu.CompilerParams(dimension_semantics=("arbitrary",),
                                             vmem_limit_bytes=VMEM_LIMIT),
        name="mix_ffn",
    )(att_t, gla, x2d, ada, g_post_mix, g_pre_ffn, g_post_ffn,
      w_out_a, w_out_g, w_ffn_in, w_ffn_out)


def _rope_tables(seq):
    half = HEAD_DIM // 2
    inv_freq = ROPE_BASE ** (-jnp.arange(0, half, 2, dtype=F32) / half)
    dim = np.arange(HEAD_DIM)
    freq = jnp.tile(inv_freq, HEAD_DIM // ROPE_BLK)[:, None]
    pos = jnp.arange(seq)[None, :]
    row = (pos // GRID_W).astype(F32)
    col = (pos % GRID_W).astype(F32)
    ang = jnp.where((dim < half)[:, None], row, col) * freq
    cos, sin = jnp.cos(ang), jnp.sin(ang)
    even = ((dim // ROPE_BLK) % 2 == 0)[:, None]
    sa = jnp.where(even, -sin, 0.0)
    sb = jnp.where(even, 0.0, sin)
    return cos, sa, sb


def kernel(x, c, ctx, c_ctx, w_ada, b_ada, g_pre_mix, g_post_mix, g_pre_ffn, g_post_ffn,
           w_in, attn_sink, w_gate_fwd, b_gate_fwd, w_gate_bwd, b_gate_bwd, g_gla_norm,
           w_out, w_ffn_in, w_ffn_out):
    batch, seq, _ = x.shape
    n_ctx = ctx.shape[1]
    depth = w_ada.shape[0]
    assert depth == 1
    l = 0

    c8 = jnp.zeros((8, D), F32).at[0:batch].set(c).at[4].set(c_ctx)
    ada = _ada(c8, w_ada[l], b_ada[l][None, :])

    wi = w_in[l]
    g0 = NQ + 2 * NKV
    w_qkv_t = wi[:, 0:g0].T.astype(BF16)
    gk0 = g0 + GQ
    gg0 = gk0 + KV_W
    w_rest = jnp.concatenate(
        [wi[:, gk0:gg0], wi[:, g0:gk0], wi[:, gg0:],
         jnp.zeros((D, Z_W - 2 * GATE_RANK), F32)], axis=1).astype(BF16)
    w_gate_p = jnp.zeros((Z_W, LA_W), F32)
    w_gate_p = w_gate_p.at[0:GATE_RANK, 0:GQ].set(w_gate_fwd[l])
    w_gate_p = w_gate_p.at[GATE_RANK:2 * GATE_RANK, GQ:].set(w_gate_bwd[l]).astype(BF16)
    b_gate_p = jnp.concatenate([b_gate_fwd[l], b_gate_bwd[l]])[None, :]

    tm = 512
    tables = _rope_tables(seq)
    x2d = x.reshape(batch * seq, D)
    g_pre = g_pre_mix[l][None, :]
    qt, k, vt, gin, lah, lal = _inproj(x2d, ada, g_pre, w_qkv_t, w_rest, w_gate_p, b_gate_p,
                                      tables, batch=batch, tokens=seq, tm=tm)
    _, k_c, vt_c, gin_c, lah_c, lal_c = _inproj(
        ctx.reshape(batch * n_ctx, D), ada, g_pre, w_qkv_t, w_rest, w_gate_p, b_gate_p, None,
        batch=batch, tokens=n_ctx, tm=n_ctx)

    att_t = _attn(attn_sink[l], qt, k, vt, k_c, vt_c, batch=batch, seq=seq, n_ctx=n_ctx)

    gn_tiled = jnp.tile(g_gla_norm[l], GLA_HEADS)[None, :]
    gla = _gla(gin, lah, lal, gin_c, lah_c, lal_c, gn_tiled, batch=batch, seq=seq, n_ctx=n_ctx)

    wo = w_out[l]
    out = _mix_ffn(att_t, gla.reshape(batch * seq, GV), x2d, ada,
                   g_post_mix[l][None, :], g_pre_ffn[l][None, :], g_post_ffn[l][None, :],
                   wo[:NQ].astype(BF16), wo[NQ:].astype(BF16),
                   w_ffn_in[l].astype(BF16), w_ffn_out[l].astype(BF16),
                   tm=tm, tiles_per_batch=seq // tm)
    return out.reshape(batch, seq, D)
```

```python
import functools

import jax
import jax.numpy as jnp
import numpy as np
from jax import lax
from jax.experimental import pallas as pl
from jax.experimental.pallas import tpu as pltpu

D = 1024
HEAD_DIM = 64
ATT_HEADS = 8
ATT_KV_HEADS = 2
GROUP = ATT_HEADS // ATT_KV_HEADS
WINDOW = 128
GRID_W = 64
ROPE_BASE = 10000.0
ROPE_BLK = HEAD_DIM // 4
GLA_HEADS = 8
GLA_DK = 32
GLA_DV = 64
CHUNK = 64
GATE_RANK = 16
GATE_TAU = 16.0
FFN_HIDDEN = 2816
NEG_INF = -1e30
EPS = 1e-6
LOG2E = 1.4426950408889634
Q_SCALE = LOG2E * HEAD_DIM ** -0.5
LA_SCALE = LOG2E / GATE_TAU

NQ = ATT_HEADS * HEAD_DIM
NKV = ATT_KV_HEADS * HEAD_DIM
GQ = GLA_HEADS * GLA_DK
GV = GLA_HEADS * GLA_DV
GIN_W = 2 * GQ + 2 * GV
KV_W = GQ + GV
K_COLS = slice(0, GQ)
V_COLS = slice(GQ, KV_W)
Q_COLS = slice(KV_W, KV_W + GQ)
G_COLS = slice(KV_W + GQ, GIN_W)
Z_W = 128
LA_W = 2 * GQ
REST_W = GIN_W + Z_W

LANES = 128
VMEM_LIMIT = 56 * 1024 * 1024

BF16 = jnp.bfloat16
F32 = jnp.float32


def _dot(a, b):
    return jnp.dot(a, b, preferred_element_type=F32)


def _dot_nt(a, b):
    return lax.dot_general(a, b, (((1,), (1,)), ((), ())), preferred_element_type=F32)


def _dot_tn(a, b):
    return lax.dot_general(a, b, (((0,), (0,)), ((), ())), preferred_element_type=F32)


def _rms(x):
    return x * lax.rsqrt(jnp.mean(x * x, axis=-1, keepdims=True) + EPS)


def _silu(x):
    return x * (1.0 / (1.0 + jnp.exp(-x)))


def _ada_kernel(c_ref, w_ref, b_ref, o_ref):
    a = _silu(c_ref[...]).astype(BF16)
    o_ref[...] = _dot(a, w_ref[...].astype(BF16)) + b_ref[...]


def _ada(c8, w_ada, b_ada):
    n = w_ada.shape[1]
    bn = 1024
    return pl.pallas_call(
        _ada_kernel,
        out_shape=jax.ShapeDtypeStruct((8, n), F32),
        grid=(n // bn,),
        in_specs=[pl.BlockSpec((8, D), lambda j: (0, 0)),
                  pl.BlockSpec((D, bn), lambda j: (0, j)),
                  pl.BlockSpec((1, bn), lambda j: (0, j))],
        out_specs=pl.BlockSpec((8, bn), lambda j: (0, j)),
        compiler_params=pltpu.CompilerParams(dimension_semantics=("arbitrary",)),
        name="ada",
    )(c8, w_ada, b_ada)


NORM_ROWS = 128


def _inproj_kernel(*refs, tiles_per_batch, rope):
    if rope:
        (x_ref, ada_ref, g_ref, wqkv_ref, w_ref, wg_ref, bg_ref,
         cost_ref, sat_ref, sbt_ref,
         qt_ref, k_ref, vt_ref, gin_ref, lah_ref, lal_ref) = refs
        row = pl.program_id(0) // tiles_per_batch
    else:
        (x_ref, ada_ref, g_ref, wqkv_ref, w_ref, wg_ref, bg_ref,
         qt_ref, k_ref, vt_ref, gin_ref, lah_ref, lal_ref) = refs
        row = 4
    ada = ada_ref[pl.ds(row, 1), :]
    sh1 = ada[:, 0:D]
    sc1 = ada[:, D:2 * D]
    tm = x_ref.shape[0]
    hb = jnp.concatenate(
        [((_rms(x_ref[r0:r0 + NORM_ROWS, :]) * g_ref[...]) * (1.0 + sc1) + sh1).astype(BF16)
         for r0 in range(0, tm, NORM_ROWS)], axis=0)

    z = _dot(hb, w_ref[:, GIN_W:REST_W])
    qkv = _dot_nt(wqkv_ref[...], hb)
    logits = _dot(z.astype(BF16), wg_ref[...]) + bg_ref[...]
    gate = _dot(hb, w_ref[:, G_COLS])
    gin = _dot(hb, w_ref[:, 0:KV_W + GQ])

    def head_t(hd):
        xh = qkv[hd * HEAD_DIM:(hd + 1) * HEAD_DIM]
        if not rope:
            return xh
        up = jnp.concatenate([xh[ROPE_BLK:], xh[:ROPE_BLK]], axis=0)
        dn = jnp.concatenate([xh[HEAD_DIM - ROPE_BLK:], xh[:HEAD_DIM - ROPE_BLK]], axis=0)
        return xh * cost_ref[...] + up * sat_ref[...] + dn * sbt_ref[...]

    for hd in range(ATT_HEADS):
        qh = head_t(hd)
        if rope:
            qh = qh * Q_SCALE
        qt_ref[hd * HEAD_DIM:(hd + 1) * HEAD_DIM, :] = qh.astype(BF16)
    kt = jnp.concatenate([head_t(ATT_HEADS + hd) for hd in range(ATT_KV_HEADS)], axis=0)
    k_ref[...] = kt.T.astype(BF16)
    vt_ref[...] = qkv[NQ + NKV:NQ + 2 * NKV].astype(BF16)

    log_sig = jnp.minimum(logits, 0.0) - jnp.log(1.0 + jnp.exp(-jnp.abs(logits)))
    la = log_sig * LA_SCALE
    hi = la.astype(BF16)
    lah_ref[...] = hi
    lal_ref[...] = (la - hi.astype(F32)).astype(BF16)

    gin_ref[:, G_COLS] = _silu(gate).astype(BF16)
    gin_ref[:, 0:KV_W] = gin[:, 0:KV_W].astype(BF16)
    gin_ref[:, Q_COLS] = (gin[:, Q_COLS] * (GLA_DK ** -0.5)).astype(BF16)


def _inproj(x2d, ada, g_pre, w_qkv_t, w_rest, w_gate_p, b_gate_p, tables, *,
            batch, tokens, tm):
    n = x2d.shape[0]
    tpb = tokens // tm
    rope = tables is not None
    const = lambda i: (0, 0)
    in_specs = [pl.BlockSpec((tm, D), lambda i: (i, 0)),
                pl.BlockSpec((8, 6 * D), const),
                pl.BlockSpec((1, D), const),
                pl.BlockSpec((NQ + 2 * NKV, D), const),
                pl.BlockSpec((D, REST_W), const),
                pl.BlockSpec((Z_W, LA_W), const),
                pl.BlockSpec((1, LA_W), const)]
    args = [x2d, ada, g_pre, w_qkv_t, w_rest, w_gate_p, b_gate_p]
    if rope:
        in_specs += [pl.BlockSpec((HEAD_DIM, tm), lambda i: (0, i % tpb))] * 3
        args += list(tables)
    row_blk = lambda w: pl.BlockSpec((tm, w), lambda i: (i, 0))
    col_blk = lambda w: pl.BlockSpec((None, w, tm), lambda i: (i // tpb, 0, i % tpb))
    return pl.pallas_call(
        functools.partial(_inproj_kernel, tiles_per_batch=tpb, rope=rope),
        out_shape=(jax.ShapeDtypeStruct((batch, NQ, tokens), BF16),
                   jax.ShapeDtypeStruct((n, NKV), BF16),
                   jax.ShapeDtypeStruct((batch, NKV, tokens), BF16),
                   jax.ShapeDtypeStruct((n, GIN_W), BF16),
                   jax.ShapeDtypeStruct((n, LA_W), BF16),
                   jax.ShapeDtypeStruct((n, LA_W), BF16)),
        grid=(n // tm,),
        in_specs=in_specs,
        out_specs=(col_blk(NQ), row_blk(NKV), col_blk(NKV),
                   row_blk(GIN_W), row_blk(LA_W), row_blk(LA_W)),
        compiler_params=pltpu.CompilerParams(dimension_semantics=("arbitrary",),
                                             vmem_limit_bytes=VMEM_LIMIT),
        name="inproj_lat" if rope else "inproj_ctx",
    )(*args)


HEADS_PER_DOT = 2


ATT_SUB = 8
ONES_ROWS = 16
ATT_WAVE = 32


def _attn_kernel(sink_ref, qt_ref, kp_ref, kc_ref, kn_ref, kx_ref,
                 vp_ref, vc_ref, vn_ref, vx_ref, o_ref, *, n_steps):
    i = pl.program_id(1)
    bq = WINDOW
    nw = 3 * bq
    nq = HEADS_PER_DOT * bq
    c = lax.broadcasted_iota(jnp.int32, (nw, nq), 0)
    r = lax.broadcasted_iota(jnp.int32, (nw, nq), 1) % bq
    rel = c - bq - r
    in_win = (rel <= WINDOW) & (rel >= -WINDOW)
    lane = lax.broadcasted_iota(jnp.int32, (1, nq), 1)
    zeros = jnp.zeros((HEAD_DIM, nq), BF16)

    k_blocks = [kp_ref[...]] + [kc_ref[s * bq:(s + 1) * bq] for s in range(ATT_SUB)] + [kn_ref[...]]
    v_blocks = ([vp_ref[...]] + [vc_ref[:, s * bq:(s + 1) * bq] for s in range(ATT_SUB)]
                + [vn_ref[...]])
    kx = kx_ref[...]
    vx = vx_ref[...]

    subs = []
    for s in range(ATT_SUB):
        keep = in_win
        if s == 0:
            keep = keep & ((c >= bq) | (i > 0))
        if s == ATT_SUB - 1:
            keep = keep & ((c < 2 * bq) | (i < n_steps - 1))
        kcat = jnp.concatenate(k_blocks[s:s + 3] + [kx], axis=0)
        vtcat = jnp.concatenate(v_blocks[s:s + 3] + [vx], axis=1)
        ones = jnp.ones((ONES_ROWS, vtcat.shape[1]), BF16)
        vaug = [jnp.concatenate([vtcat[kvh * HEAD_DIM:(kvh + 1) * HEAD_DIM], ones], axis=0)
                for kvh in range(ATT_KV_HEADS)]
        subs.append((keep, kcat, vaug))

    tiles = [(s, kvh, kvh * GROUP + pr * HEADS_PER_DOT)
             for s in range(ATT_SUB) for kvh in range(ATT_KV_HEADS)
             for pr in range(GROUP // HEADS_PER_DOT)]

    def score(tile):
        s, kvh, h0 = tile
        qt = jnp.concatenate(
            [qt_ref[(h0 + u) * HEAD_DIM:(h0 + u + 1) * HEAD_DIM, s * bq:(s + 1) * bq]
             for u in range(HEADS_PER_DOT)], axis=1)
        rhs = jnp.concatenate([qt, zeros] if kvh == 0 else [zeros, qt], axis=0)
        return _dot(subs[s][1], rhs)

    def softmax_numerators(tile, st):
        s, kvh, h0 = tile
        keep = subs[s][0]
        parts = [jnp.where(keep[0:bq], st[0:bq], NEG_INF), st[bq:2 * bq],
                 jnp.where(keep[2 * bq:nw], st[2 * bq:nw], NEG_INF), st[nw:]]
        sink = jnp.where(lane < bq, sink_ref[h0], sink_ref[h0 + 1]) * LOG2E
        m = sink
        for p in parts:
            m = jnp.maximum(m, jnp.max(p, axis=0, keepdims=True))
        et = jnp.concatenate([jnp.exp2(p - m).astype(BF16) for p in parts], axis=0)
        return et, jnp.exp2(sink - m)

    def values(tile, et, e_sink):
        s, kvh, h0 = tile
        ot = _dot(subs[s][2][kvh], et)
        denom = ot[HEAD_DIM:HEAD_DIM + 1] + e_sink
        on = ot[0:HEAD_DIM] * (1.0 / denom)
        for u in range(HEADS_PER_DOT):
            o_ref[(h0 + u) * HEAD_DIM:(h0 + u + 1) * HEAD_DIM, s * bq:(s + 1) * bq] = (
                on[:, u * bq:(u + 1) * bq].astype(BF16))

    waves = [tiles[w0:w0 + ATT_WAVE] for w0 in range(0, len(tiles), ATT_WAVE)]
    scores, probs = {}, {}
    for w in range(len(waves) + 2):
        if w < len(waves):
            scores[w] = [score(t) for t in waves[w]]
        if 0 <= w - 1 < len(waves):
            probs[w - 1] = [softmax_numerators(t, st)
                            for t, st in zip(waves[w - 1], scores.pop(w - 1))]
        if 0 <= w - 2 < len(waves):
            for t, (et, e_sink) in zip(waves[w - 2], probs.pop(w - 2)):
                values(t, et, e_sink)


def _attn(sink, qt, k, vt, k_ctx, vt_ctx, *, batch, seq, n_ctx):
    bq = WINDOW
    bs = ATT_SUB * bq
    nb = seq // bq
    ns = seq // bs
    k3 = k.reshape(batch, seq, NKV)
    kx3 = k_ctx.reshape(batch, n_ctx, NKV)
    prev = lambda i: jnp.maximum(ATT_SUB * i - 1, 0)
    nxt = lambda i: jnp.minimum(ATT_SUB * (i + 1), nb - 1)
    return pl.pallas_call(
        functools.partial(_attn_kernel, n_steps=ns),
        out_shape=jax.ShapeDtypeStruct((batch, NQ, seq), BF16),
        grid=(batch, ns),
        in_specs=[
            pl.BlockSpec(memory_space=pltpu.SMEM),
            pl.BlockSpec((None, NQ, bs), lambda b, i: (b, 0, i)),
            pl.BlockSpec((None, bq, NKV), lambda b, i: (b, prev(i), 0)),
            pl.BlockSpec((None, bs, NKV), lambda b, i: (b, i, 0)),
            pl.BlockSpec((None, bq, NKV), lambda b, i: (b, nxt(i), 0)),
            pl.BlockSpec((None, n_ctx, NKV), lambda b, i: (b, 0, 0)),
            pl.BlockSpec((None, NKV, bq), lambda b, i: (b, 0, prev(i))),
            pl.BlockSpec((None, NKV, bs), lambda b, i: (b, 0, i)),
            pl.BlockSpec((None, NKV, bq), lambda b, i: (b, 0, nxt(i))),
            pl.BlockSpec((None, NKV, n_ctx), lambda b, i: (b, 0, 0)),
        ],
        out_specs=pl.BlockSpec((None, NQ, bs), lambda b, i: (b, 0, i)),
        compiler_params=pltpu.CompilerParams(dimension_semantics=("arbitrary", "arbitrary"),
                                             vmem_limit_bytes=VMEM_LIMIT),
        name="attn",
    )(sink, qt, k3, k3, k3, kx3, vt, vt, vt, vt_ctx)


GRP_K = 4 * GLA_DK
GRP_V = 4 * GLA_DV
N_GRP = GLA_HEADS // 4
GLA_TB = 512
CUM_BLK = 256


def _gla_constants():
    t = CUM_BLK
    r = np.arange(t)[:, None]
    c = np.arange(t)[None, :]
    same_chunk = (r // CHUNK) == (c // CHUNK)
    tri_l = (same_chunk & (c <= r)).astype(np.float32)
    tri_u = (same_chunk & (c >= r)).astype(np.float32)
    rk = np.arange(4 * CHUNK)[:, None] // CHUNK
    kmask = (rk == (np.arange(GRP_K)[None, :] // GLA_DK)).astype(np.float32)
    vmask = (rk == (np.arange(GRP_V)[None, :] // GLA_DV)).astype(np.float32)
    smask = ((np.arange(GRP_K)[:, None] // GLA_DK)
             == (np.arange(GRP_V)[None, :] // GLA_DV)).astype(np.float32)
    ai = np.arange(CHUNK)[:, None]
    aj = np.arange(4 * CHUNK)[None, :] % CHUNK
    causal = (aj <= ai).astype(np.float32)
    anti = (aj >= ai).astype(np.float32)
    hmean = ((np.arange(GRP_V)[:, None] // GLA_DV)
             == (np.arange(GRP_V)[None, :] // GLA_DV)).astype(np.float32) / GLA_DV
    return (jnp.asarray(tri_l, BF16), jnp.asarray(tri_u, BF16), jnp.asarray(kmask, BF16),
            jnp.asarray(vmask, BF16), jnp.asarray(smask, F32), jnp.asarray(causal, F32),
            jnp.asarray(anti, F32), jnp.asarray(hmean, BF16))


def _gk(g):
    return slice(g * GRP_K, (g + 1) * GRP_K)


def _gv(g):
    return slice(g * GRP_V, (g + 1) * GRP_V)


def _chunk_slices(t):
    return [slice(ci * CHUNK, (ci + 1) * CHUNK) for ci in range(t // CHUNK)]


def _cum_and_totals(la_hi, la_lo, tri, *, reverse):
    parts = [slice(p, p + CUM_BLK) for p in range(0, la_hi.shape[0], CUM_BLK)]
    cum = jnp.concatenate([_dot(tri, la_hi[p]) + _dot(tri, la_lo[p]) for p in parts], axis=0)
    edge = 0 if reverse else CHUNK - 1
    tots = [cum[sl.start + edge:sl.start + edge + 1] for sl in _chunk_slices(cum.shape[0])]
    return cum, tots


def _bcast_chunks(rows, width):
    return jnp.concatenate([jnp.broadcast_to(r, (CHUNK, width)) for r in rows], axis=0)


def _decay(x, log2_factor):
    return x * jnp.exp2(log2_factor).astype(BF16)


def _state_increments(kd, v, smask):
    return [[_dot_tn(kd[sl, _gk(g)], v[sl, _gv(g)]) * smask for g in range(N_GRP)]
            for sl in _chunk_slices(kd.shape[0])]


def _decay_columns(tots):
    width = tots[0].shape[1]
    row = lax.broadcasted_iota(jnp.int32, (8, width), 0)
    rows = jnp.zeros((8, width), F32)
    for ci, t in enumerate(tots):
        rows = jnp.where(row == ci, jnp.broadcast_to(t, (8, width)), rows)
    return jnp.exp2(rows).T


def _interleave(stage_gens):
    live = list(stage_gens)
    next(live[0], None)
    while live:
        for gen in list(live):
            try:
                next(gen)
            except StopIteration:
                live.remove(gen)


def _state_stages(st_ref, k, v, la_hi, la_lo, tri, smask, *, reverse,
                  store_ref=None, store_base=None):
    nch = k.shape[0] // CHUNK
    cum, tots = _cum_and_totals(la_hi, la_lo, tri, reverse=reverse)
    yield
    kd = _decay(k, _bcast_chunks(tots, GQ) - cum)
    dcol = _decay_columns(tots)
    yield
    ds = _state_increments(kd, v, smask)
    yield
    state = [st_ref[g] for g in range(N_GRP)]
    for ci in (range(nch - 1, -1, -1) if reverse else range(nch)):
        for g in range(N_GRP):
            if store_ref is not None:
                store_ref[store_base + ci, g] = state[g].astype(BF16)
            state[g] = state[g] * dcol[_gk(g), ci:ci + 1] + ds[ci][g]
    for g in range(N_GRP):
        st_ref[g] = state[g]


def _output_stages(q, k, v, la_f, la_b, gate, st_ref, bwd_state, consts, finish):
    tri_l, tri_u, kmask, vmask, smask, causal, anti, hmean, gn = consts
    chunks = _chunk_slices(q.shape[0])

    cum_f, tots_f = _cum_and_totals(la_f[0], la_f[1], tri_l, reverse=False)
    cum_b, tots_b = _cum_and_totals(la_b[0], la_b[1], tri_u, reverse=True)
    yield

    qd_f = _decay(q, cum_f)
    qd_b = _decay(q, cum_b)
    ki_f = _decay(k, -cum_f)
    ki_b = _decay(k, -cum_b)
    kd_f = _decay(k, _bcast_chunks(tots_f, GQ) - cum_f)
    dcol_f = _decay_columns(tots_f)
    v_bd = [[jnp.concatenate([v[sl, _gv(g)]] * 4, axis=0) * vmask for g in range(N_GRP)]
            for sl in chunks]
    yield

    def scores(qd, ki, sl, g):
        ki_bd = jnp.concatenate([ki[sl, _gk(g)]] * 4, axis=0) * kmask
        return _dot_nt(qd[sl, _gk(g)], ki_bd)

    a_f = [[scores(qd_f, ki_f, sl, g) for g in range(N_GRP)] for sl in chunks]
    a_b = [[scores(qd_b, ki_b, sl, g) for g in range(N_GRP)] for sl in chunks]
    ds = _state_increments(kd_f, v, smask)
    yield

    a = [[(af[g] * causal + ab[g] * anti).astype(BF16) for g in range(N_GRP)]
         for af, ab in zip(a_f, a_b)]
    state = [st_ref[g] for g in range(N_GRP)]
    st_in = []
    for ci in range(len(chunks)):
        st_in.append([s.astype(BF16) for s in state])
        state = [state[g] * dcol_f[_gk(g), ci:ci + 1] + ds[ci][g] for g in range(N_GRP)]
    for g in range(N_GRP):
        st_ref[g] = state[g]
    yield

    outs = []
    for ci, sl in enumerate(chunks):
        parts = []
        for g in range(N_GRP):
            intra = _dot(a[ci][g], v_bd[ci][g])
            qd2 = jnp.concatenate([qd_f[sl, _gk(g)], qd_b[sl, _gk(g)]], axis=1)
            st2 = jnp.concatenate([st_in[ci][g], bwd_state(ci, g)], axis=0)
            parts.append(intra + _dot(qd2, st2))
        outs.append(jnp.concatenate(parts, axis=1))
    o = jnp.concatenate(outs, axis=0)
    yield

    o2 = (o * o).astype(BF16)
    ms = jnp.concatenate([_dot(o2[:, _gv(g)], hmean) for g in range(N_GRP)], axis=1)
    yield
    finish((o * lax.rsqrt(ms + EPS) * gn * gate.astype(F32)).astype(BF16))


def _gla_kernel(kv_ref, qg_ref, lahf_ref, lahb_ref, lalf_ref, lalb_ref,
                ginc_ref, lahc_ref, lalc_ref, gn_ref,
                tril_ref, triu_ref, kmask_ref, vmask_ref, smask_ref, causal_ref, anti_ref,
                hmean_ref, o_ref, st_ref, sb_ref, *, n_blocks):
    phase = pl.program_id(1)
    j = pl.program_id(2)
    ch_per_blk = GLA_TB // CHUNK
    fwd = slice(0, GQ)
    bwd = slice(GQ, 2 * GQ)
    halves = [slice(r0, r0 + CUM_BLK) for r0 in range(0, GLA_TB, CUM_BLK)]
    ch_per_half = CUM_BLK // CHUNK

    def ctx_state(cols, tri_ref, reverse):
        st_ref[...] = jnp.zeros_like(st_ref)
        ctx_blocks = [slice(r0, r0 + CUM_BLK) for r0 in range(0, ginc_ref.shape[0], CUM_BLK)]
        for rs in (ctx_blocks[::-1] if reverse else ctx_blocks):
            for _ in _state_stages(st_ref, ginc_ref[rs, K_COLS], ginc_ref[rs, V_COLS],
                                   lahc_ref[rs, cols], lalc_ref[rs, cols], tri_ref[...],
                                   smask_ref[...], reverse=reverse):
                pass

    @pl.when(phase == 0)
    def _backward_states():
        @pl.when(j == 0)
        def _():
            ctx_state(bwd, triu_ref, True)

        blk = n_blocks - 1 - j
        _interleave([
            _state_stages(st_ref, kv_ref[rs, K_COLS], kv_ref[rs, V_COLS],
                          lahb_ref[rs, :], lalb_ref[rs, :], triu_ref[...], smask_ref[...],
                          reverse=True, store_ref=sb_ref,
                          store_base=blk * ch_per_blk + hi * ch_per_half)
            for hi, rs in reversed(list(enumerate(halves)))])

    @pl.when(phase == 1)
    def _outputs():
        @pl.when(j == 0)
        def _():
            ctx_state(fwd, tril_ref, False)

        consts = (tril_ref[...], triu_ref[...], kmask_ref[...], vmask_ref[...],
                  smask_ref[...], causal_ref[...], anti_ref[...], hmean_ref[...], gn_ref[...])

        def half(hi, rs):
            base = j * ch_per_blk + hi * ch_per_half

            def finish(val):
                o_ref[rs, :] = val

            return _output_stages(
                qg_ref[rs, 0:GQ], kv_ref[rs, K_COLS], kv_ref[rs, V_COLS],
                (lahf_ref[rs, :], lalf_ref[rs, :]), (lahb_ref[rs, :], lalb_ref[rs, :]),
                qg_ref[rs, GQ:KV_W], st_ref, lambda ci, g: sb_ref[base + ci, g], consts, finish)

        _interleave([half(hi, rs) for hi, rs in enumerate(halves)])


def _gla(gin, la_hi, la_lo, gin_ctx, la_hi_ctx, la_lo_ctx, gn_tiled, *, batch, seq, n_ctx):
    assert n_ctx % CUM_BLK == 0 and n_ctx // CHUNK <= 8 and GLA_TB // CHUNK <= 8
    tb = GLA_TB
    nb = seq // tb
    nch = seq // CHUNK
    r3 = lambda a, t: a.reshape(batch, t, a.shape[-1])
    consts = _gla_constants()

    def both(col):
        return lambda b, p, j: (b, jnp.where(p == 0, nb - 1 - j, j), col)

    def out_only(col):
        return lambda b, p, j: (b, jnp.where(p == 0, 0, j), col)

    ctx_map = lambda b, p, j: (b, 0, 0)
    const_map = lambda b, p, j: (0, 0)
    gin3, lah3, lal3 = r3(gin, seq), r3(la_hi, seq), r3(la_lo, seq)
    return pl.pallas_call(
        functools.partial(_gla_kernel, n_blocks=nb),
        out_shape=jax.ShapeDtypeStruct((batch, seq, GV), BF16),
        grid=(batch, 2, nb),
        in_specs=[pl.BlockSpec((None, tb, KV_W), both(0)),
                  pl.BlockSpec((None, tb, KV_W), out_only(1)),
                  pl.BlockSpec((None, tb, GQ), out_only(0)),
                  pl.BlockSpec((None, tb, GQ), both(1)),
                  pl.BlockSpec((None, tb, GQ), out_only(0)),
                  pl.BlockSpec((None, tb, GQ), both(1)),
                  pl.BlockSpec((None, n_ctx, GIN_W), ctx_map),
                  pl.BlockSpec((None, n_ctx, LA_W), ctx_map),
                  pl.BlockSpec((None, n_ctx, LA_W), ctx_map),
                  pl.BlockSpec((1, GV), const_map)]
                 + [pl.BlockSpec(cst.shape, const_map) for cst in consts],
        out_specs=pl.BlockSpec((None, tb, GV), lambda b, p, j: (b, jnp.where(p == 0, 0, j), 0)),
        scratch_shapes=[pltpu.VMEM((N_GRP, GRP_K, GRP_V), F32),
                        pltpu.VMEM((nch, N_GRP, GRP_K, GRP_V), BF16)],
        compiler_params=pltpu.CompilerParams(
            dimension_semantics=("arbitrary", "arbitrary", "arbitrary"),
            vmem_limit_bytes=VMEM_LIMIT),
        name="gla",
    )(gin3, gin3, lah3, lah3, lal3, lal3, r3(gin_ctx, n_ctx), r3(la_hi_ctx, n_ctx),
      r3(la_lo_ctx, n_ctx), gn_tiled, *consts)


FFN_CHUNK = 256
MIX_SLICES = 8


def _mix_ffn_kernel(att_ref, gla_ref, x_ref, ada_ref, gpm_ref, gpf_ref, gqf_ref,
                    woa_ref, wog_ref, wfi_ref, wfo_ref, o_ref, x1_ref, h_ref, y_ref, acc_ref, *,
                    tiles_per_batch, n_tiles):
    i = pl.program_id(0)
    tm = x_ref.shape[0]
    n_chunks = FFN_HIDDEN // FFN_CHUNK
    mix_rows = tm // MIX_SLICES

    def mix_matmul():
        y_ref[...] = _dot_tn(att_ref[...], woa_ref[...]) + _dot(gla_ref[...], wog_ref[...])

    def mix_rows_slice(slot, si, anchor=None):
        rs = slice(si * mix_rows, (si + 1) * mix_rows)
        row = jnp.minimum(i, n_tiles - 1) // tiles_per_batch
        ada = ada_ref[pl.ds(row, 1), :]
        gt1 = ada[:, 2 * D:3 * D]
        if anchor is not None:
            bits = pltpu.bitcast(anchor, jnp.uint32)
            sixteen = jnp.uint32(16)
            gt1 = gt1 + pltpu.bitcast(
                lax.shift_right_logical(lax.shift_right_logical(bits, sixteen), sixteen), F32)
        sh2 = ada[:, 3 * D:4 * D]
        sc2 = ada[:, 4 * D:5 * D]
        x1 = x_ref[rs, :] + gt1 * (_rms(y_ref[rs, :]) * gpm_ref[...])
        x1_ref[slot, rs, :] = x1
        h_ref[slot, rs, :] = ((_rms(x1) * gpf_ref[...]) * (1.0 + sc2) + sh2).astype(BF16)

    def ffn_chunk(slot, ci):
        c0 = ci * FFN_CHUNK
        h = h_ref[slot]
        g = _dot(h, wfi_ref[:, c0:c0 + FFN_CHUNK])
        u = _dot(h, wfi_ref[:, FFN_HIDDEN + c0:FFN_HIDDEN + c0 + FFN_CHUNK])
        a = (_silu(g) * u).astype(BF16)
        part = _dot(a, wfo_ref[c0:c0 + FFN_CHUNK, :])
        if ci == 0:
            acc_ref[...] = part
        else:
            acc_ref[...] += part
        return part[0:8, :]

    def ffn_finish(slot):
        row = jnp.maximum(i - 1, 0) // tiles_per_batch
        gt2 = ada_ref[pl.ds(row, 1), 5 * D:6 * D]
        o_ref[...] = x1_ref[slot] + gt2 * (_rms(acc_ref[...]) * gqf_ref[...])

    @pl.when(i == 0)
    def _():
        mix_matmul()
        for si in range(MIX_SLICES):
            mix_rows_slice(0, si)

    for parity in range(2):
        @pl.when((i % 2 == parity) & (i > 0))
        def _():
            for ci in range(n_chunks):
                anchor = ffn_chunk(1 - parity, ci)
                if ci == 0:
                    mix_matmul()
                elif ci - 1 < MIX_SLICES:
                    mix_rows_slice(parity, ci - 1, anchor[0:1, :])
            ffn_finish(1 - parity)


def _mix_ffn(att_t, gla, x2d, ada, g_post_mix, g_pre_ffn, g_post_ffn,
             w_out_a, w_out_g, w_ffn_in, w_ffn_out, *, tm, tiles_per_batch):
    n = x2d.shape[0]
    tpb = tiles_per_batch
    nt = n // tm
    const = lambda i: (0, 0)
    resident = functools.partial(pl.BlockSpec, index_map=const, pipeline_mode=pl.Buffered(1))
    cur = lambda i: jnp.minimum(i, nt - 1)
    done = lambda i: jnp.maximum(i - 1, 0)
    return pl.pallas_call(
        functools.partial(_mix_ffn_kernel, tiles_per_batch=tpb, n_tiles=nt),
        out_shape=jax.ShapeDtypeStruct((n, D), F32),
        grid=(nt + 1,),
        in_specs=[pl.BlockSpec((None, NQ, tm), lambda i: (cur(i) // tpb, 0, cur(i) % tpb)),
                  pl.BlockSpec((tm, GV), lambda i: (cur(i), 0)),
                  pl.BlockSpec((tm, D), lambda i: (cur(i), 0)),
                  pl.BlockSpec((8, 6 * D), const),
                  pl.BlockSpec((1, D), const),
                  pl.BlockSpec((1, D), const),
                  pl.BlockSpec((1, D), const),
                  resident((NQ, D)),
                  resident((GV, D)),
                  resident((D, 2 * FFN_HIDDEN)),
                  resident((FFN_HIDDEN, D))],
        out_specs=pl.BlockSpec((tm, D), lambda i: (done(i), 0)),
        scratch_shapes=[pltpu.VMEM((2, tm, D), F32),
                        pltpu.VMEM((2, tm, D), BF16),
                        pltpu.VMEM((tm, D), F32),
                        pltpu.VMEM((tm, D), F32)],
        compiler_params=pltpu.CompilerParams(dimension_semantics=("arbitrary",),
                                             vmem_limit_bytes=VMEM_LIMIT),
        name="mix_ffn",
    )(att_t, gla, x2d, ada, g_post_mix, g_pre_ffn, g_post_ffn,
      w_out_a, w_out_g, w_ffn_in, w_ffn_out)


def _rope_tables(seq):
    half = HEAD_DIM // 2
    inv_freq = ROPE_BASE ** (-jnp.arange(0, half, 2, dtype=F32) / half)
    dim = np.arange(HEAD_DIM)
    freq = jnp.tile(inv_freq, HEAD_DIM // ROPE_BLK)[:, None]
    pos = jnp.arange(seq)[None, :]
    row = (pos // GRID_W).astype(F32)
    col = (pos % GRID_W).astype(F32)
    ang = jnp.where((dim < half)[:, None], row, col) * freq
    cos, sin = jnp.cos(ang), jnp.sin(ang)
    even = ((dim // ROPE_BLK) % 2 == 0)[:, None]
    sa = jnp.where(even, -sin, 0.0)
    sb = jnp.where(even, 0.0, sin)
    return cos, sa, sb


def kernel(x, c, ctx, c_ctx, w_ada, b_ada, g_pre_mix, g_post_mix, g_pre_ffn, g_post_ffn,
           w_in, attn_sink, w_gate_fwd, b_gate_fwd, w_gate_bwd, b_gate_bwd, g_gla_norm,
           w_out, w_ffn_in, w_ffn_out):
    batch, seq, _ = x.shape
    n_ctx = ctx.shape[1]
    depth = w_ada.shape[0]
    assert depth == 1
    l = 0

    c8 = jnp.zeros((8, D), F32).at[0:batch].set(c).at[4].set(c_ctx)
    ada = _ada(c8, w_ada[l], b_ada[l][None, :])

    wi = w_in[l]
    g0 = NQ + 2 * NKV
    w_qkv_t = wi[:, 0:g0].T.astype(BF16)
    gk0 = g0 + GQ
    gg0 = gk0 + KV_W
    w_rest = jnp.concatenate(
        [wi[:, gk0:gg0], wi[:, g0:gk0], wi[:, gg0:],
         jnp.zeros((D, Z_W - 2 * GATE_RANK), F32)], axis=1).astype(BF16)
    w_gate_p = jnp.zeros((Z_W, LA_W), F32)
    w_gate_p = w_gate_p.at[0:GATE_RANK, 0:GQ].set(w_gate_fwd[l])
    w_gate_p = w_gate_p.at[GATE_RANK:2 * GATE_RANK, GQ:].set(w_gate_bwd[l]).astype(BF16)
    b_gate_p = jnp.concatenate([b_gate_fwd[l], b_gate_bwd[l]])[None, :]

    tm = 512
    tables = _rope_tables(seq)
    x2d = x.reshape(batch * seq, D)
    g_pre = g_pre_mix[l][None, :]
    qt, k, vt, gin, lah, lal = _inproj(x2d, ada, g_pre, w_qkv_t, w_rest, w_gate_p, b_gate_p,
                                      tables, batch=batch, tokens=seq, tm=tm)
    _, k_c, vt_c, gin_c, lah_c, lal_c = _inproj(
        ctx.reshape(batch * n_ctx, D), ada, g_pre, w_qkv_t, w_rest, w_gate_p, b_gate_p, None,
        batch=batch, tokens=n_ctx, tm=n_ctx)

    att_t = _attn(attn_sink[l], qt, k, vt, k_c, vt_c, batch=batch, seq=seq, n_ctx=n_ctx)

    gn_tiled = jnp.tile(g_gla_norm[l], GLA_HEADS)[None, :]
    gla = _gla(gin, lah, lal, gin_c, lah_c, lal_c, gn_tiled, batch=batch, seq=seq, n_ctx=n_ctx)

    wo = w_out[l]
    out = _mix_ffn(att_t, gla.reshape(batch * seq, GV), x2d, ada,
                   g_post_mix[l][None, :], g_pre_ffn[l][None, :], g_post_ffn[l][None, :],
                   wo[:NQ].astype(BF16), wo[NQ:].astype(BF16),
                   w_ffn_in[l].astype(BF16), w_ffn_out[l].astype(BF16),
                   tm=tm, tiles_per_batch=seq // tm)
    return out.reshape(batch, seq, D)
```

```python
import functools

import jax
import jax.numpy as jnp
import numpy as np
from jax import lax
from jax.experimental import pallas as pl
from jax.experimental.pallas import tpu as pltpu

D = 1024
HEAD_DIM = 64
ATT_HEADS = 8
ATT_KV_HEADS = 2
GROUP = ATT_HEADS // ATT_KV_HEADS
WINDOW = 128
GRID_W = 64
ROPE_BASE = 10000.0
ROPE_BLK = HEAD_DIM // 4
GLA_HEADS = 8
GLA_DK = 32
GLA_DV = 64
CHUNK = 64
GATE_RANK = 16
GATE_TAU = 16.0
FFN_HIDDEN = 2816
NEG_INF = -1e30
EPS = 1e-6
LOG2E = 1.4426950408889634
Q_SCALE = LOG2E * HEAD_DIM ** -0.5
LA_SCALE = LOG2E / GATE_TAU

NQ = ATT_HEADS * HEAD_DIM
NKV = ATT_KV_HEADS * HEAD_DIM
GQ = GLA_HEADS * GLA_DK
GV = GLA_HEADS * GLA_DV
GIN_W = 2 * GQ + 2 * GV
KV_W = GQ + GV
K_COLS = slice(0, GQ)
V_COLS = slice(GQ, KV_W)
Q_COLS = slice(KV_W, KV_W + GQ)
G_COLS = slice(KV_W + GQ, GIN_W)
Z_W = 128
LA_W = 2 * GQ
REST_W = GIN_W + Z_W

LANES = 128
VMEM_LIMIT = 56 * 1024 * 1024

BF16 = jnp.bfloat16
F32 = jnp.float32


def _dot(a, b):
    return jnp.dot(a, b, preferred_element_type=F32)


def _dot_nt(a, b):
    return lax.dot_general(a, b, (((1,), (1,)), ((), ())), preferred_element_type=F32)


def _dot_tn(a, b):
    return lax.dot_general(a, b, (((0,), (0,)), ((), ())), preferred_element_type=F32)


def _rms(x):
    return x * lax.rsqrt(jnp.mean(x * x, axis=-1, keepdims=True) + EPS)


def _silu(x):
    return x * (1.0 / (1.0 + jnp.exp(-x)))


def _ada_kernel(c_ref, w_ref, b_ref, o_ref):
    a = _silu(c_ref[...]).astype(BF16)
    o_ref[...] = _dot(a, w_ref[...].astype(BF16)) + b_ref[...]


def _ada(c8, w_ada, b_ada):
    n = w_ada.shape[1]
    bn = 1024
    return pl.pallas_call(
        _ada_kernel,
        out_shape=jax.ShapeDtypeStruct((8, n), F32),
        grid=(n // bn,),
        in_specs=[pl.BlockSpec((8, D), lambda j: (0, 0)),
                  pl.BlockSpec((D, bn), lambda j: (0, j)),
                  pl.BlockSpec((1, bn), lambda j: (0, j))],
        out_specs=pl.BlockSpec((8, bn), lambda j: (0, j)),
        compiler_params=pltpu.CompilerParams(dimension_semantics=("arbitrary",)),
        name="ada",
    )(c8, w_ada, b_ada)


NORM_ROWS = 128
INPROJ_TM = 1024


def _inproj_kernel(*refs, tiles_per_batch, rope):
    if rope:
        (x_ref, ada_ref, g_ref, wqkv_ref, w_ref, wg_ref, bg_ref,
         cost_ref, sat_ref, sbt_ref,
         qt_ref, k_ref, vt_ref, gin_ref, lah_ref, lal_ref) = refs
        row = pl.program_id(0) // tiles_per_batch
    else:
        (x_ref, ada_ref, g_ref, wqkv_ref, w_ref, wg_ref, bg_ref,
         qt_ref, k_ref, vt_ref, gin_ref, lah_ref, lal_ref) = refs
        row = 4
    ada = ada_ref[pl.ds(row, 1), :]
    sh1 = ada[:, 0:D]
    sc1 = ada[:, D:2 * D]
    tm = x_ref.shape[0]
    hb = jnp.concatenate(
        [((_rms(x_ref[r0:r0 + NORM_ROWS, :]) * g_ref[...]) * (1.0 + sc1) + sh1).astype(BF16)
         for r0 in range(0, tm, NORM_ROWS)], axis=0)

    z = _dot(hb, w_ref[:, GIN_W:REST_W])
    qkv = _dot_nt(wqkv_ref[...], hb)
    logits = _dot(z.astype(BF16), wg_ref[...]) + bg_ref[...]
    gate = _dot(hb, w_ref[:, G_COLS])
    gin = _dot(hb, w_ref[:, 0:KV_W + GQ])

    def head_t(hd):
        xh = qkv[hd * HEAD_DIM:(hd + 1) * HEAD_DIM]
        if not rope:
            return xh
        up = jnp.concatenate([xh[ROPE_BLK:], xh[:ROPE_BLK]], axis=0)
        dn = jnp.concatenate([xh[HEAD_DIM - ROPE_BLK:], xh[:HEAD_DIM - ROPE_BLK]], axis=0)
        return xh * cost_ref[...] + up * sat_ref[...] + dn * sbt_ref[...]

    for hd in range(ATT_HEADS):
        qh = head_t(hd)
        if rope:
            qh = qh * Q_SCALE
        qt_ref[hd * HEAD_DIM:(hd + 1) * HEAD_DIM, :] = qh.astype(BF16)
    kt = jnp.concatenate([head_t(ATT_HEADS + hd) for hd in range(ATT_KV_HEADS)], axis=0)
    k_ref[...] = kt.T.astype(BF16)
    vt_ref[...] = qkv[NQ + NKV:NQ + 2 * NKV].astype(BF16)

    log_sig = jnp.minimum(logits, 0.0) - jnp.log(1.0 + jnp.exp(-jnp.abs(logits)))
    la = log_sig * LA_SCALE
    hi = la.astype(BF16)
    lah_ref[...] = hi
    lal_ref[...] = (la - hi.astype(F32)).astype(BF16)

    gin_ref[:, G_COLS] = _silu(gate).astype(BF16)
    gin_ref[:, 0:KV_W] = gin[:, 0:KV_W].astype(BF16)
    gin_ref[:, Q_COLS] = (gin[:, Q_COLS] * (GLA_DK ** -0.5)).astype(BF16)


def _inproj(x2d, ada, g_pre, w_qkv_t, w_rest, w_gate_p, b_gate_p, tables, *,
            batch, tokens, tm):
    n = x2d.shape[0]
    tpb = tokens // tm
    rope = tables is not None
    const = lambda i: (0, 0)
    in_specs = [pl.BlockSpec((tm, D), lambda i: (i, 0)),
                pl.BlockSpec((8, 6 * D), const),
                pl.BlockSpec((1, D), const),
                pl.BlockSpec((NQ + 2 * NKV, D), const),
                pl.BlockSpec((D, REST_W), const),
                pl.BlockSpec((Z_W, LA_W), const),
                pl.BlockSpec((1, LA_W), const)]
    args = [x2d, ada, g_pre, w_qkv_t, w_rest, w_gate_p, b_gate_p]
    if rope:
        in_specs += [pl.BlockSpec((HEAD_DIM, tm), lambda i: (0, i % tpb))] * 3
        args += list(tables)
    row_blk = lambda w: pl.BlockSpec((tm, w), lambda i: (i, 0))
    col_blk = lambda w: pl.BlockSpec((None, w, tm), lambda i: (i // tpb, 0, i % tpb))
    return pl.pallas_call(
        functools.partial(_inproj_kernel, tiles_per_batch=tpb, rope=rope),
        out_shape=(jax.ShapeDtypeStruct((batch, NQ, tokens), BF16),
                   jax.ShapeDtypeStruct((n, NKV), BF16),
                   jax.ShapeDtypeStruct((batch, NKV, tokens), BF16),
                   jax.ShapeDtypeStruct((n, GIN_W), BF16),
                   jax.ShapeDtypeStruct((n, LA_W), BF16),
                   jax.ShapeDtypeStruct((n, LA_W), BF16)),
        grid=(n // tm,),
        in_specs=in_specs,
        out_specs=(col_blk(NQ), row_blk(NKV), col_blk(NKV),
                   row_blk(GIN_W), row_blk(LA_W), row_blk(LA_W)),
        compiler_params=pltpu.CompilerParams(dimension_semantics=("arbitrary",),
                                             vmem_limit_bytes=VMEM_LIMIT),
        name="inproj_lat" if rope else "inproj_ctx",
    )(*args)


HEADS_PER_DOT = 2


ATT_SUB = 8
ONES_ROWS = 16
ATT_WAVE = 32


def _attn_kernel(sink_ref, qt_ref, kp_ref, kc_ref, kn_ref, kx_ref,
                 vp_ref, vc_ref, vn_ref, vx_ref, o_ref, *, n_steps):
    i = pl.program_id(1)
    bq = WINDOW
    nw = 3 * bq
    nq = HEADS_PER_DOT * bq
    c = lax.broadcasted_iota(jnp.int32, (nw, nq), 0)
    r = lax.broadcasted_iota(jnp.int32, (nw, nq), 1) % bq
    rel = c - bq - r
    in_win = (rel <= WINDOW) & (rel >= -WINDOW)
    lane = lax.broadcasted_iota(jnp.int32, (1, nq), 1)
    zeros = jnp.zeros((HEAD_DIM, nq), BF16)

    k_blocks = [kp_ref[...]] + [kc_ref[s * bq:(s + 1) * bq] for s in range(ATT_SUB)] + [kn_ref[...]]
    v_blocks = ([vp_ref[...]] + [vc_ref[:, s * bq:(s + 1) * bq] for s in range(ATT_SUB)]
                + [vn_ref[...]])
    kx = kx_ref[...]
    vx = vx_ref[...]

    subs = []
    for s in range(ATT_SUB):
        keep = in_win
        if s == 0:
            keep = keep & ((c >= bq) | (i > 0))
        if s == ATT_SUB - 1:
            keep = keep & ((c < 2 * bq) | (i < n_steps - 1))
        kcat = jnp.concatenate(k_blocks[s:s + 3] + [kx], axis=0)
        vtcat = jnp.concatenate(v_blocks[s:s + 3] + [vx], axis=1)
        ones = jnp.ones((ONES_ROWS, vtcat.shape[1]), BF16)
        vaug = [jnp.concatenate([vtcat[kvh * HEAD_DIM:(kvh + 1) * HEAD_DIM], ones], axis=0)
                for kvh in range(ATT_KV_HEADS)]
        subs.append((keep, kcat, vaug))

    tiles = [(s, kvh, kvh * GROUP + pr * HEADS_PER_DOT)
             for s in range(ATT_SUB) for kvh in range(ATT_KV_HEADS)
             for pr in range(GROUP // HEADS_PER_DOT)]

    def score(tile):
        s, kvh, h0 = tile
        qt = jnp.concatenate(
            [qt_ref[(h0 + u) * HEAD_DIM:(h0 + u + 1) * HEAD_DIM, s * bq:(s + 1) * bq]
             for u in range(HEADS_PER_DOT)], axis=1)
        rhs = jnp.concatenate([qt, zeros] if kvh == 0 else [zeros, qt], axis=0)
        return _dot(subs[s][1], rhs)

    def softmax_numerators(tile, st):
        s, kvh, h0 = tile
        keep = subs[s][0]
        parts = [jnp.where(keep[0:bq], st[0:bq], NEG_INF), st[bq:2 * bq],
                 jnp.where(keep[2 * bq:nw], st[2 * bq:nw], NEG_INF), st[nw:]]
        sink = jnp.where(lane < bq, sink_ref[h0], sink_ref[h0 + 1]) * LOG2E
        m = sink
        for p in parts:
            m = jnp.maximum(m, jnp.max(p, axis=0, keepdims=True))
        et = jnp.concatenate([jnp.exp2(p - m).astype(BF16) for p in parts], axis=0)
        return et, jnp.exp2(sink - m)

    def values(tile, et, e_sink):
        s, kvh, h0 = tile
        ot = _dot(subs[s][2][kvh], et)
        denom = ot[HEAD_DIM:HEAD_DIM + 1] + e_sink
        on = ot[0:HEAD_DIM] * (1.0 / denom)
        for u in range(HEADS_PER_DOT):
            o_ref[(h0 + u) * HEAD_DIM:(h0 + u + 1) * HEAD_DIM, s * bq:(s + 1) * bq] = (
                on[:, u * bq:(u + 1) * bq].astype(BF16))

    waves = [tiles[w0:w0 + ATT_WAVE] for w0 in range(0, len(tiles), ATT_WAVE)]
    scores, probs = {}, {}
    for w in range(len(waves) + 2):
        if w < len(waves):
            scores[w] = [score(t) for t in waves[w]]
        if 0 <= w - 1 < len(waves):
            probs[w - 1] = [softmax_numerators(t, st)
                            for t, st in zip(waves[w - 1], scores.pop(w - 1))]
        if 0 <= w - 2 < len(waves):
            for t, (et, e_sink) in zip(waves[w - 2], probs.pop(w - 2)):
                values(t, et, e_sink)


def _attn(sink, qt, k, vt, k_ctx, vt_ctx, *, batch, seq, n_ctx):
    bq = WINDOW
    bs = ATT_SUB * bq
    nb = seq // bq
    ns = seq // bs
    k3 = k.reshape(batch, seq, NKV)
    kx3 = k_ctx.reshape(batch, n_ctx, NKV)
    prev = lambda i: jnp.maximum(ATT_SUB * i - 1, 0)
    nxt = lambda i: jnp.minimum(ATT_SUB * (i + 1), nb - 1)
    return pl.pallas_call(
        functools.partial(_attn_kernel, n_steps=ns),
        out_shape=jax.ShapeDtypeStruct((batch, NQ, seq), BF16),
        grid=(batch, ns),
        in_specs=[
            pl.BlockSpec(memory_space=pltpu.SMEM),
            pl.BlockSpec((None, NQ, bs), lambda b, i: (b, 0, i)),
            pl.BlockSpec((None, bq, NKV), lambda b, i: (b, prev(i), 0)),
            pl.BlockSpec((None, bs, NKV), lambda b, i: (b, i, 0)),
            pl.BlockSpec((None, bq, NKV), lambda b, i: (b, nxt(i), 0)),
            pl.BlockSpec((None, n_ctx, NKV), lambda b, i: (b, 0, 0)),
            pl.BlockSpec((None, NKV, bq), lambda b, i: (b, 0, prev(i))),
            pl.BlockSpec((None, NKV, bs), lambda b, i: (b, 0, i)),
            pl.BlockSpec((None, NKV, bq), lambda b, i: (b, 0, nxt(i))),
            pl.BlockSpec((None, NKV, n_ctx), lambda b, i: (b, 0, 0)),
        ],
        out_specs=pl.BlockSpec((None, NQ, bs), lambda b, i: (b, 0, i)),
        compiler_params=pltpu.CompilerParams(dimension_semantics=("arbitrary", "arbitrary"),
                                             vmem_limit_bytes=VMEM_LIMIT),
        name="attn",
    )(sink, qt, k3, k3, k3, kx3, vt, vt, vt, vt_ctx)


GRP_K = 4 * GLA_DK
GRP_V = 4 * GLA_DV
N_GRP = GLA_HEADS // 4
GLA_TB = 512
GLA_STATE_TB = 1024
CUM_BLK = 256


def _gla_constants():
    t = CUM_BLK
    r = np.arange(t)[:, None]
    c = np.arange(t)[None, :]
    same_chunk = (r // CHUNK) == (c // CHUNK)
    tri_l = (same_chunk & (c <= r)).astype(np.float32)
    tri_u = (same_chunk & (c >= r)).astype(np.float32)
    rk = np.arange(4 * CHUNK)[:, None] // CHUNK
    kmask = (rk == (np.arange(GRP_K)[None, :] // GLA_DK)).astype(np.float32)
    vmask = (rk == (np.arange(GRP_V)[None, :] // GLA_DV)).astype(np.float32)
    smask = ((np.arange(GRP_K)[:, None] // GLA_DK)
             == (np.arange(GRP_V)[None, :] // GLA_DV)).astype(np.float32)
    ai = np.arange(CHUNK)[:, None]
    aj = np.arange(4 * CHUNK)[None, :] % CHUNK
    causal = (aj <= ai).astype(np.float32)
    anti = (aj >= ai).astype(np.float32)
    hmean = ((np.arange(GRP_V)[:, None] // GLA_DV)
             == (np.arange(GRP_V)[None, :] // GLA_DV)).astype(np.float32) / GLA_DV
    return (jnp.asarray(tri_l, BF16), jnp.asarray(tri_u, BF16), jnp.asarray(kmask, BF16),
            jnp.asarray(vmask, BF16), jnp.asarray(smask, F32), jnp.asarray(causal, F32),
            jnp.asarray(anti, F32), jnp.asarray(hmean, BF16))


def _gk(g):
    return slice(g * GRP_K, (g + 1) * GRP_K)


def _gv(g):
    return slice(g * GRP_V, (g + 1) * GRP_V)


def _chunk_slices(t):
    return [slice(ci * CHUNK, (ci + 1) * CHUNK) for ci in range(t // CHUNK)]


def _cum_and_totals(la_hi, la_lo, tri, *, reverse):
    parts = [slice(p, p + CUM_BLK) for p in range(0, la_hi.shape[0], CUM_BLK)]
    cum = jnp.concatenate([_dot(tri, la_hi[p]) + _dot(tri, la_lo[p]) for p in parts], axis=0)
    edge = 0 if reverse else CHUNK - 1
    tots = [cum[sl.start + edge:sl.start + edge + 1] for sl in _chunk_slices(cum.shape[0])]
    return cum, tots


def _bcast_chunks(rows, width):
    return jnp.concatenate([jnp.broadcast_to(r, (CHUNK, width)) for r in rows], axis=0)


def _decay(x, log2_factor):
    return x * jnp.exp2(log2_factor).astype(BF16)


def _state_increments(kd, v, smask):
    return [[_dot_tn(kd[sl, _gk(g)], v[sl, _gv(g)]) * smask for g in range(N_GRP)]
            for sl in _chunk_slices(kd.shape[0])]


def _decay_columns(tots):
    width = tots[0].shape[1]
    row = lax.broadcasted_iota(jnp.int32, (8, width), 0)
    rows = jnp.zeros((8, width), F32)
    for ci, t in enumerate(tots):
        rows = jnp.where(row == ci, jnp.broadcast_to(t, (8, width)), rows)
    return jnp.exp2(rows).T


def _interleave(stage_gens):
    gens = list(stage_gens)
    done = [False] * len(gens)
    rnd = 0
    while not all(done):
        for gi, gen in enumerate(gens):
            if rnd >= gi and not done[gi]:
                try:
                    next(gen)
                except StopIteration:
                    done[gi] = True
        rnd += 1


def _state_stages(st_ref, k, v, la_hi, la_lo, tri, smask, *, reverse,
                  store_ref=None, store_base=None):
    nch = k.shape[0] // CHUNK
    cum, tots = _cum_and_totals(la_hi, la_lo, tri, reverse=reverse)
    yield
    kd = _decay(k, _bcast_chunks(tots, GQ) - cum)
    dcol = _decay_columns(tots)
    yield
    ds = _state_increments(kd, v, smask)
    yield
    state = [st_ref[g] for g in range(N_GRP)]
    for ci in (range(nch - 1, -1, -1) if reverse else range(nch)):
        for g in range(N_GRP):
            if store_ref is not None:
                store_ref[store_base + ci, g] = state[g].astype(BF16)
            state[g] = state[g] * dcol[_gk(g), ci:ci + 1] + ds[ci][g]
    for g in range(N_GRP):
        st_ref[g] = state[g]


def _output_stages(q, k, v, la_f, la_b, gate, st_ref, bwd_state, consts, finish):
    tri_l, tri_u, kmask, vmask, smask, causal, anti, hmean, gn = consts
    chunks = _chunk_slices(q.shape[0])

    cum_f, tots_f = _cum_and_totals(la_f[0], la_f[1], tri_l, reverse=False)
    cum_b, tots_b = _cum_and_totals(la_b[0], la_b[1], tri_u, reverse=True)
    yield

    qd_f = _decay(q, cum_f)
    qd_b = _decay(q, cum_b)
    ki_f = _decay(k, -cum_f)
    ki_b = _decay(k, -cum_b)
    kd_f = _decay(k, _bcast_chunks(tots_f, GQ) - cum_f)
    dcol_f = _decay_columns(tots_f)
    v_bd = [[jnp.concatenate([v[sl, _gv(g)]] * 4, axis=0) * vmask for g in range(N_GRP)]
            for sl in chunks]
    yield

    def scores(qd, ki, sl, g):
        ki_bd = jnp.concatenate([ki[sl, _gk(g)]] * 4, axis=0) * kmask
        return _dot_nt(qd[sl, _gk(g)], ki_bd)

    a_f = [[scores(qd_f, ki_f, sl, g) for g in range(N_GRP)] for sl in chunks]
    a_b = [[scores(qd_b, ki_b, sl, g) for g in range(N_GRP)] for sl in chunks]
    ds = _state_increments(kd_f, v, smask)
    yield

    a = [[(af[g] * causal + ab[g] * anti).astype(BF16) for g in range(N_GRP)]
         for af, ab in zip(a_f, a_b)]
    state = [st_ref[g] for g in range(N_GRP)]
    st_in = []
    for ci in range(len(chunks)):
        st_in.append([s.astype(BF16) for s in state])
        state = [state[g] * dcol_f[_gk(g), ci:ci + 1] + ds[ci][g] for g in range(N_GRP)]
    for g in range(N_GRP):
        st_ref[g] = state[g]
    yield

    outs = []
    for ci, sl in enumerate(chunks):
        parts = []
        for g in range(N_GRP):
            intra = _dot(a[ci][g], v_bd[ci][g])
            qd2 = jnp.concatenate([qd_f[sl, _gk(g)], qd_b[sl, _gk(g)]], axis=1)
            st2 = jnp.concatenate([st_in[ci][g], bwd_state(ci, g)], axis=0)
            parts.append(intra + _dot(qd2, st2))
        outs.append(jnp.concatenate(parts, axis=1))
    o = jnp.concatenate(outs, axis=0)
    yield

    o2 = (o * o).astype(BF16)
    ms = jnp.concatenate([_dot(o2[:, _gv(g)], hmean) for g in range(N_GRP)], axis=1)
    yield
    finish((o * lax.rsqrt(ms + EPS) * gn * gate.astype(F32)).astype(BF16))


def _gla_kernel(skv_ref, slah_ref, slal_ref,
                kv_ref, qg_ref, lahf_ref, lahb_ref, lalf_ref, lalb_ref,
                ginc_ref, lahc_ref, lalc_ref, gn_ref,
                tril_ref, triu_ref, kmask_ref, vmask_ref, smask_ref, causal_ref, anti_ref,
                hmean_ref, o_ref, st_ref, sb_ref, *, n_state_steps):
    step = pl.program_id(1)
    ch_per_half = CUM_BLK // CHUNK
    fwd = slice(0, GQ)
    bwd = slice(GQ, 2 * GQ)

    def ctx_state(cols, tri_ref, reverse):
        st_ref[...] = jnp.zeros_like(st_ref)
        ctx_blocks = [slice(r0, r0 + CUM_BLK) for r0 in range(0, ginc_ref.shape[0], CUM_BLK)]
        for rs in (ctx_blocks[::-1] if reverse else ctx_blocks):
            for _ in _state_stages(st_ref, ginc_ref[rs, K_COLS], ginc_ref[rs, V_COLS],
                                   lahc_ref[rs, cols], lalc_ref[rs, cols], tri_ref[...],
                                   smask_ref[...], reverse=reverse):
                pass

    @pl.when(step < n_state_steps)
    def _backward_states():
        @pl.when(step == 0)
        def _():
            ctx_state(bwd, triu_ref, True)

        blk = n_state_steps - 1 - step
        parts = [slice(r0, r0 + CUM_BLK) for r0 in range(0, GLA_STATE_TB, CUM_BLK)]
        _interleave([
            _state_stages(st_ref, skv_ref[rs, K_COLS], skv_ref[rs, V_COLS],
                          slah_ref[rs, :], slal_ref[rs, :], triu_ref[...], smask_ref[...],
                          reverse=True, store_ref=sb_ref,
                          store_base=blk * (GLA_STATE_TB // CHUNK) + pi * ch_per_half)
            for pi, rs in reversed(list(enumerate(parts)))])

    @pl.when(step >= n_state_steps)
    def _outputs():
        j = step - n_state_steps

        @pl.when(j == 0)
        def _():
            ctx_state(fwd, tril_ref, False)

        consts = (tril_ref[...], triu_ref[...], kmask_ref[...], vmask_ref[...],
                  smask_ref[...], causal_ref[...], anti_ref[...], hmean_ref[...], gn_ref[...])
        halves = [slice(r0, r0 + CUM_BLK) for r0 in range(0, GLA_TB, CUM_BLK)]

        def half(hi, rs):
            base = j * (GLA_TB // CHUNK) + hi * ch_per_half

            def finish(val):
                o_ref[rs, :] = val

            return _output_stages(
                qg_ref[rs, 0:GQ], kv_ref[rs, K_COLS], kv_ref[rs, V_COLS],
                (lahf_ref[rs, :], lalf_ref[rs, :]), (lahb_ref[rs, :], lalb_ref[rs, :]),
                qg_ref[rs, GQ:KV_W], st_ref, lambda ci, g: sb_ref[base + ci, g], consts, finish)

        _interleave([half(hi, rs) for hi, rs in enumerate(halves)])


def _gla(gin, la_hi, la_lo, gin_ctx, la_hi_ctx, la_lo_ctx, gn_tiled, *, batch, seq, n_ctx):
    assert n_ctx % CUM_BLK == 0 and n_ctx // CHUNK <= 8 and GLA_TB // CHUNK <= 8
    tb, stb = GLA_TB, GLA_STATE_TB
    ns, nb = seq // stb, seq // tb
    nch = seq // CHUNK
    r3 = lambda a, t: a.reshape(batch, t, a.shape[-1])
    consts = _gla_constants()

    def state_blk(col):
        return lambda b, s: (b, jnp.maximum(ns - 1 - s, 0), col)

    def out_blk(col):
        return lambda b, s: (b, jnp.maximum(s - ns, 0), col)

    ctx_map = lambda b, s: (b, 0, 0)
    const_map = lambda b, s: (0, 0)
    gin3, lah3, lal3 = r3(gin, seq), r3(la_hi, seq), r3(la_lo, seq)
    return pl.pallas_call(
        functools.partial(_gla_kernel, n_state_steps=ns),
        out_shape=jax.ShapeDtypeStruct((batch, seq, GV), BF16),
        grid=(batch, ns + nb),
        in_specs=[pl.BlockSpec((None, stb, KV_W), state_blk(0)),
                  pl.BlockSpec((None, stb, GQ), state_blk(1)),
                  pl.BlockSpec((None, stb, GQ), state_blk(1)),
                  pl.BlockSpec((None, tb, KV_W), out_blk(0)),
                  pl.BlockSpec((None, tb, KV_W), out_blk(1)),
                  pl.BlockSpec((None, tb, GQ), out_blk(0)),
                  pl.BlockSpec((None, tb, GQ), out_blk(1)),
                  pl.BlockSpec((None, tb, GQ), out_blk(0)),
                  pl.BlockSpec((None, tb, GQ), out_blk(1)),
                  pl.BlockSpec((None, n_ctx, GIN_W), ctx_map),
                  pl.BlockSpec((None, n_ctx, LA_W), ctx_map),
                  pl.BlockSpec((None, n_ctx, LA_W), ctx_map),
                  pl.BlockSpec((1, GV), const_map)]
                 + [pl.BlockSpec(cst.shape, const_map) for cst in consts],
        out_specs=pl.BlockSpec((None, tb, GV), out_blk(0)),
        scratch_shapes=[pltpu.VMEM((N_GRP, GRP_K, GRP_V), F32),
                        pltpu.VMEM((nch, N_GRP, GRP_K, GRP_V), BF16)],
        compiler_params=pltpu.CompilerParams(
            dimension_semantics=("arbitrary", "arbitrary"),
            vmem_limit_bytes=VMEM_LIMIT),
        name="gla",
    )(gin3, lah3, lal3, gin3, gin3, lah3, lah3, lal3, lal3, r3(gin_ctx, n_ctx),
      r3(la_hi_ctx, n_ctx), r3(la_lo_ctx, n_ctx), gn_tiled, *consts)


FFN_CHUNK = 256
MIX_SLICES = 8


def _mix_ffn_kernel(att_ref, gla_ref, x_ref, ada_ref, gpm_ref, gpf_ref, gqf_ref,
                    woa_ref, wog_ref, wfi_ref, wfo_ref, o_ref, x1_ref, h_ref, y_ref, acc_ref, *,
                    tiles_per_batch, n_tiles):
    i = pl.program_id(0)
    tm = x_ref.shape[0]
    n_chunks = FFN_HIDDEN // FFN_CHUNK
    mix_rows = tm // MIX_SLICES

    def mix_matmul():
        y_ref[...] = _dot_tn(att_ref[...], woa_ref[...]) + _dot(gla_ref[...], wog_ref[...])

    def mix_rows_slice(slot, si, anchor=None):
        rs = slice(si * mix_rows, (si + 1) * mix_rows)
        row = jnp.minimum(i, n_tiles - 1) // tiles_per_batch
        ada = ada_ref[pl.ds(row, 1), :]
        gt1 = ada[:, 2 * D:3 * D]
        if anchor is not None:
            bits = pltpu.bitcast(anchor, jnp.uint32)
            sixteen = jnp.uint32(16)
            gt1 = gt1 + pltpu.bitcast(
                lax.shift_right_logical(lax.shift_right_logical(bits, sixteen), sixteen), F32)
        sh2 = ada[:, 3 * D:4 * D]
        sc2 = ada[:, 4 * D:5 * D]
        x1 = x_ref[rs, :] + gt1 * (_rms(y_ref[rs, :]) * gpm_ref[...])
        x1_ref[slot, rs, :] = x1
        h_ref[slot, rs, :] = ((_rms(x1) * gpf_ref[...]) * (1.0 + sc2) + sh2).astype(BF16)

    def ffn_chunk(slot, ci):
        c0 = ci * FFN_CHUNK
        h = h_ref[slot]
        g = _dot(h, wfi_ref[:, c0:c0 + FFN_CHUNK])
        u = _dot(h, wfi_ref[:, FFN_HIDDEN + c0:FFN_HIDDEN + c0 + FFN_CHUNK])
        a = (_silu(g) * u).astype(BF16)
        part = _dot(a, wfo_ref[c0:c0 + FFN_CHUNK, :])
        if ci == 0:
            acc_ref[...] = part
        else:
            acc_ref[...] += part
        return part[0:8, :]

    def ffn_finish(slot):
        row = jnp.maximum(i - 1, 0) // tiles_per_batch
        gt2 = ada_ref[pl.ds(row, 1), 5 * D:6 * D]
        o_ref[...] = x1_ref[slot] + gt2 * (_rms(acc_ref[...]) * gqf_ref[...])

    @pl.when(i == 0)
    def _():
        mix_matmul()
        for si in range(MIX_SLICES):
            mix_rows_slice(0, si)

    for parity in range(2):
        @pl.when((i % 2 == parity) & (i > 0))
        def _():
            for ci in range(n_chunks):
                anchor = ffn_chunk(1 - parity, ci)
                if ci == 0:
                    mix_matmul()
                elif ci - 1 < MIX_SLICES:
                    mix_rows_slice(parity, ci - 1, anchor[0:1, :])
            ffn_finish(1 - parity)


def _mix_ffn(att_t, gla, x2d, ada, g_post_mix, g_pre_ffn, g_post_ffn,
             w_out_a, w_out_g, w_ffn_in, w_ffn_out, *, tm, tiles_per_batch):
    n = x2d.shape[0]
    tpb = tiles_per_batch
    nt = n // tm
    const = lambda i: (0, 0)
    resident = functools.partial(pl.BlockSpec, index_map=const, pipeline_mode=pl.Buffered(1))
    cur = lambda i: jnp.minimum(i, nt - 1)
    done = lambda i: jnp.maximum(i - 1, 0)
    return pl.pallas_call(
        functools.partial(_mix_ffn_kernel, tiles_per_batch=tpb, n_tiles=nt),
        out_shape=jax.ShapeDtypeStruct((n, D), F32),
        grid=(nt + 1,),
        in_specs=[pl.BlockSpec((None, NQ, tm), lambda i: (cur(i) // tpb, 0, cur(i) % tpb)),
                  pl.BlockSpec((tm, GV), lambda i: (cur(i), 0)),
                  pl.BlockSpec((tm, D), lambda i: (cur(i), 0)),
                  pl.BlockSpec((8, 6 * D), const),
                  pl.BlockSpec((1, D), const),
                  pl.BlockSpec((1, D), const),
                  pl.BlockSpec((1, D), const),
                  resident((NQ, D)),
                  resident((GV, D)),
                  resident((D, 2 * FFN_HIDDEN)),
                  resident((FFN_HIDDEN, D))],
        out_specs=pl.BlockSpec((tm, D), lambda i: (done(i), 0)),
        scratch_shapes=[pltpu.VMEM((2, tm, D), F32),
                        pltpu.VMEM((2, tm, D), BF16),
                        pltpu.VMEM((tm, D), F32),
                        pltpu.VMEM((tm, D), F32)],
        compiler_params=pltpu.CompilerParams(dimension_semantics=("arbitrary",),
                                             vmem_limit_bytes=VMEM_LIMIT),
        name="mix_ffn",
    )(att_t, gla, x2d, ada, g_post_mix, g_pre_ffn, g_post_ffn,
      w_out_a, w_out_g, w_ffn_in, w_ffn_out)


def _rope_tables(seq):
    half = HEAD_DIM // 2
    inv_freq = ROPE_BASE ** (-jnp.arange(0, half, 2, dtype=F32) / half)
    dim = np.arange(HEAD_DIM)
    freq = jnp.tile(inv_freq, HEAD_DIM // ROPE_BLK)[:, None]
    pos = jnp.arange(seq)[None, :]
    row = (pos // GRID_W).astype(F32)
    col = (pos % GRID_W).astype(F32)
    ang = jnp.where((dim < half)[:, None], row, col) * freq
    cos, sin = jnp.cos(ang), jnp.sin(ang)
    even = ((dim // ROPE_BLK) % 2 == 0)[:, None]
    sa = jnp.where(even, -sin, 0.0)
    sb = jnp.where(even, 0.0, sin)
    return cos, sa, sb


def kernel(x, c, ctx, c_ctx, w_ada, b_ada, g_pre_mix, g_post_mix, g_pre_ffn, g_post_ffn,
           w_in, attn_sink, w_gate_fwd, b_gate_fwd, w_gate_bwd, b_gate_bwd, g_gla_norm,
           w_out, w_ffn_in, w_ffn_out):
    batch, seq, _ = x.shape
    n_ctx = ctx.shape[1]
    depth = w_ada.shape[0]
    assert depth == 1
    l = 0

    c8 = jnp.zeros((8, D), F32).at[0:batch].set(c).at[4].set(c_ctx)
    ada = _ada(c8, w_ada[l], b_ada[l][None, :])

    wi = w_in[l]
    g0 = NQ + 2 * NKV
    w_qkv_t = wi[:, 0:g0].T.astype(BF16)
    gk0 = g0 + GQ
    gg0 = gk0 + KV_W
    w_rest = jnp.concatenate(
        [wi[:, gk0:gg0], wi[:, g0:gk0], wi[:, gg0:],
         jnp.zeros((D, Z_W - 2 * GATE_RANK), F32)], axis=1).astype(BF16)
    w_gate_p = jnp.zeros((Z_W, LA_W), F32)
    w_gate_p = w_gate_p.at[0:GATE_RANK, 0:GQ].set(w_gate_fwd[l])
    w_gate_p = w_gate_p.at[GATE_RANK:2 * GATE_RANK, GQ:].set(w_gate_bwd[l]).astype(BF16)
    b_gate_p = jnp.concatenate([b_gate_fwd[l], b_gate_bwd[l]])[None, :]

    tm = 512
    tables = _rope_tables(seq)
    x2d = x.reshape(batch * seq, D)
    g_pre = g_pre_mix[l][None, :]
    qt, k, vt, gin, lah, lal = _inproj(x2d, ada, g_pre, w_qkv_t, w_rest, w_gate_p, b_gate_p,
                                      tables, batch=batch, tokens=seq, tm=INPROJ_TM)
    _, k_c, vt_c, gin_c, lah_c, lal_c = _inproj(
        ctx.reshape(batch * n_ctx, D), ada, g_pre, w_qkv_t, w_rest, w_gate_p, b_gate_p, None,
        batch=batch, tokens=n_ctx, tm=n_ctx)

    att_t = _attn(attn_sink[l], qt, k, vt, k_c, vt_c, batch=batch, seq=seq, n_ctx=n_ctx)

    gn_tiled = jnp.tile(g_gla_norm[l], GLA_HEADS)[None, :]
    gla = _gla(gin, lah, lal, gin_c, lah_c, lal_c, gn_tiled, batch=batch, seq=seq, n_ctx=n_ctx)

    wo = w_out[l]
    out = _mix_ffn(att_t, gla.reshape(batch * seq, GV), x2d, ada,
                   g_post_mix[l][None, :], g_pre_ffn[l][None, :], g_post_ffn[l][None, :],
                   wo[:NQ].astype(BF16), wo[NQ:].astype(BF16),
                   w_ffn_in[l].astype(BF16), w_ffn_out[l].astype(BF16),
                   tm=tm, tiles_per_batch=seq // tm)
    return out.reshape(batch, seq, D)
```

```python
import functools

import jax
import jax.numpy as jnp
import numpy as np
from jax import lax
from jax.experimental import pallas as pl
from jax.experimental.pallas import tpu as pltpu

D = 1024
HEAD_DIM = 64
ATT_HEADS = 8
ATT_KV_HEADS = 2
GROUP = ATT_HEADS // ATT_KV_HEADS
WINDOW = 128
GRID_W = 64
ROPE_BASE = 10000.0
ROPE_BLK = HEAD_DIM // 4
GLA_HEADS = 8
GLA_DK = 32
GLA_DV = 64
CHUNK = 64
GATE_RANK = 16
GATE_TAU = 16.0
FFN_HIDDEN = 2816
NEG_INF = -1e30
EPS = 1e-6
LOG2E = 1.4426950408889634
Q_SCALE = LOG2E * HEAD_DIM ** -0.5
LA_SCALE = LOG2E / GATE_TAU

NQ = ATT_HEADS * HEAD_DIM
NKV = ATT_KV_HEADS * HEAD_DIM
GQ = GLA_HEADS * GLA_DK
GV = GLA_HEADS * GLA_DV
GIN_W = 2 * GQ + 2 * GV
KV_W = GQ + GV
K_COLS = slice(0, GQ)
V_COLS = slice(GQ, KV_W)
Q_COLS = slice(KV_W, KV_W + GQ)
G_COLS = slice(KV_W + GQ, GIN_W)
Z_W = 128
LA_W = 2 * GQ
REST_W = GIN_W + Z_W

LANES = 128
VMEM_LIMIT = 56 * 1024 * 1024

BF16 = jnp.bfloat16
F32 = jnp.float32


def _dot(a, b):
    return jnp.dot(a, b, preferred_element_type=F32)


def _dot_nt(a, b):
    return lax.dot_general(a, b, (((1,), (1,)), ((), ())), preferred_element_type=F32)


def _dot_tn(a, b):
    return lax.dot_general(a, b, (((0,), (0,)), ((), ())), preferred_element_type=F32)


def _rms(x):
    return x * lax.rsqrt(jnp.mean(x * x, axis=-1, keepdims=True) + EPS)


def _silu(x):
    return x * (1.0 / (1.0 + jnp.exp(-x)))


def _ada_kernel(c_ref, w_ref, b_ref, o_ref):
    a = _silu(c_ref[...]).astype(BF16)
    o_ref[...] = _dot(a, w_ref[...].astype(BF16)) + b_ref[...]


def _ada(c8, w_ada, b_ada):
    n = w_ada.shape[1]
    bn = 1024
    return pl.pallas_call(
        _ada_kernel,
        out_shape=jax.ShapeDtypeStruct((8, n), F32),
        grid=(n // bn,),
        in_specs=[pl.BlockSpec((8, D), lambda j: (0, 0)),
                  pl.BlockSpec((D, bn), lambda j: (0, j)),
                  pl.BlockSpec((1, bn), lambda j: (0, j))],
        out_specs=pl.BlockSpec((8, bn), lambda j: (0, j)),
        compiler_params=pltpu.CompilerParams(dimension_semantics=("arbitrary",)),
        name="ada",
    )(c8, w_ada, b_ada)


NORM_ROWS = 128
INPROJ_TM = 1024


def _inproj_kernel(*refs, tiles_per_batch, rope):
    if rope:
        (x_ref, ada_ref, g_ref, wqkv_ref, w_ref, wg_ref, bg_ref,
         cost_ref, sat_ref, sbt_ref,
         qt_ref, k_ref, vt_ref, gin_ref, lah_ref, lal_ref) = refs
        row = pl.program_id(0) // tiles_per_batch
    else:
        (x_ref, ada_ref, g_ref, wqkv_ref, w_ref, wg_ref, bg_ref,
         qt_ref, k_ref, vt_ref, gin_ref, lah_ref, lal_ref) = refs
        row = 4
    ada = ada_ref[pl.ds(row, 1), :]
    sh1 = ada[:, 0:D]
    sc1 = ada[:, D:2 * D]
    tm = x_ref.shape[0]
    hb = jnp.concatenate(
        [((_rms(x_ref[r0:r0 + NORM_ROWS, :]) * g_ref[...]) * (1.0 + sc1) + sh1).astype(BF16)
         for r0 in range(0, tm, NORM_ROWS)], axis=0)

    z = _dot(hb, w_ref[:, GIN_W:REST_W])
    qkv = _dot_nt(wqkv_ref[...], hb)
    logits = _dot(z.astype(BF16), wg_ref[...]) + bg_ref[...]
    gate = _dot(hb, w_ref[:, G_COLS])
    gin = _dot(hb, w_ref[:, 0:KV_W + GQ])

    def head_t(hd):
        xh = qkv[hd * HEAD_DIM:(hd + 1) * HEAD_DIM]
        if not rope:
            return xh
        up = jnp.concatenate([xh[ROPE_BLK:], xh[:ROPE_BLK]], axis=0)
        dn = jnp.concatenate([xh[HEAD_DIM - ROPE_BLK:], xh[:HEAD_DIM - ROPE_BLK]], axis=0)
        return xh * cost_ref[...] + up * sat_ref[...] + dn * sbt_ref[...]

    for hd in range(ATT_HEADS):
        qh = head_t(hd)
        if rope:
            qh = qh * Q_SCALE
        qt_ref[hd * HEAD_DIM:(hd + 1) * HEAD_DIM, :] = qh.astype(BF16)
    kt = jnp.concatenate([head_t(ATT_HEADS + hd) for hd in range(ATT_KV_HEADS)], axis=0)
    k_ref[...] = kt.T.astype(BF16)
    vt_ref[...] = qkv[NQ + NKV:NQ + 2 * NKV].astype(BF16)

    log_sig = jnp.minimum(logits, 0.0) - jnp.log(1.0 + jnp.exp(-jnp.abs(logits)))
    la = log_sig * LA_SCALE
    hi = la.astype(BF16)
    lah_ref[...] = hi
    lal_ref[...] = (la - hi.astype(F32)).astype(BF16)

    gin_ref[:, G_COLS] = _silu(gate).astype(BF16)
    gin_ref[:, 0:KV_W] = gin[:, 0:KV_W].astype(BF16)
    gin_ref[:, Q_COLS] = (gin[:, Q_COLS] * (GLA_DK ** -0.5)).astype(BF16)


def _inproj(x2d, ada, g_pre, w_qkv_t, w_rest, w_gate_p, b_gate_p, tables, *,
            batch, tokens, tm):
    n = x2d.shape[0]
    tpb = tokens // tm
    rope = tables is not None
    const = lambda i: (0, 0)
    in_specs = [pl.BlockSpec((tm, D), lambda i: (i, 0)),
                pl.BlockSpec((8, 6 * D), const),
                pl.BlockSpec((1, D), const),
                pl.BlockSpec((NQ + 2 * NKV, D), const),
                pl.BlockSpec((D, REST_W), const),
                pl.BlockSpec((Z_W, LA_W), const),
                pl.BlockSpec((1, LA_W), const)]
    args = [x2d, ada, g_pre, w_qkv_t, w_rest, w_gate_p, b_gate_p]
    if rope:
        in_specs += [pl.BlockSpec((HEAD_DIM, tm), lambda i: (0, i % tpb))] * 3
        args += list(tables)
    row_blk = lambda w: pl.BlockSpec((tm, w), lambda i: (i, 0))
    col_blk = lambda w: pl.BlockSpec((None, w, tm), lambda i: (i // tpb, 0, i % tpb))
    return pl.pallas_call(
        functools.partial(_inproj_kernel, tiles_per_batch=tpb, rope=rope),
        out_shape=(jax.ShapeDtypeStruct((batch, NQ, tokens), BF16),
                   jax.ShapeDtypeStruct((n, NKV), BF16),
                   jax.ShapeDtypeStruct((batch, NKV, tokens), BF16),
                   jax.ShapeDtypeStruct((n, GIN_W), BF16),
                   jax.ShapeDtypeStruct((n, LA_W), BF16),
                   jax.ShapeDtypeStruct((n, LA_W), BF16)),
        grid=(n // tm,),
        in_specs=in_specs,
        out_specs=(col_blk(NQ), row_blk(NKV), col_blk(NKV),
                   row_blk(GIN_W), row_blk(LA_W), row_blk(LA_W)),
        compiler_params=pltpu.CompilerParams(dimension_semantics=("arbitrary",),
                                             vmem_limit_bytes=VMEM_LIMIT),
        name="inproj_lat" if rope else "inproj_ctx",
    )(*args)


HEADS_PER_DOT = 2


ATT_SUB = 8
ONES_ROWS = 16
ATT_WAVE = 8


def _attn_kernel(sink_ref, qt_ref, kp_ref, kc_ref, kn_ref, kx_ref,
                 vp_ref, vc_ref, vn_ref, vx_ref, o_ref, *, n_steps):
    i = pl.program_id(1)
    bq = WINDOW
    nw = 3 * bq
    nq = HEADS_PER_DOT * bq
    c = lax.broadcasted_iota(jnp.int32, (nw, nq), 0)
    r = lax.broadcasted_iota(jnp.int32, (nw, nq), 1) % bq
    rel = c - bq - r
    in_win = (rel <= WINDOW) & (rel >= -WINDOW)
    lane = lax.broadcasted_iota(jnp.int32, (1, nq), 1)
    zeros = jnp.zeros((HEAD_DIM, nq), BF16)

    k_blocks = [kp_ref[...]] + [kc_ref[s * bq:(s + 1) * bq] for s in range(ATT_SUB)] + [kn_ref[...]]
    v_blocks = ([vp_ref[...]] + [vc_ref[:, s * bq:(s + 1) * bq] for s in range(ATT_SUB)]
                + [vn_ref[...]])
    kx = kx_ref[...]
    vx = vx_ref[...]

    subs = []
    for s in range(ATT_SUB):
        keep = in_win
        if s == 0:
            keep = keep & ((c >= bq) | (i > 0))
        if s == ATT_SUB - 1:
            keep = keep & ((c < 2 * bq) | (i < n_steps - 1))
        kcat = jnp.concatenate(k_blocks[s:s + 3] + [kx], axis=0)
        vtcat = jnp.concatenate(v_blocks[s:s + 3] + [vx], axis=1)
        ones = jnp.ones((ONES_ROWS, vtcat.shape[1]), BF16)
        vaug = [jnp.concatenate([vtcat[kvh * HEAD_DIM:(kvh + 1) * HEAD_DIM], ones], axis=0)
                for kvh in range(ATT_KV_HEADS)]
        subs.append((keep, kcat, vaug))

    tiles = [(s, kvh, kvh * GROUP + pr * HEADS_PER_DOT)
             for s in range(ATT_SUB) for kvh in range(ATT_KV_HEADS)
             for pr in range(GROUP // HEADS_PER_DOT)]

    def score(tile):
        s, kvh, h0 = tile
        qt = jnp.concatenate(
            [qt_ref[(h0 + u) * HEAD_DIM:(h0 + u + 1) * HEAD_DIM, s * bq:(s + 1) * bq]
             for u in range(HEADS_PER_DOT)], axis=1)
        rhs = jnp.concatenate([qt, zeros] if kvh == 0 else [zeros, qt], axis=0)
        return _dot(subs[s][1], rhs)

    def softmax_numerators(tile, st):
        s, kvh, h0 = tile
        keep = subs[s][0]
        parts = [jnp.where(keep[0:bq], st[0:bq], NEG_INF), st[bq:2 * bq],
                 jnp.where(keep[2 * bq:nw], st[2 * bq:nw], NEG_INF), st[nw:]]
        sink = jnp.where(lane < bq, sink_ref[h0], sink_ref[h0 + 1]) * LOG2E
        m = sink
        for p in parts:
            m = jnp.maximum(m, jnp.max(p, axis=0, keepdims=True))
        et = jnp.concatenate([jnp.exp2(p - m).astype(BF16) for p in parts], axis=0)
        return et, jnp.exp2(sink - m)

    def values(tile, et, e_sink):
        s, kvh, h0 = tile
        ot = _dot(subs[s][2][kvh], et)
        denom = ot[HEAD_DIM:HEAD_DIM + 1] + e_sink
        on = ot[0:HEAD_DIM] * (1.0 / denom)
        for u in range(HEADS_PER_DOT):
            o_ref[(h0 + u) * HEAD_DIM:(h0 + u + 1) * HEAD_DIM, s * bq:(s + 1) * bq] = (
                on[:, u * bq:(u + 1) * bq].astype(BF16))

    waves = [tiles[w0:w0 + ATT_WAVE] for w0 in range(0, len(tiles), ATT_WAVE)]
    scores, probs = {}, {}
    for w in range(len(waves) + 2):
        if w < len(waves):
            scores[w] = [score(t) for t in waves[w]]
        if 0 <= w - 1 < len(waves):
            probs[w - 1] = [softmax_numerators(t, st)
                            for t, st in zip(waves[w - 1], scores.pop(w - 1))]
        if 0 <= w - 2 < len(waves):
            for t, (et, e_sink) in zip(waves[w - 2], probs.pop(w - 2)):
                values(t, et, e_sink)


def _attn(sink, qt, k, vt, k_ctx, vt_ctx, *, batch, seq, n_ctx):
    bq = WINDOW
    bs = ATT_SUB * bq
    nb = seq // bq
    ns = seq // bs
    k3 = k.reshape(batch, seq, NKV)
    kx3 = k_ctx.reshape(batch, n_ctx, NKV)
    prev = lambda i: jnp.maximum(ATT_SUB * i - 1, 0)
    nxt = lambda i: jnp.minimum(ATT_SUB * (i + 1), nb - 1)
    return pl.pallas_call(
        functools.partial(_attn_kernel, n_steps=ns),
        out_shape=jax.ShapeDtypeStruct((batch, NQ, seq), BF16),
        grid=(batch, ns),
        in_specs=[
            pl.BlockSpec(memory_space=pltpu.SMEM),
            pl.BlockSpec((None, NQ, bs), lambda b, i: (b, 0, i)),
            pl.BlockSpec((None, bq, NKV), lambda b, i: (b, prev(i), 0)),
            pl.BlockSpec((None, bs, NKV), lambda b, i: (b, i, 0)),
            pl.BlockSpec((None, bq, NKV), lambda b, i: (b, nxt(i), 0)),
            pl.BlockSpec((None, n_ctx, NKV), lambda b, i: (b, 0, 0)),
            pl.BlockSpec((None, NKV, bq), lambda b, i: (b, 0, prev(i))),
            pl.BlockSpec((None, NKV, bs), lambda b, i: (b, 0, i)),
            pl.BlockSpec((None, NKV, bq), lambda b, i: (b, 0, nxt(i))),
            pl.BlockSpec((None, NKV, n_ctx), lambda b, i: (b, 0, 0)),
        ],
        out_specs=pl.BlockSpec((None, NQ, bs), lambda b, i: (b, 0, i)),
        compiler_params=pltpu.CompilerParams(dimension_semantics=("arbitrary", "arbitrary"),
                                             vmem_limit_bytes=VMEM_LIMIT),
        name="attn",
    )(sink, qt, k3, k3, k3, kx3, vt, vt, vt, vt_ctx)


GRP_K = 4 * GLA_DK
GRP_V = 4 * GLA_DV
N_GRP = GLA_HEADS // 4
GLA_TB = 512
GLA_STATE_TB = 1024
CUM_BLK = 256


def _gla_constants():
    t = CUM_BLK
    r = np.arange(t)[:, None]
    c = np.arange(t)[None, :]
    same_chunk = (r // CHUNK) == (c // CHUNK)
    tri_l = (same_chunk & (c <= r)).astype(np.float32)
    tri_u = (same_chunk & (c >= r)).astype(np.float32)
    rk = np.arange(4 * CHUNK)[:, None] // CHUNK
    kmask = (rk == (np.arange(GRP_K)[None, :] // GLA_DK)).astype(np.float32)
    vmask = (rk == (np.arange(GRP_V)[None, :] // GLA_DV)).astype(np.float32)
    smask = ((np.arange(GRP_K)[:, None] // GLA_DK)
             == (np.arange(GRP_V)[None, :] // GLA_DV)).astype(np.float32)
    ai = np.arange(CHUNK)[:, None]
    aj = np.arange(4 * CHUNK)[None, :] % CHUNK
    causal = (aj <= ai).astype(np.float32)
    anti = (aj >= ai).astype(np.float32)
    hmean = ((np.arange(GRP_V)[:, None] // GLA_DV)
             == (np.arange(GRP_V)[None, :] // GLA_DV)).astype(np.float32) / GLA_DV
    return (jnp.asarray(tri_l, BF16), jnp.asarray(tri_u, BF16), jnp.asarray(kmask, BF16),
            jnp.asarray(vmask, BF16), jnp.asarray(smask, F32), jnp.asarray(causal, F32),
            jnp.asarray(anti, F32), jnp.asarray(hmean, BF16))


def _gk(g):
    return slice(g * GRP_K, (g + 1) * GRP_K)


def _gv(g):
    return slice(g * GRP_V, (g + 1) * GRP_V)


def _chunk_slices(t):
    return [slice(ci * CHUNK, (ci + 1) * CHUNK) for ci in range(t // CHUNK)]


def _cum_and_totals(la_hi, la_lo, tri, *, reverse):
    parts = [slice(p, p + CUM_BLK) for p in range(0, la_hi.shape[0], CUM_BLK)]
    cum = jnp.concatenate([_dot(tri, la_hi[p]) + _dot(tri, la_lo[p]) for p in parts], axis=0)
    edge = 0 if reverse else CHUNK - 1
    tots = [cum[sl.start + edge:sl.start + edge + 1] for sl in _chunk_slices(cum.shape[0])]
    return cum, tots


def _bcast_chunks(rows, width):
    return jnp.concatenate([jnp.broadcast_to(r, (CHUNK, width)) for r in rows], axis=0)


def _decay(x, log2_factor):
    return x * jnp.exp2(log2_factor).astype(BF16)


def _state_increments(kd, v, smask):
    return [[_dot_tn(kd[sl, _gk(g)], v[sl, _gv(g)]) * smask for g in range(N_GRP)]
            for sl in _chunk_slices(kd.shape[0])]


def _decay_columns(tots):
    width = tots[0].shape[1]
    row = lax.broadcasted_iota(jnp.int32, (8, width), 0)
    rows = jnp.zeros((8, width), F32)
    for ci, t in enumerate(tots):
        rows = jnp.where(row == ci, jnp.broadcast_to(t, (8, width)), rows)
    return jnp.exp2(rows).T


def _interleave(stage_gens):
    gens = list(stage_gens)
    done = [False] * len(gens)
    rnd = 0
    while not all(done):
        for gi, gen in enumerate(gens):
            if rnd >= gi and not done[gi]:
                try:
                    next(gen)
                except StopIteration:
                    done[gi] = True
        rnd += 1


def _state_stages(st_ref, k, v, la_hi, la_lo, tri, smask, *, reverse,
                  store_ref=None, store_base=None):
    nch = k.shape[0] // CHUNK
    cum, tots = _cum_and_totals(la_hi, la_lo, tri, reverse=reverse)
    yield
    kd = _decay(k, _bcast_chunks(tots, GQ) - cum)
    dcol = _decay_columns(tots)
    yield
    ds = _state_increments(kd, v, smask)
    yield
    state = [st_ref[g] for g in range(N_GRP)]
    for ci in (range(nch - 1, -1, -1) if reverse else range(nch)):
        for g in range(N_GRP):
            if store_ref is not None:
                store_ref[store_base + ci, g] = state[g].astype(BF16)
            state[g] = state[g] * dcol[_gk(g), ci:ci + 1] + ds[ci][g]
    for g in range(N_GRP):
        st_ref[g] = state[g]


def _output_stages(q, k, v, la_f, la_b, gate, st_ref, bwd_state, consts, finish):
    tri_l, tri_u, kmask, vmask, smask, causal, anti, hmean, gn = consts
    chunks = _chunk_slices(q.shape[0])

    cum_f, tots_f = _cum_and_totals(la_f[0], la_f[1], tri_l, reverse=False)
    cum_b, tots_b = _cum_and_totals(la_b[0], la_b[1], tri_u, reverse=True)
    yield

    qd_f = _decay(q, cum_f)
    qd_b = _decay(q, cum_b)
    ki_f = _decay(k, -cum_f)
    ki_b = _decay(k, -cum_b)
    kd_f = _decay(k, _bcast_chunks(tots_f, GQ) - cum_f)
    dcol_f = _decay_columns(tots_f)
    v_bd = [[jnp.concatenate([v[sl, _gv(g)]] * 4, axis=0) * vmask for g in range(N_GRP)]
            for sl in chunks]
    yield

    def scores(qd, ki, sl, g):
        ki_bd = jnp.concatenate([ki[sl, _gk(g)]] * 4, axis=0) * kmask
        return _dot_nt(qd[sl, _gk(g)], ki_bd)

    a_f = [[scores(qd_f, ki_f, sl, g) for g in range(N_GRP)] for sl in chunks]
    a_b = [[scores(qd_b, ki_b, sl, g) for g in range(N_GRP)] for sl in chunks]
    ds = _state_increments(kd_f, v, smask)
    yield

    a = [[(af[g] * causal + ab[g] * anti).astype(BF16) for g in range(N_GRP)]
         for af, ab in zip(a_f, a_b)]
    state = [st_ref[g] for g in range(N_GRP)]
    st_in = []
    for ci in range(len(chunks)):
        st_in.append([s.astype(BF16) for s in state])
        state = [state[g] * dcol_f[_gk(g), ci:ci + 1] + ds[ci][g] for g in range(N_GRP)]
    for g in range(N_GRP):
        st_ref[g] = state[g]
    yield

    outs = []
    for ci, sl in enumerate(chunks):
        parts = []
        for g in range(N_GRP):
            intra = _dot(a[ci][g], v_bd[ci][g])
            qd2 = jnp.concatenate([qd_f[sl, _gk(g)], qd_b[sl, _gk(g)]], axis=1)
            st2 = jnp.concatenate([st_in[ci][g], bwd_state(ci, g)], axis=0)
            parts.append(intra + _dot(qd2, st2))
        outs.append(jnp.concatenate(parts, axis=1))
    o = jnp.concatenate(outs, axis=0)
    yield

    o2 = (o * o).astype(BF16)
    ms = jnp.concatenate([_dot(o2[:, _gv(g)], hmean) for g in range(N_GRP)], axis=1)
    yield
    finish((o * lax.rsqrt(ms + EPS) * gn * gate.astype(F32)).astype(BF16))


def _gla_kernel(skv_ref, slah_ref, slal_ref,
                kv_ref, qg_ref, lahf_ref, lahb_ref, lalf_ref, lalb_ref,
                ginc_ref, lahc_ref, lalc_ref, gn_ref,
                tril_ref, triu_ref, kmask_ref, vmask_ref, smask_ref, causal_ref, anti_ref,
                hmean_ref, o_ref, st_ref, sb_ref, *, n_state_steps):
    step = pl.program_id(1)
    ch_per_half = CUM_BLK // CHUNK
    fwd = slice(0, GQ)
    bwd = slice(GQ, 2 * GQ)

    def ctx_state(cols, tri_ref, reverse):
        st_ref[...] = jnp.zeros_like(st_ref)
        ctx_blocks = [slice(r0, r0 + CUM_BLK) for r0 in range(0, ginc_ref.shape[0], CUM_BLK)]
        for rs in (ctx_blocks[::-1] if reverse else ctx_blocks):
            for _ in _state_stages(st_ref, ginc_ref[rs, K_COLS], ginc_ref[rs, V_COLS],
                                   lahc_ref[rs, cols], lalc_ref[rs, cols], tri_ref[...],
                                   smask_ref[...], reverse=reverse):
                pass

    @pl.when(step < n_state_steps)
    def _backward_states():
        @pl.when(step == 0)
        def _():
            ctx_state(bwd, triu_ref, True)

        blk = n_state_steps - 1 - step
        parts = [slice(r0, r0 + CUM_BLK) for r0 in range(0, GLA_STATE_TB, CUM_BLK)]
        _interleave([
            _state_stages(st_ref, skv_ref[rs, K_COLS], skv_ref[rs, V_COLS],
                          slah_ref[rs, :], slal_ref[rs, :], triu_ref[...], smask_ref[...],
                          reverse=True, store_ref=sb_ref,
                          store_base=blk * (GLA_STATE_TB // CHUNK) + pi * ch_per_half)
            for pi, rs in reversed(list(enumerate(parts)))])

    @pl.when(step >= n_state_steps)
    def _outputs():
        j = step - n_state_steps

        @pl.when(j == 0)
        def _():
            ctx_state(fwd, tril_ref, False)

        consts = (tril_ref[...], triu_ref[...], kmask_ref[...], vmask_ref[...],
                  smask_ref[...], causal_ref[...], anti_ref[...], hmean_ref[...], gn_ref[...])
        halves = [slice(r0, r0 + CUM_BLK) for r0 in range(0, GLA_TB, CUM_BLK)]

        def half(hi, rs):
            base = j * (GLA_TB // CHUNK) + hi * ch_per_half

            def finish(val):
                o_ref[rs, :] = val

            return _output_stages(
                qg_ref[rs, 0:GQ], kv_ref[rs, K_COLS], kv_ref[rs, V_COLS],
                (lahf_ref[rs, :], lalf_ref[rs, :]), (lahb_ref[rs, :], lalb_ref[rs, :]),
                qg_ref[rs, GQ:KV_W], st_ref, lambda ci, g: sb_ref[base + ci, g], consts, finish)

        _interleave([half(hi, rs) for hi, rs in enumerate(halves)])


def _gla(gin, la_hi, la_lo, gin_ctx, la_hi_ctx, la_lo_ctx, gn_tiled, *, batch, seq, n_ctx):
    assert n_ctx % CUM_BLK == 0 and n_ctx // CHUNK <= 8 and GLA_TB // CHUNK <= 8
    tb, stb = GLA_TB, GLA_STATE_TB
    ns, nb = seq // stb, seq // tb
    nch = seq // CHUNK
    r3 = lambda a, t: a.reshape(batch, t, a.shape[-1])
    consts = _gla_constants()

    def state_blk(col):
        return lambda b, s: (b, jnp.maximum(ns - 1 - s, 0), col)

    def out_blk(col):
        return lambda b, s: (b, jnp.maximum(s - ns, 0), col)

    ctx_map = lambda b, s: (b, 0, 0)
    const_map = lambda b, s: (0, 0)
    gin3, lah3, lal3 = r3(gin, seq), r3(la_hi, seq), r3(la_lo, seq)
    return pl.pallas_call(
        functools.partial(_gla_kernel, n_state_steps=ns),
        out_shape=jax.ShapeDtypeStruct((batch, seq, GV), BF16),
        grid=(batch, ns + nb),
        in_specs=[pl.BlockSpec((None, stb, KV_W), state_blk(0)),
                  pl.BlockSpec((None, stb, GQ), state_blk(1)),
                  pl.BlockSpec((None, stb, GQ), state_blk(1)),
                  pl.BlockSpec((None, tb, KV_W), out_blk(0)),
                  pl.BlockSpec((None, tb, KV_W), out_blk(1)),
                  pl.BlockSpec((None, tb, GQ), out_blk(0)),
                  pl.BlockSpec((None, tb, GQ), out_blk(1)),
                  pl.BlockSpec((None, tb, GQ), out_blk(0)),
                  pl.BlockSpec((None, tb, GQ), out_blk(1)),
                  pl.BlockSpec((None, n_ctx, GIN_W), ctx_map),
                  pl.BlockSpec((None, n_ctx, LA_W), ctx_map),
                  pl.BlockSpec((None, n_ctx, LA_W), ctx_map),
                  pl.BlockSpec((1, GV), const_map)]
                 + [pl.BlockSpec(cst.shape, const_map) for cst in consts],
        out_specs=pl.BlockSpec((None, tb, GV), out_blk(0)),
        scratch_shapes=[pltpu.VMEM((N_GRP, GRP_K, GRP_V), F32),
                        pltpu.VMEM((nch, N_GRP, GRP_K, GRP_V), BF16)],
        compiler_params=pltpu.CompilerParams(
            dimension_semantics=("arbitrary", "arbitrary"),
            vmem_limit_bytes=VMEM_LIMIT),
        name="gla",
    )(gin3, lah3, lal3, gin3, gin3, lah3, lah3, lal3, lal3, r3(gin_ctx, n_ctx),
      r3(la_hi_ctx, n_ctx), r3(la_lo_ctx, n_ctx), gn_tiled, *consts)


FFN_CHUNK = 256
MIX_SLICES = 8


def _mix_ffn_kernel(att_ref, gla_ref, x_ref, ada_ref, gpm_ref, gpf_ref, gqf_ref,
                    woa_ref, wog_ref, wfi_ref, wfo_ref, o_ref, x1_ref, h_ref, y_ref, acc_ref, *,
                    tiles_per_batch, n_tiles):
    i = pl.program_id(0)
    tm = x_ref.shape[0]
    chunks = [(c0, min(c0 + FFN_CHUNK, FFN_HIDDEN)) for c0 in range(0, FFN_HIDDEN, FFN_CHUNK)]
    n_chunks = len(chunks)
    mix_rows = tm // MIX_SLICES

    def mix_matmul():
        y_ref[...] = _dot_tn(att_ref[...], woa_ref[...]) + _dot(gla_ref[...], wog_ref[...])

    def mix_rows_slice(slot, si, anchor=None):
        rs = slice(si * mix_rows, (si + 1) * mix_rows)
        row = jnp.minimum(i, n_tiles - 1) // tiles_per_batch
        ada = ada_ref[pl.ds(row, 1), :]
        gt1 = ada[:, 2 * D:3 * D]
        if anchor is not None:
            bits = pltpu.bitcast(anchor, jnp.uint32)
            sixteen = jnp.uint32(16)
            gt1 = gt1 + pltpu.bitcast(
                lax.shift_right_logical(lax.shift_right_logical(bits, sixteen), sixteen), F32)
        sh2 = ada[:, 3 * D:4 * D]
        sc2 = ada[:, 4 * D:5 * D]
        x1 = x_ref[rs, :] + gt1 * (_rms(y_ref[rs, :]) * gpm_ref[...])
        x1_ref[slot, rs, :] = x1
        h_ref[slot, rs, :] = ((_rms(x1) * gpf_ref[...]) * (1.0 + sc2) + sh2).astype(BF16)

    def ffn_chunk(slot, ci):
        c0, c1 = chunks[ci]
        h = h_ref[slot]
        g = _dot(h, wfi_ref[:, c0:c1])
        u = _dot(h, wfi_ref[:, FFN_HIDDEN + c0:FFN_HIDDEN + c1])
        a = (_silu(g) * u).astype(BF16)
        part = _dot(a, wfo_ref[c0:c1, :])
        if ci == 0:
            acc_ref[...] = part
        else:
            acc_ref[...] += part
        return part[0:8, :]

    def ffn_finish(slot):
        row = jnp.maximum(i - 1, 0) // tiles_per_batch
        gt2 = ada_ref[pl.ds(row, 1), 5 * D:6 * D]
        o_ref[...] = x1_ref[slot] + gt2 * (_rms(acc_ref[...]) * gqf_ref[...])

    @pl.when(i == 0)
    def _():
        mix_matmul()
        for si in range(MIX_SLICES):
            mix_rows_slice(0, si)

    for parity in range(2):
        @pl.when((i % 2 == parity) & (i > 0))
        def _():
            per_gap = -(-MIX_SLICES // (n_chunks - 1))
            for ci in range(n_chunks):
                anchor = ffn_chunk(1 - parity, ci)
                if ci == 0:
                    mix_matmul()
                else:
                    for si in range((ci - 1) * per_gap, min(ci * per_gap, MIX_SLICES)):
                        mix_rows_slice(parity, si, anchor[0:1, :])
            ffn_finish(1 - parity)


def _mix_ffn(att_t, gla, x2d, ada, g_post_mix, g_pre_ffn, g_post_ffn,
             w_out_a, w_out_g, w_ffn_in, w_ffn_out, *, tm, tiles_per_batch):
    n = x2d.shape[0]
    tpb = tiles_per_batch
    nt = n // tm
    const = lambda i: (0, 0)
    resident = functools.partial(pl.BlockSpec, index_map=const, pipeline_mode=pl.Buffered(1))
    cur = lambda i: jnp.minimum(i, nt - 1)
    done = lambda i: jnp.maximum(i - 1, 0)
    return pl.pallas_call(
        functools.partial(_mix_ffn_kernel, tiles_per_batch=tpb, n_tiles=nt),
        out_shape=jax.ShapeDtypeStruct((n, D), F32),
        grid=(nt + 1,),
        in_specs=[pl.BlockSpec((None, NQ, tm), lambda i: (cur(i) // tpb, 0, cur(i) % tpb)),
                  pl.BlockSpec((tm, GV), lambda i: (cur(i), 0)),
                  pl.BlockSpec((tm, D), lambda i: (cur(i), 0)),
                  pl.BlockSpec((8, 6 * D), const),
                  pl.BlockSpec((1, D), const),
                  pl.BlockSpec((1, D), const),
                  pl.BlockSpec((1, D), const),
                  resident((NQ, D)),
                  resident((GV, D)),
                  resident((D, 2 * FFN_HIDDEN)),
                  resident((FFN_HIDDEN, D))],
        out_specs=pl.BlockSpec((tm, D), lambda i: (done(i), 0)),
        scratch_shapes=[pltpu.VMEM((2, tm, D), F32),
                        pltpu.VMEM((2, tm, D), BF16),
                        pltpu.VMEM((tm, D), F32),
                        pltpu.VMEM((tm, D), F32)],
        compiler_params=pltpu.CompilerParams(dimension_semantics=("arbitrary",),
                                             vmem_limit_bytes=VMEM_LIMIT),
        name="mix_ffn",
    )(att_t, gla, x2d, ada, g_post_mix, g_pre_ffn, g_post_ffn,
      w_out_a, w_out_g, w_ffn_in, w_ffn_out)


def _rope_tables(seq):
    half = HEAD_DIM // 2
    inv_freq = ROPE_BASE ** (-jnp.arange(0, half, 2, dtype=F32) / half)
    dim = np.arange(HEAD_DIM)
    freq = jnp.tile(inv_freq, HEAD_DIM // ROPE_BLK)[:, None]
    pos = jnp.arange(seq)[None, :]
    row = (pos // GRID_W).astype(F32)
    col = (pos % GRID_W).astype(F32)
    ang = jnp.where((dim < half)[:, None], row, col) * freq
    cos, sin = jnp.cos(ang), jnp.sin(ang)
    even = ((dim // ROPE_BLK) % 2 == 0)[:, None]
    sa = jnp.where(even, -sin, 0.0)
    sb = jnp.where(even, 0.0, sin)
    return cos, sa, sb


def kernel(x, c, ctx, c_ctx, w_ada, b_ada, g_pre_mix, g_post_mix, g_pre_ffn, g_post_ffn,
           w_in, attn_sink, w_gate_fwd, b_gate_fwd, w_gate_bwd, b_gate_bwd, g_gla_norm,
           w_out, w_ffn_in, w_ffn_out):
    batch, seq, _ = x.shape
    n_ctx = ctx.shape[1]
    depth = w_ada.shape[0]
    assert depth == 1
    l = 0

    c8 = jnp.zeros((8, D), F32).at[0:batch].set(c).at[4].set(c_ctx)
    ada = _ada(c8, w_ada[l], b_ada[l][None, :])

    wi = w_in[l]
    g0 = NQ + 2 * NKV
    w_qkv_t = wi[:, 0:g0].T.astype(BF16)
    gk0 = g0 + GQ
    gg0 = gk0 + KV_W
    w_rest = jnp.concatenate(
        [wi[:, gk0:gg0], wi[:, g0:gk0], wi[:, gg0:],
         jnp.zeros((D, Z_W - 2 * GATE_RANK), F32)], axis=1).astype(BF16)
    w_gate_p = jnp.zeros((Z_W, LA_W), F32)
    w_gate_p = w_gate_p.at[0:GATE_RANK, 0:GQ].set(w_gate_fwd[l])
    w_gate_p = w_gate_p.at[GATE_RANK:2 * GATE_RANK, GQ:].set(w_gate_bwd[l]).astype(BF16)
    b_gate_p = jnp.concatenate([b_gate_fwd[l], b_gate_bwd[l]])[None, :]

    tm = 512
    tables = _rope_tables(seq)
    x2d = x.reshape(batch * seq, D)
    g_pre = g_pre_mix[l][None, :]
    qt, k, vt, gin, lah, lal = _inproj(x2d, ada, g_pre, w_qkv_t, w_rest, w_gate_p, b_gate_p,
                                      tables, batch=batch, tokens=seq, tm=INPROJ_TM)
    _, k_c, vt_c, gin_c, lah_c, lal_c = _inproj(
        ctx.reshape(batch * n_ctx, D), ada, g_pre, w_qkv_t, w_rest, w_gate_p, b_gate_p, None,
        batch=batch, tokens=n_ctx, tm=n_ctx)

    att_t = _attn(attn_sink[l], qt, k, vt, k_c, vt_c, batch=batch, seq=seq, n_ctx=n_ctx)

    gn_tiled = jnp.tile(g_gla_norm[l], GLA_HEADS)[None, :]
    gla = _gla(gin, lah, lal, gin_c, lah_c, lal_c, gn_tiled, batch=batch, seq=seq, n_ctx=n_ctx)

    wo = w_out[l]
    out = _mix_ffn(att_t, gla.reshape(batch * seq, GV), x2d, ada,
                   g_post_mix[l][None, :], g_pre_ffn[l][None, :], g_post_ffn[l][None, :],
                   wo[:NQ].astype(BF16), wo[NQ:].astype(BF16),
                   w_ffn_in[l].astype(BF16), w_ffn_out[l].astype(BF16),
                   tm=tm, tiles_per_batch=seq // tm)
    return out.reshape(batch, seq, D)
```

```python
import functools

import jax
import jax.numpy as jnp
import numpy as np
from jax import lax
from jax.experimental import pallas as pl
from jax.experimental.pallas import tpu as pltpu

D = 1024
HEAD_DIM = 64
ATT_HEADS = 8
ATT_KV_HEADS = 2
GROUP = ATT_HEADS // ATT_KV_HEADS
WINDOW = 128
GRID_W = 64
ROPE_BASE = 10000.0
ROPE_BLK = HEAD_DIM // 4
GLA_HEADS = 8
GLA_DK = 32
GLA_DV = 64
CHUNK = 64
GATE_RANK = 16
GATE_TAU = 16.0
FFN_HIDDEN = 2816
NEG_INF = -1e30
EPS = 1e-6
LOG2E = 1.4426950408889634
Q_SCALE = LOG2E * HEAD_DIM ** -0.5
LA_SCALE = LOG2E / GATE_TAU

NQ = ATT_HEADS * HEAD_DIM
NKV = ATT_KV_HEADS * HEAD_DIM
GQ = GLA_HEADS * GLA_DK
GV = GLA_HEADS * GLA_DV
GIN_W = 2 * GQ + 2 * GV
KV_W = GQ + GV
K_COLS = slice(0, GQ)
V_COLS = slice(GQ, KV_W)
Q_COLS = slice(KV_W, KV_W + GQ)
G_COLS = slice(KV_W + GQ, GIN_W)
Z_W = 128
LA_W = 2 * GQ
REST_W = GIN_W + Z_W

LANES = 128
VMEM_LIMIT = 56 * 1024 * 1024

BF16 = jnp.bfloat16
F32 = jnp.float32


def _dot(a, b):
    return jnp.dot(a, b, preferred_element_type=F32)


def _dot_nt(a, b):
    return lax.dot_general(a, b, (((1,), (1,)), ((), ())), preferred_element_type=F32)


def _dot_tn(a, b):
    return lax.dot_general(a, b, (((0,), (0,)), ((), ())), preferred_element_type=F32)


def _rms(x):
    return x * lax.rsqrt(jnp.mean(x * x, axis=-1, keepdims=True) + EPS)


def _silu(x):
    return x * (1.0 / (1.0 + jnp.exp(-x)))


def _ada_kernel(c_ref, w_ref, b_ref, o_ref):
    a = _silu(c_ref[...]).astype(BF16)
    o_ref[...] = _dot(a, w_ref[...].astype(BF16)) + b_ref[...]


def _ada(c8, w_ada, b_ada):
    n = w_ada.shape[1]
    bn = 1024
    return pl.pallas_call(
        _ada_kernel,
        out_shape=jax.ShapeDtypeStruct((8, n), F32),
        grid=(n // bn,),
        in_specs=[pl.BlockSpec((8, D), lambda j: (0, 0)),
                  pl.BlockSpec((D, bn), lambda j: (0, j)),
                  pl.BlockSpec((1, bn), lambda j: (0, j))],
        out_specs=pl.BlockSpec((8, bn), lambda j: (0, j)),
        compiler_params=pltpu.CompilerParams(dimension_semantics=("arbitrary",)),
        name="ada",
    )(c8, w_ada, b_ada)


NORM_ROWS = 128
INPROJ_TM = 1024


def _inproj_kernel(*refs, tiles_per_batch, rope):
    if rope:
        (x_ref, ada_ref, g_ref, wqkv_ref, w_ref, wg_ref, bg_ref,
         cost_ref, sat_ref, sbt_ref,
         qt_ref, k_ref, vt_ref, gin_ref, lah_ref, lal_ref) = refs
        row = pl.program_id(0) // tiles_per_batch
    else:
        (x_ref, ada_ref, g_ref, wqkv_ref, w_ref, wg_ref, bg_ref,
         qt_ref, k_ref, vt_ref, gin_ref, lah_ref, lal_ref) = refs
        row = 4
    ada = ada_ref[pl.ds(row, 1), :]
    sh1 = ada[:, 0:D]
    sc1 = ada[:, D:2 * D]
    tm = x_ref.shape[0]
    hb = jnp.concatenate(
        [((_rms(x_ref[r0:r0 + NORM_ROWS, :]) * g_ref[...]) * (1.0 + sc1) + sh1).astype(BF16)
         for r0 in range(0, tm, NORM_ROWS)], axis=0)

    z = _dot(hb, w_ref[:, GIN_W:REST_W])
    qkv = _dot_nt(wqkv_ref[...], hb)
    logits = _dot(z.astype(BF16), wg_ref[...]) + bg_ref[...]
    gate = _dot(hb, w_ref[:, G_COLS])
    gin = _dot(hb, w_ref[:, 0:KV_W + GQ])

    def head_t(hd):
        xh = qkv[hd * HEAD_DIM:(hd + 1) * HEAD_DIM]
        if not rope:
            return xh
        up = jnp.concatenate([xh[ROPE_BLK:], xh[:ROPE_BLK]], axis=0)
        dn = jnp.concatenate([xh[HEAD_DIM - ROPE_BLK:], xh[:HEAD_DIM - ROPE_BLK]], axis=0)
        return xh * cost_ref[...] + up * sat_ref[...] + dn * sbt_ref[...]

    for hd in range(ATT_HEADS):
        qh = head_t(hd)
        if rope:
            qh = qh * Q_SCALE
        qt_ref[hd * HEAD_DIM:(hd + 1) * HEAD_DIM, :] = qh.astype(BF16)
    kt = jnp.concatenate([head_t(ATT_HEADS + hd) for hd in range(ATT_KV_HEADS)], axis=0)
    k_ref[...] = kt.T.astype(BF16)
    vt_ref[...] = qkv[NQ + NKV:NQ + 2 * NKV].astype(BF16)

    log_sig = jnp.minimum(logits, 0.0) - jnp.log(1.0 + jnp.exp(-jnp.abs(logits)))
    la = log_sig * LA_SCALE
    hi = la.astype(BF16)
    lah_ref[...] = hi
    lal_ref[...] = (la - hi.astype(F32)).astype(BF16)

    gin_ref[:, G_COLS] = _silu(gate).astype(BF16)
    gin_ref[:, 0:KV_W] = gin[:, 0:KV_W].astype(BF16)
    gin_ref[:, Q_COLS] = (gin[:, Q_COLS] * (GLA_DK ** -0.5)).astype(BF16)


def _inproj(x2d, ada, g_pre, w_qkv_t, w_rest, w_gate_p, b_gate_p, tables, *,
            batch, tokens, tm):
    n = x2d.shape[0]
    tpb = tokens // tm
    rope = tables is not None
    const = lambda i: (0, 0)
    in_specs = [pl.BlockSpec((tm, D), lambda i: (i, 0)),
                pl.BlockSpec((8, 6 * D), const),
                pl.BlockSpec((1, D), const),
                pl.BlockSpec((NQ + 2 * NKV, D), const),
                pl.BlockSpec((D, REST_W), const),
                pl.BlockSpec((Z_W, LA_W), const),
                pl.BlockSpec((1, LA_W), const)]
    args = [x2d, ada, g_pre, w_qkv_t, w_rest, w_gate_p, b_gate_p]
    if rope:
        in_specs += [pl.BlockSpec((HEAD_DIM, tm), lambda i: (0, i % tpb))] * 3
        args += list(tables)
    row_blk = lambda w: pl.BlockSpec((tm, w), lambda i: (i, 0))
    col_blk = lambda w: pl.BlockSpec((None, w, tm), lambda i: (i // tpb, 0, i % tpb))
    return pl.pallas_call(
        functools.partial(_inproj_kernel, tiles_per_batch=tpb, rope=rope),
        out_shape=(jax.ShapeDtypeStruct((batch, NQ, tokens), BF16),
                   jax.ShapeDtypeStruct((n, NKV), BF16),
                   jax.ShapeDtypeStruct((batch, NKV, tokens), BF16),
                   jax.ShapeDtypeStruct((n, GIN_W), BF16),
                   jax.ShapeDtypeStruct((n, LA_W), BF16),
                   jax.ShapeDtypeStruct((n, LA_W), BF16)),
        grid=(n // tm,),
        in_specs=in_specs,
        out_specs=(col_blk(NQ), row_blk(NKV), col_blk(NKV),
                   row_blk(GIN_W), row_blk(LA_W), row_blk(LA_W)),
        compiler_params=pltpu.CompilerParams(dimension_semantics=("arbitrary",),
                                             vmem_limit_bytes=VMEM_LIMIT),
        name="inproj_lat" if rope else "inproj_ctx",
    )(*args)


ATT_SUB = 8
ATT_Q = 64
ONES_ROWS = 16


def _attn_kernel(sink_ref, qt_ref, kp_ref, kc_ref, kn_ref, kx_ref,
                 vp_ref, vc_ref, vn_ref, vx_ref, o_ref, *, n_steps):
    i = pl.program_id(1)
    bq = WINDOW
    nwin = ATT_Q + 2 * WINDOW
    nq = GROUP * ATT_Q
    n_sub = ATT_SUB * bq // ATT_Q
    c = lax.broadcasted_iota(jnp.int32, (ATT_Q, nq), 0)
    r = lax.broadcasted_iota(jnp.int32, (ATT_Q, nq), 1) % ATT_Q
    behind_ok = c >= r
    ahead_ok = c <= r
    lane = lax.broadcasted_iota(jnp.int32, (1, nq), 1)
    zeros_q = jnp.zeros((HEAD_DIM, nq), BF16)
    zeros_e = jnp.zeros((ATT_Q, nq), BF16)

    kc_all = jnp.concatenate([kp_ref[...], kc_ref[...], kn_ref[...]], axis=0)
    vc_all = jnp.concatenate([vp_ref[...], vc_ref[...], vn_ref[...]], axis=1)
    kx = kx_ref[...]
    vx = vx_ref[...]

    vaug = []
    for a in range(ATT_SUB):
        vwin = jnp.concatenate([vc_all[:, a * bq:(a + 3) * bq], vx], axis=1)
        ones = jnp.ones((ONES_ROWS, vwin.shape[1]), BF16)
        vaug.append([jnp.concatenate([vwin[kvh * HEAD_DIM:(kvh + 1) * HEAD_DIM], ones], axis=0)
                     for kvh in range(ATT_KV_HEADS)])

    tiles = [(s, kvh) for s in range(n_sub) for kvh in range(ATT_KV_HEADS)]

    def score(tile):
        s, kvh = tile
        q0 = s * ATT_Q
        h0 = kvh * GROUP
        qt = jnp.concatenate(
            [qt_ref[(h0 + u) * HEAD_DIM:(h0 + u + 1) * HEAD_DIM, q0:q0 + ATT_Q]
             for u in range(GROUP)], axis=1)
        rhs = jnp.concatenate([qt, zeros_q] if kvh == 0 else [zeros_q, qt], axis=0)
        kwin = jnp.concatenate([kc_all[q0:q0 + nwin], kx], axis=0)
        return _dot(kwin, rhs)

    def window_blocks(s, st):
        blocks = []
        for b in range(nwin // ATT_Q):
            blk = st[b * ATT_Q:(b + 1) * ATT_Q]
            if b == 0:
                blk = jnp.where(behind_ok, blk, NEG_INF)
            if b == nwin // ATT_Q - 1:
                blk = jnp.where(ahead_ok, blk, NEG_INF)
            first_key = s * ATT_Q - WINDOW + b * ATT_Q
            if first_key < 0:
                blk = jnp.where(i > 0, blk, NEG_INF)
            if first_key >= ATT_SUB * bq:
                blk = jnp.where(i < n_steps - 1, blk, NEG_INF)
            blocks.append(blk)
        return blocks

    def softmax_numerators(tile, st):
        s, kvh = tile
        h0 = kvh * GROUP
        parts = window_blocks(s, st) + [st[nwin:]]
        sink = sink_ref[h0 + GROUP - 1]
        for u in range(GROUP - 2, -1, -1):
            sink = jnp.where(lane < (u + 1) * ATT_Q, sink_ref[h0 + u], sink)
        sink = sink * LOG2E
        m = sink
        for p in parts:
            m = jnp.maximum(m, jnp.max(p, axis=0, keepdims=True))
        e = [jnp.exp2(p - m).astype(BF16) for p in parts]
        pad = [zeros_e]
        et = jnp.concatenate((e[:-1] + pad if s % 2 == 0 else pad + e[:-1]) + e[-1:], axis=0)
        return et, jnp.exp2(sink - m)

    def values(tile, et, e_sink):
        s, kvh = tile
        q0 = s * ATT_Q
        ot = _dot(vaug[s // 2][kvh], et)
        denom = ot[HEAD_DIM:HEAD_DIM + 1] + e_sink
        on = ot[0:HEAD_DIM] * (1.0 / denom)
        for u in range(GROUP):
            hd = kvh * GROUP + u
            o_ref[hd * HEAD_DIM:(hd + 1) * HEAD_DIM, q0:q0 + ATT_Q] = (
                on[:, u * ATT_Q:(u + 1) * ATT_Q].astype(BF16))

    scores = [score(t) for t in tiles]
    probs = [softmax_numerators(t, st) for t, st in zip(tiles, scores)]
    for t, (et, e_sink) in zip(tiles, probs):
        values(t, et, e_sink)


def _attn(sink, qt, k, vt, k_ctx, vt_ctx, *, batch, seq, n_ctx):
    bq = WINDOW
    bs = ATT_SUB * bq
    nb = seq // bq
    ns = seq // bs
    k3 = k.reshape(batch, seq, NKV)
    kx3 = k_ctx.reshape(batch, n_ctx, NKV)
    prev = lambda i: jnp.maximum(ATT_SUB * i - 1, 0)
    nxt = lambda i: jnp.minimum(ATT_SUB * (i + 1), nb - 1)
    return pl.pallas_call(
        functools.partial(_attn_kernel, n_steps=ns),
        out_shape=jax.ShapeDtypeStruct((batch, NQ, seq), BF16),
        grid=(batch, ns),
        in_specs=[
            pl.BlockSpec(memory_space=pltpu.SMEM),
            pl.BlockSpec((None, NQ, bs), lambda b, i: (b, 0, i)),
            pl.BlockSpec((None, bq, NKV), lambda b, i: (b, prev(i), 0)),
            pl.BlockSpec((None, bs, NKV), lambda b, i: (b, i, 0)),
            pl.BlockSpec((None, bq, NKV), lambda b, i: (b, nxt(i), 0)),
            pl.BlockSpec((None, n_ctx, NKV), lambda b, i: (b, 0, 0)),
            pl.BlockSpec((None, NKV, bq), lambda b, i: (b, 0, prev(i))),
            pl.BlockSpec((None, NKV, bs), lambda b, i: (b, 0, i)),
            pl.BlockSpec((None, NKV, bq), lambda b, i: (b, 0, nxt(i))),
            pl.BlockSpec((None, NKV, n_ctx), lambda b, i: (b, 0, 0)),
        ],
        out_specs=pl.BlockSpec((None, NQ, bs), lambda b, i: (b, 0, i)),
        compiler_params=pltpu.CompilerParams(dimension_semantics=("arbitrary", "arbitrary"),
                                             vmem_limit_bytes=VMEM_LIMIT),
        name="attn",
    )(sink, qt, k3, k3, k3, kx3, vt, vt, vt, vt_ctx)


GRP_K = 4 * GLA_DK
GRP_V = 4 * GLA_DV
N_GRP = GLA_HEADS // 4
GLA_TB = 512
GLA_STATE_TB = 1024
CUM_BLK = 256


def _gla_constants():
    t = CUM_BLK
    r = np.arange(t)[:, None]
    c = np.arange(t)[None, :]
    same_chunk = (r // CHUNK) == (c // CHUNK)
    tri_l = (same_chunk & (c <= r)).astype(np.float32)
    tri_u = (same_chunk & (c >= r)).astype(np.float32)
    rk = np.arange(4 * CHUNK)[:, None] // CHUNK
    kmask = (rk == (np.arange(GRP_K)[None, :] // GLA_DK)).astype(np.float32)
    vmask = (rk == (np.arange(GRP_V)[None, :] // GLA_DV)).astype(np.float32)
    smask = ((np.arange(GRP_K)[:, None] // GLA_DK)
             == (np.arange(GRP_V)[None, :] // GLA_DV)).astype(np.float32)
    ai = np.arange(CHUNK)[:, None]
    aj = np.arange(4 * CHUNK)[None, :] % CHUNK
    causal = (aj <= ai).astype(np.float32)
    anti = (aj >= ai).astype(np.float32)
    hmean = ((np.arange(GRP_V)[:, None] // GLA_DV)
             == (np.arange(GRP_V)[None, :] // GLA_DV)).astype(np.float32) / GLA_DV
    return (jnp.asarray(tri_l, BF16), jnp.asarray(tri_u, BF16), jnp.asarray(kmask, BF16),
            jnp.asarray(vmask, BF16), jnp.asarray(smask, F32), jnp.asarray(causal, F32),
            jnp.asarray(anti, F32), jnp.asarray(hmean, BF16))


def _gk(g):
    return slice(g * GRP_K, (g + 1) * GRP_K)


def _gv(g):
    return slice(g * GRP_V, (g + 1) * GRP_V)


def _chunk_slices(t):
    return [slice(ci * CHUNK, (ci + 1) * CHUNK) for ci in range(t // CHUNK)]


def _cum_and_totals(la_hi, la_lo, tri, *, reverse):
    parts = [slice(p, p + CUM_BLK) for p in range(0, la_hi.shape[0], CUM_BLK)]
    cum = jnp.concatenate([_dot(tri, la_hi[p]) + _dot(tri, la_lo[p]) for p in parts], axis=0)
    edge = 0 if reverse else CHUNK - 1
    tots = [cum[sl.start + edge:sl.start + edge + 1] for sl in _chunk_slices(cum.shape[0])]
    return cum, tots


def _bcast_chunks(rows, width):
    return jnp.concatenate([jnp.broadcast_to(r, (CHUNK, width)) for r in rows], axis=0)


def _decay(x, log2_factor):
    return x * jnp.exp2(log2_factor).astype(BF16)


def _state_increments(kd, v, smask):
    return [[_dot_tn(kd[sl, _gk(g)], v[sl, _gv(g)]) * smask for g in range(N_GRP)]
            for sl in _chunk_slices(kd.shape[0])]


def _decay_columns(tots):
    width = tots[0].shape[1]
    row = lax.broadcasted_iota(jnp.int32, (8, width), 0)
    rows = jnp.zeros((8, width), F32)
    for ci, t in enumerate(tots):
        rows = jnp.where(row == ci, jnp.broadcast_to(t, (8, width)), rows)
    return jnp.exp2(rows).T


def _interleave(stage_gens):
    gens = list(stage_gens)
    done = [False] * len(gens)
    rnd = 0
    while not all(done):
        for gi, gen in enumerate(gens):
            if rnd >= gi and not done[gi]:
                try:
                    next(gen)
                except StopIteration:
                    done[gi] = True
        rnd += 1


def _state_stages(st_ref, k, v, la_hi, la_lo, tri, smask, *, reverse,
                  store_ref=None, store_base=None):
    nch = k.shape[0] // CHUNK
    cum, tots = _cum_and_totals(la_hi, la_lo, tri, reverse=reverse)
    yield
    kd = _decay(k, _bcast_chunks(tots, GQ) - cum)
    dcol = _decay_columns(tots)
    yield
    ds = _state_increments(kd, v, smask)
    yield
    state = [st_ref[g] for g in range(N_GRP)]
    for ci in (range(nch - 1, -1, -1) if reverse else range(nch)):
        for g in range(N_GRP):
            if store_ref is not None:
                store_ref[store_base + ci, g] = state[g].astype(BF16)
            state[g] = state[g] * dcol[_gk(g), ci:ci + 1] + ds[ci][g]
    for g in range(N_GRP):
        st_ref[g] = state[g]


def _output_stages(q, k, v, la_f, la_b, gate, st_ref, bwd_state, consts, finish):
    tri_l, tri_u, kmask, vmask, smask, causal, anti, hmean, gn = consts
    chunks = _chunk_slices(q.shape[0])

    cum_f, tots_f = _cum_and_totals(la_f[0], la_f[1], tri_l, reverse=False)
    cum_b, tots_b = _cum_and_totals(la_b[0], la_b[1], tri_u, reverse=True)
    yield

    qd_f = _decay(q, cum_f)
    qd_b = _decay(q, cum_b)
    ki_f = _decay(k, -cum_f)
    ki_b = _decay(k, -cum_b)
    kd_f = _decay(k, _bcast_chunks(tots_f, GQ) - cum_f)
    dcol_f = _decay_columns(tots_f)
    v_bd = [[jnp.concatenate([v[sl, _gv(g)]] * 4, axis=0) * vmask for g in range(N_GRP)]
            for sl in chunks]
    yield

    def scores(qd, ki, sl, g):
        ki_bd = jnp.concatenate([ki[sl, _gk(g)]] * 4, axis=0) * kmask
        return _dot_nt(qd[sl, _gk(g)], ki_bd)

    a_f = [[scores(qd_f, ki_f, sl, g) for g in range(N_GRP)] for sl in chunks]
    a_b = [[scores(qd_b, ki_b, sl, g) for g in range(N_GRP)] for sl in chunks]
    ds = _state_increments(kd_f, v, smask)
    yield

    a = [[(af[g] * causal + ab[g] * anti).astype(BF16) for g in range(N_GRP)]
         for af, ab in zip(a_f, a_b)]
    state = [st_ref[g] for g in range(N_GRP)]
    st_in = []
    for ci in range(len(chunks)):
        st_in.append([s.astype(BF16) for s in state])
        state = [state[g] * dcol_f[_gk(g), ci:ci + 1] + ds[ci][g] for g in range(N_GRP)]
    for g in range(N_GRP):
        st_ref[g] = state[g]
    yield

    outs = []
    for ci, sl in enumerate(chunks):
        parts = []
        for g in range(N_GRP):
            intra = _dot(a[ci][g], v_bd[ci][g])
            qd2 = jnp.concatenate([qd_f[sl, _gk(g)], qd_b[sl, _gk(g)]], axis=1)
            st2 = jnp.concatenate([st_in[ci][g], bwd_state(ci, g)], axis=0)
            parts.append(intra + _dot(qd2, st2))
        outs.append(jnp.concatenate(parts, axis=1))
    o = jnp.concatenate(outs, axis=0)
    yield

    o2 = (o * o).astype(BF16)
    ms = jnp.concatenate([_dot(o2[:, _gv(g)], hmean) for g in range(N_GRP)], axis=1)
    yield
    finish((o * lax.rsqrt(ms + EPS) * gn * gate.astype(F32)).astype(BF16))


def _gla_kernel(skv_ref, slah_ref, slal_ref,
                kv_ref, qg_ref, lahf_ref, lahb_ref, lalf_ref, lalb_ref,
                ginc_ref, lahc_ref, lalc_ref, gn_ref,
                tril_ref, triu_ref, kmask_ref, vmask_ref, smask_ref, causal_ref, anti_ref,
                hmean_ref, o_ref, st_ref, sb_ref, *, n_state_steps):
    step = pl.program_id(1)
    ch_per_half = CUM_BLK // CHUNK
    fwd = slice(0, GQ)
    bwd = slice(GQ, 2 * GQ)

    def ctx_state(cols, tri_ref, reverse):
        st_ref[...] = jnp.zeros_like(st_ref)
        ctx_blocks = [slice(r0, r0 + CUM_BLK) for r0 in range(0, ginc_ref.shape[0], CUM_BLK)]
        for rs in (ctx_blocks[::-1] if reverse else ctx_blocks):
            for _ in _state_stages(st_ref, ginc_ref[rs, K_COLS], ginc_ref[rs, V_COLS],
                                   lahc_ref[rs, cols], lalc_ref[rs, cols], tri_ref[...],
                                   smask_ref[...], reverse=reverse):
                pass

    @pl.when(step < n_state_steps)
    def _backward_states():
        @pl.when(step == 0)
        def _():
            ctx_state(bwd, triu_ref, True)

        blk = n_state_steps - 1 - step
        parts = [slice(r0, r0 + CUM_BLK) for r0 in range(0, GLA_STATE_TB, CUM_BLK)]
        _interleave([
            _state_stages(st_ref, skv_ref[rs, K_COLS], skv_ref[rs, V_COLS],
                          slah_ref[rs, :], slal_ref[rs, :], triu_ref[...], smask_ref[...],
                          reverse=True, store_ref=sb_ref,
                          store_base=blk * (GLA_STATE_TB // CHUNK) + pi * ch_per_half)
            for pi, rs in reversed(list(enumerate(parts)))])

    @pl.when(step >= n_state_steps)
    def _outputs():
        j = step - n_state_steps

        @pl.when(j == 0)
        def _():
            ctx_state(fwd, tril_ref, False)

        consts = (tril_ref[...], triu_ref[...], kmask_ref[...], vmask_ref[...],
                  smask_ref[...], causal_ref[...], anti_ref[...], hmean_ref[...], gn_ref[...])
        halves = [slice(r0, r0 + CUM_BLK) for r0 in range(0, GLA_TB, CUM_BLK)]

        def half(hi, rs):
            base = j * (GLA_TB // CHUNK) + hi * ch_per_half

            def finish(val):
                o_ref[rs, :] = val

            return _output_stages(
                qg_ref[rs, 0:GQ], kv_ref[rs, K_COLS], kv_ref[rs, V_COLS],
                (lahf_ref[rs, :], lalf_ref[rs, :]), (lahb_ref[rs, :], lalb_ref[rs, :]),
                qg_ref[rs, GQ:KV_W], st_ref, lambda ci, g: sb_ref[base + ci, g], consts, finish)

        _interleave([half(hi, rs) for hi, rs in enumerate(halves)])


def _gla(gin, la_hi, la_lo, gin_ctx, la_hi_ctx, la_lo_ctx, gn_tiled, *, batch, seq, n_ctx):
    assert n_ctx % CUM_BLK == 0 and n_ctx // CHUNK <= 8 and GLA_TB // CHUNK <= 8
    tb, stb = GLA_TB, GLA_STATE_TB
    ns, nb = seq // stb, seq // tb
    nch = seq // CHUNK
    r3 = lambda a, t: a.reshape(batch, t, a.shape[-1])
    consts = _gla_constants()

    def state_blk(col):
        return lambda b, s: (b, jnp.maximum(ns - 1 - s, 0), col)

    def out_blk(col):
        return lambda b, s: (b, jnp.maximum(s - ns, 0), col)

    ctx_map = lambda b, s: (b, 0, 0)
    const_map = lambda b, s: (0, 0)
    gin3, lah3, lal3 = r3(gin, seq), r3(la_hi, seq), r3(la_lo, seq)
    return pl.pallas_call(
        functools.partial(_gla_kernel, n_state_steps=ns),
        out_shape=jax.ShapeDtypeStruct((batch, seq, GV), BF16),
        grid=(batch, ns + nb),
        in_specs=[pl.BlockSpec((None, stb, KV_W), state_blk(0)),
                  pl.BlockSpec((None, stb, GQ), state_blk(1)),
                  pl.BlockSpec((None, stb, GQ), state_blk(1)),
                  pl.BlockSpec((None, tb, KV_W), out_blk(0)),
                  pl.BlockSpec((None, tb, KV_W), out_blk(1)),
                  pl.BlockSpec((None, tb, GQ), out_blk(0)),
                  pl.BlockSpec((None, tb, GQ), out_blk(1)),
                  pl.BlockSpec((None, tb, GQ), out_blk(0)),
                  pl.BlockSpec((None, tb, GQ), out_blk(1)),
                  pl.BlockSpec((None, n_ctx, GIN_W), ctx_map),
                  pl.BlockSpec((None, n_ctx, LA_W), ctx_map),
                  pl.BlockSpec((None, n_ctx, LA_W), ctx_map),
                  pl.BlockSpec((1, GV), const_map)]
                 + [pl.BlockSpec(cst.shape, const_map) for cst in consts],
        out_specs=pl.BlockSpec((None, tb, GV), out_blk(0)),
        scratch_shapes=[pltpu.VMEM((N_GRP, GRP_K, GRP_V), F32),
                        pltpu.VMEM((nch, N_GRP, GRP_K, GRP_V), BF16)],
        compiler_params=pltpu.CompilerParams(
            dimension_semantics=("arbitrary", "arbitrary"),
            vmem_limit_bytes=VMEM_LIMIT),
        name="gla",
    )(gin3, lah3, lal3, gin3, gin3, lah3, lah3, lal3, lal3, r3(gin_ctx, n_ctx),
      r3(la_hi_ctx, n_ctx), r3(la_lo_ctx, n_ctx), gn_tiled, *consts)


FFN_CHUNK = 256
MIX_SLICES = 8


def _mix_ffn_kernel(att_ref, gla_ref, x_ref, ada_ref, gpm_ref, gpf_ref, gqf_ref,
                    woa_ref, wog_ref, wfi_ref, wfo_ref, o_ref, x1_ref, h_ref, y_ref, acc_ref, *,
                    tiles_per_batch, n_tiles):
    i = pl.program_id(0)
    tm = x_ref.shape[0]
    chunks = [(c0, min(c0 + FFN_CHUNK, FFN_HIDDEN)) for c0 in range(0, FFN_HIDDEN, FFN_CHUNK)]
    n_chunks = len(chunks)
    mix_rows = tm // MIX_SLICES

    def mix_matmul():
        y_ref[...] = _dot_tn(att_ref[...], woa_ref[...]) + _dot(gla_ref[...], wog_ref[...])

    def mix_rows_slice(slot, si, anchor=None):
        rs = slice(si * mix_rows, (si + 1) * mix_rows)
        row = jnp.minimum(i, n_tiles - 1) // tiles_per_batch
        ada = ada_ref[pl.ds(row, 1), :]
        gt1 = ada[:, 2 * D:3 * D]
        if anchor is not None:
            bits = pltpu.bitcast(anchor, jnp.uint32)
            sixteen = jnp.uint32(16)
            gt1 = gt1 + pltpu.bitcast(
                lax.shift_right_logical(lax.shift_right_logical(bits, sixteen), sixteen), F32)
        sh2 = ada[:, 3 * D:4 * D]
        sc2 = ada[:, 4 * D:5 * D]
        x1 = x_ref[rs, :] + gt1 * (_rms(y_ref[rs, :]) * gpm_ref[...])
        x1_ref[slot, rs, :] = x1
        h_ref[slot, rs, :] = ((_rms(x1) * gpf_ref[...]) * (1.0 + sc2) + sh2).astype(BF16)

    def ffn_chunk(slot, ci):
        c0, c1 = chunks[ci]
        h = h_ref[slot]
        g = _dot(h, wfi_ref[:, c0:c1])
        u = _dot(h, wfi_ref[:, FFN_HIDDEN + c0:FFN_HIDDEN + c1])
        a = (_silu(g) * u).astype(BF16)
        part = _dot(a, wfo_ref[c0:c1, :])
        if ci == 0:
            acc_ref[...] = part
        else:
            acc_ref[...] += part
        return part[0:8, :]

    def ffn_finish(slot):
        row = jnp.maximum(i - 1, 0) // tiles_per_batch
        gt2 = ada_ref[pl.ds(row, 1), 5 * D:6 * D]
        o_ref[...] = x1_ref[slot] + gt2 * (_rms(acc_ref[...]) * gqf_ref[...])

    @pl.when(i == 0)
    def _():
        mix_matmul()
        for si in range(MIX_SLICES):
            mix_rows_slice(0, si)

    for parity in range(2):
        @pl.when((i % 2 == parity) & (i > 0))
        def _():
            per_gap = -(-MIX_SLICES // (n_chunks - 1))
            for ci in range(n_chunks):
                anchor = ffn_chunk(1 - parity, ci)
                if ci == 0:
                    mix_matmul()
                else:
                    for si in range((ci - 1) * per_gap, min(ci * per_gap, MIX_SLICES)):
                        mix_rows_slice(parity, si, anchor[0:1, :])
            ffn_finish(1 - parity)


def _mix_ffn(att_t, gla, x2d, ada, g_post_mix, g_pre_ffn, g_post_ffn,
             w_out_a, w_out_g, w_ffn_in, w_ffn_out, *, tm, tiles_per_batch):
    n = x2d.shape[0]
    tpb = tiles_per_batch
    nt = n // tm
    const = lambda i: (0, 0)
    resident = functools.partial(pl.BlockSpec, index_map=const, pipeline_mode=pl.Buffered(1))
    cur = lambda i: jnp.minimum(i, nt - 1)
    done = lambda i: jnp.maximum(i - 1, 0)
    return pl.pallas_call(
        functools.partial(_mix_ffn_kernel, tiles_per_batch=tpb, n_tiles=nt),
        out_shape=jax.ShapeDtypeStruct((n, D), F32),
        grid=(nt + 1,),
        in_specs=[pl.BlockSpec((None, NQ, tm), lambda i: (cur(i) // tpb, 0, cur(i) % tpb)),
                  pl.BlockSpec((tm, GV), lambda i: (cur(i), 0)),
                  pl.BlockSpec((tm, D), lambda i: (cur(i), 0)),
                  pl.BlockSpec((8, 6 * D), const),
                  pl.BlockSpec((1, D), const),
                  pl.BlockSpec((1, D), const),
                  pl.BlockSpec((1, D), const),
                  resident((NQ, D)),
                  resident((GV, D)),
                  resident((D, 2 * FFN_HIDDEN)),
                  resident((FFN_HIDDEN, D))],
        out_specs=pl.BlockSpec((tm, D), lambda i: (done(i), 0)),
        scratch_shapes=[pltpu.VMEM((2, tm, D), F32),
                        pltpu.VMEM((2, tm, D), BF16),
                        pltpu.VMEM((tm, D), F32),
                        pltpu.VMEM((tm, D), F32)],
        compiler_params=pltpu.CompilerParams(dimension_semantics=("arbitrary",),
                                             vmem_limit_bytes=VMEM_LIMIT),
        name="mix_ffn",
    )(att_t, gla, x2d, ada, g_post_mix, g_pre_ffn, g_post_ffn,
      w_out_a, w_out_g, w_ffn_in, w_ffn_out)


def _rope_tables(seq):
    half = HEAD_DIM // 2
    inv_freq = ROPE_BASE ** (-np.arange(0, half, 2, dtype=np.float64) / half)
    dim = np.arange(HEAD_DIM)
    freq = np.tile(inv_freq, HEAD_DIM // ROPE_BLK)[:, None]
    pos = np.arange(seq)[None, :]
    ang = np.where((dim < half)[:, None], pos // GRID_W, pos % GRID_W) * freq
    cos, sin = np.cos(ang), np.sin(ang)
    even = ((dim // ROPE_BLK) % 2 == 0)[:, None]
    sa = np.where(even, -sin, 0.0)
    sb = np.where(even, 0.0, sin)
    return tuple(jnp.asarray(t, F32) for t in (cos, sa, sb))


def kernel(x, c, ctx, c_ctx, w_ada, b_ada, g_pre_mix, g_post_mix, g_pre_ffn, g_post_ffn,
           w_in, attn_sink, w_gate_fwd, b_gate_fwd, w_gate_bwd, b_gate_bwd, g_gla_norm,
           w_out, w_ffn_in, w_ffn_out):
    batch, seq, _ = x.shape
    n_ctx = ctx.shape[1]
    depth = w_ada.shape[0]
    assert depth == 1
    l = 0

    c8 = jnp.zeros((8, D), F32).at[0:batch].set(c).at[4].set(c_ctx)
    ada = _ada(c8, w_ada[l], b_ada[l][None, :])

    wi = w_in[l]
    g0 = NQ + 2 * NKV
    w_qkv_t = wi[:, 0:g0].T.astype(BF16)
    gk0 = g0 + GQ
    gg0 = gk0 + KV_W
    w_rest = jnp.concatenate(
        [wi[:, gk0:gg0], wi[:, g0:gk0], wi[:, gg0:],
         jnp.zeros((D, Z_W - 2 * GATE_RANK), F32)], axis=1).astype(BF16)
    w_gate_p = jnp.zeros((Z_W, LA_W), F32)
    w_gate_p = w_gate_p.at[0:GATE_RANK, 0:GQ].set(w_gate_fwd[l])
    w_gate_p = w_gate_p.at[GATE_RANK:2 * GATE_RANK, GQ:].set(w_gate_bwd[l]).astype(BF16)
    b_gate_p = jnp.concatenate([b_gate_fwd[l], b_gate_bwd[l]])[None, :]

    tm = 512
    tables = _rope_tables(seq)
    x2d = x.reshape(batch * seq, D)
    g_pre = g_pre_mix[l][None, :]
    qt, k, vt, gin, lah, lal = _inproj(x2d, ada, g_pre, w_qkv_t, w_rest, w_gate_p, b_gate_p,
                                      tables, batch=batch, tokens=seq, tm=INPROJ_TM)
    _, k_c, vt_c, gin_c, lah_c, lal_c = _inproj(
        ctx.reshape(batch * n_ctx, D), ada, g_pre, w_qkv_t, w_rest, w_gate_p, b_gate_p, None,
        batch=batch, tokens=n_ctx, tm=n_ctx)

    att_t = _attn(attn_sink[l], qt, k, vt, k_c, vt_c, batch=batch, seq=seq, n_ctx=n_ctx)

    gn_tiled = jnp.tile(g_gla_norm[l], GLA_HEADS)[None, :]
    gla = _gla(gin, lah, lal, gin_c, lah_c, lal_c, gn_tiled, batch=batch, seq=seq, n_ctx=n_ctx)

    wo = w_out[l]
    out = _mix_ffn(att_t, gla.reshape(batch * seq, GV), x2d, ada,
                   g_post_mix[l][None, :], g_pre_ffn[l][None, :], g_post_ffn[l][None, :],
                   wo[:NQ].astype(BF16), wo[NQ:].astype(BF16),
                   w_ffn_in[l].astype(BF16), w_ffn_out[l].astype(BF16),
                   tm=tm, tiles_per_batch=seq // tm)
    return out.reshape(batch, seq, D)
```

```python
import functools

import jax
import jax.numpy as jnp
import numpy as np
from jax import lax
from jax.experimental import pallas as pl
from jax.experimental.pallas import tpu as pltpu

D = 1024
HEAD_DIM = 64
ATT_HEADS = 8
ATT_KV_HEADS = 2
GROUP = ATT_HEADS // ATT_KV_HEADS
WINDOW = 128
GRID_W = 64
ROPE_BASE = 10000.0
ROPE_BLK = HEAD_DIM // 4
GLA_HEADS = 8
GLA_DK = 32
GLA_DV = 64
CHUNK = 64
GATE_RANK = 16
GATE_TAU = 16.0
FFN_HIDDEN = 2816
NEG_INF = -1e30
EPS = 1e-6
LOG2E = 1.4426950408889634
Q_SCALE = LOG2E * HEAD_DIM ** -0.5
LA_SCALE = LOG2E / GATE_TAU

NQ = ATT_HEADS * HEAD_DIM
NKV = ATT_KV_HEADS * HEAD_DIM
GQ = GLA_HEADS * GLA_DK
GV = GLA_HEADS * GLA_DV
GIN_W = 2 * GQ + 2 * GV
KV_W = GQ + GV
K_COLS = slice(0, GQ)
V_COLS = slice(GQ, KV_W)
Q_COLS = slice(KV_W, KV_W + GQ)
G_COLS = slice(KV_W + GQ, GIN_W)
Z_W = 128
LA_W = 2 * GQ
REST_W = GIN_W + Z_W

LANES = 128
VMEM_LIMIT = 56 * 1024 * 1024

BF16 = jnp.bfloat16
F32 = jnp.float32


def _dot(a, b):
    return jnp.dot(a, b, preferred_element_type=F32)


def _dot_nt(a, b):
    return lax.dot_general(a, b, (((1,), (1,)), ((), ())), preferred_element_type=F32)


def _dot_tn(a, b):
    return lax.dot_general(a, b, (((0,), (0,)), ((), ())), preferred_element_type=F32)


def _rms(x):
    return x * lax.rsqrt(jnp.mean(x * x, axis=-1, keepdims=True) + EPS)


def _silu(x):
    return x * (1.0 / (1.0 + jnp.exp(-x)))


def _ada_kernel(c_ref, w_ref, b_ref, o_ref):
    a = _silu(c_ref[...]).astype(BF16)
    o_ref[...] = _dot(a, w_ref[...].astype(BF16)) + b_ref[...]


def _ada(c8, w_ada, b_ada):
    n = w_ada.shape[1]
    bn = 1024
    return pl.pallas_call(
        _ada_kernel,
        out_shape=jax.ShapeDtypeStruct((8, n), F32),
        grid=(n // bn,),
        in_specs=[pl.BlockSpec((8, D), lambda j: (0, 0)),
                  pl.BlockSpec((D, bn), lambda j: (0, j)),
                  pl.BlockSpec((1, bn), lambda j: (0, j))],
        out_specs=pl.BlockSpec((8, bn), lambda j: (0, j)),
        compiler_params=pltpu.CompilerParams(dimension_semantics=("arbitrary",)),
        name="ada",
    )(c8, w_ada, b_ada)


NORM_ROWS = 128
INPROJ_TM = 1024


def _inproj_kernel(*refs, tiles_per_batch, rope):
    if rope:
        (x_ref, ada_ref, g_ref, wqkv_ref, w_ref, wg_ref, bg_ref,
         cost_ref, sat_ref, sbt_ref,
         qt_ref, k_ref, vt_ref, gin_ref, lah_ref, lal_ref) = refs
        row = pl.program_id(0) // tiles_per_batch
    else:
        (x_ref, ada_ref, g_ref, wqkv_ref, w_ref, wg_ref, bg_ref,
         qt_ref, k_ref, vt_ref, gin_ref, lah_ref, lal_ref) = refs
        row = 4
    ada = ada_ref[pl.ds(row, 1), :]
    sh1 = ada[:, 0:D]
    sc1 = ada[:, D:2 * D]
    tm = x_ref.shape[0]
    hb = jnp.concatenate(
        [((_rms(x_ref[r0:r0 + NORM_ROWS, :]) * g_ref[...]) * (1.0 + sc1) + sh1).astype(BF16)
         for r0 in range(0, tm, NORM_ROWS)], axis=0)

    z = _dot(hb, w_ref[:, GIN_W:REST_W])
    qkv = _dot_nt(wqkv_ref[...], hb)
    logits = _dot(z.astype(BF16), wg_ref[...]) + bg_ref[...]
    gate = _dot(hb, w_ref[:, G_COLS])
    gin = _dot(hb, w_ref[:, 0:KV_W + GQ])

    def head_t(hd):
        xh = qkv[hd * HEAD_DIM:(hd + 1) * HEAD_DIM]
        if not rope:
            return xh
        up = jnp.concatenate([xh[ROPE_BLK:], xh[:ROPE_BLK]], axis=0)
        dn = jnp.concatenate([xh[HEAD_DIM - ROPE_BLK:], xh[:HEAD_DIM - ROPE_BLK]], axis=0)
        return xh * cost_ref[...] + up * sat_ref[...] + dn * sbt_ref[...]

    for hd in range(ATT_HEADS):
        qh = head_t(hd)
        if rope:
            qh = qh * Q_SCALE
        qt_ref[hd * HEAD_DIM:(hd + 1) * HEAD_DIM, :] = qh.astype(BF16)
    kt = jnp.concatenate([head_t(ATT_HEADS + hd) for hd in range(ATT_KV_HEADS)], axis=0)
    k_ref[...] = kt.T.astype(BF16)
    vt_ref[...] = qkv[NQ + NKV:NQ + 2 * NKV].astype(BF16)

    log_sig = jnp.minimum(logits, 0.0) - jnp.log(1.0 + jnp.exp(-jnp.abs(logits)))
    la = log_sig * LA_SCALE
    hi = la.astype(BF16)
    lah_ref[...] = hi
    lal_ref[...] = (la - hi.astype(F32)).astype(BF16)

    gin_ref[:, G_COLS] = _silu(gate).astype(BF16)
    gin_ref[:, 0:KV_W] = gin[:, 0:KV_W].astype(BF16)
    gin_ref[:, Q_COLS] = (gin[:, Q_COLS] * (GLA_DK ** -0.5)).astype(BF16)


def _inproj(x2d, ada, g_pre, w_qkv_t, w_rest, w_gate_p, b_gate_p, tables, *,
            batch, tokens, tm):
    n = x2d.shape[0]
    tpb = tokens // tm
    rope = tables is not None
    const = lambda i: (0, 0)
    in_specs = [pl.BlockSpec((tm, D), lambda i: (i, 0)),
                pl.BlockSpec((8, 6 * D), const),
                pl.BlockSpec((1, D), const),
                pl.BlockSpec((NQ + 2 * NKV, D), const),
                pl.BlockSpec((D, REST_W), const),
                pl.BlockSpec((Z_W, LA_W), const),
                pl.BlockSpec((1, LA_W), const)]
    args = [x2d, ada, g_pre, w_qkv_t, w_rest, w_gate_p, b_gate_p]
    if rope:
        in_specs += [pl.BlockSpec((HEAD_DIM, tm), lambda i: (0, i % tpb))] * 3
        args += list(tables)
    row_blk = lambda w: pl.BlockSpec((tm, w), lambda i: (i, 0))
    col_blk = lambda w: pl.BlockSpec((None, w, tm), lambda i: (i // tpb, 0, i % tpb))
    return pl.pallas_call(
        functools.partial(_inproj_kernel, tiles_per_batch=tpb, rope=rope),
        out_shape=(jax.ShapeDtypeStruct((batch, NQ, tokens), BF16),
                   jax.ShapeDtypeStruct((n, NKV), BF16),
                   jax.ShapeDtypeStruct((batch, NKV, tokens), BF16),
                   jax.ShapeDtypeStruct((n, GIN_W), BF16),
                   jax.ShapeDtypeStruct((n, LA_W), BF16),
                   jax.ShapeDtypeStruct((n, LA_W), BF16)),
        grid=(n // tm,),
        in_specs=in_specs,
        out_specs=(col_blk(NQ), row_blk(NKV), col_blk(NKV),
                   row_blk(GIN_W), row_blk(LA_W), row_blk(LA_W)),
        compiler_params=pltpu.CompilerParams(dimension_semantics=("arbitrary",),
                                             vmem_limit_bytes=VMEM_LIMIT),
        name="inproj_lat" if rope else "inproj_ctx",
    )(*args)


ATT_SUB = 8
ATT_Q = 64
ONES_ROWS = 16


def _attn_kernel(sink_ref, qt_ref, kp_ref, kc_ref, kn_ref, kx_ref,
                 vp_ref, vc_ref, vn_ref, vx_ref, o_ref, *, n_steps):
    i = pl.program_id(1)
    bq = WINDOW
    nwin = ATT_Q + 2 * WINDOW
    nq = GROUP * ATT_Q
    n_sub = ATT_SUB * bq // ATT_Q
    c = lax.broadcasted_iota(jnp.int32, (ATT_Q, nq), 0)
    r = lax.broadcasted_iota(jnp.int32, (ATT_Q, nq), 1) % ATT_Q
    behind_ok = c >= r
    ahead_ok = c <= r
    lane = lax.broadcasted_iota(jnp.int32, (1, nq), 1)
    zeros_q = jnp.zeros((HEAD_DIM, nq), BF16)
    zeros_e = jnp.zeros((ATT_Q, nq), BF16)

    kc_all = jnp.concatenate([kp_ref[...], kc_ref[...], kn_ref[...]], axis=0)
    vc_all = jnp.concatenate([vp_ref[...], vc_ref[...], vn_ref[...]], axis=1)
    kx = kx_ref[...]
    vx = vx_ref[...]

    vaug = []
    for a in range(ATT_SUB):
        vwin = jnp.concatenate([vc_all[:, a * bq:(a + 3) * bq], vx], axis=1)
        ones = jnp.ones((ONES_ROWS, vwin.shape[1]), BF16)
        vaug.append([jnp.concatenate([vwin[kvh * HEAD_DIM:(kvh + 1) * HEAD_DIM], ones], axis=0)
                     for kvh in range(ATT_KV_HEADS)])

    tiles = [(s, kvh) for s in range(n_sub) for kvh in range(ATT_KV_HEADS)]

    def score(tile):
        s, kvh = tile
        q0 = s * ATT_Q
        h0 = kvh * GROUP
        qt = jnp.concatenate(
            [qt_ref[(h0 + u) * HEAD_DIM:(h0 + u + 1) * HEAD_DIM, q0:q0 + ATT_Q]
             for u in range(GROUP)], axis=1)
        rhs = jnp.concatenate([qt, zeros_q] if kvh == 0 else [zeros_q, qt], axis=0)
        kwin = jnp.concatenate([kc_all[q0:q0 + nwin], kx], axis=0)
        return _dot(kwin, rhs)

    def window_blocks(s, st):
        blocks = []
        for b in range(nwin // ATT_Q):
            blk = st[b * ATT_Q:(b + 1) * ATT_Q]
            if b == 0:
                blk = jnp.where(behind_ok, blk, NEG_INF)
            if b == nwin // ATT_Q - 1:
                blk = jnp.where(ahead_ok, blk, NEG_INF)
            first_key = s * ATT_Q - WINDOW + b * ATT_Q
            if first_key < 0:
                blk = jnp.where(i > 0, blk, NEG_INF)
            if first_key >= ATT_SUB * bq:
                blk = jnp.where(i < n_steps - 1, blk, NEG_INF)
            blocks.append(blk)
        return blocks

    def softmax_numerators(tile, st):
        s, kvh = tile
        h0 = kvh * GROUP
        parts = window_blocks(s, st) + [st[nwin:]]
        sink = sink_ref[h0 + GROUP - 1]
        for u in range(GROUP - 2, -1, -1):
            sink = jnp.where(lane < (u + 1) * ATT_Q, sink_ref[h0 + u], sink)
        sink = sink * LOG2E
        m = sink
        for p in parts:
            m = jnp.maximum(m, jnp.max(p, axis=0, keepdims=True))
        e = [jnp.exp2(p - m).astype(BF16) for p in parts]
        pad = [zeros_e]
        et = jnp.concatenate((e[:-1] + pad if s % 2 == 0 else pad + e[:-1]) + e[-1:], axis=0)
        return et, jnp.exp2(sink - m)

    def values(tile, et, e_sink):
        s, kvh = tile
        q0 = s * ATT_Q
        ot = _dot(vaug[s // 2][kvh], et)
        denom = ot[HEAD_DIM:HEAD_DIM + 1] + e_sink
        on = ot[0:HEAD_DIM] * (1.0 / denom)
        for u in range(GROUP):
            hd = kvh * GROUP + u
            o_ref[hd * HEAD_DIM:(hd + 1) * HEAD_DIM, q0:q0 + ATT_Q] = (
                on[:, u * ATT_Q:(u + 1) * ATT_Q].astype(BF16))

    scores = [score(t) for t in tiles]
    probs = [softmax_numerators(t, st) for t, st in zip(tiles, scores)]
    for t, (et, e_sink) in zip(tiles, probs):
        values(t, et, e_sink)


def _attn(sink, qt, k, vt, k_ctx, vt_ctx, *, batch, seq, n_ctx):
    bq = WINDOW
    bs = ATT_SUB * bq
    nb = seq // bq
    ns = seq // bs
    k3 = k.reshape(batch, seq, NKV)
    kx3 = k_ctx.reshape(batch, n_ctx, NKV)
    prev = lambda i: jnp.maximum(ATT_SUB * i - 1, 0)
    nxt = lambda i: jnp.minimum(ATT_SUB * (i + 1), nb - 1)
    return pl.pallas_call(
        functools.partial(_attn_kernel, n_steps=ns),
        out_shape=jax.ShapeDtypeStruct((batch, NQ, seq), BF16),
        grid=(batch, ns),
        in_specs=[
            pl.BlockSpec(memory_space=pltpu.SMEM),
            pl.BlockSpec((None, NQ, bs), lambda b, i: (b, 0, i)),
            pl.BlockSpec((None, bq, NKV), lambda b, i: (b, prev(i), 0)),
            pl.BlockSpec((None, bs, NKV), lambda b, i: (b, i, 0)),
            pl.BlockSpec((None, bq, NKV), lambda b, i: (b, nxt(i), 0)),
            pl.BlockSpec((None, n_ctx, NKV), lambda b, i: (b, 0, 0)),
            pl.BlockSpec((None, NKV, bq), lambda b, i: (b, 0, prev(i))),
            pl.BlockSpec((None, NKV, bs), lambda b, i: (b, 0, i)),
            pl.BlockSpec((None, NKV, bq), lambda b, i: (b, 0, nxt(i))),
            pl.BlockSpec((None, NKV, n_ctx), lambda b, i: (b, 0, 0)),
        ],
        out_specs=pl.BlockSpec((None, NQ, bs), lambda b, i: (b, 0, i)),
        compiler_params=pltpu.CompilerParams(dimension_semantics=("arbitrary", "arbitrary"),
                                             vmem_limit_bytes=VMEM_LIMIT),
        name="attn",
    )(sink, qt, k3, k3, k3, kx3, vt, vt, vt, vt_ctx)


GRP_K = 4 * GLA_DK
GRP_V = 4 * GLA_DV
N_GRP = GLA_HEADS // 4
GLA_TB = 512
GLA_STATE_TB = 1024
CUM_BLK = 256


def _gla_constants():
    t = CUM_BLK
    r = np.arange(t)[:, None]
    c = np.arange(t)[None, :]
    same_chunk = (r // CHUNK) == (c // CHUNK)
    tri_l = (same_chunk & (c <= r)).astype(np.float32)
    tri_u = (same_chunk & (c >= r)).astype(np.float32)
    rk = np.arange(4 * CHUNK)[:, None] // CHUNK
    kmask = (rk == (np.arange(GRP_K)[None, :] // GLA_DK)).astype(np.float32)
    vmask = (rk == (np.arange(GRP_V)[None, :] // GLA_DV)).astype(np.float32)
    smask = ((np.arange(GRP_K)[:, None] // GLA_DK)
             == (np.arange(GRP_V)[None, :] // GLA_DV)).astype(np.float32)
    ai = np.arange(CHUNK)[:, None]
    aj = np.arange(4 * CHUNK)[None, :] % CHUNK
    causal = (aj <= ai).astype(np.float32)
    anti = (aj >= ai).astype(np.float32)
    hmean = ((np.arange(GRP_V)[:, None] // GLA_DV)
             == (np.arange(GRP_V)[None, :] // GLA_DV)).astype(np.float32) / GLA_DV
    return (jnp.asarray(tri_l, BF16), jnp.asarray(tri_u, BF16), jnp.asarray(kmask, BF16),
            jnp.asarray(vmask, BF16), jnp.asarray(smask, F32), jnp.asarray(causal, F32),
            jnp.asarray(anti, F32), jnp.asarray(hmean, BF16))


def _gk(g):
    return slice(g * GRP_K, (g + 1) * GRP_K)


def _gv(g):
    return slice(g * GRP_V, (g + 1) * GRP_V)


def _chunk_slices(t):
    return [slice(ci * CHUNK, (ci + 1) * CHUNK) for ci in range(t // CHUNK)]


def _cum_and_totals(la_hi, la_lo, tri, *, reverse):
    parts = [slice(p, p + CUM_BLK) for p in range(0, la_hi.shape[0], CUM_BLK)]
    cum = jnp.concatenate([_dot(tri, la_hi[p]) + _dot(tri, la_lo[p]) for p in parts], axis=0)
    edge = 0 if reverse else CHUNK - 1
    tots = [cum[sl.start + edge:sl.start + edge + 1] for sl in _chunk_slices(cum.shape[0])]
    return cum, tots


def _bcast_chunks(rows, width):
    return jnp.concatenate([jnp.broadcast_to(r, (CHUNK, width)) for r in rows], axis=0)


def _decay(x, log2_factor):
    return x * jnp.exp2(log2_factor).astype(BF16)


def _state_increments(kd, v, smask):
    return [[_dot_tn(kd[sl, _gk(g)], v[sl, _gv(g)]) * smask for g in range(N_GRP)]
            for sl in _chunk_slices(kd.shape[0])]


def _decay_columns(tots):
    width = tots[0].shape[1]
    row = lax.broadcasted_iota(jnp.int32, (8, width), 0)
    rows = jnp.zeros((8, width), F32)
    for ci, t in enumerate(tots):
        rows = jnp.where(row == ci, jnp.broadcast_to(t, (8, width)), rows)
    return jnp.exp2(rows).T


def _interleave(stage_gens):
    gens = list(stage_gens)
    done = [False] * len(gens)
    rnd = 0
    while not all(done):
        for gi, gen in enumerate(gens):
            if rnd >= gi and not done[gi]:
                try:
                    next(gen)
                except StopIteration:
                    done[gi] = True
        rnd += 1


def _state_stages(st_ref, k, v, la_hi, la_lo, tri, smask, *, reverse,
                  store_ref=None, store_base=None):
    nch = k.shape[0] // CHUNK
    cum, tots = _cum_and_totals(la_hi, la_lo, tri, reverse=reverse)
    yield
    kd = _decay(k, _bcast_chunks(tots, GQ) - cum)
    dcol = _decay_columns(tots)
    yield
    ds = _state_increments(kd, v, smask)
    yield
    state = [st_ref[g] for g in range(N_GRP)]
    for ci in (range(nch - 1, -1, -1) if reverse else range(nch)):
        for g in range(N_GRP):
            if store_ref is not None:
                store_ref[store_base + ci, g] = state[g].astype(BF16)
            state[g] = state[g] * dcol[_gk(g), ci:ci + 1] + ds[ci][g]
    for g in range(N_GRP):
        st_ref[g] = state[g]


def _output_stages(q, k, v, la_f, la_b, gate, st_ref, bwd_state, consts, finish):
    tri_l, tri_u, kmask, vmask, smask, causal, anti, hmean, gn = consts
    chunks = _chunk_slices(q.shape[0])

    cum_f, tots_f = _cum_and_totals(la_f[0], la_f[1], tri_l, reverse=False)
    cum_b, tots_b = _cum_and_totals(la_b[0], la_b[1], tri_u, reverse=True)
    yield

    qd_f = _decay(q, cum_f)
    qd_b = _decay(q, cum_b)
    ki_f = _decay(k, -cum_f)
    ki_b = _decay(k, -cum_b)
    kd_f = _decay(k, _bcast_chunks(tots_f, GQ) - cum_f)
    dcol_f = _decay_columns(tots_f)
    v_bd = [[jnp.concatenate([v[sl, _gv(g)]] * 4, axis=0) * vmask for g in range(N_GRP)]
            for sl in chunks]
    yield

    def scores(qd, ki, sl, g):
        ki_bd = jnp.concatenate([ki[sl, _gk(g)]] * 4, axis=0) * kmask
        return _dot_nt(qd[sl, _gk(g)], ki_bd)

    a_f = [[scores(qd_f, ki_f, sl, g) for g in range(N_GRP)] for sl in chunks]
    a_b = [[scores(qd_b, ki_b, sl, g) for g in range(N_GRP)] for sl in chunks]
    ds = _state_increments(kd_f, v, smask)
    yield

    a = [[(af[g] * causal + ab[g] * anti).astype(BF16) for g in range(N_GRP)]
         for af, ab in zip(a_f, a_b)]
    state = [st_ref[g] for g in range(N_GRP)]
    st_in = []
    for ci in range(len(chunks)):
        st_in.append([s.astype(BF16) for s in state])
        state = [state[g] * dcol_f[_gk(g), ci:ci + 1] + ds[ci][g] for g in range(N_GRP)]
    for g in range(N_GRP):
        st_ref[g] = state[g]
    yield

    outs = []
    for ci, sl in enumerate(chunks):
        parts = []
        for g in range(N_GRP):
            intra = _dot(a[ci][g], v_bd[ci][g])
            qd2 = jnp.concatenate([qd_f[sl, _gk(g)], qd_b[sl, _gk(g)]], axis=1)
            st2 = jnp.concatenate([st_in[ci][g], bwd_state(ci, g)], axis=0)
            parts.append(intra + _dot(qd2, st2))
        outs.append(jnp.concatenate(parts, axis=1))
    o = jnp.concatenate(outs, axis=0)
    yield

    o2 = (o * o).astype(BF16)
    ms = jnp.concatenate([_dot(o2[:, _gv(g)], hmean) for g in range(N_GRP)], axis=1)
    yield
    finish((o * lax.rsqrt(ms + EPS) * gn * gate.astype(F32)).astype(BF16))


def _gla_kernel(skv_ref, slah_ref, slal_ref,
                kv_ref, qg_ref, lahf_ref, lahb_ref, lalf_ref, lalb_ref,
                ginc_ref, lahc_ref, lalc_ref, gn_ref,
                tril_ref, triu_ref, kmask_ref, vmask_ref, smask_ref, causal_ref, anti_ref,
                hmean_ref, o_ref, st_ref, sb_ref, *, n_state_steps):
    step = pl.program_id(1)
    ch_per_half = CUM_BLK // CHUNK
    fwd = slice(0, GQ)
    bwd = slice(GQ, 2 * GQ)

    def ctx_state(cols, tri_ref, reverse):
        st_ref[...] = jnp.zeros_like(st_ref)
        ctx_blocks = [slice(r0, r0 + CUM_BLK) for r0 in range(0, ginc_ref.shape[0], CUM_BLK)]
        for rs in (ctx_blocks[::-1] if reverse else ctx_blocks):
            for _ in _state_stages(st_ref, ginc_ref[rs, K_COLS], ginc_ref[rs, V_COLS],
                                   lahc_ref[rs, cols], lalc_ref[rs, cols], tri_ref[...],
                                   smask_ref[...], reverse=reverse):
                pass

    @pl.when(step < n_state_steps)
    def _backward_states():
        @pl.when(step == 0)
        def _():
            ctx_state(bwd, triu_ref, True)

        blk = n_state_steps - 1 - step
        parts = [slice(r0, r0 + CUM_BLK) for r0 in range(0, GLA_STATE_TB, CUM_BLK)]
        _interleave([
            _state_stages(st_ref, skv_ref[rs, K_COLS], skv_ref[rs, V_COLS],
                          slah_ref[rs, :], slal_ref[rs, :], triu_ref[...], smask_ref[...],
                          reverse=True, store_ref=sb_ref,
                          store_base=blk * (GLA_STATE_TB // CHUNK) + pi * ch_per_half)
            for pi, rs in reversed(list(enumerate(parts)))])

    @pl.when(step >= n_state_steps)
    def _outputs():
        j = step - n_state_steps

        @pl.when(j == 0)
        def _():
            ctx_state(fwd, tril_ref, False)

        consts = (tril_ref[...], triu_ref[...], kmask_ref[...], vmask_ref[...],
                  smask_ref[...], causal_ref[...], anti_ref[...], hmean_ref[...], gn_ref[...])
        halves = [slice(r0, r0 + CUM_BLK) for r0 in range(0, GLA_TB, CUM_BLK)]

        def half(hi, rs):
            base = j * (GLA_TB // CHUNK) + hi * ch_per_half

            def finish(val):
                o_ref[rs, :] = val

            return _output_stages(
                qg_ref[rs, 0:GQ], kv_ref[rs, K_COLS], kv_ref[rs, V_COLS],
                (lahf_ref[rs, :], lalf_ref[rs, :]), (lahb_ref[rs, :], lalb_ref[rs, :]),
                qg_ref[rs, GQ:KV_W], st_ref, lambda ci, g: sb_ref[base + ci, g], consts, finish)

        _interleave([half(hi, rs) for hi, rs in enumerate(halves)])


def _gla(gin, la_hi, la_lo, gin_ctx, la_hi_ctx, la_lo_ctx, gn_tiled, *, batch, seq, n_ctx):
    assert n_ctx % CUM_BLK == 0 and n_ctx // CHUNK <= 8 and GLA_TB // CHUNK <= 8
    tb, stb = GLA_TB, GLA_STATE_TB
    ns, nb = seq // stb, seq // tb
    nch = seq // CHUNK
    r3 = lambda a, t: a.reshape(batch, t, a.shape[-1])
    consts = _gla_constants()

    def state_blk(col):
        return lambda b, s: (b, jnp.maximum(ns - 1 - s, 0), col)

    def out_blk(col):
        return lambda b, s: (b, jnp.maximum(s - ns, 0), col)

    ctx_map = lambda b, s: (b, 0, 0)
    const_map = lambda b, s: (0, 0)
    gin3, lah3, lal3 = r3(gin, seq), r3(la_hi, seq), r3(la_lo, seq)
    return pl.pallas_call(
        functools.partial(_gla_kernel, n_state_steps=ns),
        out_shape=jax.ShapeDtypeStruct((batch, seq, GV), BF16),
        grid=(batch, ns + nb),
        in_specs=[pl.BlockSpec((None, stb, KV_W), state_blk(0)),
                  pl.BlockSpec((None, stb, GQ), state_blk(1)),
                  pl.BlockSpec((None, stb, GQ), state_blk(1)),
                  pl.BlockSpec((None, tb, KV_W), out_blk(0)),
                  pl.BlockSpec((None, tb, KV_W), out_blk(1)),
                  pl.BlockSpec((None, tb, GQ), out_blk(0)),
                  pl.BlockSpec((None, tb, GQ), out_blk(1)),
                  pl.BlockSpec((None, tb, GQ), out_blk(0)),
                  pl.BlockSpec((None, tb, GQ), out_blk(1)),
                  pl.BlockSpec((None, n_ctx, GIN_W), ctx_map),
                  pl.BlockSpec((None, n_ctx, LA_W), ctx_map),
                  pl.BlockSpec((None, n_ctx, LA_W), ctx_map),
                  pl.BlockSpec((1, GV), const_map)]
                 + [pl.BlockSpec(cst.shape, const_map) for cst in consts],
        out_specs=pl.BlockSpec((None, tb, GV), out_blk(0)),
        scratch_shapes=[pltpu.VMEM((N_GRP, GRP_K, GRP_V), F32),
                        pltpu.VMEM((nch, N_GRP, GRP_K, GRP_V), BF16)],
        compiler_params=pltpu.CompilerParams(
            dimension_semantics=("arbitrary", "arbitrary"),
            vmem_limit_bytes=VMEM_LIMIT),
        name="gla",
    )(gin3, lah3, lal3, gin3, gin3, lah3, lah3, lal3, lal3, r3(gin_ctx, n_ctx),
      r3(la_hi_ctx, n_ctx), r3(la_lo_ctx, n_ctx), gn_tiled, *consts)


FFN_CHUNK = 256
MIX_SLICES = 8


def _mix_ffn_kernel(att_ref, gla_ref, x_ref, ada_ref, gpm_ref, gpf_ref, gqf_ref,
                    woa_ref, wog_ref, wfi_ref, wfo_ref, o_ref, x1_ref, h_ref, y_ref, act_ref, *,
                    tiles_per_batch, n_tiles):
    i = pl.program_id(0)
    tm = x_ref.shape[0]
    chunks = [(c0, min(c0 + FFN_CHUNK, FFN_HIDDEN)) for c0 in range(0, FFN_HIDDEN, FFN_CHUNK)]
    n_chunks = len(chunks)
    mix_rows = tm // MIX_SLICES

    def mix_matmul():
        y_ref[...] = _dot_tn(att_ref[...], woa_ref[...]) + _dot(gla_ref[...], wog_ref[...])

    def mix_rows_slice(slot, si, anchor=None):
        rs = slice(si * mix_rows, (si + 1) * mix_rows)
        row = jnp.minimum(i, n_tiles - 1) // tiles_per_batch
        ada = ada_ref[pl.ds(row, 1), :]
        gt1 = ada[:, 2 * D:3 * D]
        if anchor is not None:
            bits = pltpu.bitcast(anchor, jnp.uint32)
            sixteen = jnp.uint32(16)
            zero = pltpu.bitcast(
                lax.shift_right_logical(lax.shift_right_logical(bits, sixteen), sixteen), F32)
            gt1 = gt1 + zero[0:1, 0:1]
        sh2 = ada[:, 3 * D:4 * D]
        sc2 = ada[:, 4 * D:5 * D]
        x1 = x_ref[rs, :] + gt1 * (_rms(y_ref[rs, :]) * gpm_ref[...])
        x1_ref[slot, rs, :] = x1
        h_ref[slot, rs, :] = ((_rms(x1) * gpf_ref[...]) * (1.0 + sc2) + sh2).astype(BF16)

    def ffn_chunk(slot, ci):
        c0, c1 = chunks[ci]
        h = h_ref[slot]
        g = _dot(h, wfi_ref[:, c0:c1])
        u = _dot(h, wfi_ref[:, FFN_HIDDEN + c0:FFN_HIDDEN + c1])
        act_ref[:, c0:c1] = (_silu(g) * u).astype(BF16)
        return u[0:8, 0:LANES]

    def ffn_finish(slot):
        f = _dot(act_ref[...], wfo_ref[...])
        row = jnp.maximum(i - 1, 0) // tiles_per_batch
        gt2 = ada_ref[pl.ds(row, 1), 5 * D:6 * D]
        o_ref[...] = x1_ref[slot] + gt2 * (_rms(f) * gqf_ref[...])

    @pl.when(i == 0)
    def _():
        mix_matmul()
        for si in range(MIX_SLICES):
            mix_rows_slice(0, si)

    for parity in range(2):
        @pl.when((i % 2 == parity) & (i > 0))
        def _():
            per_gap = -(-MIX_SLICES // (n_chunks - 1))
            for ci in range(n_chunks):
                anchor = ffn_chunk(1 - parity, ci)
                if ci == 0:
                    mix_matmul()
                else:
                    for si in range((ci - 1) * per_gap, min(ci * per_gap, MIX_SLICES)):
                        mix_rows_slice(parity, si, anchor)
            ffn_finish(1 - parity)


def _mix_ffn(att_t, gla, x2d, ada, g_post_mix, g_pre_ffn, g_post_ffn,
             w_out_a, w_out_g, w_ffn_in, w_ffn_out, *, tm, tiles_per_batch):
    n = x2d.shape[0]
    tpb = tiles_per_batch
    nt = n // tm
    const = lambda i: (0, 0)
    resident = functools.partial(pl.BlockSpec, index_map=const, pipeline_mode=pl.Buffered(1))
    cur = lambda i: jnp.minimum(i, nt - 1)
    done = lambda i: jnp.maximum(i - 1, 0)
    return pl.pallas_call(
        functools.partial(_mix_ffn_kernel, tiles_per_batch=tpb, n_tiles=nt),
        out_shape=jax.ShapeDtypeStruct((n, D), F32),
        grid=(nt + 1,),
        in_specs=[pl.BlockSpec((None, NQ, tm), lambda i: (cur(i) // tpb, 0, cur(i) % tpb)),
                  pl.BlockSpec((tm, GV), lambda i: (cur(i), 0)),
                  pl.BlockSpec((tm, D), lambda i: (cur(i), 0)),
                  pl.BlockSpec((8, 6 * D), const),
                  pl.BlockSpec((1, D), const),
                  pl.BlockSpec((1, D), const),
                  pl.BlockSpec((1, D), const),
                  resident((NQ, D)),
                  resident((GV, D)),
                  resident((D, 2 * FFN_HIDDEN)),
                  resident((FFN_HIDDEN, D))],
        out_specs=pl.BlockSpec((tm, D), lambda i: (done(i), 0)),
        scratch_shapes=[pltpu.VMEM((2, tm, D), F32),
                        pltpu.VMEM((2, tm, D), BF16),
                        pltpu.VMEM((tm, D), F32),
                        pltpu.VMEM((tm, FFN_HIDDEN), BF16)],
        compiler_params=pltpu.CompilerParams(dimension_semantics=("arbitrary",),
                                             vmem_limit_bytes=VMEM_LIMIT),
        name="mix_ffn",
    )(att_t, gla, x2d, ada, g_post_mix, g_pre_ffn, g_post_ffn,
      w_out_a, w_out_g, w_ffn_in, w_ffn_out)


def _rope_tables(seq):
    half = HEAD_DIM // 2
    inv_freq = ROPE_BASE ** (-np.arange(0, half, 2, dtype=np.float64) / half)
    dim = np.arange(HEAD_DIM)
    freq = np.tile(inv_freq, HEAD_DIM // ROPE_BLK)[:, None]
    pos = np.arange(seq)[None, :]
    ang = np.where((dim < half)[:, None], pos // GRID_W, pos % GRID_W) * freq
    cos, sin = np.cos(ang), np.sin(ang)
    even = ((dim // ROPE_BLK) % 2 == 0)[:, None]
    sa = np.where(even, -sin, 0.0)
    sb = np.where(even, 0.0, sin)
    return tuple(jnp.asarray(t, F32) for t in (cos, sa, sb))


def kernel(x, c, ctx, c_ctx, w_ada, b_ada, g_pre_mix, g_post_mix, g_pre_ffn, g_post_ffn,
           w_in, attn_sink, w_gate_fwd, b_gate_fwd, w_gate_bwd, b_gate_bwd, g_gla_norm,
           w_out, w_ffn_in, w_ffn_out):
    batch, seq, _ = x.shape
    n_ctx = ctx.shape[1]
    depth = w_ada.shape[0]
    assert depth == 1
    l = 0

    c8 = jnp.zeros((8, D), F32).at[0:batch].set(c).at[4].set(c_ctx)
    ada = _ada(c8, w_ada[l], b_ada[l][None, :])

    wi = w_in[l]
    g0 = NQ + 2 * NKV
    w_qkv_t = wi[:, 0:g0].T.astype(BF16)
    gk0 = g0 + GQ
    gg0 = gk0 + KV_W
    w_rest = jnp.concatenate(
        [wi[:, gk0:gg0], wi[:, g0:gk0], wi[:, gg0:],
         jnp.zeros((D, Z_W - 2 * GATE_RANK), F32)], axis=1).astype(BF16)
    w_gate_p = jnp.zeros((Z_W, LA_W), F32)
    w_gate_p = w_gate_p.at[0:GATE_RANK, 0:GQ].set(w_gate_fwd[l])
    w_gate_p = w_gate_p.at[GATE_RANK:2 * GATE_RANK, GQ:].set(w_gate_bwd[l]).astype(BF16)
    b_gate_p = jnp.concatenate([b_gate_fwd[l], b_gate_bwd[l]])[None, :]

    tm = 512
    tables = _rope_tables(seq)
    x2d = x.reshape(batch * seq, D)
    g_pre = g_pre_mix[l][None, :]
    qt, k, vt, gin, lah, lal = _inproj(x2d, ada, g_pre, w_qkv_t, w_rest, w_gate_p, b_gate_p,
                                      tables, batch=batch, tokens=seq, tm=INPROJ_TM)
    _, k_c, vt_c, gin_c, lah_c, lal_c = _inproj(
        ctx.reshape(batch * n_ctx, D), ada, g_pre, w_qkv_t, w_rest, w_gate_p, b_gate_p, None,
        batch=batch, tokens=n_ctx, tm=n_ctx)

    att_t = _attn(attn_sink[l], qt, k, vt, k_c, vt_c, batch=batch, seq=seq, n_ctx=n_ctx)

    gn_tiled = jnp.tile(g_gla_norm[l], GLA_HEADS)[None, :]
    gla = _gla(gin, lah, lal, gin_c, lah_c, lal_c, gn_tiled, batch=batch, seq=seq, n_ctx=n_ctx)

    wo = w_out[l]
    out = _mix_ffn(att_t, gla.reshape(batch * seq, GV), x2d, ada,
                   g_post_mix[l][None, :], g_pre_ffn[l][None, :], g_post_ffn[l][None, :],
                   wo[:NQ].astype(BF16), wo[NQ:].astype(BF16),
                   w_ffn_in[l].astype(BF16), w_ffn_out[l].astype(BF16),
                   tm=tm, tiles_per_batch=seq // tm)
    return out.reshape(batch, seq, D)
```

```python
import functools

import jax
import jax.numpy as jnp
import numpy as np
from jax import lax
from jax.experimental import pallas as pl
from jax.experimental.pallas import tpu as pltpu

D = 1024
HEAD_DIM = 64
ATT_HEADS = 8
ATT_KV_HEADS = 2
GROUP = ATT_HEADS // ATT_KV_HEADS
WINDOW = 128
GRID_W = 64
ROPE_BASE = 10000.0
ROPE_BLK = HEAD_DIM // 4
GLA_HEADS = 8
GLA_DK = 32
GLA_DV = 64
CHUNK = 64
GATE_RANK = 16
GATE_TAU = 16.0
FFN_HIDDEN = 2816
NEG_INF = -1e30
EPS = 1e-6
LOG2E = 1.4426950408889634
Q_SCALE = LOG2E * HEAD_DIM ** -0.5
LA_SCALE = LOG2E / GATE_TAU

NQ = ATT_HEADS * HEAD_DIM
NKV = ATT_KV_HEADS * HEAD_DIM
GQ = GLA_HEADS * GLA_DK
GV = GLA_HEADS * GLA_DV
GIN_W = 2 * GQ + 2 * GV
KV_W = GQ + GV
K_COLS = slice(0, GQ)
V_COLS = slice(GQ, KV_W)
Q_COLS = slice(KV_W, KV_W + GQ)
G_COLS = slice(KV_W + GQ, GIN_W)
Z_W = 128
LA_W = 2 * GQ
REST_W = GIN_W + Z_W

LANES = 128
VMEM_LIMIT = 56 * 1024 * 1024

BF16 = jnp.bfloat16
F32 = jnp.float32


def _dot(a, b):
    return jnp.dot(a, b, preferred_element_type=F32)


def _dot_nt(a, b):
    return lax.dot_general(a, b, (((1,), (1,)), ((), ())), preferred_element_type=F32)


def _dot_tn(a, b):
    return lax.dot_general(a, b, (((0,), (0,)), ((), ())), preferred_element_type=F32)


def _rms(x):
    return x * lax.rsqrt(jnp.mean(x * x, axis=-1, keepdims=True) + EPS)


def _silu(x):
    return x * (1.0 / (1.0 + jnp.exp(-x)))


def _ada_kernel(c_ref, w_ref, b_ref, o_ref):
    a = _silu(c_ref[...]).astype(BF16)
    o_ref[...] = _dot(a, w_ref[...].astype(BF16)) + b_ref[...]


def _ada(c8, w_ada, b_ada):
    n = w_ada.shape[1]
    bn = 1024
    return pl.pallas_call(
        _ada_kernel,
        out_shape=jax.ShapeDtypeStruct((8, n), F32),
        grid=(n // bn,),
        in_specs=[pl.BlockSpec((8, D), lambda j: (0, 0)),
                  pl.BlockSpec((D, bn), lambda j: (0, j)),
                  pl.BlockSpec((1, bn), lambda j: (0, j))],
        out_specs=pl.BlockSpec((8, bn), lambda j: (0, j)),
        compiler_params=pltpu.CompilerParams(dimension_semantics=("arbitrary",)),
        name="ada",
    )(c8, w_ada, b_ada)


NORM_ROWS = 128
INPROJ_TM = 1024


def _inproj_kernel(*refs, tiles_per_batch, rope):
    if rope:
        (x_ref, ada_ref, g_ref, wqkv_ref, w_ref, wg_ref, bg_ref, gn_ref,
         cost_ref, sat_ref, sbt_ref,
         qt_ref, k_ref, vt_ref, gin_ref, lah_ref, lal_ref) = refs
        row = pl.program_id(0) // tiles_per_batch
    else:
        (x_ref, ada_ref, g_ref, wqkv_ref, w_ref, wg_ref, bg_ref, gn_ref,
         qt_ref, k_ref, vt_ref, gin_ref, lah_ref, lal_ref) = refs
        row = 4
    ada = ada_ref[pl.ds(row, 1), :]
    sh1 = ada[:, 0:D]
    sc1 = ada[:, D:2 * D]
    tm = x_ref.shape[0]
    hb = jnp.concatenate(
        [((_rms(x_ref[r0:r0 + NORM_ROWS, :]) * g_ref[...]) * (1.0 + sc1) + sh1).astype(BF16)
         for r0 in range(0, tm, NORM_ROWS)], axis=0)

    z = _dot(hb, w_ref[:, GIN_W:REST_W])
    qkv = _dot_nt(wqkv_ref[...], hb)
    logits = _dot(z.astype(BF16), wg_ref[...]) + bg_ref[...]
    gate = _dot(hb, w_ref[:, G_COLS])
    gin = _dot(hb, w_ref[:, 0:KV_W + GQ])

    def head_t(hd):
        xh = qkv[hd * HEAD_DIM:(hd + 1) * HEAD_DIM]
        if not rope:
            return xh
        up = jnp.concatenate([xh[ROPE_BLK:], xh[:ROPE_BLK]], axis=0)
        dn = jnp.concatenate([xh[HEAD_DIM - ROPE_BLK:], xh[:HEAD_DIM - ROPE_BLK]], axis=0)
        return xh * cost_ref[...] + up * sat_ref[...] + dn * sbt_ref[...]

    for hd in range(ATT_HEADS):
        qh = head_t(hd)
        if rope:
            qh = qh * Q_SCALE
        qt_ref[hd * HEAD_DIM:(hd + 1) * HEAD_DIM, :] = qh.astype(BF16)
    kt = jnp.concatenate([head_t(ATT_HEADS + hd) for hd in range(ATT_KV_HEADS)], axis=0)
    k_ref[...] = kt.T.astype(BF16)
    vt_ref[...] = qkv[NQ + NKV:NQ + 2 * NKV].astype(BF16)

    log_sig = jnp.minimum(logits, 0.0) - jnp.log(1.0 + jnp.exp(-jnp.abs(logits)))
    la = log_sig * LA_SCALE
    hi = la.astype(BF16)
    lah_ref[...] = hi
    lal_ref[...] = (la - hi.astype(F32)).astype(BF16)

    gin_ref[:, G_COLS] = (_silu(gate) * gn_ref[...]).astype(BF16)
    gin_ref[:, 0:KV_W] = gin[:, 0:KV_W].astype(BF16)
    gin_ref[:, Q_COLS] = (gin[:, Q_COLS] * (GLA_DK ** -0.5)).astype(BF16)


def _inproj(x2d, ada, g_pre, w_qkv_t, w_rest, w_gate_p, b_gate_p, gn_tiled, tables, *,
            batch, tokens, tm):
    n = x2d.shape[0]
    tpb = tokens // tm
    rope = tables is not None
    const = lambda i: (0, 0)
    in_specs = [pl.BlockSpec((tm, D), lambda i: (i, 0)),
                pl.BlockSpec((8, 6 * D), const),
                pl.BlockSpec((1, D), const),
                pl.BlockSpec((NQ + 2 * NKV, D), const),
                pl.BlockSpec((D, REST_W), const),
                pl.BlockSpec((Z_W, LA_W), const),
                pl.BlockSpec((1, LA_W), const),
                pl.BlockSpec((1, GV), const)]
    args = [x2d, ada, g_pre, w_qkv_t, w_rest, w_gate_p, b_gate_p, gn_tiled]
    if rope:
        in_specs += [pl.BlockSpec((HEAD_DIM, tm), lambda i: (0, i % tpb))] * 3
        args += list(tables)
    row_blk = lambda w: pl.BlockSpec((tm, w), lambda i: (i, 0))
    col_blk = lambda w: pl.BlockSpec((None, w, tm), lambda i: (i // tpb, 0, i % tpb))
    return pl.pallas_call(
        functools.partial(_inproj_kernel, tiles_per_batch=tpb, rope=rope),
        out_shape=(jax.ShapeDtypeStruct((batch, NQ, tokens), BF16),
                   jax.ShapeDtypeStruct((n, NKV), BF16),
                   jax.ShapeDtypeStruct((batch, NKV, tokens), BF16),
                   jax.ShapeDtypeStruct((n, GIN_W), BF16),
                   jax.ShapeDtypeStruct((n, LA_W), BF16),
                   jax.ShapeDtypeStruct((n, LA_W), BF16)),
        grid=(n // tm,),
        in_specs=in_specs,
        out_specs=(col_blk(NQ), row_blk(NKV), col_blk(NKV),
                   row_blk(GIN_W), row_blk(LA_W), row_blk(LA_W)),
        compiler_params=pltpu.CompilerParams(dimension_semantics=("arbitrary",),
                                             vmem_limit_bytes=VMEM_LIMIT),
        name="inproj_lat" if rope else "inproj_ctx",
    )(*args)


ATT_SUB = 8
ATT_Q = 64
ONES_ROWS = 16


def _attn_kernel(sink_ref, qt_ref, kp_ref, kc_ref, kn_ref, kx_ref,
                 vp_ref, vc_ref, vn_ref, vx_ref, o_ref, *, n_steps):
    i = pl.program_id(1)
    bq = WINDOW
    nwin = ATT_Q + 2 * WINDOW
    nq = GROUP * ATT_Q
    n_sub = ATT_SUB * bq // ATT_Q
    c = lax.broadcasted_iota(jnp.int32, (ATT_Q, nq), 0)
    r = lax.broadcasted_iota(jnp.int32, (ATT_Q, nq), 1) % ATT_Q
    behind_ok = c >= r
    ahead_ok = c <= r
    lane = lax.broadcasted_iota(jnp.int32, (1, nq), 1)
    zeros_q = jnp.zeros((HEAD_DIM, nq), BF16)
    zeros_e = jnp.zeros((ATT_Q, nq), BF16)

    kc_all = jnp.concatenate([kp_ref[...], kc_ref[...], kn_ref[...]], axis=0)
    vc_all = jnp.concatenate([vp_ref[...], vc_ref[...], vn_ref[...]], axis=1)
    kx = kx_ref[...]
    vx = vx_ref[...]

    vaug = []
    for a in range(ATT_SUB):
        vwin = jnp.concatenate([vc_all[:, a * bq:(a + 3) * bq], vx], axis=1)
        ones = jnp.ones((ONES_ROWS, vwin.shape[1]), BF16)
        vaug.append([jnp.concatenate([vwin[kvh * HEAD_DIM:(kvh + 1) * HEAD_DIM], ones], axis=0)
                     for kvh in range(ATT_KV_HEADS)])

    tiles = [(s, kvh) for s in range(n_sub) for kvh in range(ATT_KV_HEADS)]

    def score(tile):
        s, kvh = tile
        q0 = s * ATT_Q
        h0 = kvh * GROUP
        qt = jnp.concatenate(
            [qt_ref[(h0 + u) * HEAD_DIM:(h0 + u + 1) * HEAD_DIM, q0:q0 + ATT_Q]
             for u in range(GROUP)], axis=1)
        rhs = jnp.concatenate([qt, zeros_q] if kvh == 0 else [zeros_q, qt], axis=0)
        kwin = jnp.concatenate([kc_all[q0:q0 + nwin], kx], axis=0)
        return _dot(kwin, rhs)

    def window_blocks(s, st):
        blocks = []
        for b in range(nwin // ATT_Q):
            blk = st[b * ATT_Q:(b + 1) * ATT_Q]
            if b == 0:
                blk = jnp.where(behind_ok, blk, NEG_INF)
            if b == nwin // ATT_Q - 1:
                blk = jnp.where(ahead_ok, blk, NEG_INF)
            first_key = s * ATT_Q - WINDOW + b * ATT_Q
            if first_key < 0:
                blk = jnp.where(i > 0, blk, NEG_INF)
            if first_key >= ATT_SUB * bq:
                blk = jnp.where(i < n_steps - 1, blk, NEG_INF)
            blocks.append(blk)
        return blocks

    def softmax_numerators(tile, st):
        s, kvh = tile
        h0 = kvh * GROUP
        parts = window_blocks(s, st) + [st[nwin:]]
        sink = sink_ref[h0 + GROUP - 1]
        for u in range(GROUP - 2, -1, -1):
            sink = jnp.where(lane < (u + 1) * ATT_Q, sink_ref[h0 + u], sink)
        sink = sink * LOG2E
        m = sink
        for p in parts:
            m = jnp.maximum(m, jnp.max(p, axis=0, keepdims=True))
        e = [jnp.exp2(p - m).astype(BF16) for p in parts]
        pad = [zeros_e]
        et = jnp.concatenate((e[:-1] + pad if s % 2 == 0 else pad + e[:-1]) + e[-1:], axis=0)
        return et, jnp.exp2(sink - m)

    def values(tile, et, e_sink):
        s, kvh = tile
        q0 = s * ATT_Q
        ot = _dot(vaug[s // 2][kvh], et)
        denom = ot[HEAD_DIM:HEAD_DIM + 1] + e_sink
        on = ot[0:HEAD_DIM] * (1.0 / denom)
        for u in range(GROUP):
            hd = kvh * GROUP + u
            o_ref[hd * HEAD_DIM:(hd + 1) * HEAD_DIM, q0:q0 + ATT_Q] = (
                on[:, u * ATT_Q:(u + 1) * ATT_Q].astype(BF16))

    scores = [score(t) for t in tiles]
    probs = [softmax_numerators(t, st) for t, st in zip(tiles, scores)]
    for t, (et, e_sink) in zip(tiles, probs):
        values(t, et, e_sink)


def _attn(sink, qt, k, vt, k_ctx, vt_ctx, *, batch, seq, n_ctx):
    bq = WINDOW
    bs = ATT_SUB * bq
    nb = seq // bq
    ns = seq // bs
    k3 = k.reshape(batch, seq, NKV)
    kx3 = k_ctx.reshape(batch, n_ctx, NKV)
    prev = lambda i: jnp.maximum(ATT_SUB * i - 1, 0)
    nxt = lambda i: jnp.minimum(ATT_SUB * (i + 1), nb - 1)
    return pl.pallas_call(
        functools.partial(_attn_kernel, n_steps=ns),
        out_shape=jax.ShapeDtypeStruct((batch, NQ, seq), BF16),
        grid=(batch, ns),
        in_specs=[
            pl.BlockSpec(memory_space=pltpu.SMEM),
            pl.BlockSpec((None, NQ, bs), lambda b, i: (b, 0, i)),
            pl.BlockSpec((None, bq, NKV), lambda b, i: (b, prev(i), 0)),
            pl.BlockSpec((None, bs, NKV), lambda b, i: (b, i, 0)),
            pl.BlockSpec((None, bq, NKV), lambda b, i: (b, nxt(i), 0)),
            pl.BlockSpec((None, n_ctx, NKV), lambda b, i: (b, 0, 0)),
            pl.BlockSpec((None, NKV, bq), lambda b, i: (b, 0, prev(i))),
            pl.BlockSpec((None, NKV, bs), lambda b, i: (b, 0, i)),
            pl.BlockSpec((None, NKV, bq), lambda b, i: (b, 0, nxt(i))),
            pl.BlockSpec((None, NKV, n_ctx), lambda b, i: (b, 0, 0)),
        ],
        out_specs=pl.BlockSpec((None, NQ, bs), lambda b, i: (b, 0, i)),
        compiler_params=pltpu.CompilerParams(dimension_semantics=("arbitrary", "arbitrary"),
                                             vmem_limit_bytes=VMEM_LIMIT),
        name="attn",
    )(sink, qt, k3, k3, k3, kx3, vt, vt, vt, vt_ctx)


GRP_K = 4 * GLA_DK
GRP_V = 4 * GLA_DV
N_GRP = GLA_HEADS // 4
GLA_TB = 512
GLA_STATE_TB = 1024
CUM_BLK = 256


def _gla_constants():
    t = CUM_BLK
    r = np.arange(t)[:, None]
    c = np.arange(t)[None, :]
    same_chunk = (r // CHUNK) == (c // CHUNK)
    tri_l = (same_chunk & (c <= r)).astype(np.float32)
    tri_u = (same_chunk & (c >= r)).astype(np.float32)
    rk = np.arange(4 * CHUNK)[:, None] // CHUNK
    kmask = (rk == (np.arange(GRP_K)[None, :] // GLA_DK)).astype(np.float32)
    vmask = (rk == (np.arange(GRP_V)[None, :] // GLA_DV)).astype(np.float32)
    smask = ((np.arange(GRP_K)[:, None] // GLA_DK)
             == (np.arange(GRP_V)[None, :] // GLA_DV)).astype(np.float32)
    ai = np.arange(CHUNK)[:, None]
    aj = np.arange(4 * CHUNK)[None, :] % CHUNK
    causal = (aj <= ai).astype(np.float32)
    anti = (aj >= ai).astype(np.float32)
    hmean = ((np.arange(GRP_V)[:, None] // GLA_DV)
             == (np.arange(GRP_V)[None, :] // GLA_DV)).astype(np.float32) / GLA_DV
    return (jnp.asarray(tri_l, BF16), jnp.asarray(tri_u, BF16), jnp.asarray(kmask, BF16),
            jnp.asarray(vmask, BF16), jnp.asarray(smask, BF16), jnp.asarray(causal, BF16),
            jnp.asarray(anti, BF16), jnp.asarray(hmean, BF16))


def _gk(g):
    return slice(g * GRP_K, (g + 1) * GRP_K)


def _gv(g):
    return slice(g * GRP_V, (g + 1) * GRP_V)


def _chunk_slices(t):
    return [slice(ci * CHUNK, (ci + 1) * CHUNK) for ci in range(t // CHUNK)]


def _cum_and_totals(la_hi, la_lo, tri, *, reverse):
    parts = [slice(p, p + CUM_BLK) for p in range(0, la_hi.shape[0], CUM_BLK)]
    cum = jnp.concatenate([_dot(tri, la_hi[p]) + _dot(tri, la_lo[p]) for p in parts], axis=0)
    edge = 0 if reverse else CHUNK - 1
    tots = [cum[sl.start + edge:sl.start + edge + 1] for sl in _chunk_slices(cum.shape[0])]
    return cum, tots


def _bcast_chunks(rows, width):
    return jnp.concatenate([jnp.broadcast_to(r, (CHUNK, width)) for r in rows], axis=0)


def _decay(x, log2_factor):
    return x * jnp.exp2(log2_factor).astype(BF16)


def _state_increments(kd, v):
    return [[_dot_tn(kd[sl, _gk(g)], v[sl, _gv(g)]) for g in range(N_GRP)]
            for sl in _chunk_slices(kd.shape[0])]


def _masked_state(state, smask):
    return state.astype(BF16) * smask


def _decay_columns(tots):
    width = tots[0].shape[1]
    row = lax.broadcasted_iota(jnp.int32, (8, width), 0)
    rows = jnp.zeros((8, width), F32)
    for ci, t in enumerate(tots):
        rows = jnp.where(row == ci, jnp.broadcast_to(t, (8, width)), rows)
    return jnp.exp2(rows).T


def _interleave(stage_gens):
    gens = list(stage_gens)
    done = [False] * len(gens)
    rnd = 0
    while not all(done):
        for gi, gen in enumerate(gens):
            if rnd >= gi and not done[gi]:
                try:
                    next(gen)
                except StopIteration:
                    done[gi] = True
        rnd += 1


def _state_stages(st_ref, k, v, la_hi, la_lo, tri, smask, *, reverse,
                  store_ref=None, store_base=None):
    nch = k.shape[0] // CHUNK
    cum, tots = _cum_and_totals(la_hi, la_lo, tri, reverse=reverse)
    yield
    kd = _decay(k, _bcast_chunks(tots, GQ) - cum)
    dcol = _decay_columns(tots)
    yield
    ds = _state_increments(kd, v)
    yield
    state = [st_ref[g] for g in range(N_GRP)]
    for ci in (range(nch - 1, -1, -1) if reverse else range(nch)):
        for g in range(N_GRP):
            if store_ref is not None:
                store_ref[store_base + ci, g] = _masked_state(state[g], smask)
            state[g] = state[g] * dcol[_gk(g), ci:ci + 1] + ds[ci][g]
    for g in range(N_GRP):
        st_ref[g] = state[g]


def _output_stages(q, k, v, la_f, la_b, gate, st_ref, bwd_state, consts, finish):
    tri_l, tri_u, kmask, vmask, smask, causal, anti, hmean = consts
    chunks = _chunk_slices(q.shape[0])

    cum_f, tots_f = _cum_and_totals(la_f[0], la_f[1], tri_l, reverse=False)
    cum_b, tots_b = _cum_and_totals(la_b[0], la_b[1], tri_u, reverse=True)
    yield

    qd_f = _decay(q, cum_f)
    qd_b = _decay(q, cum_b)
    ki_f = _decay(k, -cum_f)
    ki_b = _decay(k, -cum_b)
    kd_f = _decay(k, _bcast_chunks(tots_f, GQ) - cum_f)
    dcol_f = _decay_columns(tots_f)
    v_bd = [[jnp.concatenate([v[sl, _gv(g)]] * 4, axis=0) * vmask for g in range(N_GRP)]
            for sl in chunks]
    yield

    def scores(qd, ki, sl, g):
        ki_bd = jnp.concatenate([ki[sl, _gk(g)]] * 4, axis=0) * kmask
        return _dot_nt(qd[sl, _gk(g)], ki_bd)

    a_f = [[scores(qd_f, ki_f, sl, g) for g in range(N_GRP)] for sl in chunks]
    a_b = [[scores(qd_b, ki_b, sl, g) for g in range(N_GRP)] for sl in chunks]
    ds = _state_increments(kd_f, v)
    yield

    a = [[af[g].astype(BF16) * causal + ab[g].astype(BF16) * anti for g in range(N_GRP)]
         for af, ab in zip(a_f, a_b)]
    state = [st_ref[g] for g in range(N_GRP)]
    st_in = []
    for ci in range(len(chunks)):
        st_in.append([_masked_state(s, smask) for s in state])
        state = [state[g] * dcol_f[_gk(g), ci:ci + 1] + ds[ci][g] for g in range(N_GRP)]
    for g in range(N_GRP):
        st_ref[g] = state[g]
    yield

    outs = []
    for ci, sl in enumerate(chunks):
        parts = []
        for g in range(N_GRP):
            intra = _dot(a[ci][g], v_bd[ci][g])
            qd2 = jnp.concatenate([qd_f[sl, _gk(g)], qd_b[sl, _gk(g)]], axis=1)
            st2 = jnp.concatenate([st_in[ci][g], bwd_state(ci, g)], axis=0)
            parts.append(intra + _dot(qd2, st2))
        outs.append(jnp.concatenate(parts, axis=1))
    o = jnp.concatenate(outs, axis=0)
    yield

    o2 = (o * o).astype(BF16)
    ms = jnp.concatenate([_dot(o2[:, _gv(g)], hmean) for g in range(N_GRP)], axis=1)
    yield
    finish((o * lax.rsqrt(ms + EPS) * gate.astype(F32)).astype(BF16))


def _gla_kernel(skv_ref, slah_ref, slal_ref,
                kv_ref, qg_ref, lahf_ref, lahb_ref, lalf_ref, lalb_ref,
                ginc_ref, lahc_ref, lalc_ref,
                tril_ref, triu_ref, kmask_ref, vmask_ref, smask_ref, causal_ref, anti_ref,
                hmean_ref, o_ref, st_ref, sb_ref, *, n_state_steps):
    step = pl.program_id(1)
    ch_per_half = CUM_BLK // CHUNK
    fwd = slice(0, GQ)
    bwd = slice(GQ, 2 * GQ)

    def ctx_state(cols, tri_ref, reverse):
        st_ref[...] = jnp.zeros_like(st_ref)
        ctx_blocks = [slice(r0, r0 + CUM_BLK) for r0 in range(0, ginc_ref.shape[0], CUM_BLK)]
        for rs in (ctx_blocks[::-1] if reverse else ctx_blocks):
            for _ in _state_stages(st_ref, ginc_ref[rs, K_COLS], ginc_ref[rs, V_COLS],
                                   lahc_ref[rs, cols], lalc_ref[rs, cols], tri_ref[...],
                                   smask_ref[...], reverse=reverse):
                pass

    @pl.when(step < n_state_steps)
    def _backward_states():
        @pl.when(step == 0)
        def _():
            ctx_state(bwd, triu_ref, True)

        blk = n_state_steps - 1 - step
        parts = [slice(r0, r0 + CUM_BLK) for r0 in range(0, GLA_STATE_TB, CUM_BLK)]
        _interleave([
            _state_stages(st_ref, skv_ref[rs, K_COLS], skv_ref[rs, V_COLS],
                          slah_ref[rs, :], slal_ref[rs, :], triu_ref[...], smask_ref[...],
                          reverse=True, store_ref=sb_ref,
                          store_base=blk * (GLA_STATE_TB // CHUNK) + pi * ch_per_half)
            for pi, rs in reversed(list(enumerate(parts)))])

    @pl.when(step >= n_state_steps)
    def _outputs():
        j = step - n_state_steps

        @pl.when(j == 0)
        def _():
            ctx_state(fwd, tril_ref, False)

        consts = (tril_ref[...], triu_ref[...], kmask_ref[...], vmask_ref[...],
                  smask_ref[...], causal_ref[...], anti_ref[...], hmean_ref[...])
        halves = [slice(r0, r0 + CUM_BLK) for r0 in range(0, GLA_TB, CUM_BLK)]

        def half(hi, rs):
            base = j * (GLA_TB // CHUNK) + hi * ch_per_half

            def finish(val):
                o_ref[rs, :] = val

            return _output_stages(
                qg_ref[rs, 0:GQ], kv_ref[rs, K_COLS], kv_ref[rs, V_COLS],
                (lahf_ref[rs, :], lalf_ref[rs, :]), (lahb_ref[rs, :], lalb_ref[rs, :]),
                qg_ref[rs, GQ:KV_W], st_ref, lambda ci, g: sb_ref[base + ci, g], consts, finish)

        _interleave([half(hi, rs) for hi, rs in enumerate(halves)])


def _gla(gin, la_hi, la_lo, gin_ctx, la_hi_ctx, la_lo_ctx, *, batch, seq, n_ctx):
    assert n_ctx % CUM_BLK == 0 and n_ctx // CHUNK <= 8 and GLA_TB // CHUNK <= 8
    tb, stb = GLA_TB, GLA_STATE_TB
    ns, nb = seq // stb, seq // tb
    nch = seq // CHUNK
    r3 = lambda a, t: a.reshape(batch, t, a.shape[-1])
    consts = _gla_constants()

    def state_blk(col):
        return lambda b, s: (b, jnp.maximum(ns - 1 - s, 0), col)

    def out_blk(col):
        return lambda b, s: (b, jnp.maximum(s - ns, 0), col)

    ctx_map = lambda b, s: (b, 0, 0)
    const_map = lambda b, s: (0, 0)
    gin3, lah3, lal3 = r3(gin, seq), r3(la_hi, seq), r3(la_lo, seq)
    return pl.pallas_call(
        functools.partial(_gla_kernel, n_state_steps=ns),
        out_shape=jax.ShapeDtypeStruct((batch, seq, GV), BF16),
        grid=(batch, ns + nb),
        in_specs=[pl.BlockSpec((None, stb, KV_W), state_blk(0)),
                  pl.BlockSpec((None, stb, GQ), state_blk(1)),
                  pl.BlockSpec((None, stb, GQ), state_blk(1)),
                  pl.BlockSpec((None, tb, KV_W), out_blk(0)),
                  pl.BlockSpec((None, tb, KV_W), out_blk(1)),
                  pl.BlockSpec((None, tb, GQ), out_blk(0)),
                  pl.BlockSpec((None, tb, GQ), out_blk(1)),
                  pl.BlockSpec((None, tb, GQ), out_blk(0)),
                  pl.BlockSpec((None, tb, GQ), out_blk(1)),
                  pl.BlockSpec((None, n_ctx, GIN_W), ctx_map),
                  pl.BlockSpec((None, n_ctx, LA_W), ctx_map),
                  pl.BlockSpec((None, n_ctx, LA_W), ctx_map)]
                 + [pl.BlockSpec(cst.shape, const_map) for cst in consts],
        out_specs=pl.BlockSpec((None, tb, GV), out_blk(0)),
        scratch_shapes=[pltpu.VMEM((N_GRP, GRP_K, GRP_V), F32),
                        pltpu.VMEM((nch, N_GRP, GRP_K, GRP_V), BF16)],
        compiler_params=pltpu.CompilerParams(
            dimension_semantics=("arbitrary", "arbitrary"),
            vmem_limit_bytes=VMEM_LIMIT),
        name="gla",
    )(gin3, lah3, lal3, gin3, gin3, lah3, lah3, lal3, lal3, r3(gin_ctx, n_ctx),
      r3(la_hi_ctx, n_ctx), r3(la_lo_ctx, n_ctx), *consts)


FFN_CHUNK = 256
MIX_SLICES = 8


def _mix_ffn_kernel(att_ref, gla_ref, x_ref, ada_ref, gpm_ref, gpf_ref, gqf_ref,
                    woa_ref, wog_ref, wfi_ref, wfo_ref, o_ref, x1_ref, h_ref, y_ref, acc_ref, *,
                    tiles_per_batch, n_tiles):
    i = pl.program_id(0)
    tm = x_ref.shape[0]
    chunks = [(c0, min(c0 + FFN_CHUNK, FFN_HIDDEN)) for c0 in range(0, FFN_HIDDEN, FFN_CHUNK)]
    n_chunks = len(chunks)
    mix_rows = tm // MIX_SLICES

    def mix_matmul():
        y_ref[...] = _dot_tn(att_ref[...], woa_ref[...]) + _dot(gla_ref[...], wog_ref[...])

    def mix_rows_slice(slot, si, anchor=None):
        rs = slice(si * mix_rows, (si + 1) * mix_rows)
        row = jnp.minimum(i, n_tiles - 1) // tiles_per_batch
        ada = ada_ref[pl.ds(row, 1), :]
        gt1 = ada[:, 2 * D:3 * D]
        if anchor is not None:
            bits = pltpu.bitcast(anchor, jnp.uint32)
            sixteen = jnp.uint32(16)
            gt1 = gt1 + pltpu.bitcast(
                lax.shift_right_logical(lax.shift_right_logical(bits, sixteen), sixteen), F32)
        sh2 = ada[:, 3 * D:4 * D]
        sc2 = ada[:, 4 * D:5 * D]
        x1 = x_ref[rs, :] + gt1 * (_rms(y_ref[rs, :]) * gpm_ref[...])
        x1_ref[slot, rs, :] = x1
        h_ref[slot, rs, :] = ((_rms(x1) * gpf_ref[...]) * (1.0 + sc2) + sh2).astype(BF16)

    def ffn_chunk(slot, ci):
        c0, c1 = chunks[ci]
        h = h_ref[slot]
        g = _dot(h, wfi_ref[:, c0:c1])
        u = _dot(h, wfi_ref[:, FFN_HIDDEN + c0:FFN_HIDDEN + c1])
        a = (_silu(g) * u).astype(BF16)
        part = _dot(a, wfo_ref[c0:c1, :])
        if ci == 0:
            acc_ref[...] = part
        else:
            acc_ref[...] += part
        return part[0:8, :]

    def ffn_finish(slot):
        row = jnp.maximum(i - 1, 0) // tiles_per_batch
        gt2 = ada_ref[pl.ds(row, 1), 5 * D:6 * D]
        o_ref[...] = x1_ref[slot] + gt2 * (_rms(acc_ref[...]) * gqf_ref[...])

    @pl.when(i == 0)
    def _():
        mix_matmul()
        for si in range(MIX_SLICES):
            mix_rows_slice(0, si)

    for parity in range(2):
        @pl.when((i % 2 == parity) & (i > 0))
        def _():
            per_gap = -(-MIX_SLICES // (n_chunks - 1))
            for ci in range(n_chunks):
                anchor = ffn_chunk(1 - parity, ci)
                if ci == 0:
                    mix_matmul()
                else:
                    for si in range((ci - 1) * per_gap, min(ci * per_gap, MIX_SLICES)):
                        mix_rows_slice(parity, si, anchor[0:1, :])
            ffn_finish(1 - parity)


def _mix_ffn(att_t, gla, x2d, ada, g_post_mix, g_pre_ffn, g_post_ffn,
             w_out_a, w_out_g, w_ffn_in, w_ffn_out, *, tm, tiles_per_batch):
    n = x2d.shape[0]
    tpb = tiles_per_batch
    nt = n // tm
    const = lambda i: (0, 0)
    resident = functools.partial(pl.BlockSpec, index_map=const, pipeline_mode=pl.Buffered(1))
    cur = lambda i: jnp.minimum(i, nt - 1)
    done = lambda i: jnp.maximum(i - 1, 0)
    return pl.pallas_call(
        functools.partial(_mix_ffn_kernel, tiles_per_batch=tpb, n_tiles=nt),
        out_shape=jax.ShapeDtypeStruct((n, D), F32),
        grid=(nt + 1,),
        in_specs=[pl.BlockSpec((None, NQ, tm), lambda i: (cur(i) // tpb, 0, cur(i) % tpb)),
                  pl.BlockSpec((tm, GV), lambda i: (cur(i), 0)),
                  pl.BlockSpec((tm, D), lambda i: (cur(i), 0)),
                  pl.BlockSpec((8, 6 * D), const),
                  pl.BlockSpec((1, D), const),
                  pl.BlockSpec((1, D), const),
                  pl.BlockSpec((1, D), const),
                  resident((NQ, D)),
                  resident((GV, D)),
                  resident((D, 2 * FFN_HIDDEN)),
                  resident((FFN_HIDDEN, D))],
        out_specs=pl.BlockSpec((tm, D), lambda i: (done(i), 0)),
        scratch_shapes=[pltpu.VMEM((2, tm, D), F32),
                        pltpu.VMEM((2, tm, D), BF16),
                        pltpu.VMEM((tm, D), F32),
                        pltpu.VMEM((tm, D), F32)],
        compiler_params=pltpu.CompilerParams(dimension_semantics=("arbitrary",),
                                             vmem_limit_bytes=VMEM_LIMIT),
        name="mix_ffn",
    )(att_t, gla, x2d, ada, g_post_mix, g_pre_ffn, g_post_ffn,
      w_out_a, w_out_g, w_ffn_in, w_ffn_out)


def _rope_tables(seq):
    half = HEAD_DIM // 2
    inv_freq = ROPE_BASE ** (-np.arange(0, half, 2, dtype=np.float64) / half)
    dim = np.arange(HEAD_DIM)
    freq = np.tile(inv_freq, HEAD_DIM // ROPE_BLK)[:, None]
    pos = np.arange(seq)[None, :]
    ang = np.where((dim < half)[:, None], pos // GRID_W, pos % GRID_W) * freq
    cos, sin = np.cos(ang), np.sin(ang)
    even = ((dim // ROPE_BLK) % 2 == 0)[:, None]
    sa = np.where(even, -sin, 0.0)
    sb = np.where(even, 0.0, sin)
    return tuple(jnp.asarray(t, F32) for t in (cos, sa, sb))


def kernel(x, c, ctx, c_ctx, w_ada, b_ada, g_pre_mix, g_post_mix, g_pre_ffn, g_post_ffn,
           w_in, attn_sink, w_gate_fwd, b_gate_fwd, w_gate_bwd, b_gate_bwd, g_gla_norm,
           w_out, w_ffn_in, w_ffn_out):
    batch, seq, _ = x.shape
    n_ctx = ctx.shape[1]
    depth = w_ada.shape[0]
    assert depth == 1
    l = 0

    c8 = jnp.zeros((8, D), F32).at[0:batch].set(c).at[4].set(c_ctx)
    ada = _ada(c8, w_ada[l], b_ada[l][None, :])

    wi = w_in[l]
    g0 = NQ + 2 * NKV
    w_qkv_t = wi[:, 0:g0].T.astype(BF16)
    gk0 = g0 + GQ
    gg0 = gk0 + KV_W
    w_rest = jnp.concatenate(
        [wi[:, gk0:gg0], wi[:, g0:gk0], wi[:, gg0:],
         jnp.zeros((D, Z_W - 2 * GATE_RANK), F32)], axis=1).astype(BF16)
    w_gate_p = jnp.zeros((Z_W, LA_W), F32)
    w_gate_p = w_gate_p.at[0:GATE_RANK, 0:GQ].set(w_gate_fwd[l])
    w_gate_p = w_gate_p.at[GATE_RANK:2 * GATE_RANK, GQ:].set(w_gate_bwd[l]).astype(BF16)
    b_gate_p = jnp.concatenate([b_gate_fwd[l], b_gate_bwd[l]])[None, :]

    tm = 512
    tables = _rope_tables(seq)
    x2d = x.reshape(batch * seq, D)
    g_pre = g_pre_mix[l][None, :]
    gn_tiled = jnp.tile(g_gla_norm[l], GLA_HEADS)[None, :]
    qt, k, vt, gin, lah, lal = _inproj(x2d, ada, g_pre, w_qkv_t, w_rest, w_gate_p, b_gate_p,
                                      gn_tiled, tables, batch=batch, tokens=seq, tm=INPROJ_TM)
    _, k_c, vt_c, gin_c, lah_c, lal_c = _inproj(
        ctx.reshape(batch * n_ctx, D), ada, g_pre, w_qkv_t, w_rest, w_gate_p, b_gate_p,
        gn_tiled, None, batch=batch, tokens=n_ctx, tm=n_ctx)

    att_t = _attn(attn_sink[l], qt, k, vt, k_c, vt_c, batch=batch, seq=seq, n_ctx=n_ctx)

    gla = _gla(gin, lah, lal, gin_c, lah_c, lal_c, batch=batch, seq=seq, n_ctx=n_ctx)

    wo = w_out[l]
    out = _mix_ffn(att_t, gla.reshape(batch * seq, GV), x2d, ada,
                   g_post_mix[l][None, :], g_pre_ffn[l][None, :], g_post_ffn[l][None, :],
                   wo[:NQ].astype(BF16), wo[NQ:].astype(BF16),
                   w_ffn_in[l].astype(BF16), w_ffn_out[l].astype(BF16),
                   tm=tm, tiles_per_batch=seq // tm)
    return out.reshape(batch, seq, D)
```

```python
import functools

import jax
import jax.numpy as jnp
import numpy as np
from jax import lax
from jax.experimental import pallas as pl
from jax.experimental.pallas import tpu as pltpu

D = 1024
HEAD_DIM = 64
ATT_HEADS = 8
ATT_KV_HEADS = 2
GROUP = ATT_HEADS // ATT_KV_HEADS
WINDOW = 128
GRID_W = 64
ROPE_BASE = 10000.0
ROPE_BLK = HEAD_DIM // 4
GLA_HEADS = 8
GLA_DK = 32
GLA_DV = 64
CHUNK = 64
GATE_RANK = 16
GATE_TAU = 16.0
FFN_HIDDEN = 2816
NEG_INF = -1e30
EPS = 1e-6
LOG2E = 1.4426950408889634
Q_SCALE = LOG2E * HEAD_DIM ** -0.5
LA_SCALE = LOG2E / GATE_TAU

NQ = ATT_HEADS * HEAD_DIM
NKV = ATT_KV_HEADS * HEAD_DIM
GQ = GLA_HEADS * GLA_DK
GV = GLA_HEADS * GLA_DV
GIN_W = 2 * GQ + 2 * GV
KV_W = GQ + GV
K_COLS = slice(0, GQ)
V_COLS = slice(GQ, KV_W)
Q_COLS = slice(KV_W, KV_W + GQ)
G_COLS = slice(KV_W + GQ, GIN_W)
Z_W = 128
LA_W = 2 * GQ
REST_W = GIN_W + Z_W

LANES = 128
VMEM_LIMIT = 56 * 1024 * 1024

BF16 = jnp.bfloat16
F32 = jnp.float32


def _dot(a, b):
    return jnp.dot(a, b, preferred_element_type=F32)


def _dot_nt(a, b):
    return lax.dot_general(a, b, (((1,), (1,)), ((), ())), preferred_element_type=F32)


def _dot_tn(a, b):
    return lax.dot_general(a, b, (((0,), (0,)), ((), ())), preferred_element_type=F32)


def _rms(x):
    return x * lax.rsqrt(jnp.mean(x * x, axis=-1, keepdims=True) + EPS)


def _silu(x):
    return x * (1.0 / (1.0 + jnp.exp(-x)))


def _ada_kernel(c_ref, w_ref, b_ref, o_ref):
    a = _silu(c_ref[...]).astype(BF16)
    o_ref[...] = _dot(a, w_ref[...].astype(BF16)) + b_ref[...]


def _ada(c8, w_ada, b_ada):
    n = w_ada.shape[1]
    bn = 1024
    return pl.pallas_call(
        _ada_kernel,
        out_shape=jax.ShapeDtypeStruct((8, n), F32),
        grid=(n // bn,),
        in_specs=[pl.BlockSpec((8, D), lambda j: (0, 0)),
                  pl.BlockSpec((D, bn), lambda j: (0, j)),
                  pl.BlockSpec((1, bn), lambda j: (0, j))],
        out_specs=pl.BlockSpec((8, bn), lambda j: (0, j)),
        compiler_params=pltpu.CompilerParams(dimension_semantics=("arbitrary",)),
        name="ada",
    )(c8, w_ada, b_ada)


NORM_ROWS = 128
INPROJ_TM = 1024


def _inproj_kernel(*refs, tiles_per_batch, rope):
    if rope:
        (x_ref, ada_ref, g_ref, wqkv_ref, w_ref, wg_ref, bg_ref, gn_ref,
         cost_ref, sat_ref, sbt_ref,
         qt_ref, k_ref, vt_ref, gin_ref, lah_ref, lal_ref) = refs
        row = pl.program_id(0) // tiles_per_batch
    else:
        (x_ref, ada_ref, g_ref, wqkv_ref, w_ref, wg_ref, bg_ref, gn_ref,
         qt_ref, k_ref, vt_ref, gin_ref, lah_ref, lal_ref) = refs
        row = 4
    ada = ada_ref[pl.ds(row, 1), :]
    sh1 = ada[:, 0:D]
    sc1 = ada[:, D:2 * D]
    tm = x_ref.shape[0]
    hb = jnp.concatenate(
        [((_rms(x_ref[r0:r0 + NORM_ROWS, :]) * g_ref[...]) * (1.0 + sc1) + sh1).astype(BF16)
         for r0 in range(0, tm, NORM_ROWS)], axis=0)

    z = _dot(hb, w_ref[:, GIN_W:REST_W])
    qkv = _dot_nt(wqkv_ref[...], hb)
    logits = _dot(z.astype(BF16), wg_ref[...]) + bg_ref[...]
    gate = _dot(hb, w_ref[:, G_COLS])
    gin = _dot(hb, w_ref[:, 0:KV_W + GQ])

    def head_t(hd):
        xh = qkv[hd * HEAD_DIM:(hd + 1) * HEAD_DIM]
        if not rope:
            return xh
        up = jnp.concatenate([xh[ROPE_BLK:], xh[:ROPE_BLK]], axis=0)
        dn = jnp.concatenate([xh[HEAD_DIM - ROPE_BLK:], xh[:HEAD_DIM - ROPE_BLK]], axis=0)
        return xh * cost_ref[...] + up * sat_ref[...] + dn * sbt_ref[...]

    for hd in range(ATT_HEADS):
        qh = head_t(hd)
        if rope:
            qh = qh * Q_SCALE
        qt_ref[hd * HEAD_DIM:(hd + 1) * HEAD_DIM, :] = qh.astype(BF16)
    kt = jnp.concatenate([head_t(ATT_HEADS + hd) for hd in range(ATT_KV_HEADS)], axis=0)
    k_ref[...] = kt.T.astype(BF16)
    vt_ref[...] = qkv[NQ + NKV:NQ + 2 * NKV].astype(BF16)

    log_sig = jnp.minimum(logits, 0.0) - jnp.log(1.0 + jnp.exp(-jnp.abs(logits)))
    la = log_sig * LA_SCALE
    hi = la.astype(BF16)
    lah_ref[...] = hi
    lal_ref[...] = (la - hi.astype(F32)).astype(BF16)

    gin_ref[:, G_COLS] = (_silu(gate) * gn_ref[...]).astype(BF16)
    gin_ref[:, 0:KV_W] = gin[:, 0:KV_W].astype(BF16)
    gin_ref[:, Q_COLS] = (gin[:, Q_COLS] * (GLA_DK ** -0.5)).astype(BF16)


def _inproj(x2d, ada, g_pre, w_qkv_t, w_rest, w_gate_p, b_gate_p, gn_tiled, tables, *,
            batch, tokens, tm):
    n = x2d.shape[0]
    tpb = tokens // tm
    rope = tables is not None
    const = lambda i: (0, 0)
    in_specs = [pl.BlockSpec((tm, D), lambda i: (i, 0)),
                pl.BlockSpec((8, 6 * D), const),
                pl.BlockSpec((1, D), const),
                pl.BlockSpec((NQ + 2 * NKV, D), const),
                pl.BlockSpec((D, REST_W), const),
                pl.BlockSpec((Z_W, LA_W), const),
                pl.BlockSpec((1, LA_W), const),
                pl.BlockSpec((1, GV), const)]
    args = [x2d, ada, g_pre, w_qkv_t, w_rest, w_gate_p, b_gate_p, gn_tiled]
    if rope:
        in_specs += [pl.BlockSpec((HEAD_DIM, tm), lambda i: (0, i % tpb))] * 3
        args += list(tables)
    row_blk = lambda w: pl.BlockSpec((tm, w), lambda i: (i, 0))
    col_blk = lambda w: pl.BlockSpec((None, w, tm), lambda i: (i // tpb, 0, i % tpb))
    return pl.pallas_call(
        functools.partial(_inproj_kernel, tiles_per_batch=tpb, rope=rope),
        out_shape=(jax.ShapeDtypeStruct((batch, NQ, tokens), BF16),
                   jax.ShapeDtypeStruct((n, NKV), BF16),
                   jax.ShapeDtypeStruct((batch, NKV, tokens), BF16),
                   jax.ShapeDtypeStruct((n, GIN_W), BF16),
                   jax.ShapeDtypeStruct((n, LA_W), BF16),
                   jax.ShapeDtypeStruct((n, LA_W), BF16)),
        grid=(n // tm,),
        in_specs=in_specs,
        out_specs=(col_blk(NQ), row_blk(NKV), col_blk(NKV),
                   row_blk(GIN_W), row_blk(LA_W), row_blk(LA_W)),
        compiler_params=pltpu.CompilerParams(dimension_semantics=("arbitrary",),
                                             vmem_limit_bytes=VMEM_LIMIT),
        name="inproj_lat" if rope else "inproj_ctx",
    )(*args)


ATT_SUB = 16
ATT_Q = 64
ONES_ROWS = 16


def _attn_kernel(sink_ref, qt_ref, kp_ref, kc_ref, kn_ref, kx_ref,
                 vp_ref, vc_ref, vn_ref, vx_ref, o_ref, *, n_steps):
    i = pl.program_id(1)
    bq = WINDOW
    nwin = ATT_Q + 2 * WINDOW
    nq = GROUP * ATT_Q
    n_sub = ATT_SUB * bq // ATT_Q
    c = lax.broadcasted_iota(jnp.int32, (ATT_Q, nq), 0)
    r = lax.broadcasted_iota(jnp.int32, (ATT_Q, nq), 1) % ATT_Q
    behind_ok = c >= r
    ahead_ok = c <= r
    lane = lax.broadcasted_iota(jnp.int32, (1, nq), 1)
    zeros_q = jnp.zeros((HEAD_DIM, nq), BF16)
    zeros_e = jnp.zeros((ATT_Q, nq), BF16)

    kc_all = jnp.concatenate([kp_ref[...], kc_ref[...], kn_ref[...]], axis=0)
    vc_all = jnp.concatenate([vp_ref[...], vc_ref[...], vn_ref[...]], axis=1)
    kx = kx_ref[...]
    vx = vx_ref[...]

    vaug = []
    for a in range(ATT_SUB):
        vwin = jnp.concatenate([vc_all[:, a * bq:(a + 3) * bq], vx], axis=1)
        ones = jnp.ones((ONES_ROWS, vwin.shape[1]), BF16)
        vaug.append([jnp.concatenate([vwin[kvh * HEAD_DIM:(kvh + 1) * HEAD_DIM], ones], axis=0)
                     for kvh in range(ATT_KV_HEADS)])

    tiles = [(s, kvh) for s in range(n_sub) for kvh in range(ATT_KV_HEADS)]

    def score(tile):
        s, kvh = tile
        q0 = s * ATT_Q
        h0 = kvh * GROUP
        qt = jnp.concatenate(
            [qt_ref[(h0 + u) * HEAD_DIM:(h0 + u + 1) * HEAD_DIM, q0:q0 + ATT_Q]
             for u in range(GROUP)], axis=1)
        rhs = jnp.concatenate([qt, zeros_q] if kvh == 0 else [zeros_q, qt], axis=0)
        kwin = jnp.concatenate([kc_all[q0:q0 + nwin], kx], axis=0)
        return _dot(kwin, rhs)

    def window_blocks(s, st):
        blocks = []
        for b in range(nwin // ATT_Q):
            blk = st[b * ATT_Q:(b + 1) * ATT_Q]
            if b == 0:
                blk = jnp.where(behind_ok, blk, NEG_INF)
            if b == nwin // ATT_Q - 1:
                blk = jnp.where(ahead_ok, blk, NEG_INF)
            first_key = s * ATT_Q - WINDOW + b * ATT_Q
            if first_key < 0:
                blk = jnp.where(i > 0, blk, NEG_INF)
            if first_key >= ATT_SUB * bq:
                blk = jnp.where(i < n_steps - 1, blk, NEG_INF)
            blocks.append(blk)
        return blocks

    def softmax_numerators(tile, st):
        s, kvh = tile
        h0 = kvh * GROUP
        parts = window_blocks(s, st) + [st[nwin:]]
        sink = sink_ref[h0 + GROUP - 1]
        for u in range(GROUP - 2, -1, -1):
            sink = jnp.where(lane < (u + 1) * ATT_Q, sink_ref[h0 + u], sink)
        sink = sink * LOG2E
        m = sink
        for p in parts:
            m = jnp.maximum(m, jnp.max(p, axis=0, keepdims=True))
        e = [jnp.exp2(p - m).astype(BF16) for p in parts]
        pad = [zeros_e]
        et = jnp.concatenate((e[:-1] + pad if s % 2 == 0 else pad + e[:-1]) + e[-1:], axis=0)
        return et, jnp.exp2(sink - m)

    def values(tile, et, e_sink):
        s, kvh = tile
        q0 = s * ATT_Q
        ot = _dot(vaug[s // 2][kvh], et)
        denom = ot[HEAD_DIM:HEAD_DIM + 1] + e_sink
        on = ot[0:HEAD_DIM] * (1.0 / denom)
        for u in range(GROUP):
            hd = kvh * GROUP + u
            o_ref[hd * HEAD_DIM:(hd + 1) * HEAD_DIM, q0:q0 + ATT_Q] = (
                on[:, u * ATT_Q:(u + 1) * ATT_Q].astype(BF16))

    scores = [score(t) for t in tiles]
    probs = [softmax_numerators(t, st) for t, st in zip(tiles, scores)]
    for t, (et, e_sink) in zip(tiles, probs):
        values(t, et, e_sink)


def _attn(sink, qt, k, vt, k_ctx, vt_ctx, *, batch, seq, n_ctx):
    bq = WINDOW
    bs = ATT_SUB * bq
    nb = seq // bq
    ns = seq // bs
    k3 = k.reshape(batch, seq, NKV)
    kx3 = k_ctx.reshape(batch, n_ctx, NKV)
    prev = lambda i: jnp.maximum(ATT_SUB * i - 1, 0)
    nxt = lambda i: jnp.minimum(ATT_SUB * (i + 1), nb - 1)
    return pl.pallas_call(
        functools.partial(_attn_kernel, n_steps=ns),
        out_shape=jax.ShapeDtypeStruct((batch, NQ, seq), BF16),
        grid=(batch, ns),
        in_specs=[
            pl.BlockSpec(memory_space=pltpu.SMEM),
            pl.BlockSpec((None, NQ, bs), lambda b, i: (b, 0, i)),
            pl.BlockSpec((None, bq, NKV), lambda b, i: (b, prev(i), 0)),
            pl.BlockSpec((None, bs, NKV), lambda b, i: (b, i, 0)),
            pl.BlockSpec((None, bq, NKV), lambda b, i: (b, nxt(i), 0)),
            pl.BlockSpec((None, n_ctx, NKV), lambda b, i: (b, 0, 0)),
            pl.BlockSpec((None, NKV, bq), lambda b, i: (b, 0, prev(i))),
            pl.BlockSpec((None, NKV, bs), lambda b, i: (b, 0, i)),
            pl.BlockSpec((None, NKV, bq), lambda b, i: (b, 0, nxt(i))),
            pl.BlockSpec((None, NKV, n_ctx), lambda b, i: (b, 0, 0)),
        ],
        out_specs=pl.BlockSpec((None, NQ, bs), lambda b, i: (b, 0, i)),
        compiler_params=pltpu.CompilerParams(dimension_semantics=("arbitrary", "arbitrary"),
                                             vmem_limit_bytes=VMEM_LIMIT),
        name="attn",
    )(sink, qt, k3, k3, k3, kx3, vt, vt, vt, vt_ctx)


GRP_K = 4 * GLA_DK
GRP_V = 4 * GLA_DV
N_GRP = GLA_HEADS // 4
GLA_TB = 512
GLA_STATE_TB = 1024
CUM_BLK = 256


def _gla_constants():
    t = CUM_BLK
    r = np.arange(t)[:, None]
    c = np.arange(t)[None, :]
    same_chunk = (r // CHUNK) == (c // CHUNK)
    tri_l = (same_chunk & (c <= r)).astype(np.float32)
    tri_u = (same_chunk & (c >= r)).astype(np.float32)
    rk = np.arange(4 * CHUNK)[:, None] // CHUNK
    kmask = (rk == (np.arange(GRP_K)[None, :] // GLA_DK)).astype(np.float32)
    vmask = (rk == (np.arange(GRP_V)[None, :] // GLA_DV)).astype(np.float32)
    smask = ((np.arange(GRP_K)[:, None] // GLA_DK)
             == (np.arange(GRP_V)[None, :] // GLA_DV)).astype(np.float32)
    ai = np.arange(CHUNK)[:, None]
    aj = np.arange(4 * CHUNK)[None, :] % CHUNK
    causal = (aj <= ai).astype(np.float32)
    anti = (aj >= ai).astype(np.float32)
    hmean = ((np.arange(GRP_V)[:, None] // GLA_DV)
             == (np.arange(GRP_V)[None, :] // GLA_DV)).astype(np.float32) / GLA_DV
    return (jnp.asarray(tri_l, BF16), jnp.asarray(tri_u, BF16), jnp.asarray(kmask, BF16),
            jnp.asarray(vmask, BF16), jnp.asarray(smask, BF16), jnp.asarray(causal, BF16),
            jnp.asarray(anti, BF16), jnp.asarray(hmean, BF16))


def _gk(g):
    return slice(g * GRP_K, (g + 1) * GRP_K)


def _gv(g):
    return slice(g * GRP_V, (g + 1) * GRP_V)


def _chunk_slices(t):
    return [slice(ci * CHUNK, (ci + 1) * CHUNK) for ci in range(t // CHUNK)]


def _cum_and_totals(la_hi, la_lo, tri, *, reverse):
    parts = [slice(p, p + CUM_BLK) for p in range(0, la_hi.shape[0], CUM_BLK)]
    cum = jnp.concatenate([_dot(tri, la_hi[p]) + _dot(tri, la_lo[p]) for p in parts], axis=0)
    edge = 0 if reverse else CHUNK - 1
    tots = [cum[sl.start + edge:sl.start + edge + 1] for sl in _chunk_slices(cum.shape[0])]
    return cum, tots


def _bcast_chunks(rows, width):
    return jnp.concatenate([jnp.broadcast_to(r, (CHUNK, width)) for r in rows], axis=0)


def _decay(x, log2_factor):
    return x * jnp.exp2(log2_factor).astype(BF16)


def _state_increments(kd, v):
    return [[_dot_tn(kd[sl, _gk(g)], v[sl, _gv(g)]) for g in range(N_GRP)]
            for sl in _chunk_slices(kd.shape[0])]


def _masked_state(state, smask):
    return state.astype(BF16) * smask


def _decay_columns(tots):
    width = tots[0].shape[1]
    row = lax.broadcasted_iota(jnp.int32, (8, width), 0)
    rows = jnp.zeros((8, width), F32)
    for ci, t in enumerate(tots):
        rows = jnp.where(row == ci, jnp.broadcast_to(t, (8, width)), rows)
    return jnp.exp2(rows).T


def _interleave(stage_gens):
    gens = list(stage_gens)
    done = [False] * len(gens)
    rnd = 0
    while not all(done):
        for gi, gen in enumerate(gens):
            if rnd >= gi and not done[gi]:
                try:
                    next(gen)
                except StopIteration:
                    done[gi] = True
        rnd += 1


def _state_stages(st_ref, k, v, la_hi, la_lo, tri, smask, *, reverse,
                  store_ref=None, store_base=None):
    nch = k.shape[0] // CHUNK
    cum, tots = _cum_and_totals(la_hi, la_lo, tri, reverse=reverse)
    yield
    kd = _decay(k, _bcast_chunks(tots, GQ) - cum)
    dcol = _decay_columns(tots)
    yield
    ds = _state_increments(kd, v)
    yield
    state = [st_ref[g] for g in range(N_GRP)]
    for ci in (range(nch - 1, -1, -1) if reverse else range(nch)):
        for g in range(N_GRP):
            if store_ref is not None:
                store_ref[store_base + ci, g] = _masked_state(state[g], smask)
            state[g] = state[g] * dcol[_gk(g), ci:ci + 1] + ds[ci][g]
    for g in range(N_GRP):
        st_ref[g] = state[g]


def _output_stages(q, k, v, la_f, la_b, gate, st_ref, bwd_state, consts, finish):
    tri_l, tri_u, kmask, vmask, smask, causal, anti, hmean = consts
    chunks = _chunk_slices(q.shape[0])

    cum_f, tots_f = _cum_and_totals(la_f[0], la_f[1], tri_l, reverse=False)
    cum_b, tots_b = _cum_and_totals(la_b[0], la_b[1], tri_u, reverse=True)
    yield

    qd_f = _decay(q, cum_f)
    qd_b = _decay(q, cum_b)
    ki_f = _decay(k, -cum_f)
    ki_b = _decay(k, -cum_b)
    kd_f = _decay(k, _bcast_chunks(tots_f, GQ) - cum_f)
    dcol_f = _decay_columns(tots_f)
    v_bd = [[jnp.concatenate([v[sl, _gv(g)]] * 4, axis=0) * vmask for g in range(N_GRP)]
            for sl in chunks]
    yield

    def scores(qd, ki, sl, g):
        ki_bd = jnp.concatenate([ki[sl, _gk(g)]] * 4, axis=0) * kmask
        return _dot_nt(qd[sl, _gk(g)], ki_bd)

    a_f = [[scores(qd_f, ki_f, sl, g) for g in range(N_GRP)] for sl in chunks]
    a_b = [[scores(qd_b, ki_b, sl, g) for g in range(N_GRP)] for sl in chunks]
    ds = _state_increments(kd_f, v)
    yield

    a = [[af[g].astype(BF16) * causal + ab[g].astype(BF16) * anti for g in range(N_GRP)]
         for af, ab in zip(a_f, a_b)]
    state = [st_ref[g] for g in range(N_GRP)]
    st_in = []
    for ci in range(len(chunks)):
        st_in.append([_masked_state(s, smask) for s in state])
        state = [state[g] * dcol_f[_gk(g), ci:ci + 1] + ds[ci][g] for g in range(N_GRP)]
    for g in range(N_GRP):
        st_ref[g] = state[g]
    yield

    outs = []
    for ci, sl in enumerate(chunks):
        parts = []
        for g in range(N_GRP):
            intra = _dot(a[ci][g], v_bd[ci][g])
            qd2 = jnp.concatenate([qd_f[sl, _gk(g)], qd_b[sl, _gk(g)]], axis=1)
            st2 = jnp.concatenate([st_in[ci][g], bwd_state(ci, g)], axis=0)
            parts.append(intra + _dot(qd2, st2))
        outs.append(jnp.concatenate(parts, axis=1))
    o = jnp.concatenate(outs, axis=0)
    yield

    o2 = (o * o).astype(BF16)
    ms = jnp.concatenate([_dot(o2[:, _gv(g)], hmean) for g in range(N_GRP)], axis=1)
    yield
    finish((o * lax.rsqrt(ms + EPS) * gate.astype(F32)).astype(BF16))


def _gla_kernel(skv_ref, slah_ref, slal_ref,
                kv_ref, qg_ref, lahf_ref, lahb_ref, lalf_ref, lalb_ref,
                ginc_ref, lahc_ref, lalc_ref,
                tril_ref, triu_ref, kmask_ref, vmask_ref, smask_ref, causal_ref, anti_ref,
                hmean_ref, o_ref, st_ref, sb_ref, *, n_state_steps):
    step = pl.program_id(1)
    ch_per_half = CUM_BLK // CHUNK
    fwd = slice(0, GQ)
    bwd = slice(GQ, 2 * GQ)

    def ctx_state(cols, tri_ref, reverse):
        st_ref[...] = jnp.zeros_like(st_ref)
        ctx_blocks = [slice(r0, r0 + CUM_BLK) for r0 in range(0, ginc_ref.shape[0], CUM_BLK)]
        for rs in (ctx_blocks[::-1] if reverse else ctx_blocks):
            for _ in _state_stages(st_ref, ginc_ref[rs, K_COLS], ginc_ref[rs, V_COLS],
                                   lahc_ref[rs, cols], lalc_ref[rs, cols], tri_ref[...],
                                   smask_ref[...], reverse=reverse):
                pass

    @pl.when(step < n_state_steps)
    def _backward_states():
        @pl.when(step == 0)
        def _():
            ctx_state(bwd, triu_ref, True)

        blk = n_state_steps - 1 - step
        parts = [slice(r0, r0 + CUM_BLK) for r0 in range(0, GLA_STATE_TB, CUM_BLK)]
        _interleave([
            _state_stages(st_ref, skv_ref[rs, K_COLS], skv_ref[rs, V_COLS],
                          slah_ref[rs, :], slal_ref[rs, :], triu_ref[...], smask_ref[...],
                          reverse=True, store_ref=sb_ref,
                          store_base=blk * (GLA_STATE_TB // CHUNK) + pi * ch_per_half)
            for pi, rs in reversed(list(enumerate(parts)))])

    @pl.when(step >= n_state_steps)
    def _outputs():
        j = step - n_state_steps

        @pl.when(j == 0)
        def _():
            ctx_state(fwd, tril_ref, False)

        consts = (tril_ref[...], triu_ref[...], kmask_ref[...], vmask_ref[...],
                  smask_ref[...], causal_ref[...], anti_ref[...], hmean_ref[...])
        halves = [slice(r0, r0 + CUM_BLK) for r0 in range(0, GLA_TB, CUM_BLK)]

        def half(hi, rs):
            base = j * (GLA_TB // CHUNK) + hi * ch_per_half

            def finish(val):
                o_ref[rs, :] = val

            return _output_stages(
                qg_ref[rs, 0:GQ], kv_ref[rs, K_COLS], kv_ref[rs, V_COLS],
                (lahf_ref[rs, :], lalf_ref[rs, :]), (lahb_ref[rs, :], lalb_ref[rs, :]),
                qg_ref[rs, GQ:KV_W], st_ref, lambda ci, g: sb_ref[base + ci, g], consts, finish)

        _interleave([half(hi, rs) for hi, rs in enumerate(halves)])


def _gla(gin, la_hi, la_lo, gin_ctx, la_hi_ctx, la_lo_ctx, *, batch, seq, n_ctx):
    assert n_ctx % CUM_BLK == 0 and n_ctx // CHUNK <= 8 and GLA_TB // CHUNK <= 8
    tb, stb = GLA_TB, GLA_STATE_TB
    ns, nb = seq // stb, seq // tb
    nch = seq // CHUNK
    r3 = lambda a, t: a.reshape(batch, t, a.shape[-1])
    consts = _gla_constants()

    def state_blk(col):
        return lambda b, s: (b, jnp.maximum(ns - 1 - s, 0), col)

    def out_blk(col):
        return lambda b, s: (b, jnp.maximum(s - ns, 0), col)

    ctx_map = lambda b, s: (b, 0, 0)
    const_map = lambda b, s: (0, 0)
    gin3, lah3, lal3 = r3(gin, seq), r3(la_hi, seq), r3(la_lo, seq)
    return pl.pallas_call(
        functools.partial(_gla_kernel, n_state_steps=ns),
        out_shape=jax.ShapeDtypeStruct((batch, seq, GV), BF16),
        grid=(batch, ns + nb),
        in_specs=[pl.BlockSpec((None, stb, KV_W), state_blk(0)),
                  pl.BlockSpec((None, stb, GQ), state_blk(1)),
                  pl.BlockSpec((None, stb, GQ), state_blk(1)),
                  pl.BlockSpec((None, tb, KV_W), out_blk(0)),
                  pl.BlockSpec((None, tb, KV_W), out_blk(1)),
                  pl.BlockSpec((None, tb, GQ), out_blk(0)),
                  pl.BlockSpec((None, tb, GQ), out_blk(1)),
                  pl.BlockSpec((None, tb, GQ), out_blk(0)),
                  pl.BlockSpec((None, tb, GQ), out_blk(1)),
                  pl.BlockSpec((None, n_ctx, GIN_W), ctx_map),
                  pl.BlockSpec((None, n_ctx, LA_W), ctx_map),
                  pl.BlockSpec((None, n_ctx, LA_W), ctx_map)]
                 + [pl.BlockSpec(cst.shape, const_map) for cst in consts],
        out_specs=pl.BlockSpec((None, tb, GV), out_blk(0)),
        scratch_shapes=[pltpu.VMEM((N_GRP, GRP_K, GRP_V), F32),
                        pltpu.VMEM((nch, N_GRP, GRP_K, GRP_V), BF16)],
        compiler_params=pltpu.CompilerParams(
            dimension_semantics=("arbitrary", "arbitrary"),
            vmem_limit_bytes=VMEM_LIMIT),
        name="gla",
    )(gin3, lah3, lal3, gin3, gin3, lah3, lah3, lal3, lal3, r3(gin_ctx, n_ctx),
      r3(la_hi_ctx, n_ctx), r3(la_lo_ctx, n_ctx), *consts)


FFN_CHUNK = 256
MIX_SLICES = 8
MIX_TM = 512


def _mix_ffn_kernel(att_ref, gla_ref, x_ref, ada_ref, gpm_ref, gpf_ref, gqf_ref,
                    woa_ref, wog_ref, wfi_ref, wfo_ref, o_ref, x1_ref, h_ref, y_ref, acc_ref, *,
                    tiles_per_batch, n_tiles):
    i = pl.program_id(0)
    tm = x_ref.shape[0]
    chunks = [(c0, min(c0 + FFN_CHUNK, FFN_HIDDEN)) for c0 in range(0, FFN_HIDDEN, FFN_CHUNK)]
    n_chunks = len(chunks)
    mix_rows = tm // MIX_SLICES

    def mix_matmul():
        y_ref[...] = _dot_tn(att_ref[...], woa_ref[...]) + _dot(gla_ref[...], wog_ref[...])

    def mix_rows_slice(slot, si, anchor=None):
        rs = slice(si * mix_rows, (si + 1) * mix_rows)
        row = jnp.minimum(i, n_tiles - 1) // tiles_per_batch
        ada = ada_ref[pl.ds(row, 1), :]
        gt1 = ada[:, 2 * D:3 * D]
        if anchor is not None:
            bits = pltpu.bitcast(anchor, jnp.uint32)
            sixteen = jnp.uint32(16)
            gt1 = gt1 + pltpu.bitcast(
                lax.shift_right_logical(lax.shift_right_logical(bits, sixteen), sixteen), F32)
        sh2 = ada[:, 3 * D:4 * D]
        sc2 = ada[:, 4 * D:5 * D]
        x1 = x_ref[rs, :] + gt1 * (_rms(y_ref[rs, :]) * gpm_ref[...])
        x1_ref[slot, rs, :] = x1
        h_ref[slot, rs, :] = ((_rms(x1) * gpf_ref[...]) * (1.0 + sc2) + sh2).astype(BF16)

    def ffn_chunk(slot, ci):
        c0, c1 = chunks[ci]
        h = h_ref[slot]
        g = _dot(h, wfi_ref[:, c0:c1])
        u = _dot(h, wfi_ref[:, FFN_HIDDEN + c0:FFN_HIDDEN + c1])
        a = (_silu(g) * u).astype(BF16)
        part = _dot(a, wfo_ref[c0:c1, :])
        if ci == 0:
            acc_ref[...] = part
        else:
            acc_ref[...] += part
        return part[0:8, :]

    def ffn_finish(slot):
        row = jnp.maximum(i - 1, 0) // tiles_per_batch
        gt2 = ada_ref[pl.ds(row, 1), 5 * D:6 * D]
        o_ref[...] = x1_ref[slot] + gt2 * (_rms(acc_ref[...]) * gqf_ref[...])

    @pl.when(i == 0)
    def _():
        mix_matmul()
        for si in range(MIX_SLICES):
            mix_rows_slice(0, si)

    for parity in range(2):
        @pl.when((i % 2 == parity) & (i > 0))
        def _():
            per_gap = -(-MIX_SLICES // (n_chunks - 1))
            for ci in range(n_chunks):
                anchor = ffn_chunk(1 - parity, ci)
                if ci == 0:
                    mix_matmul()
                else:
                    for si in range((ci - 1) * per_gap, min(ci * per_gap, MIX_SLICES)):
                        mix_rows_slice(parity, si, anchor[0:1, :])
            ffn_finish(1 - parity)


def _mix_ffn(att_t, gla, x2d, ada, g_post_mix, g_pre_ffn, g_post_ffn,
             w_out_a, w_out_g, w_ffn_in, w_ffn_out, *, tm, tiles_per_batch):
    n = x2d.shape[0]
    tpb = tiles_per_batch
    nt = n // tm
    const = lambda i: (0, 0)
    resident = functools.partial(pl.BlockSpec, index_map=const, pipeline_mode=pl.Buffered(1))
    cur = lambda i: jnp.minimum(i, nt - 1)
    done = lambda i: jnp.maximum(i - 1, 0)
    return pl.pallas_call(
        functools.partial(_mix_ffn_kernel, tiles_per_batch=tpb, n_tiles=nt),
        out_shape=jax.ShapeDtypeStruct((n, D), F32),
        grid=(nt + 1,),
        in_specs=[pl.BlockSpec((None, NQ, tm), lambda i: (cur(i) // tpb, 0, cur(i) % tpb)),
                  pl.BlockSpec((tm, GV), lambda i: (cur(i), 0)),
                  pl.BlockSpec((tm, D), lambda i: (cur(i), 0)),
                  pl.BlockSpec((8, 6 * D), const),
                  pl.BlockSpec((1, D), const),
                  pl.BlockSpec((1, D), const),
                  pl.BlockSpec((1, D), const),
                  resident((NQ, D)),
                  resident((GV, D)),
                  resident((D, 2 * FFN_HIDDEN)),
                  resident((FFN_HIDDEN, D))],
        out_specs=pl.BlockSpec((tm, D), lambda i: (done(i), 0)),
        scratch_shapes=[pltpu.VMEM((2, tm, D), F32),
                        pltpu.VMEM((2, tm, D), BF16),
                        pltpu.VMEM((tm, D), F32),
                        pltpu.VMEM((tm, D), F32)],
        compiler_params=pltpu.CompilerParams(dimension_semantics=("arbitrary",),
                                             vmem_limit_bytes=VMEM_LIMIT),
        name="mix_ffn",
    )(att_t, gla, x2d, ada, g_post_mix, g_pre_ffn, g_post_ffn,
      w_out_a, w_out_g, w_ffn_in, w_ffn_out)


def _rope_tables(seq):
    half = HEAD_DIM // 2
    inv_freq = ROPE_BASE ** (-np.arange(0, half, 2, dtype=np.float64) / half)
    dim = np.arange(HEAD_DIM)
    freq = np.tile(inv_freq, HEAD_DIM // ROPE_BLK)[:, None]
    pos = np.arange(seq)[None, :]
    ang = np.where((dim < half)[:, None], pos // GRID_W, pos % GRID_W) * freq
    cos, sin = np.cos(ang), np.sin(ang)
    even = ((dim // ROPE_BLK) % 2 == 0)[:, None]
    sa = np.where(even, -sin, 0.0)
    sb = np.where(even, 0.0, sin)
    return tuple(jnp.asarray(t, F32) for t in (cos, sa, sb))


def kernel(x, c, ctx, c_ctx, w_ada, b_ada, g_pre_mix, g_post_mix, g_pre_ffn, g_post_ffn,
           w_in, attn_sink, w_gate_fwd, b_gate_fwd, w_gate_bwd, b_gate_bwd, g_gla_norm,
           w_out, w_ffn_in, w_ffn_out):
    batch, seq, _ = x.shape
    n_ctx = ctx.shape[1]
    depth = w_ada.shape[0]
    assert depth == 1
    l = 0

    c8 = jnp.zeros((8, D), F32).at[0:batch].set(c).at[4].set(c_ctx)
    ada = _ada(c8, w_ada[l], b_ada[l][None, :])

    wi = w_in[l]
    g0 = NQ + 2 * NKV
    w_qkv_t = wi[:, 0:g0].T.astype(BF16)
    gk0 = g0 + GQ
    gg0 = gk0 + KV_W
    w_rest = jnp.concatenate(
        [wi[:, gk0:gg0], wi[:, g0:gk0], wi[:, gg0:],
         jnp.zeros((D, Z_W - 2 * GATE_RANK), F32)], axis=1).astype(BF16)
    w_gate_p = jnp.zeros((Z_W, LA_W), F32)
    w_gate_p = w_gate_p.at[0:GATE_RANK, 0:GQ].set(w_gate_fwd[l])
    w_gate_p = w_gate_p.at[GATE_RANK:2 * GATE_RANK, GQ:].set(w_gate_bwd[l]).astype(BF16)
    b_gate_p = jnp.concatenate([b_gate_fwd[l], b_gate_bwd[l]])[None, :]

    tables = _rope_tables(seq)
    x2d = x.reshape(batch * seq, D)
    g_pre = g_pre_mix[l][None, :]
    gn_tiled = jnp.tile(g_gla_norm[l], GLA_HEADS)[None, :]
    qt, k, vt, gin, lah, lal = _inproj(x2d, ada, g_pre, w_qkv_t, w_rest, w_gate_p, b_gate_p,
                                      gn_tiled, tables, batch=batch, tokens=seq, tm=INPROJ_TM)
    _, k_c, vt_c, gin_c, lah_c, lal_c = _inproj(
        ctx.reshape(batch * n_ctx, D), ada, g_pre, w_qkv_t, w_rest, w_gate_p, b_gate_p,
        gn_tiled, None, batch=batch, tokens=n_ctx, tm=n_ctx)

    att_t = _attn(attn_sink[l], qt, k, vt, k_c, vt_c, batch=batch, seq=seq, n_ctx=n_ctx)

    gla = _gla(gin, lah, lal, gin_c, lah_c, lal_c, batch=batch, seq=seq, n_ctx=n_ctx)

    wo = w_out[l]
    out = _mix_ffn(att_t, gla.reshape(batch * seq, GV), x2d, ada,
                   g_post_mix[l][None, :], g_pre_ffn[l][None, :], g_post_ffn[l][None, :],
                   wo[:NQ].astype(BF16), wo[NQ:].astype(BF16),
                   w_ffn_in[l].astype(BF16), w_ffn_out[l].astype(BF16),
                   tm=MIX_TM, tiles_per_batch=seq // MIX_TM)
    return out.reshape(batch, seq, D)
```

```python
import functools

import jax
import jax.numpy as jnp
import numpy as np
from jax import lax
from jax.experimental import pallas as pl
from jax.experimental.pallas import tpu as pltpu

D = 1024
HEAD_DIM = 64
ATT_HEADS = 8
ATT_KV_HEADS = 2
GROUP = ATT_HEADS // ATT_KV_HEADS
WINDOW = 128
GRID_W = 64
ROPE_BASE = 10000.0
ROPE_BLK = HEAD_DIM // 4
GLA_HEADS = 8
GLA_DK = 32
GLA_DV = 64
CHUNK = 64
GATE_RANK = 16
GATE_TAU = 16.0
FFN_HIDDEN = 2816
NEG_INF = -1e30
EPS = 1e-6
LOG2E = 1.4426950408889634
Q_SCALE = LOG2E * HEAD_DIM ** -0.5
LA_SCALE = LOG2E / GATE_TAU

NQ = ATT_HEADS * HEAD_DIM
NKV = ATT_KV_HEADS * HEAD_DIM
GQ = GLA_HEADS * GLA_DK
GV = GLA_HEADS * GLA_DV
GIN_W = 2 * GQ + 2 * GV
KV_W = GQ + GV
K_COLS = slice(0, GQ)
V_COLS = slice(GQ, KV_W)
Q_COLS = slice(KV_W, KV_W + GQ)
G_COLS = slice(KV_W + GQ, GIN_W)
Z_W = 128
LA_W = 2 * GQ
REST_W = GIN_W + Z_W

LANES = 128
VMEM_LIMIT = 56 * 1024 * 1024

BF16 = jnp.bfloat16
F32 = jnp.float32


def _dot(a, b):
    return jnp.dot(a, b, preferred_element_type=F32)


def _dot_nt(a, b):
    return lax.dot_general(a, b, (((1,), (1,)), ((), ())), preferred_element_type=F32)


def _dot_tn(a, b):
    return lax.dot_general(a, b, (((0,), (0,)), ((), ())), preferred_element_type=F32)


def _rms(x):
    return x * lax.rsqrt(jnp.mean(x * x, axis=-1, keepdims=True) + EPS)


def _silu(x):
    return x * (1.0 / (1.0 + jnp.exp(-x)))


def _ada_kernel(c_ref, w_ref, b_ref, o_ref):
    a = _silu(c_ref[...]).astype(BF16)
    o_ref[...] = _dot(a, w_ref[...].astype(BF16)) + b_ref[...]


def _ada(c8, w_ada, b_ada):
    n = w_ada.shape[1]
    bn = 1024
    return pl.pallas_call(
        _ada_kernel,
        out_shape=jax.ShapeDtypeStruct((8, n), F32),
        grid=(n // bn,),
        in_specs=[pl.BlockSpec((8, D), lambda j: (0, 0)),
                  pl.BlockSpec((D, bn), lambda j: (0, j)),
                  pl.BlockSpec((1, bn), lambda j: (0, j))],
        out_specs=pl.BlockSpec((8, bn), lambda j: (0, j)),
        compiler_params=pltpu.CompilerParams(dimension_semantics=("arbitrary",)),
        name="ada",
    )(c8, w_ada, b_ada)


NORM_ROWS = 128
INPROJ_TM = 1024


def _inproj_kernel(*refs, tiles_per_batch, rope):
    if rope:
        (x_ref, ada_ref, g_ref, wqkv_ref, w_ref, wg_ref, bg_ref, gn_ref,
         cost_ref, sat_ref, sbt_ref,
         qt_ref, k_ref, vt_ref, gin_ref, lah_ref, lal_ref) = refs
        row = pl.program_id(0) // tiles_per_batch
    else:
        (x_ref, ada_ref, g_ref, wqkv_ref, w_ref, wg_ref, bg_ref, gn_ref,
         qt_ref, k_ref, vt_ref, gin_ref, lah_ref, lal_ref) = refs
        row = 4
    ada = ada_ref[pl.ds(row, 1), :]
    sh1 = ada[:, 0:D]
    sc1 = ada[:, D:2 * D]
    tm = x_ref.shape[0]
    hb = jnp.concatenate(
        [((_rms(x_ref[r0:r0 + NORM_ROWS, :]) * g_ref[...]) * (1.0 + sc1) + sh1).astype(BF16)
         for r0 in range(0, tm, NORM_ROWS)], axis=0)

    z = _dot(hb, w_ref[:, GIN_W:REST_W])
    qkv = _dot_nt(wqkv_ref[...], hb)
    logits = _dot(z.astype(BF16), wg_ref[...]) + bg_ref[...]
    gate = _dot(hb, w_ref[:, G_COLS])
    gin = _dot(hb, w_ref[:, 0:KV_W + GQ])

    def head_t(hd):
        xh = qkv[hd * HEAD_DIM:(hd + 1) * HEAD_DIM]
        if not rope:
            return xh
        up = jnp.concatenate([xh[ROPE_BLK:], xh[:ROPE_BLK]], axis=0)
        dn = jnp.concatenate([xh[HEAD_DIM - ROPE_BLK:], xh[:HEAD_DIM - ROPE_BLK]], axis=0)
        return xh * cost_ref[...] + up * sat_ref[...] + dn * sbt_ref[...]

    for hd in range(ATT_HEADS):
        qh = head_t(hd)
        if rope:
            qh = qh * Q_SCALE
        qt_ref[hd * HEAD_DIM:(hd + 1) * HEAD_DIM, :] = qh.astype(BF16)
    kt = jnp.concatenate([head_t(ATT_HEADS + hd) for hd in range(ATT_KV_HEADS)], axis=0)
    k_ref[...] = kt.T.astype(BF16)
    vt_ref[...] = qkv[NQ + NKV:NQ + 2 * NKV].astype(BF16)

    log_sig = jnp.minimum(logits, 0.0) - jnp.log(1.0 + jnp.exp(-jnp.abs(logits)))
    la = log_sig * LA_SCALE
    hi = la.astype(BF16)
    lah_ref[...] = hi
    lal_ref[...] = (la - hi.astype(F32)).astype(BF16)

    gin_ref[:, G_COLS] = (_silu(gate) * gn_ref[...]).astype(BF16)
    gin_ref[:, 0:KV_W] = gin[:, 0:KV_W].astype(BF16)
    gin_ref[:, Q_COLS] = (gin[:, Q_COLS] * (GLA_DK ** -0.5)).astype(BF16)


def _inproj(x2d, ada, g_pre, w_qkv_t, w_rest, w_gate_p, b_gate_p, gn_tiled, tables, *,
            batch, tokens, tm):
    n = x2d.shape[0]
    tpb = tokens // tm
    rope = tables is not None
    const = lambda i: (0, 0)
    in_specs = [pl.BlockSpec((tm, D), lambda i: (i, 0)),
                pl.BlockSpec((8, 6 * D), const),
                pl.BlockSpec((1, D), const),
                pl.BlockSpec((NQ + 2 * NKV, D), const),
                pl.BlockSpec((D, REST_W), const),
                pl.BlockSpec((Z_W, LA_W), const),
                pl.BlockSpec((1, LA_W), const),
                pl.BlockSpec((1, GV), const)]
    args = [x2d, ada, g_pre, w_qkv_t, w_rest, w_gate_p, b_gate_p, gn_tiled]
    if rope:
        in_specs += [pl.BlockSpec((HEAD_DIM, tm), lambda i: (0, i % tpb))] * 3
        args += list(tables)
    row_blk = lambda w: pl.BlockSpec((tm, w), lambda i: (i, 0))
    col_blk = lambda w: pl.BlockSpec((None, w, tm), lambda i: (i // tpb, 0, i % tpb))
    return pl.pallas_call(
        functools.partial(_inproj_kernel, tiles_per_batch=tpb, rope=rope),
        out_shape=(jax.ShapeDtypeStruct((batch, NQ, tokens), BF16),
                   jax.ShapeDtypeStruct((n, NKV), BF16),
                   jax.ShapeDtypeStruct((batch, NKV, tokens), BF16),
                   jax.ShapeDtypeStruct((n, GIN_W), BF16),
                   jax.ShapeDtypeStruct((n, LA_W), BF16),
                   jax.ShapeDtypeStruct((n, LA_W), BF16)),
        grid=(n // tm,),
        in_specs=in_specs,
        out_specs=(col_blk(NQ), row_blk(NKV), col_blk(NKV),
                   row_blk(GIN_W), row_blk(LA_W), row_blk(LA_W)),
        compiler_params=pltpu.CompilerParams(dimension_semantics=("arbitrary",),
                                             vmem_limit_bytes=VMEM_LIMIT),
        name="inproj_lat" if rope else "inproj_ctx",
    )(*args)


ATT_SUB = 16
ATT_Q = 64
ONES_ROWS = 16


def _attn_kernel(sink_ref, qt_ref, kp_ref, kc_ref, kn_ref, kx_ref,
                 vp_ref, vc_ref, vn_ref, vx_ref, o_ref, *, n_steps):
    i = pl.program_id(1)
    bq = WINDOW
    nwin = ATT_Q + 2 * WINDOW
    nq = GROUP * ATT_Q
    n_sub = ATT_SUB * bq // ATT_Q
    c = lax.broadcasted_iota(jnp.int32, (ATT_Q, nq), 0)
    r = lax.broadcasted_iota(jnp.int32, (ATT_Q, nq), 1) % ATT_Q
    behind_ok = c >= r
    ahead_ok = c <= r
    lane = lax.broadcasted_iota(jnp.int32, (1, nq), 1)
    zeros_q = jnp.zeros((HEAD_DIM, nq), BF16)
    zeros_e = jnp.zeros((ATT_Q, nq), BF16)

    kc_all = jnp.concatenate([kp_ref[...], kc_ref[...], kn_ref[...]], axis=0)
    vc_all = jnp.concatenate([vp_ref[...], vc_ref[...], vn_ref[...]], axis=1)
    kx = kx_ref[...]
    vx = vx_ref[...]

    vaug = []
    for a in range(ATT_SUB):
        vwin = jnp.concatenate([vc_all[:, a * bq:(a + 3) * bq], vx], axis=1)
        ones = jnp.ones((ONES_ROWS, vwin.shape[1]), BF16)
        vaug.append([jnp.concatenate([vwin[kvh * HEAD_DIM:(kvh + 1) * HEAD_DIM], ones], axis=0)
                     for kvh in range(ATT_KV_HEADS)])

    tiles = [(s, kvh) for s in range(n_sub) for kvh in range(ATT_KV_HEADS)]

    def score(tile):
        s, kvh = tile
        q0 = s * ATT_Q
        h0 = kvh * GROUP
        qt = jnp.concatenate(
            [qt_ref[(h0 + u) * HEAD_DIM:(h0 + u + 1) * HEAD_DIM, q0:q0 + ATT_Q]
             for u in range(GROUP)], axis=1)
        rhs = jnp.concatenate([qt, zeros_q] if kvh == 0 else [zeros_q, qt], axis=0)
        kwin = jnp.concatenate([kc_all[q0:q0 + nwin], kx], axis=0)
        return _dot(kwin, rhs)

    def window_blocks(s, st):
        blocks = []
        for b in range(nwin // ATT_Q):
            blk = st[b * ATT_Q:(b + 1) * ATT_Q]
            if b == 0:
                blk = jnp.where(behind_ok, blk, NEG_INF)
            if b == nwin // ATT_Q - 1:
                blk = jnp.where(ahead_ok, blk, NEG_INF)
            first_key = s * ATT_Q - WINDOW + b * ATT_Q
            if first_key < 0:
                blk = jnp.where(i > 0, blk, NEG_INF)
            if first_key >= ATT_SUB * bq:
                blk = jnp.where(i < n_steps - 1, blk, NEG_INF)
            blocks.append(blk)
        return blocks

    def softmax_numerators(tile, st):
        s, kvh = tile
        h0 = kvh * GROUP
        parts = window_blocks(s, st) + [st[nwin:]]
        sink = sink_ref[h0 + GROUP - 1]
        for u in range(GROUP - 2, -1, -1):
            sink = jnp.where(lane < (u + 1) * ATT_Q, sink_ref[h0 + u], sink)
        sink = sink * LOG2E
        m = sink
        for p in parts:
            m = jnp.maximum(m, jnp.max(p, axis=0, keepdims=True))
        e = [jnp.exp2(p - m).astype(BF16) for p in parts]
        pad = [zeros_e]
        et = jnp.concatenate((e[:-1] + pad if s % 2 == 0 else pad + e[:-1]) + e[-1:], axis=0)
        return et, jnp.exp2(sink - m)

    def values(tile, et, e_sink):
        s, kvh = tile
        q0 = s * ATT_Q
        ot = _dot(vaug[s // 2][kvh], et)
        denom = ot[HEAD_DIM:HEAD_DIM + 1] + e_sink
        on = ot[0:HEAD_DIM] * (1.0 / denom)
        for u in range(GROUP):
            hd = kvh * GROUP + u
            o_ref[hd * HEAD_DIM:(hd + 1) * HEAD_DIM, q0:q0 + ATT_Q] = (
                on[:, u * ATT_Q:(u + 1) * ATT_Q].astype(BF16))

    scores = [score(t) for t in tiles]
    probs = [softmax_numerators(t, st) for t, st in zip(tiles, scores)]
    for t, (et, e_sink) in zip(tiles, probs):
        values(t, et, e_sink)


def _attn(sink, qt, k, vt, k_ctx, vt_ctx, *, batch, seq, n_ctx):
    bq = WINDOW
    bs = ATT_SUB * bq
    nb = seq // bq
    ns = seq // bs
    k3 = k.reshape(batch, seq, NKV)
    kx3 = k_ctx.reshape(batch, n_ctx, NKV)
    prev = lambda i: jnp.maximum(ATT_SUB * i - 1, 0)
    nxt = lambda i: jnp.minimum(ATT_SUB * (i + 1), nb - 1)
    return pl.pallas_call(
        functools.partial(_attn_kernel, n_steps=ns),
        out_shape=jax.ShapeDtypeStruct((batch, NQ, seq), BF16),
        grid=(batch, ns),
        in_specs=[
            pl.BlockSpec(memory_space=pltpu.SMEM),
            pl.BlockSpec((None, NQ, bs), lambda b, i: (b, 0, i)),
            pl.BlockSpec((None, bq, NKV), lambda b, i: (b, prev(i), 0)),
            pl.BlockSpec((None, bs, NKV), lambda b, i: (b, i, 0)),
            pl.BlockSpec((None, bq, NKV), lambda b, i: (b, nxt(i), 0)),
            pl.BlockSpec((None, n_ctx, NKV), lambda b, i: (b, 0, 0)),
            pl.BlockSpec((None, NKV, bq), lambda b, i: (b, 0, prev(i))),
            pl.BlockSpec((None, NKV, bs), lambda b, i: (b, 0, i)),
            pl.BlockSpec((None, NKV, bq), lambda b, i: (b, 0, nxt(i))),
            pl.BlockSpec((None, NKV, n_ctx), lambda b, i: (b, 0, 0)),
        ],
        out_specs=pl.BlockSpec((None, NQ, bs), lambda b, i: (b, 0, i)),
        compiler_params=pltpu.CompilerParams(dimension_semantics=("arbitrary", "arbitrary"),
                                             vmem_limit_bytes=VMEM_LIMIT),
        name="attn",
    )(sink, qt, k3, k3, k3, kx3, vt, vt, vt, vt_ctx)


GRP_K = 4 * GLA_DK
GRP_V = 4 * GLA_DV
N_GRP = GLA_HEADS // 4
GLA_TB = 1024
GLA_STATE_TB = 1024
CUM_BLK = 256


def _gla_constants():
    t = CUM_BLK
    r = np.arange(t)[:, None]
    c = np.arange(t)[None, :]
    same_chunk = (r // CHUNK) == (c // CHUNK)
    tri_l = (same_chunk & (c <= r)).astype(np.float32)
    tri_u = (same_chunk & (c >= r)).astype(np.float32)
    rk = np.arange(4 * CHUNK)[:, None] // CHUNK
    kmask = (rk == (np.arange(GRP_K)[None, :] // GLA_DK)).astype(np.float32)
    vmask = (rk == (np.arange(GRP_V)[None, :] // GLA_DV)).astype(np.float32)
    smask = ((np.arange(GRP_K)[:, None] // GLA_DK)
             == (np.arange(GRP_V)[None, :] // GLA_DV)).astype(np.float32)
    ai = np.arange(CHUNK)[:, None]
    aj = np.arange(4 * CHUNK)[None, :] % CHUNK
    causal = (aj <= ai).astype(np.float32)
    anti = (aj >= ai).astype(np.float32)
    hmean = ((np.arange(GRP_V)[:, None] // GLA_DV)
             == (np.arange(GRP_V)[None, :] // GLA_DV)).astype(np.float32) / GLA_DV
    return (jnp.asarray(tri_l, BF16), jnp.asarray(tri_u, BF16), jnp.asarray(kmask, BF16),
            jnp.asarray(vmask, BF16), jnp.asarray(smask, BF16), jnp.asarray(causal, BF16),
            jnp.asarray(anti, BF16), jnp.asarray(hmean, BF16))


def _gk(g):
    return slice(g * GRP_K, (g + 1) * GRP_K)


def _gv(g):
    return slice(g * GRP_V, (g + 1) * GRP_V)


def _chunk_slices(t):
    return [slice(ci * CHUNK, (ci + 1) * CHUNK) for ci in range(t // CHUNK)]


def _cum_and_totals(la_hi, la_lo, tri, *, reverse):
    parts = [slice(p, p + CUM_BLK) for p in range(0, la_hi.shape[0], CUM_BLK)]
    cum = jnp.concatenate([_dot(tri, la_hi[p]) + _dot(tri, la_lo[p]) for p in parts], axis=0)
    edge = 0 if reverse else CHUNK - 1
    tots = [cum[sl.start + edge:sl.start + edge + 1] for sl in _chunk_slices(cum.shape[0])]
    return cum, tots


def _bcast_chunks(rows, width):
    return jnp.concatenate([jnp.broadcast_to(r, (CHUNK, width)) for r in rows], axis=0)


def _decay(x, log2_factor):
    return x * jnp.exp2(log2_factor).astype(BF16)


def _state_increments(kd, v):
    return [[_dot_tn(kd[sl, _gk(g)], v[sl, _gv(g)]) for g in range(N_GRP)]
            for sl in _chunk_slices(kd.shape[0])]


def _masked_state(state, smask):
    return state.astype(BF16) * smask


def _decay_columns(tots):
    width = tots[0].shape[1]
    row = lax.broadcasted_iota(jnp.int32, (8, width), 0)
    rows = jnp.zeros((8, width), F32)
    for ci, t in enumerate(tots):
        rows = jnp.where(row == ci, jnp.broadcast_to(t, (8, width)), rows)
    return jnp.exp2(rows).T


def _interleave(stage_gens):
    gens = list(stage_gens)
    done = [False] * len(gens)
    rnd = 0
    while not all(done):
        for gi, gen in enumerate(gens):
            if rnd >= gi and not done[gi]:
                try:
                    next(gen)
                except StopIteration:
                    done[gi] = True
        rnd += 1


def _state_stages(st_ref, k, v, la_hi, la_lo, tri, smask, *, reverse,
                  store_ref=None, store_base=None):
    nch = k.shape[0] // CHUNK
    cum, tots = _cum_and_totals(la_hi, la_lo, tri, reverse=reverse)
    yield
    kd = _decay(k, _bcast_chunks(tots, GQ) - cum)
    dcol = _decay_columns(tots)
    yield
    ds = _state_increments(kd, v)
    yield
    state = [st_ref[g] for g in range(N_GRP)]
    for ci in (range(nch - 1, -1, -1) if reverse else range(nch)):
        for g in range(N_GRP):
            if store_ref is not None:
                store_ref[store_base + ci, g] = _masked_state(state[g], smask)
            state[g] = state[g] * dcol[_gk(g), ci:ci + 1] + ds[ci][g]
    for g in range(N_GRP):
        st_ref[g] = state[g]


def _output_stages(q, k, v, la_f, la_b, gate, st_ref, bwd_state, consts, finish):
    tri_l, tri_u, kmask, vmask, smask, causal, anti, hmean = consts
    chunks = _chunk_slices(q.shape[0])

    cum_f, tots_f = _cum_and_totals(la_f[0], la_f[1], tri_l, reverse=False)
    cum_b, tots_b = _cum_and_totals(la_b[0], la_b[1], tri_u, reverse=True)
    yield

    qd_f = _decay(q, cum_f)
    qd_b = _decay(q, cum_b)
    ki_f = _decay(k, -cum_f)
    ki_b = _decay(k, -cum_b)
    kd_f = _decay(k, _bcast_chunks(tots_f, GQ) - cum_f)
    dcol_f = _decay_columns(tots_f)
    v_bd = [[jnp.concatenate([v[sl, _gv(g)]] * 4, axis=0) * vmask for g in range(N_GRP)]
            for sl in chunks]
    yield

    def scores(qd, ki, sl, g):
        ki_bd = jnp.concatenate([ki[sl, _gk(g)]] * 4, axis=0) * kmask
        return _dot_nt(qd[sl, _gk(g)], ki_bd)

    a_f = [[scores(qd_f, ki_f, sl, g) for g in range(N_GRP)] for sl in chunks]
    a_b = [[scores(qd_b, ki_b, sl, g) for g in range(N_GRP)] for sl in chunks]
    ds = _state_increments(kd_f, v)
    yield

    a = [[af[g].astype(BF16) * causal + ab[g].astype(BF16) * anti for g in range(N_GRP)]
         for af, ab in zip(a_f, a_b)]
    state = [st_ref[g] for g in range(N_GRP)]
    st_in = []
    for ci in range(len(chunks)):
        st_in.append([_masked_state(s, smask) for s in state])
        state = [state[g] * dcol_f[_gk(g), ci:ci + 1] + ds[ci][g] for g in range(N_GRP)]
    for g in range(N_GRP):
        st_ref[g] = state[g]
    yield

    outs = []
    for ci, sl in enumerate(chunks):
        parts = []
        for g in range(N_GRP):
            intra = _dot(a[ci][g], v_bd[ci][g])
            qd2 = jnp.concatenate([qd_f[sl, _gk(g)], qd_b[sl, _gk(g)]], axis=1)
            st2 = jnp.concatenate([st_in[ci][g], bwd_state(ci, g)], axis=0)
            parts.append(intra + _dot(qd2, st2))
        outs.append(jnp.concatenate(parts, axis=1))
    o = jnp.concatenate(outs, axis=0)
    yield

    o2 = (o * o).astype(BF16)
    ms = jnp.concatenate([_dot(o2[:, _gv(g)], hmean) for g in range(N_GRP)], axis=1)
    yield
    finish((o * lax.rsqrt(ms + EPS) * gate.astype(F32)).astype(BF16))


def _gla_kernel(skv_ref, slah_ref, slal_ref,
                kv_ref, qg_ref, lahf_ref, lahb_ref, lalf_ref, lalb_ref,
                ginc_ref, lahc_ref, lalc_ref,
                tril_ref, triu_ref, kmask_ref, vmask_ref, smask_ref, causal_ref, anti_ref,
                hmean_ref, o_ref, st_ref, sb_ref, *, n_state_steps):
    step = pl.program_id(1)
    ch_per_half = CUM_BLK // CHUNK
    fwd = slice(0, GQ)
    bwd = slice(GQ, 2 * GQ)

    def ctx_state(cols, tri_ref, reverse):
        st_ref[...] = jnp.zeros_like(st_ref)
        ctx_blocks = [slice(r0, r0 + CUM_BLK) for r0 in range(0, ginc_ref.shape[0], CUM_BLK)]
        for rs in (ctx_blocks[::-1] if reverse else ctx_blocks):
            for _ in _state_stages(st_ref, ginc_ref[rs, K_COLS], ginc_ref[rs, V_COLS],
                                   lahc_ref[rs, cols], lalc_ref[rs, cols], tri_ref[...],
                                   smask_ref[...], reverse=reverse):
                pass

    @pl.when(step < n_state_steps)
    def _backward_states():
        @pl.when(step == 0)
        def _():
            ctx_state(bwd, triu_ref, True)

        blk = n_state_steps - 1 - step
        parts = [slice(r0, r0 + CUM_BLK) for r0 in range(0, GLA_STATE_TB, CUM_BLK)]
        _interleave([
            _state_stages(st_ref, skv_ref[rs, K_COLS], skv_ref[rs, V_COLS],
                          slah_ref[rs, :], slal_ref[rs, :], triu_ref[...], smask_ref[...],
                          reverse=True, store_ref=sb_ref,
                          store_base=blk * (GLA_STATE_TB // CHUNK) + pi * ch_per_half)
            for pi, rs in reversed(list(enumerate(parts)))])

    @pl.when(step >= n_state_steps)
    def _outputs():
        j = step - n_state_steps

        @pl.when(j == 0)
        def _():
            ctx_state(fwd, tril_ref, False)

        consts = (tril_ref[...], triu_ref[...], kmask_ref[...], vmask_ref[...],
                  smask_ref[...], causal_ref[...], anti_ref[...], hmean_ref[...])
        halves = [slice(r0, r0 + CUM_BLK) for r0 in range(0, GLA_TB, CUM_BLK)]

        def half(hi, rs):
            base = j * (GLA_TB // CHUNK) + hi * ch_per_half

            def finish(val):
                o_ref[rs, :] = val

            return _output_stages(
                qg_ref[rs, 0:GQ], kv_ref[rs, K_COLS], kv_ref[rs, V_COLS],
                (lahf_ref[rs, :], lalf_ref[rs, :]), (lahb_ref[rs, :], lalb_ref[rs, :]),
                qg_ref[rs, GQ:KV_W], st_ref, lambda ci, g: sb_ref[base + ci, g], consts, finish)

        _interleave([half(hi, rs) for hi, rs in enumerate(halves)])


def _gla(gin, la_hi, la_lo, gin_ctx, la_hi_ctx, la_lo_ctx, *, batch, seq, n_ctx):
    assert n_ctx % CUM_BLK == 0 and CUM_BLK // CHUNK <= 8
    tb, stb = GLA_TB, GLA_STATE_TB
    ns, nb = seq // stb, seq // tb
    nch = seq // CHUNK
    r3 = lambda a, t: a.reshape(batch, t, a.shape[-1])
    consts = _gla_constants()

    def state_blk(col):
        return lambda b, s: (b, jnp.maximum(ns - 1 - s, 0), col)

    def out_blk(col):
        return lambda b, s: (b, jnp.maximum(s - ns, 0), col)

    ctx_map = lambda b, s: (b, 0, 0)
    const_map = lambda b, s: (0, 0)
    gin3, lah3, lal3 = r3(gin, seq), r3(la_hi, seq), r3(la_lo, seq)
    return pl.pallas_call(
        functools.partial(_gla_kernel, n_state_steps=ns),
        out_shape=jax.ShapeDtypeStruct((batch, seq, GV), BF16),
        grid=(batch, ns + nb),
        in_specs=[pl.BlockSpec((None, stb, KV_W), state_blk(0)),
                  pl.BlockSpec((None, stb, GQ), state_blk(1)),
                  pl.BlockSpec((None, stb, GQ), state_blk(1)),
                  pl.BlockSpec((None, tb, KV_W), out_blk(0)),
                  pl.BlockSpec((None, tb, KV_W), out_blk(1)),
                  pl.BlockSpec((None, tb, GQ), out_blk(0)),
                  pl.BlockSpec((None, tb, GQ), out_blk(1)),
                  pl.BlockSpec((None, tb, GQ), out_blk(0)),
                  pl.BlockSpec((None, tb, GQ), out_blk(1)),
                  pl.BlockSpec((None, n_ctx, GIN_W), ctx_map),
                  pl.BlockSpec((None, n_ctx, LA_W), ctx_map),
                  pl.BlockSpec((None, n_ctx, LA_W), ctx_map)]
                 + [pl.BlockSpec(cst.shape, const_map) for cst in consts],
        out_specs=pl.BlockSpec((None, tb, GV), out_blk(0)),
        scratch_shapes=[pltpu.VMEM((N_GRP, GRP_K, GRP_V), F32),
                        pltpu.VMEM((nch, N_GRP, GRP_K, GRP_V), BF16)],
        compiler_params=pltpu.CompilerParams(
            dimension_semantics=("arbitrary", "arbitrary"),
            vmem_limit_bytes=VMEM_LIMIT),
        name="gla",
    )(gin3, lah3, lal3, gin3, gin3, lah3, lah3, lal3, lal3, r3(gin_ctx, n_ctx),
      r3(la_hi_ctx, n_ctx), r3(la_lo_ctx, n_ctx), *consts)


FFN_CHUNK = 256
MIX_SLICES = 8
MIX_TM = 512


def _mix_ffn_kernel(att_ref, gla_ref, x_ref, ada_ref, gpm_ref, gpf_ref, gqf_ref,
                    woa_ref, wog_ref, wfi_ref, wfo_ref, o_ref, x1_ref, h_ref, y_ref, acc_ref, *,
                    tiles_per_batch, n_tiles):
    i = pl.program_id(0)
    tm = x_ref.shape[0]
    chunks = [(c0, min(c0 + FFN_CHUNK, FFN_HIDDEN)) for c0 in range(0, FFN_HIDDEN, FFN_CHUNK)]
    n_chunks = len(chunks)
    mix_rows = tm // MIX_SLICES

    def mix_matmul():
        y_ref[...] = _dot_tn(att_ref[...], woa_ref[...]) + _dot(gla_ref[...], wog_ref[...])

    def mix_rows_slice(slot, si, anchor=None):
        rs = slice(si * mix_rows, (si + 1) * mix_rows)
        row = jnp.minimum(i, n_tiles - 1) // tiles_per_batch
        ada = ada_ref[pl.ds(row, 1), :]
        gt1 = ada[:, 2 * D:3 * D]
        if anchor is not None:
            bits = pltpu.bitcast(anchor, jnp.uint32)
            sixteen = jnp.uint32(16)
            gt1 = gt1 + pltpu.bitcast(
                lax.shift_right_logical(lax.shift_right_logical(bits, sixteen), sixteen), F32)
        sh2 = ada[:, 3 * D:4 * D]
        sc2 = ada[:, 4 * D:5 * D]
        x1 = x_ref[rs, :] + gt1 * (_rms(y_ref[rs, :]) * gpm_ref[...])
        x1_ref[slot, rs, :] = x1
        h_ref[slot, rs, :] = ((_rms(x1) * gpf_ref[...]) * (1.0 + sc2) + sh2).astype(BF16)

    def ffn_chunk(slot, ci):
        c0, c1 = chunks[ci]
        h = h_ref[slot]
        g = _dot(h, wfi_ref[:, c0:c1])
        u = _dot(h, wfi_ref[:, FFN_HIDDEN + c0:FFN_HIDDEN + c1])
        a = (_silu(g) * u).astype(BF16)
        part = _dot(a, wfo_ref[c0:c1, :])
        if ci == 0:
            acc_ref[...] = part
        else:
            acc_ref[...] += part
        return part[0:8, :]

    def ffn_finish(slot):
        row = jnp.maximum(i - 1, 0) // tiles_per_batch
        gt2 = ada_ref[pl.ds(row, 1), 5 * D:6 * D]
        o_ref[...] = x1_ref[slot] + gt2 * (_rms(acc_ref[...]) * gqf_ref[...])

    @pl.when(i == 0)
    def _():
        mix_matmul()
        for si in range(MIX_SLICES):
            mix_rows_slice(0, si)

    for parity in range(2):
        @pl.when((i % 2 == parity) & (i > 0))
        def _():
            per_gap = -(-MIX_SLICES // (n_chunks - 1))
            for ci in range(n_chunks):
                anchor = ffn_chunk(1 - parity, ci)
                if ci == 0:
                    mix_matmul()
                else:
                    for si in range((ci - 1) * per_gap, min(ci * per_gap, MIX_SLICES)):
                        mix_rows_slice(parity, si, anchor[0:1, :])
            ffn_finish(1 - parity)


def _mix_ffn(att_t, gla, x2d, ada, g_post_mix, g_pre_ffn, g_post_ffn,
             w_out_a, w_out_g, w_ffn_in, w_ffn_out, *, tm, tiles_per_batch):
    n = x2d.shape[0]
    tpb = tiles_per_batch
    nt = n // tm
    const = lambda i: (0, 0)
    resident = functools.partial(pl.BlockSpec, index_map=const, pipeline_mode=pl.Buffered(1))
    cur = lambda i: jnp.minimum(i, nt - 1)
    done = lambda i: jnp.maximum(i - 1, 0)
    return pl.pallas_call(
        functools.partial(_mix_ffn_kernel, tiles_per_batch=tpb, n_tiles=nt),
        out_shape=jax.ShapeDtypeStruct((n, D), F32),
        grid=(nt + 1,),
        in_specs=[pl.BlockSpec((None, NQ, tm), lambda i: (cur(i) // tpb, 0, cur(i) % tpb)),
                  pl.BlockSpec((tm, GV), lambda i: (cur(i), 0)),
                  pl.BlockSpec((tm, D), lambda i: (cur(i), 0)),
                  pl.BlockSpec((8, 6 * D), const),
                  pl.BlockSpec((1, D), const),
                  pl.BlockSpec((1, D), const),
                  pl.BlockSpec((1, D), const),
                  resident((NQ, D)),
                  resident((GV, D)),
                  resident((D, 2 * FFN_HIDDEN)),
                  resident((FFN_HIDDEN, D))],
        out_specs=pl.BlockSpec((tm, D), lambda i: (done(i), 0)),
        scratch_shapes=[pltpu.VMEM((2, tm, D), F32),
                        pltpu.VMEM((2, tm, D), BF16),
                        pltpu.VMEM((tm, D), F32),
                        pltpu.VMEM((tm, D), F32)],
        compiler_params=pltpu.CompilerParams(dimension_semantics=("arbitrary",),
                                             vmem_limit_bytes=VMEM_LIMIT),
        name="mix_ffn",
    )(att_t, gla, x2d, ada, g_post_mix, g_pre_ffn, g_post_ffn,
      w_out_a, w_out_g, w_ffn_in, w_ffn_out)


def _rope_tables(seq):
    half = HEAD_DIM // 2
    inv_freq = ROPE_BASE ** (-np.arange(0, half, 2, dtype=np.float64) / half)
    dim = np.arange(HEAD_DIM)
    freq = np.tile(inv_freq, HEAD_DIM // ROPE_BLK)[:, None]
    pos = np.arange(seq)[None, :]
    ang = np.where((dim < half)[:, None], pos // GRID_W, pos % GRID_W) * freq
    cos, sin = np.cos(ang), np.sin(ang)
    even = ((dim // ROPE_BLK) % 2 == 0)[:, None]
    sa = np.where(even, -sin, 0.0)
    sb = np.where(even, 0.0, sin)
    return tuple(jnp.asarray(t, F32) for t in (cos, sa, sb))


def kernel(x, c, ctx, c_ctx, w_ada, b_ada, g_pre_mix, g_post_mix, g_pre_ffn, g_post_ffn,
           w_in, attn_sink, w_gate_fwd, b_gate_fwd, w_gate_bwd, b_gate_bwd, g_gla_norm,
           w_out, w_ffn_in, w_ffn_out):
    batch, seq, _ = x.shape
    n_ctx = ctx.shape[1]
    depth = w_ada.shape[0]
    assert depth == 1
    l = 0

    c8 = jnp.zeros((8, D), F32).at[0:batch].set(c).at[4].set(c_ctx)
    ada = _ada(c8, w_ada[l], b_ada[l][None, :])

    wi = w_in[l]
    g0 = NQ + 2 * NKV
    w_qkv_t = wi[:, 0:g0].T.astype(BF16)
    gk0 = g0 + GQ
    gg0 = gk0 + KV_W
    w_rest = jnp.concatenate(
        [wi[:, gk0:gg0], wi[:, g0:gk0], wi[:, gg0:],
         jnp.zeros((D, Z_W - 2 * GATE_RANK), F32)], axis=1).astype(BF16)
    w_gate_p = jnp.zeros((Z_W, LA_W), F32)
    w_gate_p = w_gate_p.at[0:GATE_RANK, 0:GQ].set(w_gate_fwd[l])
    w_gate_p = w_gate_p.at[GATE_RANK:2 * GATE_RANK, GQ:].set(w_gate_bwd[l]).astype(BF16)
    b_gate_p = jnp.concatenate([b_gate_fwd[l], b_gate_bwd[l]])[None, :]

    tables = _rope_tables(seq)
    x2d = x.reshape(batch * seq, D)
    g_pre = g_pre_mix[l][None, :]
    gn_tiled = jnp.tile(g_gla_norm[l], GLA_HEADS)[None, :]
    qt, k, vt, gin, lah, lal = _inproj(x2d, ada, g_pre, w_qkv_t, w_rest, w_gate_p, b_gate_p,
                                      gn_tiled, tables, batch=batch, tokens=seq, tm=INPROJ_TM)
    _, k_c, vt_c, gin_c, lah_c, lal_c = _inproj(
        ctx.reshape(batch * n_ctx, D), ada, g_pre, w_qkv_t, w_rest, w_gate_p, b_gate_p,
        gn_tiled, None, batch=batch, tokens=n_ctx, tm=n_ctx)

    att_t = _attn(attn_sink[l], qt, k, vt, k_c, vt_c, batch=batch, seq=seq, n_ctx=n_ctx)

    gla = _gla(gin, lah, lal, gin_c, lah_c, lal_c, batch=batch, seq=seq, n_ctx=n_ctx)

    wo = w_out[l]
    out = _mix_ffn(att_t, gla.reshape(batch * seq, GV), x2d, ada,
                   g_post_mix[l][None, :], g_pre_ffn[l][None, :], g_post_ffn[l][None, :],
                   wo[:NQ].astype(BF16), wo[NQ:].astype(BF16),
                   w_ffn_in[l].astype(BF16), w_ffn_out[l].astype(BF16),
                   tm=MIX_TM, tiles_per_batch=seq // MIX_TM)
    return out.reshape(batch, seq, D)
```

```python
import functools

import jax
import jax.numpy as jnp
import numpy as np
from jax import lax
from jax.experimental import pallas as pl
from jax.experimental.pallas import tpu as pltpu

D = 1024
HEAD_DIM = 64
ATT_HEADS = 8
ATT_KV_HEADS = 2
GROUP = ATT_HEADS // ATT_KV_HEADS
WINDOW = 128
GRID_W = 64
ROPE_BASE = 10000.0
ROPE_BLK = HEAD_DIM // 4
GLA_HEADS = 8
GLA_DK = 32
GLA_DV = 64
CHUNK = 64
GATE_RANK = 16
GATE_TAU = 16.0
FFN_HIDDEN = 2816
NEG_INF = -1e30
EPS = 1e-6
LOG2E = 1.4426950408889634
Q_SCALE = LOG2E * HEAD_DIM ** -0.5
LA_SCALE = LOG2E / GATE_TAU

NQ = ATT_HEADS * HEAD_DIM
NKV = ATT_KV_HEADS * HEAD_DIM
GQ = GLA_HEADS * GLA_DK
GV = GLA_HEADS * GLA_DV
GIN_W = 2 * GQ + 2 * GV
KV_W = GQ + GV
K_COLS = slice(0, GQ)
V_COLS = slice(GQ, KV_W)
Q_COLS = slice(KV_W, KV_W + GQ)
G_COLS = slice(KV_W + GQ, GIN_W)
Z_W = 128
LA_W = 2 * GQ
REST_W = GIN_W + Z_W

LANES = 128
VMEM_LIMIT = 56 * 1024 * 1024

BF16 = jnp.bfloat16
F32 = jnp.float32


def _dot(a, b):
    return jnp.dot(a, b, preferred_element_type=F32)


def _dot_nt(a, b):
    return lax.dot_general(a, b, (((1,), (1,)), ((), ())), preferred_element_type=F32)


def _dot_tn(a, b):
    return lax.dot_general(a, b, (((0,), (0,)), ((), ())), preferred_element_type=F32)


def _rms(x):
    return x * lax.rsqrt(jnp.mean(x * x, axis=-1, keepdims=True) + EPS)


def _silu(x):
    return x * (1.0 / (1.0 + jnp.exp(-x)))


def _ada_kernel(c_ref, w_ref, b_ref, o_ref):
    a = _silu(c_ref[...]).astype(BF16)
    o_ref[...] = _dot(a, w_ref[...].astype(BF16)) + b_ref[...]


def _ada(c8, w_ada, b_ada):
    n = w_ada.shape[1]
    bn = 1024
    return pl.pallas_call(
        _ada_kernel,
        out_shape=jax.ShapeDtypeStruct((8, n), F32),
        grid=(n // bn,),
        in_specs=[pl.BlockSpec((8, D), lambda j: (0, 0)),
                  pl.BlockSpec((D, bn), lambda j: (0, j)),
                  pl.BlockSpec((1, bn), lambda j: (0, j))],
        out_specs=pl.BlockSpec((8, bn), lambda j: (0, j)),
        compiler_params=pltpu.CompilerParams(dimension_semantics=("arbitrary",)),
        name="ada",
    )(c8, w_ada, b_ada)


NORM_ROWS = 128
INPROJ_TM = 1024


def _inproj_kernel(*refs, tiles_per_batch, rope):
    if rope:
        (x_ref, ada_ref, g_ref, wqkv_ref, w_ref, wg_ref, bg_ref, gn_ref,
         cost_ref, sat_ref, sbt_ref,
         qt_ref, k_ref, vt_ref, gin_ref, lah_ref, lal_ref) = refs
        row = pl.program_id(0) // tiles_per_batch
    else:
        (x_ref, ada_ref, g_ref, wqkv_ref, w_ref, wg_ref, bg_ref, gn_ref,
         qt_ref, k_ref, vt_ref, gin_ref, lah_ref, lal_ref) = refs
        row = 4
    ada = ada_ref[pl.ds(row, 1), :]
    sh1 = ada[:, 0:D]
    sc1 = ada[:, D:2 * D]
    tm = x_ref.shape[0]
    hb = jnp.concatenate(
        [((_rms(x_ref[r0:r0 + NORM_ROWS, :]) * g_ref[...]) * (1.0 + sc1) + sh1).astype(BF16)
         for r0 in range(0, tm, NORM_ROWS)], axis=0)

    z = _dot(hb, w_ref[:, GIN_W:REST_W])
    qkv = _dot_nt(wqkv_ref[...], hb)
    logits = _dot(z.astype(BF16), wg_ref[...]) + bg_ref[...]
    gate = _dot(hb, w_ref[:, G_COLS])
    gin = _dot(hb, w_ref[:, 0:KV_W + GQ])

    def head_t(hd):
        xh = qkv[hd * HEAD_DIM:(hd + 1) * HEAD_DIM]
        if not rope:
            return xh
        up = jnp.concatenate([xh[ROPE_BLK:], xh[:ROPE_BLK]], axis=0)
        dn = jnp.concatenate([xh[HEAD_DIM - ROPE_BLK:], xh[:HEAD_DIM - ROPE_BLK]], axis=0)
        return xh * cost_ref[...] + up * sat_ref[...] + dn * sbt_ref[...]

    for hd in range(ATT_HEADS):
        qh = head_t(hd)
        if rope:
            qh = qh * Q_SCALE
        qt_ref[hd * HEAD_DIM:(hd + 1) * HEAD_DIM, :] = qh.astype(BF16)
    kt = jnp.concatenate([head_t(ATT_HEADS + hd) for hd in range(ATT_KV_HEADS)], axis=0)
    k_ref[...] = kt.T.astype(BF16)
    vt_ref[...] = qkv[NQ + NKV:NQ + 2 * NKV].astype(BF16)

    log_sig = jnp.minimum(logits, 0.0) - jnp.log(1.0 + jnp.exp(-jnp.abs(logits)))
    la = log_sig * LA_SCALE
    hi = la.astype(BF16)
    lah_ref[...] = hi
    lal_ref[...] = (la - hi.astype(F32)).astype(BF16)

    gin_ref[:, G_COLS] = (_silu(gate) * gn_ref[...]).astype(BF16)
    gin_ref[:, 0:KV_W] = gin[:, 0:KV_W].astype(BF16)
    gin_ref[:, Q_COLS] = (gin[:, Q_COLS] * (GLA_DK ** -0.5)).astype(BF16)


def _inproj(x2d, ada, g_pre, w_qkv_t, w_rest, w_gate_p, b_gate_p, gn_tiled, tables, *,
            batch, tokens, tm):
    n = x2d.shape[0]
    tpb = tokens // tm
    rope = tables is not None
    const = lambda i: (0, 0)
    in_specs = [pl.BlockSpec((tm, D), lambda i: (i, 0)),
                pl.BlockSpec((8, 6 * D), const),
                pl.BlockSpec((1, D), const),
                pl.BlockSpec((NQ + 2 * NKV, D), const),
                pl.BlockSpec((D, REST_W), const),
                pl.BlockSpec((Z_W, LA_W), const),
                pl.BlockSpec((1, LA_W), const),
                pl.BlockSpec((1, GV), const)]
    args = [x2d, ada, g_pre, w_qkv_t, w_rest, w_gate_p, b_gate_p, gn_tiled]
    if rope:
        in_specs += [pl.BlockSpec((HEAD_DIM, tm), lambda i: (0, i % tpb))] * 3
        args += list(tables)
    row_blk = lambda w: pl.BlockSpec((tm, w), lambda i: (i, 0))
    col_blk = lambda w: pl.BlockSpec((None, w, tm), lambda i: (i // tpb, 0, i % tpb))
    return pl.pallas_call(
        functools.partial(_inproj_kernel, tiles_per_batch=tpb, rope=rope),
        out_shape=(jax.ShapeDtypeStruct((batch, NQ, tokens), BF16),
                   jax.ShapeDtypeStruct((n, NKV), BF16),
                   jax.ShapeDtypeStruct((batch, NKV, tokens), BF16),
                   jax.ShapeDtypeStruct((n, GIN_W), BF16),
                   jax.ShapeDtypeStruct((n, LA_W), BF16),
                   jax.ShapeDtypeStruct((n, LA_W), BF16)),
        grid=(n // tm,),
        in_specs=in_specs,
        out_specs=(col_blk(NQ), row_blk(NKV), col_blk(NKV),
                   row_blk(GIN_W), row_blk(LA_W), row_blk(LA_W)),
        compiler_params=pltpu.CompilerParams(dimension_semantics=("arbitrary",),
                                             vmem_limit_bytes=VMEM_LIMIT),
        name="inproj_lat" if rope else "inproj_ctx",
    )(*args)


ATT_SUB = 16
ATT_Q = 64
ONES_ROWS = 16


def _attn_kernel(sink_ref, qt_ref, kp_ref, kc_ref, kn_ref, kx_ref,
                 vp_ref, vc_ref, vn_ref, vx_ref, o_ref, *, n_steps):
    i = pl.program_id(1)
    bq = WINDOW
    nwin = ATT_Q + 2 * WINDOW
    nq = GROUP * ATT_Q
    n_sub = ATT_SUB * bq // ATT_Q
    c = lax.broadcasted_iota(jnp.int32, (ATT_Q, nq), 0)
    r = lax.broadcasted_iota(jnp.int32, (ATT_Q, nq), 1) % ATT_Q
    behind_ok = c >= r
    ahead_ok = c <= r
    lane = lax.broadcasted_iota(jnp.int32, (1, nq), 1)
    zeros_q = jnp.zeros((HEAD_DIM, nq), BF16)
    zeros_e = jnp.zeros((ATT_Q, nq), BF16)

    kc_all = jnp.concatenate([kp_ref[...], kc_ref[...], kn_ref[...]], axis=0)
    vc_all = jnp.concatenate([vp_ref[...], vc_ref[...], vn_ref[...]], axis=1)
    kx = kx_ref[...]
    vx = vx_ref[...]

    vaug = []
    for a in range(ATT_SUB):
        vwin = jnp.concatenate([vc_all[:, a * bq:(a + 3) * bq], vx], axis=1)
        ones = jnp.ones((ONES_ROWS, vwin.shape[1]), BF16)
        vaug.append([jnp.concatenate([vwin[kvh * HEAD_DIM:(kvh + 1) * HEAD_DIM], ones], axis=0)
                     for kvh in range(ATT_KV_HEADS)])

    tiles = [(s, kvh) for s in range(n_sub) for kvh in range(ATT_KV_HEADS)]

    def score(tile):
        s, kvh = tile
        q0 = s * ATT_Q
        h0 = kvh * GROUP
        qt = jnp.concatenate(
            [qt_ref[(h0 + u) * HEAD_DIM:(h0 + u + 1) * HEAD_DIM, q0:q0 + ATT_Q]
             for u in range(GROUP)], axis=1)
        rhs = jnp.concatenate([qt, zeros_q] if kvh == 0 else [zeros_q, qt], axis=0)
        kwin = jnp.concatenate([kc_all[q0:q0 + nwin], kx], axis=0)
        return _dot(kwin, rhs)

    def window_blocks(s, st):
        blocks = []
        for b in range(nwin // ATT_Q):
            blk = st[b * ATT_Q:(b + 1) * ATT_Q]
            if b == 0:
                blk = jnp.where(behind_ok, blk, NEG_INF)
            if b == nwin // ATT_Q - 1:
                blk = jnp.where(ahead_ok, blk, NEG_INF)
            first_key = s * ATT_Q - WINDOW + b * ATT_Q
            if first_key < 0:
                blk = jnp.where(i > 0, blk, NEG_INF)
            if first_key >= ATT_SUB * bq:
                blk = jnp.where(i < n_steps - 1, blk, NEG_INF)
            blocks.append(blk)
        return blocks

    def softmax_numerators(tile, st):
        s, kvh = tile
        h0 = kvh * GROUP
        parts = window_blocks(s, st) + [st[nwin:]]
        sink = sink_ref[h0 + GROUP - 1]
        for u in range(GROUP - 2, -1, -1):
            sink = jnp.where(lane < (u + 1) * ATT_Q, sink_ref[h0 + u], sink)
        sink = sink * LOG2E
        m = sink
        for p in parts:
            m = jnp.maximum(m, jnp.max(p, axis=0, keepdims=True))
        e = [jnp.exp2(p - m).astype(BF16) for p in parts]
        pad = [zeros_e]
        et = jnp.concatenate((e[:-1] + pad if s % 2 == 0 else pad + e[:-1]) + e[-1:], axis=0)
        return et, jnp.exp2(sink - m)

    def values(tile, et, e_sink):
        s, kvh = tile
        q0 = s * ATT_Q
        ot = _dot(vaug[s // 2][kvh], et)
        denom = ot[HEAD_DIM:HEAD_DIM + 1] + e_sink
        on = ot[0:HEAD_DIM] * (1.0 / denom)
        for u in range(GROUP):
            hd = kvh * GROUP + u
            o_ref[hd * HEAD_DIM:(hd + 1) * HEAD_DIM, q0:q0 + ATT_Q] = (
                on[:, u * ATT_Q:(u + 1) * ATT_Q].astype(BF16))

    scores = [score(t) for t in tiles]
    probs = [softmax_numerators(t, st) for t, st in zip(tiles, scores)]
    for t, (et, e_sink) in zip(tiles, probs):
        values(t, et, e_sink)


def _attn(sink, qt, k, vt, k_ctx, vt_ctx, *, batch, seq, n_ctx):
    bq = WINDOW
    bs = ATT_SUB * bq
    nb = seq // bq
    ns = seq // bs
    k3 = k.reshape(batch, seq, NKV)
    kx3 = k_ctx.reshape(batch, n_ctx, NKV)
    prev = lambda i: jnp.maximum(ATT_SUB * i - 1, 0)
    nxt = lambda i: jnp.minimum(ATT_SUB * (i + 1), nb - 1)
    return pl.pallas_call(
        functools.partial(_attn_kernel, n_steps=ns),
        out_shape=jax.ShapeDtypeStruct((batch, NQ, seq), BF16),
        grid=(batch, ns),
        in_specs=[
            pl.BlockSpec(memory_space=pltpu.SMEM),
            pl.BlockSpec((None, NQ, bs), lambda b, i: (b, 0, i)),
            pl.BlockSpec((None, bq, NKV), lambda b, i: (b, prev(i), 0)),
            pl.BlockSpec((None, bs, NKV), lambda b, i: (b, i, 0)),
            pl.BlockSpec((None, bq, NKV), lambda b, i: (b, nxt(i), 0)),
            pl.BlockSpec((None, n_ctx, NKV), lambda b, i: (b, 0, 0)),
            pl.BlockSpec((None, NKV, bq), lambda b, i: (b, 0, prev(i))),
            pl.BlockSpec((None, NKV, bs), lambda b, i: (b, 0, i)),
            pl.BlockSpec((None, NKV, bq), lambda b, i: (b, 0, nxt(i))),
            pl.BlockSpec((None, NKV, n_ctx), lambda b, i: (0, 0, b)),
        ],
        out_specs=pl.BlockSpec((None, NQ, bs), lambda b, i: (b, 0, i)),
        compiler_params=pltpu.CompilerParams(dimension_semantics=("arbitrary", "arbitrary"),
                                             vmem_limit_bytes=VMEM_LIMIT),
        name="attn",
    )(sink, qt, k3, k3, k3, kx3, vt, vt, vt, vt_ctx)


GRP_K = 4 * GLA_DK
GRP_V = 4 * GLA_DV
N_GRP = GLA_HEADS // 4
GLA_TB = 1024
GLA_STATE_TB = 2048
CUM_BLK = 256


def _gla_constants():
    t = CUM_BLK
    r = np.arange(t)[:, None]
    c = np.arange(t)[None, :]
    same_chunk = (r // CHUNK) == (c // CHUNK)
    tri_l = (same_chunk & (c <= r)).astype(np.float32)
    tri_u = (same_chunk & (c >= r)).astype(np.float32)
    rk = np.arange(4 * CHUNK)[:, None] // CHUNK
    kmask = (rk == (np.arange(GRP_K)[None, :] // GLA_DK)).astype(np.float32)
    vmask = (rk == (np.arange(GRP_V)[None, :] // GLA_DV)).astype(np.float32)
    smask = ((np.arange(GRP_K)[:, None] // GLA_DK)
             == (np.arange(GRP_V)[None, :] // GLA_DV)).astype(np.float32)
    ai = np.arange(CHUNK)[:, None]
    aj = np.arange(4 * CHUNK)[None, :] % CHUNK
    causal = (aj <= ai).astype(np.float32)
    anti = (aj >= ai).astype(np.float32)
    hmean = ((np.arange(GRP_V)[:, None] // GLA_DV)
             == (np.arange(GRP_V)[None, :] // GLA_DV)).astype(np.float32) / GLA_DV
    return (jnp.asarray(tri_l, BF16), jnp.asarray(tri_u, BF16), jnp.asarray(kmask, BF16),
            jnp.asarray(vmask, BF16), jnp.asarray(smask, BF16), jnp.asarray(causal, BF16),
            jnp.asarray(anti, BF16), jnp.asarray(hmean, BF16))


def _gk(g):
    return slice(g * GRP_K, (g + 1) * GRP_K)


def _gv(g):
    return slice(g * GRP_V, (g + 1) * GRP_V)


def _chunk_slices(t):
    return [slice(ci * CHUNK, (ci + 1) * CHUNK) for ci in range(t // CHUNK)]


def _cum_and_totals(la_hi, la_lo, tri, *, reverse):
    parts = [slice(p, p + CUM_BLK) for p in range(0, la_hi.shape[0], CUM_BLK)]
    cum = jnp.concatenate([_dot(tri, la_hi[p]) + _dot(tri, la_lo[p]) for p in parts], axis=0)
    edge = 0 if reverse else CHUNK - 1
    tots = [cum[sl.start + edge:sl.start + edge + 1] for sl in _chunk_slices(cum.shape[0])]
    return cum, tots


def _bcast_chunks(rows, width):
    return jnp.concatenate([jnp.broadcast_to(r, (CHUNK, width)) for r in rows], axis=0)


def _decay(x, log2_factor):
    return x * jnp.exp2(log2_factor).astype(BF16)


def _state_increments(kd, v):
    return [[_dot_tn(kd[sl, _gk(g)], v[sl, _gv(g)]) for g in range(N_GRP)]
            for sl in _chunk_slices(kd.shape[0])]


def _masked_state(state, smask):
    return state.astype(BF16) * smask


def _decay_columns(tots):
    width = tots[0].shape[1]
    row = lax.broadcasted_iota(jnp.int32, (8, width), 0)
    rows = jnp.zeros((8, width), F32)
    for ci, t in enumerate(tots):
        rows = jnp.where(row == ci, jnp.broadcast_to(t, (8, width)), rows)
    return jnp.exp2(rows).T


def _interleave(stage_gens):
    gens = list(stage_gens)
    done = [False] * len(gens)
    rnd = 0
    while not all(done):
        for gi, gen in enumerate(gens):
            if rnd >= gi and not done[gi]:
                try:
                    next(gen)
                except StopIteration:
                    done[gi] = True
        rnd += 1


def _state_stages(st_ref, k, v, la_hi, la_lo, tri, smask, *, reverse,
                  store_ref=None, store_base=None):
    nch = k.shape[0] // CHUNK
    cum, tots = _cum_and_totals(la_hi, la_lo, tri, reverse=reverse)
    yield
    kd = _decay(k, _bcast_chunks(tots, GQ) - cum)
    dcol = _decay_columns(tots)
    yield
    ds = _state_increments(kd, v)
    yield
    state = [st_ref[g] for g in range(N_GRP)]
    for ci in (range(nch - 1, -1, -1) if reverse else range(nch)):
        for g in range(N_GRP):
            if store_ref is not None:
                store_ref[store_base + ci, g] = _masked_state(state[g], smask)
            state[g] = state[g] * dcol[_gk(g), ci:ci + 1] + ds[ci][g]
    for g in range(N_GRP):
        st_ref[g] = state[g]


def _output_stages(q, k, v, la_f, la_b, gate, st_ref, bwd_state, consts, finish):
    tri_l, tri_u, kmask, vmask, smask, causal, anti, hmean = consts
    chunks = _chunk_slices(q.shape[0])

    cum_f, tots_f = _cum_and_totals(la_f[0], la_f[1], tri_l, reverse=False)
    cum_b, tots_b = _cum_and_totals(la_b[0], la_b[1], tri_u, reverse=True)
    yield

    qd_f = _decay(q, cum_f)
    qd_b = _decay(q, cum_b)
    ki_f = _decay(k, -cum_f)
    ki_b = _decay(k, -cum_b)
    kd_f = _decay(k, _bcast_chunks(tots_f, GQ) - cum_f)
    dcol_f = _decay_columns(tots_f)
    v_bd = [[jnp.concatenate([v[sl, _gv(g)]] * 4, axis=0) * vmask for g in range(N_GRP)]
            for sl in chunks]
    yield

    def scores(qd, ki, sl, g):
        ki_bd = jnp.concatenate([ki[sl, _gk(g)]] * 4, axis=0) * kmask
        return _dot_nt(qd[sl, _gk(g)], ki_bd)

    a_f = [[scores(qd_f, ki_f, sl, g) for g in range(N_GRP)] for sl in chunks]
    a_b = [[scores(qd_b, ki_b, sl, g) for g in range(N_GRP)] for sl in chunks]
    ds = _state_increments(kd_f, v)
    yield

    a = [[af[g].astype(BF16) * causal + ab[g].astype(BF16) * anti for g in range(N_GRP)]
         for af, ab in zip(a_f, a_b)]
    state = [st_ref[g] for g in range(N_GRP)]
    st_in = []
    for ci in range(len(chunks)):
        st_in.append([_masked_state(s, smask) for s in state])
        state = [state[g] * dcol_f[_gk(g), ci:ci + 1] + ds[ci][g] for g in range(N_GRP)]
    for g in range(N_GRP):
        st_ref[g] = state[g]
    yield

    outs = []
    for ci, sl in enumerate(chunks):
        parts = []
        for g in range(N_GRP):
            intra = _dot(a[ci][g], v_bd[ci][g])
            qd2 = jnp.concatenate([qd_f[sl, _gk(g)], qd_b[sl, _gk(g)]], axis=1)
            st2 = jnp.concatenate([st_in[ci][g], bwd_state(ci, g)], axis=0)
            parts.append(intra + _dot(qd2, st2))
        outs.append(jnp.concatenate(parts, axis=1))
    o = jnp.concatenate(outs, axis=0)
    yield

    o2 = (o * o).astype(BF16)
    ms = jnp.concatenate([_dot(o2[:, _gv(g)], hmean) for g in range(N_GRP)], axis=1)
    yield
    finish((o * lax.rsqrt(ms + EPS) * gate.astype(F32)).astype(BF16))


def _gla_kernel(skv_ref, slah_ref, slal_ref,
                kv_ref, qg_ref, lahf_ref, lahb_ref, lalf_ref, lalb_ref,
                ginc_ref, lahc_ref, lalc_ref,
                tril_ref, triu_ref, kmask_ref, vmask_ref, smask_ref, causal_ref, anti_ref,
                hmean_ref, o_ref, st_ref, sb_ref, *, n_state_steps):
    step = pl.program_id(1)
    ch_per_half = CUM_BLK // CHUNK
    fwd = slice(0, GQ)
    bwd = slice(GQ, 2 * GQ)

    def ctx_state(cols, tri_ref, reverse):
        st_ref[...] = jnp.zeros_like(st_ref)
        ctx_blocks = [slice(r0, r0 + CUM_BLK) for r0 in range(0, ginc_ref.shape[0], CUM_BLK)]
        for rs in (ctx_blocks[::-1] if reverse else ctx_blocks):
            for _ in _state_stages(st_ref, ginc_ref[rs, K_COLS], ginc_ref[rs, V_COLS],
                                   lahc_ref[rs, cols], lalc_ref[rs, cols], tri_ref[...],
                                   smask_ref[...], reverse=reverse):
                pass

    @pl.when(step < n_state_steps)
    def _backward_states():
        @pl.when(step == 0)
        def _():
            ctx_state(bwd, triu_ref, True)

        blk = n_state_steps - 1 - step
        parts = [slice(r0, r0 + CUM_BLK) for r0 in range(0, GLA_STATE_TB, CUM_BLK)]
        _interleave([
            _state_stages(st_ref, skv_ref[rs, K_COLS], skv_ref[rs, V_COLS],
                          slah_ref[rs, :], slal_ref[rs, :], triu_ref[...], smask_ref[...],
                          reverse=True, store_ref=sb_ref,
                          store_base=blk * (GLA_STATE_TB // CHUNK) + pi * ch_per_half)
            for pi, rs in reversed(list(enumerate(parts)))])

    @pl.when(step >= n_state_steps)
    def _outputs():
        j = step - n_state_steps

        @pl.when(j == 0)
        def _():
            ctx_state(fwd, tril_ref, False)

        consts = (tril_ref[...], triu_ref[...], kmask_ref[...], vmask_ref[...],
                  smask_ref[...], causal_ref[...], anti_ref[...], hmean_ref[...])
        halves = [slice(r0, r0 + CUM_BLK) for r0 in range(0, GLA_TB, CUM_BLK)]

        def half(hi, rs):
            base = j * (GLA_TB // CHUNK) + hi * ch_per_half

            def finish(val):
                o_ref[rs, :] = val

            return _output_stages(
                qg_ref[rs, 0:GQ], kv_ref[rs, K_COLS], kv_ref[rs, V_COLS],
                (lahf_ref[rs, :], lalf_ref[rs, :]), (lahb_ref[rs, :], lalb_ref[rs, :]),
                qg_ref[rs, GQ:KV_W], st_ref, lambda ci, g: sb_ref[base + ci, g], consts, finish)

        _interleave([half(hi, rs) for hi, rs in enumerate(halves)])


def _gla(gin, la_hi, la_lo, gin_ctx, la_hi_ctx, la_lo_ctx, *, batch, seq, n_ctx):
    assert n_ctx % CUM_BLK == 0 and CUM_BLK // CHUNK <= 8
    tb, stb = GLA_TB, GLA_STATE_TB
    ns, nb = seq // stb, seq // tb
    nch = seq // CHUNK
    r3 = lambda a, t: a.reshape(batch, t, a.shape[-1])
    consts = _gla_constants()

    def state_blk(col):
        return lambda b, s: (b, jnp.maximum(ns - 1 - s, 0), col)

    def out_blk(col):
        return lambda b, s: (b, jnp.maximum(s - ns, 0), col)

    ctx_map = lambda b, s: (b, 0, 0)
    const_map = lambda b, s: (0, 0)
    gin3, lah3, lal3 = r3(gin, seq), r3(la_hi, seq), r3(la_lo, seq)
    return pl.pallas_call(
        functools.partial(_gla_kernel, n_state_steps=ns),
        out_shape=jax.ShapeDtypeStruct((batch, seq, GV), BF16),
        grid=(batch, ns + nb),
        in_specs=[pl.BlockSpec((None, stb, KV_W), state_blk(0)),
                  pl.BlockSpec((None, stb, GQ), state_blk(1)),
                  pl.BlockSpec((None, stb, GQ), state_blk(1)),
                  pl.BlockSpec((None, tb, KV_W), out_blk(0)),
                  pl.BlockSpec((None, tb, KV_W), out_blk(1)),
                  pl.BlockSpec((None, tb, GQ), out_blk(0)),
                  pl.BlockSpec((None, tb, GQ), out_blk(1)),
                  pl.BlockSpec((None, tb, GQ), out_blk(0)),
                  pl.BlockSpec((None, tb, GQ), out_blk(1)),
                  pl.BlockSpec((None, n_ctx, GIN_W), ctx_map),
                  pl.BlockSpec((None, n_ctx, LA_W), ctx_map),
                  pl.BlockSpec((None, n_ctx, LA_W), ctx_map)]
                 + [pl.BlockSpec(cst.shape, const_map) for cst in consts],
        out_specs=pl.BlockSpec((None, tb, GV), out_blk(0)),
        scratch_shapes=[pltpu.VMEM((N_GRP, GRP_K, GRP_V), F32),
                        pltpu.VMEM((nch, N_GRP, GRP_K, GRP_V), BF16)],
        compiler_params=pltpu.CompilerParams(
            dimension_semantics=("arbitrary", "arbitrary"),
            vmem_limit_bytes=VMEM_LIMIT),
        name="gla",
    )(gin3, lah3, lal3, gin3, gin3, lah3, lah3, lal3, lal3, r3(gin_ctx, n_ctx),
      r3(la_hi_ctx, n_ctx), r3(la_lo_ctx, n_ctx), *consts)


FFN_CHUNK = 256
MIX_SLICES = 16
MIX_TM = 512


def _mix_ffn_kernel(att_ref, gla_ref, x_ref, ada_ref, gpm_ref, gpf_ref, gqf_ref,
                    woa_ref, wog_ref, wfi_ref, wfo_ref, o_ref, x1_ref, h_ref, y_ref, acc_ref, *,
                    tiles_per_batch, n_tiles):
    i = pl.program_id(0)
    tm = x_ref.shape[0]
    chunks = [(c0, min(c0 + FFN_CHUNK, FFN_HIDDEN)) for c0 in range(0, FFN_HIDDEN, FFN_CHUNK)]
    n_chunks = len(chunks)
    mix_rows = tm // MIX_SLICES

    def mix_matmul():
        y_ref[...] = _dot_tn(att_ref[...], woa_ref[...]) + _dot(gla_ref[...], wog_ref[...])

    def mix_rows_slice(slot, si, anchor=None):
        rs = slice(si * mix_rows, (si + 1) * mix_rows)
        row = jnp.minimum(i, n_tiles - 1) // tiles_per_batch
        ada = ada_ref[pl.ds(row, 1), :]
        gt1 = ada[:, 2 * D:3 * D]
        if anchor is not None:
            bits = pltpu.bitcast(anchor, jnp.uint32)
            sixteen = jnp.uint32(16)
            gt1 = gt1 + pltpu.bitcast(
                lax.shift_right_logical(lax.shift_right_logical(bits, sixteen), sixteen), F32)
        sh2 = ada[:, 3 * D:4 * D]
        sc2 = ada[:, 4 * D:5 * D]
        x1 = x_ref[rs, :] + gt1 * (_rms(y_ref[rs, :]) * gpm_ref[...])
        x1_ref[slot, rs, :] = x1
        h_ref[slot, rs, :] = ((_rms(x1) * gpf_ref[...]) * (1.0 + sc2) + sh2).astype(BF16)

    def ffn_chunk(slot, ci):
        c0, c1 = chunks[ci]
        h = h_ref[slot]
        g = _dot(h, wfi_ref[:, c0:c1])
        u = _dot(h, wfi_ref[:, FFN_HIDDEN + c0:FFN_HIDDEN + c1])
        a = (_silu(g) * u).astype(BF16)
        part = _dot(a, wfo_ref[c0:c1, :])
        if ci == 0:
            acc_ref[...] = part
        else:
            acc_ref[...] += part
        return part[0:8, :]

    def ffn_finish(slot):
        row = jnp.maximum(i - 1, 0) // tiles_per_batch
        gt2 = ada_ref[pl.ds(row, 1), 5 * D:6 * D]
        o_ref[...] = x1_ref[slot] + gt2 * (_rms(acc_ref[...]) * gqf_ref[...])

    @pl.when(i == 0)
    def _():
        mix_matmul()
        for si in range(MIX_SLICES):
            mix_rows_slice(0, si)

    for parity in range(2):
        @pl.when((i % 2 == parity) & (i > 0))
        def _():
            per_gap = -(-MIX_SLICES // (n_chunks - 1))
            for ci in range(n_chunks):
                anchor = ffn_chunk(1 - parity, ci)
                if ci == 0:
                    mix_matmul()
                else:
                    for si in range((ci - 1) * per_gap, min(ci * per_gap, MIX_SLICES)):
                        mix_rows_slice(parity, si, anchor[0:1, :])
            ffn_finish(1 - parity)


def _mix_ffn(att_t, gla, x2d, ada, g_post_mix, g_pre_ffn, g_post_ffn,
             w_out_a, w_out_g, w_ffn_in, w_ffn_out, *, tm, tiles_per_batch):
    n = x2d.shape[0]
    tpb = tiles_per_batch
    nt = n // tm
    const = lambda i: (0, 0)
    resident = functools.partial(pl.BlockSpec, index_map=const, pipeline_mode=pl.Buffered(1))
    cur = lambda i: jnp.minimum(i, nt - 1)
    done = lambda i: jnp.maximum(i - 1, 0)
    return pl.pallas_call(
        functools.partial(_mix_ffn_kernel, tiles_per_batch=tpb, n_tiles=nt),
        out_shape=jax.ShapeDtypeStruct((n, D), F32),
        grid=(nt + 1,),
        in_specs=[pl.BlockSpec((None, NQ, tm), lambda i: (cur(i) // tpb, 0, cur(i) % tpb)),
                  pl.BlockSpec((tm, GV), lambda i: (cur(i), 0)),
                  pl.BlockSpec((tm, D), lambda i: (cur(i), 0)),
                  pl.BlockSpec((8, 6 * D), const),
                  pl.BlockSpec((1, D), const),
                  pl.BlockSpec((1, D), const),
                  pl.BlockSpec((1, D), const),
                  resident((NQ, D)),
                  resident((GV, D)),
                  resident((D, 2 * FFN_HIDDEN)),
                  resident((FFN_HIDDEN, D))],
        out_specs=pl.BlockSpec((tm, D), lambda i: (done(i), 0)),
        scratch_shapes=[pltpu.VMEM((2, tm, D), F32),
                        pltpu.VMEM((2, tm, D), BF16),
                        pltpu.VMEM((tm, D), F32),
                        pltpu.VMEM((tm, D), F32)],
        compiler_params=pltpu.CompilerParams(dimension_semantics=("arbitrary",),
                                             vmem_limit_bytes=VMEM_LIMIT),
        name="mix_ffn",
    )(att_t, gla, x2d, ada, g_post_mix, g_pre_ffn, g_post_ffn,
      w_out_a, w_out_g, w_ffn_in, w_ffn_out)


def _rope_tables(seq):
    half = HEAD_DIM // 2
    inv_freq = ROPE_BASE ** (-np.arange(0, half, 2, dtype=np.float64) / half)
    dim = np.arange(HEAD_DIM)
    freq = np.tile(inv_freq, HEAD_DIM // ROPE_BLK)[:, None]
    pos = np.arange(seq)[None, :]
    ang = np.where((dim < half)[:, None], pos // GRID_W, pos % GRID_W) * freq
    cos, sin = np.cos(ang), np.sin(ang)
    even = ((dim // ROPE_BLK) % 2 == 0)[:, None]
    sa = np.where(even, -sin, 0.0)
    sb = np.where(even, 0.0, sin)
    return tuple(jnp.asarray(t, F32) for t in (cos, sa, sb))


def kernel(x, c, ctx, c_ctx, w_ada, b_ada, g_pre_mix, g_post_mix, g_pre_ffn, g_post_ffn,
           w_in, attn_sink, w_gate_fwd, b_gate_fwd, w_gate_bwd, b_gate_bwd, g_gla_norm,
           w_out, w_ffn_in, w_ffn_out):
    batch, seq, _ = x.shape
    n_ctx = ctx.shape[1]
    depth = w_ada.shape[0]
    assert depth == 1
    l = 0

    c8 = jnp.zeros((8, D), F32).at[0:batch].set(c).at[4].set(c_ctx)
    ada = _ada(c8, w_ada[l], b_ada[l][None, :])

    wi = w_in[l]
    g0 = NQ + 2 * NKV
    w_qkv_t = wi[:, 0:g0].T.astype(BF16)
    gk0 = g0 + GQ
    gg0 = gk0 + KV_W
    w_rest = jnp.concatenate(
        [wi[:, gk0:gg0], wi[:, g0:gk0], wi[:, gg0:],
         jnp.zeros((D, Z_W - 2 * GATE_RANK), F32)], axis=1).astype(BF16)
    w_gate_p = jnp.zeros((Z_W, LA_W), F32)
    w_gate_p = w_gate_p.at[0:GATE_RANK, 0:GQ].set(w_gate_fwd[l])
    w_gate_p = w_gate_p.at[GATE_RANK:2 * GATE_RANK, GQ:].set(w_gate_bwd[l]).astype(BF16)
    b_gate_p = jnp.concatenate([b_gate_fwd[l], b_gate_bwd[l]])[None, :]

    tables = _rope_tables(seq)
    x2d = x.reshape(batch * seq, D)
    g_pre = g_pre_mix[l][None, :]
    gn_tiled = jnp.tile(g_gla_norm[l], GLA_HEADS)[None, :]
    qt, k, vt, gin, lah, lal = _inproj(x2d, ada, g_pre, w_qkv_t, w_rest, w_gate_p, b_gate_p,
                                      gn_tiled, tables, batch=batch, tokens=seq, tm=INPROJ_TM)
    _, k_c, vt_c, gin_c, lah_c, lal_c = _inproj(
        ctx.reshape(batch * n_ctx, D), ada, g_pre, w_qkv_t, w_rest, w_gate_p, b_gate_p,
        gn_tiled, None, batch=1, tokens=batch * n_ctx, tm=batch * n_ctx)

    att_t = _attn(attn_sink[l], qt, k, vt, k_c, vt_c, batch=batch, seq=seq, n_ctx=n_ctx)

    gla = _gla(gin, lah, lal, gin_c, lah_c, lal_c, batch=batch, seq=seq, n_ctx=n_ctx)

    wo = w_out[l]
    out = _mix_ffn(att_t, gla.reshape(batch * seq, GV), x2d, ada,
                   g_post_mix[l][None, :], g_pre_ffn[l][None, :], g_post_ffn[l][None, :],
                   wo[:NQ].astype(BF16), wo[NQ:].astype(BF16),
                   w_ffn_in[l].astype(BF16), w_ffn_out[l].astype(BF16),
                   tm=MIX_TM, tiles_per_batch=seq // MIX_TM)
    return out.reshape(batch, seq, D)
```

```python
import functools

import jax
import jax.numpy as jnp
import numpy as np
from jax import lax
from jax.experimental import pallas as pl
from jax.experimental.pallas import tpu as pltpu

D = 1024
HEAD_DIM = 64
ATT_HEADS = 8
ATT_KV_HEADS = 2
GROUP = ATT_HEADS // ATT_KV_HEADS
WINDOW = 128
GRID_W = 64
ROPE_BASE = 10000.0
ROPE_BLK = HEAD_DIM // 4
GLA_HEADS = 8
GLA_DK = 32
GLA_DV = 64
CHUNK = 64
GATE_RANK = 16
GATE_TAU = 16.0
FFN_HIDDEN = 2816
NEG_INF = -1e30
EPS = 1e-6
LOG2E = 1.4426950408889634
Q_SCALE = LOG2E * HEAD_DIM ** -0.5
LA_SCALE = LOG2E / GATE_TAU

NQ = ATT_HEADS * HEAD_DIM
NKV = ATT_KV_HEADS * HEAD_DIM
GQ = GLA_HEADS * GLA_DK
GV = GLA_HEADS * GLA_DV
GIN_W = 2 * GQ + 2 * GV
KV_W = GQ + GV
K_COLS = slice(0, GQ)
V_COLS = slice(GQ, KV_W)
Q_COLS = slice(KV_W, KV_W + GQ)
G_COLS = slice(KV_W + GQ, GIN_W)
Z_W = 128
LA_W = 2 * GQ
REST_W = GIN_W + Z_W

LANES = 128
VMEM_LIMIT = 56 * 1024 * 1024

BF16 = jnp.bfloat16
F32 = jnp.float32


def _dot(a, b):
    return jnp.dot(a, b, preferred_element_type=F32)


def _dot_nt(a, b):
    return lax.dot_general(a, b, (((1,), (1,)), ((), ())), preferred_element_type=F32)


def _dot_tn(a, b):
    return lax.dot_general(a, b, (((0,), (0,)), ((), ())), preferred_element_type=F32)


def _rms(x):
    return x * lax.rsqrt(jnp.mean(x * x, axis=-1, keepdims=True) + EPS)


def _silu(x):
    return x * (1.0 / (1.0 + jnp.exp(-x)))


def _ada_kernel(c_ref, w_ref, b_ref, o_ref):
    a = _silu(c_ref[...]).astype(BF16)
    o_ref[...] = _dot(a, w_ref[...].astype(BF16)) + b_ref[...]


def _ada(c8, w_ada, b_ada):
    n = w_ada.shape[1]
    bn = 1024
    return pl.pallas_call(
        _ada_kernel,
        out_shape=jax.ShapeDtypeStruct((8, n), F32),
        grid=(n // bn,),
        in_specs=[pl.BlockSpec((8, D), lambda j: (0, 0)),
                  pl.BlockSpec((D, bn), lambda j: (0, j)),
                  pl.BlockSpec((1, bn), lambda j: (0, j))],
        out_specs=pl.BlockSpec((8, bn), lambda j: (0, j)),
        compiler_params=pltpu.CompilerParams(dimension_semantics=("arbitrary",)),
        name="ada",
    )(c8, w_ada, b_ada)


NORM_ROWS = 128
INPROJ_TM = 1024


def _inproj_kernel(*refs, tiles_per_batch, rope):
    if rope:
        (x_ref, ada_ref, g_ref, wqkv_ref, w_ref, wg_ref, bg_ref, gn_ref,
         cost_ref, sat_ref, sbt_ref,
         qt_ref, k_ref, vt_ref, gin_ref, lah_ref, lal_ref) = refs
        row = pl.program_id(0) // tiles_per_batch
    else:
        (x_ref, ada_ref, g_ref, wqkv_ref, w_ref, wg_ref, bg_ref, gn_ref,
         qt_ref, k_ref, vt_ref, gin_ref, lah_ref, lal_ref) = refs
        row = 4
    ada = ada_ref[pl.ds(row, 1), :]
    sh1 = ada[:, 0:D]
    sc1 = ada[:, D:2 * D]
    tm = x_ref.shape[0]
    hb = jnp.concatenate(
        [((_rms(x_ref[r0:r0 + NORM_ROWS, :]) * g_ref[...]) * (1.0 + sc1) + sh1).astype(BF16)
         for r0 in range(0, tm, NORM_ROWS)], axis=0)

    z = _dot(hb, w_ref[:, GIN_W:REST_W])
    qkv = _dot_nt(wqkv_ref[...], hb)
    logits = _dot(z.astype(BF16), wg_ref[...]) + bg_ref[...]
    gate = _dot(hb, w_ref[:, G_COLS])
    gin = _dot(hb, w_ref[:, 0:KV_W + GQ])

    def head_t(hd):
        xh = qkv[hd * HEAD_DIM:(hd + 1) * HEAD_DIM]
        if not rope:
            return xh
        up = jnp.concatenate([xh[ROPE_BLK:], xh[:ROPE_BLK]], axis=0)
        dn = jnp.concatenate([xh[HEAD_DIM - ROPE_BLK:], xh[:HEAD_DIM - ROPE_BLK]], axis=0)
        return xh * cost_ref[...] + up * sat_ref[...] + dn * sbt_ref[...]

    for hd in range(ATT_HEADS):
        qh = head_t(hd)
        if rope:
            qh = qh * Q_SCALE
        qt_ref[hd * HEAD_DIM:(hd + 1) * HEAD_DIM, :] = qh.astype(BF16)
    kt = jnp.concatenate([head_t(ATT_HEADS + hd) for hd in range(ATT_KV_HEADS)], axis=0)
    k_ref[...] = kt.T.astype(BF16)
    vt_ref[...] = qkv[NQ + NKV:NQ + 2 * NKV].astype(BF16)

    log_sig = jnp.minimum(logits, 0.0) - jnp.log(1.0 + jnp.exp(-jnp.abs(logits)))
    la = log_sig * LA_SCALE
    hi = la.astype(BF16)
    lah_ref[...] = hi
    lal_ref[...] = (la - hi.astype(F32)).astype(BF16)

    gin_ref[:, G_COLS] = (_silu(gate) * gn_ref[...]).astype(BF16)
    gin_ref[:, 0:KV_W] = gin[:, 0:KV_W].astype(BF16)
    gin_ref[:, Q_COLS] = (gin[:, Q_COLS] * (GLA_DK ** -0.5)).astype(BF16)


def _inproj(x2d, ada, g_pre, w_qkv_t, w_rest, w_gate_p, b_gate_p, gn_tiled, tables, *,
            batch, tokens, tm):
    n = x2d.shape[0]
    tpb = tokens // tm
    rope = tables is not None
    const = lambda i: (0, 0)
    in_specs = [pl.BlockSpec((tm, D), lambda i: (i, 0)),
                pl.BlockSpec((8, 6 * D), const),
                pl.BlockSpec((1, D), const),
                pl.BlockSpec((NQ + 2 * NKV, D), const),
                pl.BlockSpec((D, REST_W), const),
                pl.BlockSpec((Z_W, LA_W), const),
                pl.BlockSpec((1, LA_W), const),
                pl.BlockSpec((1, GV), const)]
    args = [x2d, ada, g_pre, w_qkv_t, w_rest, w_gate_p, b_gate_p, gn_tiled]
    if rope:
        in_specs += [pl.BlockSpec((HEAD_DIM, tm), lambda i: (0, i % tpb))] * 3
        args += list(tables)
    row_blk = lambda w: pl.BlockSpec((tm, w), lambda i: (i, 0))
    col_blk = lambda w: pl.BlockSpec((None, w, tm), lambda i: (i // tpb, 0, i % tpb))
    return pl.pallas_call(
        functools.partial(_inproj_kernel, tiles_per_batch=tpb, rope=rope),
        out_shape=(jax.ShapeDtypeStruct((batch, NQ, tokens), BF16),
                   jax.ShapeDtypeStruct((n, NKV), BF16),
                   jax.ShapeDtypeStruct((batch, NKV, tokens), BF16),
                   jax.ShapeDtypeStruct((n, GIN_W), BF16),
                   jax.ShapeDtypeStruct((n, LA_W), BF16),
                   jax.ShapeDtypeStruct((n, LA_W), BF16)),
        grid=(n // tm,),
        in_specs=in_specs,
        out_specs=(col_blk(NQ), row_blk(NKV), col_blk(NKV),
                   row_blk(GIN_W), row_blk(LA_W), row_blk(LA_W)),
        compiler_params=pltpu.CompilerParams(dimension_semantics=("arbitrary",),
                                             vmem_limit_bytes=VMEM_LIMIT),
        name="inproj_lat" if rope else "inproj_ctx",
    )(*args)


ATT_SUB = 16
ATT_Q = 64
ONES_ROWS = 16


def _attn_kernel(sink_ref, qt_ref, kp_ref, kc_ref, kn_ref, kx_ref,
                 vp_ref, vc_ref, vn_ref, vx_ref, o_ref, *, n_steps):
    i = pl.program_id(1)
    bq = WINDOW
    nwin = ATT_Q + 2 * WINDOW
    nq = GROUP * ATT_Q
    n_sub = ATT_SUB * bq // ATT_Q
    c = lax.broadcasted_iota(jnp.int32, (ATT_Q, nq), 0)
    r = lax.broadcasted_iota(jnp.int32, (ATT_Q, nq), 1) % ATT_Q
    behind_ok = c >= r
    ahead_ok = c <= r
    lane = lax.broadcasted_iota(jnp.int32, (1, nq), 1)
    zeros_q = jnp.zeros((HEAD_DIM, nq), BF16)
    zeros_e = jnp.zeros((ATT_Q, nq), BF16)

    kc_all = jnp.concatenate([kp_ref[...], kc_ref[...], kn_ref[...]], axis=0)
    vc_all = jnp.concatenate([vp_ref[...], vc_ref[...], vn_ref[...]], axis=1)
    kx = kx_ref[...]
    vx = vx_ref[...]

    vaug = []
    for a in range(ATT_SUB):
        vwin = jnp.concatenate([vc_all[:, a * bq:(a + 3) * bq], vx], axis=1)
        ones = jnp.ones((ONES_ROWS, vwin.shape[1]), BF16)
        vaug.append([jnp.concatenate([vwin[kvh * HEAD_DIM:(kvh + 1) * HEAD_DIM], ones], axis=0)
                     for kvh in range(ATT_KV_HEADS)])

    tiles = [(s, kvh) for s in range(n_sub) for kvh in range(ATT_KV_HEADS)]

    def score(tile):
        s, kvh = tile
        q0 = s * ATT_Q
        h0 = kvh * GROUP
        qt = jnp.concatenate(
            [qt_ref[(h0 + u) * HEAD_DIM:(h0 + u + 1) * HEAD_DIM, q0:q0 + ATT_Q]
             for u in range(GROUP)], axis=1)
        rhs = jnp.concatenate([qt, zeros_q] if kvh == 0 else [zeros_q, qt], axis=0)
        kwin = jnp.concatenate([kc_all[q0:q0 + nwin], kx], axis=0)
        return _dot(kwin, rhs)

    def window_blocks(s, st):
        blocks = []
        for b in range(nwin // ATT_Q):
            blk = st[b * ATT_Q:(b + 1) * ATT_Q]
            if b == 0:
                blk = jnp.where(behind_ok, blk, NEG_INF)
            if b == nwin // ATT_Q - 1:
                blk = jnp.where(ahead_ok, blk, NEG_INF)
            first_key = s * ATT_Q - WINDOW + b * ATT_Q
            if first_key < 0:
                blk = jnp.where(i > 0, blk, NEG_INF)
            if first_key >= ATT_SUB * bq:
                blk = jnp.where(i < n_steps - 1, blk, NEG_INF)
            blocks.append(blk)
        return blocks

    def softmax_numerators(tile, st):
        s, kvh = tile
        h0 = kvh * GROUP
        parts = window_blocks(s, st) + [st[nwin:]]
        sink = sink_ref[h0 + GROUP - 1]
        for u in range(GROUP - 2, -1, -1):
            sink = jnp.where(lane < (u + 1) * ATT_Q, sink_ref[h0 + u], sink)
        sink = sink * LOG2E
        m = sink
        for p in parts:
            m = jnp.maximum(m, jnp.max(p, axis=0, keepdims=True))
        e = [jnp.exp2(p - m).astype(BF16) for p in parts]
        pad = [zeros_e]
        et = jnp.concatenate((e[:-1] + pad if s % 2 == 0 else pad + e[:-1]) + e[-1:], axis=0)
        return et, jnp.exp2(sink - m)

    def values(tile, et, e_sink):
        s, kvh = tile
        q0 = s * ATT_Q
        ot = _dot(vaug[s // 2][kvh], et)
        denom = ot[HEAD_DIM:HEAD_DIM + 1] + e_sink
        on = ot[0:HEAD_DIM] * (1.0 / denom)
        for u in range(GROUP):
            hd = kvh * GROUP + u
            o_ref[hd * HEAD_DIM:(hd + 1) * HEAD_DIM, q0:q0 + ATT_Q] = (
                on[:, u * ATT_Q:(u + 1) * ATT_Q].astype(BF16))

    scores = [score(t) for t in tiles]
    probs = [softmax_numerators(t, st) for t, st in zip(tiles, scores)]
    for t, (et, e_sink) in zip(tiles, probs):
        values(t, et, e_sink)


def _attn(sink, qt, k, vt, k_ctx, vt_ctx, *, batch, seq, n_ctx):
    bq = WINDOW
    bs = ATT_SUB * bq
    nb = seq // bq
    ns = seq // bs
    k3 = k.reshape(batch, seq, NKV)
    kx3 = k_ctx.reshape(batch, n_ctx, NKV)
    prev = lambda i: jnp.maximum(ATT_SUB * i - 1, 0)
    nxt = lambda i: jnp.minimum(ATT_SUB * (i + 1), nb - 1)
    return pl.pallas_call(
        functools.partial(_attn_kernel, n_steps=ns),
        out_shape=jax.ShapeDtypeStruct((batch, NQ, seq), BF16),
        grid=(batch, ns),
        in_specs=[
            pl.BlockSpec(memory_space=pltpu.SMEM),
            pl.BlockSpec((None, NQ, bs), lambda b, i: (b, 0, i)),
            pl.BlockSpec((None, bq, NKV), lambda b, i: (b, prev(i), 0)),
            pl.BlockSpec((None, bs, NKV), lambda b, i: (b, i, 0)),
            pl.BlockSpec((None, bq, NKV), lambda b, i: (b, nxt(i), 0)),
            pl.BlockSpec((None, n_ctx, NKV), lambda b, i: (b, 0, 0)),
            pl.BlockSpec((None, NKV, bq), lambda b, i: (b, 0, prev(i))),
            pl.BlockSpec((None, NKV, bs), lambda b, i: (b, 0, i)),
            pl.BlockSpec((None, NKV, bq), lambda b, i: (b, 0, nxt(i))),
            pl.BlockSpec((None, NKV, n_ctx), lambda b, i: (b, 0, 0)),
        ],
        out_specs=pl.BlockSpec((None, NQ, bs), lambda b, i: (b, 0, i)),
        compiler_params=pltpu.CompilerParams(dimension_semantics=("arbitrary", "arbitrary"),
                                             vmem_limit_bytes=VMEM_LIMIT),
        name="attn",
    )(sink, qt, k3, k3, k3, kx3, vt, vt, vt, vt_ctx)


GRP_K = 4 * GLA_DK
GRP_V = 4 * GLA_DV
N_GRP = GLA_HEADS // 4
GLA_TB = 1024
GLA_STATE_TB = 2048
CUM_BLK = 256


def _gla_constants():
    t = CUM_BLK
    r = np.arange(t)[:, None]
    c = np.arange(t)[None, :]
    same_chunk = (r // CHUNK) == (c // CHUNK)
    tri_l = (same_chunk & (c <= r)).astype(np.float32)
    tri_u = (same_chunk & (c >= r)).astype(np.float32)
    rk = np.arange(4 * CHUNK)[:, None] // CHUNK
    kmask = (rk == (np.arange(GRP_K)[None, :] // GLA_DK)).astype(np.float32)
    vmask = (rk == (np.arange(GRP_V)[None, :] // GLA_DV)).astype(np.float32)
    smask = ((np.arange(GRP_K)[:, None] // GLA_DK)
             == (np.arange(GRP_V)[None, :] // GLA_DV)).astype(np.float32)
    ai = np.arange(CHUNK)[:, None]
    aj = np.arange(4 * CHUNK)[None, :] % CHUNK
    causal = (aj <= ai).astype(np.float32)
    anti = (aj >= ai).astype(np.float32)
    hmean = ((np.arange(GRP_V)[:, None] // GLA_DV)
             == (np.arange(GRP_V)[None, :] // GLA_DV)).astype(np.float32) / GLA_DV
    return (jnp.asarray(tri_l, BF16), jnp.asarray(tri_u, BF16), jnp.asarray(kmask, BF16),
            jnp.asarray(vmask, BF16), jnp.asarray(smask, BF16), jnp.asarray(causal, BF16),
            jnp.asarray(anti, BF16), jnp.asarray(hmean, BF16))


def _gk(g):
    return slice(g * GRP_K, (g + 1) * GRP_K)


def _gv(g):
    return slice(g * GRP_V, (g + 1) * GRP_V)


def _chunk_slices(t):
    return [slice(ci * CHUNK, (ci + 1) * CHUNK) for ci in range(t // CHUNK)]


def _cum_and_totals(la_hi, la_lo, tri, *, reverse):
    parts = [slice(p, p + CUM_BLK) for p in range(0, la_hi.shape[0], CUM_BLK)]
    cum = jnp.concatenate([_dot(tri, la_hi[p]) + _dot(tri, la_lo[p]) for p in parts], axis=0)
    edge = 0 if reverse else CHUNK - 1
    tots = [cum[sl.start + edge:sl.start + edge + 1] for sl in _chunk_slices(cum.shape[0])]
    return cum, tots


def _bcast_chunks(rows, width):
    return jnp.concatenate([jnp.broadcast_to(r, (CHUNK, width)) for r in rows], axis=0)


def _decay(x, log2_factor):
    return x * jnp.exp2(log2_factor).astype(BF16)


def _state_increments(kd, v):
    return [[_dot_tn(kd[sl, _gk(g)], v[sl, _gv(g)]) for g in range(N_GRP)]
            for sl in _chunk_slices(kd.shape[0])]


def _masked_state(state, smask):
    return state.astype(BF16) * smask


def _decay_columns(tots):
    width = tots[0].shape[1]
    row = lax.broadcasted_iota(jnp.int32, (8, width), 0)
    rows = jnp.zeros((8, width), F32)
    for ci, t in enumerate(tots):
        rows = jnp.where(row == ci, jnp.broadcast_to(t, (8, width)), rows)
    return jnp.exp2(rows).T


def _interleave(stage_gens):
    gens = list(stage_gens)
    done = [False] * len(gens)
    rnd = 0
    while not all(done):
        for gi, gen in enumerate(gens):
            if rnd >= gi and not done[gi]:
                try:
                    next(gen)
                except StopIteration:
                    done[gi] = True
        rnd += 1


def _state_stages(st_ref, k, v, la_hi, la_lo, tri, smask, *, reverse,
                  store_ref=None, store_base=None):
    nch = k.shape[0] // CHUNK
    cum, tots = _cum_and_totals(la_hi, la_lo, tri, reverse=reverse)
    yield
    kd = _decay(k, _bcast_chunks(tots, GQ) - cum)
    dcol = _decay_columns(tots)
    yield
    ds = _state_increments(kd, v)
    yield
    state = [st_ref[g] for g in range(N_GRP)]
    for ci in (range(nch - 1, -1, -1) if reverse else range(nch)):
        for g in range(N_GRP):
            if store_ref is not None:
                store_ref[store_base + ci, g] = _masked_state(state[g], smask)
            state[g] = state[g] * dcol[_gk(g), ci:ci + 1] + ds[ci][g]
    for g in range(N_GRP):
        st_ref[g] = state[g]


def _output_stages(q, k, v, la_f, la_b, gate, st_ref, bwd_state, consts, finish):
    tri_l, tri_u, kmask, vmask, smask, causal, anti, hmean = consts
    chunks = _chunk_slices(q.shape[0])

    cum_f, tots_f = _cum_and_totals(la_f[0], la_f[1], tri_l, reverse=False)
    cum_b, tots_b = _cum_and_totals(la_b[0], la_b[1], tri_u, reverse=True)
    yield

    qd_f = _decay(q, cum_f)
    qd_b = _decay(q, cum_b)
    ki_f = _decay(k, -cum_f)
    ki_b = _decay(k, -cum_b)
    kd_f = _decay(k, _bcast_chunks(tots_f, GQ) - cum_f)
    dcol_f = _decay_columns(tots_f)
    v_bd = [[jnp.concatenate([v[sl, _gv(g)]] * 4, axis=0) * vmask for g in range(N_GRP)]
            for sl in chunks]
    yield

    def scores(qd, ki, sl, g):
        ki_bd = jnp.concatenate([ki[sl, _gk(g)]] * 4, axis=0) * kmask
        return _dot_nt(qd[sl, _gk(g)], ki_bd)

    a_f = [[scores(qd_f, ki_f, sl, g) for g in range(N_GRP)] for sl in chunks]
    a_b = [[scores(qd_b, ki_b, sl, g) for g in range(N_GRP)] for sl in chunks]
    ds = _state_increments(kd_f, v)
    yield

    a = [[af[g].astype(BF16) * causal + ab[g].astype(BF16) * anti for g in range(N_GRP)]
         for af, ab in zip(a_f, a_b)]
    state = [st_ref[g] for g in range(N_GRP)]
    st_in = []
    for ci in range(len(chunks)):
        st_in.append([_masked_state(s, smask) for s in state])
        state = [state[g] * dcol_f[_gk(g), ci:ci + 1] + ds[ci][g] for g in range(N_GRP)]
    for g in range(N_GRP):
        st_ref[g] = state[g]
    yield

    outs = []
    for ci, sl in enumerate(chunks):
        parts = []
        for g in range(N_GRP):
            intra = _dot(a[ci][g], v_bd[ci][g])
            qd2 = jnp.concatenate([qd_f[sl, _gk(g)], qd_b[sl, _gk(g)]], axis=1)
            st2 = jnp.concatenate([st_in[ci][g], bwd_state(ci, g)], axis=0)
            parts.append(intra + _dot(qd2, st2))
        outs.append(jnp.concatenate(parts, axis=1))
    o = jnp.concatenate(outs, axis=0)
    yield

    o2 = (o * o).astype(BF16)
    ms = jnp.concatenate([_dot(o2[:, _gv(g)], hmean) for g in range(N_GRP)], axis=1)
    yield
    finish((o * lax.rsqrt(ms + EPS) * gate.astype(F32)).astype(BF16))


def _gla_kernel(skv_ref, slah_ref, slal_ref,
                kv_ref, qg_ref, lahf_ref, lahb_ref, lalf_ref, lalb_ref,
                ginc_ref, lahc_ref, lalc_ref,
                tril_ref, triu_ref, kmask_ref, vmask_ref, smask_ref, causal_ref, anti_ref,
                hmean_ref, o_ref, st_ref, sb_ref, *, n_state_steps):
    step = pl.program_id(1)
    ch_per_half = CUM_BLK // CHUNK
    fwd = slice(0, GQ)
    bwd = slice(GQ, 2 * GQ)

    def ctx_state(cols, tri_ref, reverse):
        st_ref[...] = jnp.zeros_like(st_ref)
        ctx_blocks = [slice(r0, r0 + CUM_BLK) for r0 in range(0, ginc_ref.shape[0], CUM_BLK)]
        for rs in (ctx_blocks[::-1] if reverse else ctx_blocks):
            for _ in _state_stages(st_ref, ginc_ref[rs, K_COLS], ginc_ref[rs, V_COLS],
                                   lahc_ref[rs, cols], lalc_ref[rs, cols], tri_ref[...],
                                   smask_ref[...], reverse=reverse):
                pass

    @pl.when(step < n_state_steps)
    def _backward_states():
        @pl.when(step == 0)
        def _():
            ctx_state(bwd, triu_ref, True)

        blk = n_state_steps - 1 - step
        parts = [slice(r0, r0 + CUM_BLK) for r0 in range(0, GLA_STATE_TB, CUM_BLK)]
        _interleave([
            _state_stages(st_ref, skv_ref[rs, K_COLS], skv_ref[rs, V_COLS],
                          slah_ref[rs, :], slal_ref[rs, :], triu_ref[...], smask_ref[...],
                          reverse=True, store_ref=sb_ref,
                          store_base=blk * (GLA_STATE_TB // CHUNK) + pi * ch_per_half)
            for pi, rs in reversed(list(enumerate(parts)))])

    @pl.when(step >= n_state_steps)
    def _outputs():
        j = step - n_state_steps

        @pl.when(j == 0)
        def _():
            ctx_state(fwd, tril_ref, False)

        consts = (tril_ref[...], triu_ref[...], kmask_ref[...], vmask_ref[...],
                  smask_ref[...], causal_ref[...], anti_ref[...], hmean_ref[...])
        halves = [slice(r0, r0 + CUM_BLK) for r0 in range(0, GLA_TB, CUM_BLK)]

        def half(hi, rs):
            base = j * (GLA_TB // CHUNK) + hi * ch_per_half

            def finish(val):
                o_ref[rs, :] = val

            return _output_stages(
                qg_ref[rs, 0:GQ], kv_ref[rs, K_COLS], kv_ref[rs, V_COLS],
                (lahf_ref[rs, :], lalf_ref[rs, :]), (lahb_ref[rs, :], lalb_ref[rs, :]),
                qg_ref[rs, GQ:KV_W], st_ref, lambda ci, g: sb_ref[base + ci, g], consts, finish)

        _interleave([half(hi, rs) for hi, rs in enumerate(halves)])


def _gla(gin, la_hi, la_lo, gin_ctx, la_hi_ctx, la_lo_ctx, *, batch, seq, n_ctx):
    assert n_ctx % CUM_BLK == 0 and CUM_BLK // CHUNK <= 8
    tb, stb = GLA_TB, GLA_STATE_TB
    ns, nb = seq // stb, seq // tb
    nch = seq // CHUNK
    r3 = lambda a, t: a.reshape(batch, t, a.shape[-1])
    consts = _gla_constants()

    def state_blk(col):
        return lambda b, s: (b, jnp.maximum(ns - 1 - s, 0), col)

    def out_blk(col):
        return lambda b, s: (b, jnp.maximum(s - ns, 0), col)

    ctx_map = lambda b, s: (b, 0, 0)
    const_map = lambda b, s: (0, 0)
    gin3, lah3, lal3 = r3(gin, seq), r3(la_hi, seq), r3(la_lo, seq)
    return pl.pallas_call(
        functools.partial(_gla_kernel, n_state_steps=ns),
        out_shape=jax.ShapeDtypeStruct((batch, seq, GV), BF16),
        grid=(batch, ns + nb),
        in_specs=[pl.BlockSpec((None, stb, KV_W), state_blk(0)),
                  pl.BlockSpec((None, stb, GQ), state_blk(1)),
                  pl.BlockSpec((None, stb, GQ), state_blk(1)),
                  pl.BlockSpec((None, tb, KV_W), out_blk(0)),
                  pl.BlockSpec((None, tb, KV_W), out_blk(1)),
                  pl.BlockSpec((None, tb, GQ), out_blk(0)),
                  pl.BlockSpec((None, tb, GQ), out_blk(1)),
                  pl.BlockSpec((None, tb, GQ), out_blk(0)),
                  pl.BlockSpec((None, tb, GQ), out_blk(1)),
                  pl.BlockSpec((None, n_ctx, GIN_W), ctx_map),
                  pl.BlockSpec((None, n_ctx, LA_W), ctx_map),
                  pl.BlockSpec((None, n_ctx, LA_W), ctx_map)]
                 + [pl.BlockSpec(cst.shape, const_map) for cst in consts],
        out_specs=pl.BlockSpec((None, tb, GV), out_blk(0)),
        scratch_shapes=[pltpu.VMEM((N_GRP, GRP_K, GRP_V), F32),
                        pltpu.VMEM((nch, N_GRP, GRP_K, GRP_V), BF16)],
        compiler_params=pltpu.CompilerParams(
            dimension_semantics=("arbitrary", "arbitrary"),
            vmem_limit_bytes=VMEM_LIMIT),
        name="gla",
    )(gin3, lah3, lal3, gin3, gin3, lah3, lah3, lal3, lal3, r3(gin_ctx, n_ctx),
      r3(la_hi_ctx, n_ctx), r3(la_lo_ctx, n_ctx), *consts)


FFN_CHUNK = 256
MIX_SLICES = 16
MIX_TM = 512


def _mix_ffn_kernel(att_ref, gla_ref, x_ref, ada_ref, gpm_ref, gpf_ref, gqf_ref,
                    woa_ref, wog_ref, wfi_ref, wfo_ref, o_ref, x1_ref, h_ref, y_ref, acc_ref, *,
                    tiles_per_batch, n_tiles):
    i = pl.program_id(0)
    tm = x_ref.shape[0]
    chunks = [(c0, min(c0 + FFN_CHUNK, FFN_HIDDEN)) for c0 in range(0, FFN_HIDDEN, FFN_CHUNK)]
    n_chunks = len(chunks)
    mix_rows = tm // MIX_SLICES

    def mix_matmul():
        y_ref[...] = _dot_tn(att_ref[...], woa_ref[...]) + _dot(gla_ref[...], wog_ref[...])

    def mix_rows_slice(slot, si, anchor=None):
        rs = slice(si * mix_rows, (si + 1) * mix_rows)
        row = jnp.minimum(i, n_tiles - 1) // tiles_per_batch
        ada = ada_ref[pl.ds(row, 1), :]
        gt1 = ada[:, 2 * D:3 * D]
        if anchor is not None:
            bits = pltpu.bitcast(anchor, jnp.uint32)
            sixteen = jnp.uint32(16)
            gt1 = gt1 + pltpu.bitcast(
                lax.shift_right_logical(lax.shift_right_logical(bits, sixteen), sixteen), F32)
        sh2 = ada[:, 3 * D:4 * D]
        sc2 = ada[:, 4 * D:5 * D]
        x1 = x_ref[rs, :] + gt1 * (_rms(y_ref[rs, :]) * gpm_ref[...])
        x1_ref[slot, rs, :] = x1
        h_ref[slot, rs, :] = ((_rms(x1) * gpf_ref[...]) * (1.0 + sc2) + sh2).astype(BF16)

    def ffn_up(slot, ci):
        c0, c1 = chunks[ci]
        h = h_ref[slot]
        g = _dot(h, wfi_ref[:, c0:c1])
        u = _dot(h, wfi_ref[:, FFN_HIDDEN + c0:FFN_HIDDEN + c1])
        return (_silu(g) * u).astype(BF16)

    def ffn_down(ci, a):
        c0, c1 = chunks[ci]
        part = _dot(a, wfo_ref[c0:c1, :])
        if ci == 0:
            acc_ref[...] = part
        else:
            acc_ref[...] += part
        return part[0:8, :]

    def ffn_finish(slot):
        row = jnp.maximum(i - 1, 0) // tiles_per_batch
        gt2 = ada_ref[pl.ds(row, 1), 5 * D:6 * D]
        o_ref[...] = x1_ref[slot] + gt2 * (_rms(acc_ref[...]) * gqf_ref[...])

    @pl.when(i == 0)
    def _():
        mix_matmul()
        for si in range(MIX_SLICES):
            mix_rows_slice(0, si)

    for parity in range(2):
        @pl.when((i % 2 == parity) & (i > 0))
        def _():
            per_gap = -(-MIX_SLICES // (n_chunks - 1))
            act = None
            for ci in range(n_chunks + 1):
                nxt = ffn_up(1 - parity, ci) if ci < n_chunks else None
                if ci >= 1:
                    anchor = ffn_down(ci - 1, act)
                    if ci == 1:
                        mix_matmul()
                    else:
                        for si in range((ci - 2) * per_gap, min((ci - 1) * per_gap, MIX_SLICES)):
                            mix_rows_slice(parity, si, anchor[0:1, :])
                act = nxt
            ffn_finish(1 - parity)


def _mix_ffn(att_t, gla, x2d, ada, g_post_mix, g_pre_ffn, g_post_ffn,
             w_out_a, w_out_g, w_ffn_in, w_ffn_out, *, tm, tiles_per_batch):
    n = x2d.shape[0]
    tpb = tiles_per_batch
    nt = n // tm
    const = lambda i: (0, 0)
    resident = functools.partial(pl.BlockSpec, index_map=const, pipeline_mode=pl.Buffered(1))
    cur = lambda i: jnp.minimum(i, nt - 1)
    done = lambda i: jnp.maximum(i - 1, 0)
    return pl.pallas_call(
        functools.partial(_mix_ffn_kernel, tiles_per_batch=tpb, n_tiles=nt),
        out_shape=jax.ShapeDtypeStruct((n, D), F32),
        grid=(nt + 1,),
        in_specs=[pl.BlockSpec((None, NQ, tm), lambda i: (cur(i) // tpb, 0, cur(i) % tpb)),
                  pl.BlockSpec((tm, GV), lambda i: (cur(i), 0)),
                  pl.BlockSpec((tm, D), lambda i: (cur(i), 0)),
                  pl.BlockSpec((8, 6 * D), const),
                  pl.BlockSpec((1, D), const),
                  pl.BlockSpec((1, D), const),
                  pl.BlockSpec((1, D), const),
                  resident((NQ, D)),
                  resident((GV, D)),
                  resident((D, 2 * FFN_HIDDEN)),
                  resident((FFN_HIDDEN, D))],
        out_specs=pl.BlockSpec((tm, D), lambda i: (done(i), 0)),
        scratch_shapes=[pltpu.VMEM((2, tm, D), F32),
                        pltpu.VMEM((2, tm, D), BF16),
                        pltpu.VMEM((tm, D), F32),
                        pltpu.VMEM((tm, D), F32)],
        compiler_params=pltpu.CompilerParams(dimension_semantics=("arbitrary",),
                                             vmem_limit_bytes=VMEM_LIMIT),
        name="mix_ffn",
    )(att_t, gla, x2d, ada, g_post_mix, g_pre_ffn, g_post_ffn,
      w_out_a, w_out_g, w_ffn_in, w_ffn_out)


def _rope_tables(seq):
    half = HEAD_DIM // 2
    inv_freq = ROPE_BASE ** (-np.arange(0, half, 2, dtype=np.float64) / half)
    dim = np.arange(HEAD_DIM)
    freq = np.tile(inv_freq, HEAD_DIM // ROPE_BLK)[:, None]
    pos = np.arange(seq)[None, :]
    ang = np.where((dim < half)[:, None], pos // GRID_W, pos % GRID_W) * freq
    cos, sin = np.cos(ang), np.sin(ang)
    even = ((dim // ROPE_BLK) % 2 == 0)[:, None]
    sa = np.where(even, -sin, 0.0)
    sb = np.where(even, 0.0, sin)
    return tuple(jnp.asarray(t, F32) for t in (cos, sa, sb))


def kernel(x, c, ctx, c_ctx, w_ada, b_ada, g_pre_mix, g_post_mix, g_pre_ffn, g_post_ffn,
           w_in, attn_sink, w_gate_fwd, b_gate_fwd, w_gate_bwd, b_gate_bwd, g_gla_norm,
           w_out, w_ffn_in, w_ffn_out):
    batch, seq, _ = x.shape
    n_ctx = ctx.shape[1]
    depth = w_ada.shape[0]
    assert depth == 1
    l = 0

    c8 = jnp.zeros((8, D), F32).at[0:batch].set(c).at[4].set(c_ctx)
    ada = _ada(c8, w_ada[l], b_ada[l][None, :])

    wi = w_in[l]
    g0 = NQ + 2 * NKV
    w_qkv_t = wi[:, 0:g0].T.astype(BF16)
    gk0 = g0 + GQ
    gg0 = gk0 + KV_W
    w_rest = jnp.concatenate(
        [wi[:, gk0:gg0], wi[:, g0:gk0], wi[:, gg0:],
         jnp.zeros((D, Z_W - 2 * GATE_RANK), F32)], axis=1).astype(BF16)
    w_gate_p = jnp.zeros((Z_W, LA_W), F32)
    w_gate_p = w_gate_p.at[0:GATE_RANK, 0:GQ].set(w_gate_fwd[l])
    w_gate_p = w_gate_p.at[GATE_RANK:2 * GATE_RANK, GQ:].set(w_gate_bwd[l]).astype(BF16)
    b_gate_p = jnp.concatenate([b_gate_fwd[l], b_gate_bwd[l]])[None, :]

    tables = _rope_tables(seq)
    x2d = x.reshape(batch * seq, D)
    g_pre = g_pre_mix[l][None, :]
    gn_tiled = jnp.tile(g_gla_norm[l], GLA_HEADS)[None, :]
    qt, k, vt, gin, lah, lal = _inproj(x2d, ada, g_pre, w_qkv_t, w_rest, w_gate_p, b_gate_p,
                                      gn_tiled, tables, batch=batch, tokens=seq, tm=INPROJ_TM)
    _, k_c, vt_c, gin_c, lah_c, lal_c = _inproj(
        ctx.reshape(batch * n_ctx, D), ada, g_pre, w_qkv_t, w_rest, w_gate_p, b_gate_p,
        gn_tiled, None, batch=batch, tokens=n_ctx, tm=n_ctx)

    att_t = _attn(attn_sink[l], qt, k, vt, k_c, vt_c, batch=batch, seq=seq, n_ctx=n_ctx)

    gla = _gla(gin, lah, lal, gin_c, lah_c, lal_c, batch=batch, seq=seq, n_ctx=n_ctx)

    wo = w_out[l]
    out = _mix_ffn(att_t, gla.reshape(batch * seq, GV), x2d, ada,
                   g_post_mix[l][None, :], g_pre_ffn[l][None, :], g_post_ffn[l][None, :],
                   wo[:NQ].astype(BF16), wo[NQ:].astype(BF16),
                   w_ffn_in[l].astype(BF16), w_ffn_out[l].astype(BF16),
                   tm=MIX_TM, tiles_per_batch=seq // MIX_TM)
    return out.reshape(batch, seq, D)
```

```python
import functools

import jax
import jax.numpy as jnp
import numpy as np
from jax import lax
from jax.experimental import pallas as pl
from jax.experimental.pallas import tpu as pltpu

D = 1024
HEAD_DIM = 64
ATT_HEADS = 8
ATT_KV_HEADS = 2
GROUP = ATT_HEADS // ATT_KV_HEADS
WINDOW = 128
GRID_W = 64
ROPE_BASE = 10000.0
ROPE_BLK = HEAD_DIM // 4
GLA_HEADS = 8
GLA_DK = 32
GLA_DV = 64
CHUNK = 64
GATE_RANK = 16
GATE_TAU = 16.0
FFN_HIDDEN = 2816
NEG_INF = -1e30
EPS = 1e-6
LOG2E = 1.4426950408889634
Q_SCALE = LOG2E * HEAD_DIM ** -0.5
LA_SCALE = LOG2E / GATE_TAU

NQ = ATT_HEADS * HEAD_DIM
NKV = ATT_KV_HEADS * HEAD_DIM
GQ = GLA_HEADS * GLA_DK
GV = GLA_HEADS * GLA_DV
GIN_W = 2 * GQ + 2 * GV
KV_W = GQ + GV
K_COLS = slice(0, GQ)
V_COLS = slice(GQ, KV_W)
Q_COLS = slice(KV_W, KV_W + GQ)
G_COLS = slice(KV_W + GQ, GIN_W)
Z_W = 128
LA_W = 2 * GQ
REST_W = GIN_W + Z_W

LANES = 128
VMEM_LIMIT = 56 * 1024 * 1024

BF16 = jnp.bfloat16
F32 = jnp.float32


def _dot(a, b):
    return jnp.dot(a, b, preferred_element_type=F32)


def _dot_nt(a, b):
    return lax.dot_general(a, b, (((1,), (1,)), ((), ())), preferred_element_type=F32)


def _dot_tn(a, b):
    return lax.dot_general(a, b, (((0,), (0,)), ((), ())), preferred_element_type=F32)


def _rms(x):
    return x * lax.rsqrt(jnp.mean(x * x, axis=-1, keepdims=True) + EPS)


def _silu(x):
    return x * (1.0 / (1.0 + jnp.exp(-x)))


def _ada_kernel(c_ref, w_ref, b_ref, o_ref):
    a = _silu(c_ref[...]).astype(BF16)
    o_ref[...] = _dot(a, w_ref[...].astype(BF16)) + b_ref[...]


def _ada(c8, w_ada, b_ada):
    n = w_ada.shape[1]
    bn = 1024
    return pl.pallas_call(
        _ada_kernel,
        out_shape=jax.ShapeDtypeStruct((8, n), F32),
        grid=(n // bn,),
        in_specs=[pl.BlockSpec((8, D), lambda j: (0, 0)),
                  pl.BlockSpec((D, bn), lambda j: (0, j)),
                  pl.BlockSpec((1, bn), lambda j: (0, j))],
        out_specs=pl.BlockSpec((8, bn), lambda j: (0, j)),
        compiler_params=pltpu.CompilerParams(dimension_semantics=("arbitrary",)),
        name="ada",
    )(c8, w_ada, b_ada)


NORM_ROWS = 128
INPROJ_TM = 1024


def _inproj_kernel(*refs, tiles_per_batch, rope):
    if rope:
        (x_ref, ada_ref, g_ref, wqkv_ref, w_ref, wg_ref, bg_ref, gn_ref,
         cost_ref, sat_ref, sbt_ref,
         qt_ref, k_ref, vt_ref, gin_ref, lah_ref, lal_ref) = refs
        row = pl.program_id(0) // tiles_per_batch
    else:
        (x_ref, ada_ref, g_ref, wqkv_ref, w_ref, wg_ref, bg_ref, gn_ref,
         qt_ref, k_ref, vt_ref, gin_ref, lah_ref, lal_ref) = refs
        row = 4
    ada = ada_ref[pl.ds(row, 1), :]
    sh1 = ada[:, 0:D]
    sc1 = ada[:, D:2 * D]
    tm = x_ref.shape[0]
    hb = jnp.concatenate(
        [((_rms(x_ref[r0:r0 + NORM_ROWS, :]) * g_ref[...]) * (1.0 + sc1) + sh1).astype(BF16)
         for r0 in range(0, tm, NORM_ROWS)], axis=0)

    z = _dot(hb, w_ref[:, GIN_W:REST_W])
    qkv = _dot_nt(wqkv_ref[...], hb)
    logits = _dot(z.astype(BF16), wg_ref[...]) + bg_ref[...]
    gate = _dot(hb, w_ref[:, G_COLS])
    gin = _dot(hb, w_ref[:, 0:KV_W + GQ])

    def head_t(hd):
        xh = qkv[hd * HEAD_DIM:(hd + 1) * HEAD_DIM]
        if not rope:
            return xh
        up = jnp.concatenate([xh[ROPE_BLK:], xh[:ROPE_BLK]], axis=0)
        dn = jnp.concatenate([xh[HEAD_DIM - ROPE_BLK:], xh[:HEAD_DIM - ROPE_BLK]], axis=0)
        return xh * cost_ref[...] + up * sat_ref[...] + dn * sbt_ref[...]

    for hd in range(ATT_HEADS):
        qh = head_t(hd)
        if rope:
            qh = qh * Q_SCALE
        qt_ref[hd * HEAD_DIM:(hd + 1) * HEAD_DIM, :] = qh.astype(BF16)
    kt = jnp.concatenate([head_t(ATT_HEADS + hd) for hd in range(ATT_KV_HEADS)], axis=0)
    k_ref[...] = kt.T.astype(BF16)
    vt_ref[...] = qkv[NQ + NKV:NQ + 2 * NKV].astype(BF16)

    log_sig = jnp.minimum(logits, 0.0) - jnp.log(1.0 + jnp.exp(-jnp.abs(logits)))
    la = log_sig * LA_SCALE
    hi = la.astype(BF16)
    lah_ref[...] = hi
    lal_ref[...] = (la - hi.astype(F32)).astype(BF16)

    gin_ref[:, G_COLS] = (_silu(gate) * gn_ref[...]).astype(BF16)
    gin_ref[:, 0:KV_W] = gin[:, 0:KV_W].astype(BF16)
    gin_ref[:, Q_COLS] = (gin[:, Q_COLS] * (GLA_DK ** -0.5)).astype(BF16)


def _inproj(x2d, ada, g_pre, w_qkv_t, w_rest, w_gate_p, b_gate_p, gn_tiled, tables, *,
            batch, tokens, tm):
    n = x2d.shape[0]
    tpb = tokens // tm
    rope = tables is not None
    const = lambda i: (0, 0)
    in_specs = [pl.BlockSpec((tm, D), lambda i: (i, 0)),
                pl.BlockSpec((8, 6 * D), const),
                pl.BlockSpec((1, D), const),
                pl.BlockSpec((NQ + 2 * NKV, D), const),
                pl.BlockSpec((D, REST_W), const),
                pl.BlockSpec((Z_W, LA_W), const),
                pl.BlockSpec((1, LA_W), const),
                pl.BlockSpec((1, GV), const)]
    args = [x2d, ada, g_pre, w_qkv_t, w_rest, w_gate_p, b_gate_p, gn_tiled]
    if rope:
        in_specs += [pl.BlockSpec((HEAD_DIM, tm), lambda i: (0, i % tpb))] * 3
        args += list(tables)
    row_blk = lambda w: pl.BlockSpec((tm, w), lambda i: (i, 0))
    col_blk = lambda w: pl.BlockSpec((None, w, tm), lambda i: (i // tpb, 0, i % tpb))
    return pl.pallas_call(
        functools.partial(_inproj_kernel, tiles_per_batch=tpb, rope=rope),
        out_shape=(jax.ShapeDtypeStruct((batch, NQ, tokens), BF16),
                   jax.ShapeDtypeStruct((n, NKV), BF16),
                   jax.ShapeDtypeStruct((batch, NKV, tokens), BF16),
                   jax.ShapeDtypeStruct((n, GIN_W), BF16),
                   jax.ShapeDtypeStruct((n, LA_W), BF16),
                   jax.ShapeDtypeStruct((n, LA_W), BF16)),
        grid=(n // tm,),
        in_specs=in_specs,
        out_specs=(col_blk(NQ), row_blk(NKV), col_blk(NKV),
                   row_blk(GIN_W), row_blk(LA_W), row_blk(LA_W)),
        compiler_params=pltpu.CompilerParams(dimension_semantics=("arbitrary",),
                                             vmem_limit_bytes=VMEM_LIMIT),
        name="inproj_lat" if rope else "inproj_ctx",
    )(*args)


ATT_SUB = 16
ATT_Q = 64
ONES_ROWS = 16


def _attn_kernel(sink_ref, qt_ref, kp_ref, kc_ref, kn_ref, kx_ref,
                 vp_ref, vc_ref, vn_ref, vx_ref, o_ref, *, n_steps):
    i = pl.program_id(1)
    bq = WINDOW
    nwin = ATT_Q + 2 * WINDOW
    nq = GROUP * ATT_Q
    n_sub = ATT_SUB * bq // ATT_Q
    c = lax.broadcasted_iota(jnp.int32, (ATT_Q, nq), 0)
    r = lax.broadcasted_iota(jnp.int32, (ATT_Q, nq), 1) % ATT_Q
    behind_ok = c >= r
    ahead_ok = c <= r
    lane = lax.broadcasted_iota(jnp.int32, (1, nq), 1)
    zeros_q = jnp.zeros((HEAD_DIM, nq), BF16)
    zeros_e = jnp.zeros((ATT_Q, nq), BF16)

    n_cur = ATT_SUB * bq

    def key_rows(lo, hi):
        parts = []
        if lo < 0:
            parts.append(kp_ref[bq + lo:bq, :])
        parts.append(kc_ref[max(lo, 0):min(hi, n_cur), :])
        if hi > n_cur:
            parts.append(kn_ref[0:hi - n_cur, :])
        return parts

    def value_cols(kvh, lo, hi):
        rows = slice(kvh * HEAD_DIM, (kvh + 1) * HEAD_DIM)
        parts = []
        if lo < 0:
            parts.append(vp_ref[rows, bq + lo:bq])
        parts.append(vc_ref[rows, max(lo, 0):min(hi, n_cur)])
        if hi > n_cur:
            parts.append(vn_ref[rows, 0:hi - n_cur])
        return parts

    tiles = [(s, kvh) for s in range(n_sub) for kvh in range(ATT_KV_HEADS)]

    def score(tile):
        s, kvh = tile
        q0 = s * ATT_Q
        h0 = kvh * GROUP
        qt = jnp.concatenate(
            [qt_ref[(h0 + u) * HEAD_DIM:(h0 + u + 1) * HEAD_DIM, q0:q0 + ATT_Q]
             for u in range(GROUP)], axis=1)
        rhs = jnp.concatenate([qt, zeros_q] if kvh == 0 else [zeros_q, qt], axis=0)
        kwin = jnp.concatenate(key_rows(q0 - WINDOW, q0 + ATT_Q + WINDOW) + [kx_ref[...]],
                               axis=0)
        return _dot(kwin, rhs)

    def window_blocks(s, st):
        blocks = []
        for b in range(nwin // ATT_Q):
            blk = st[b * ATT_Q:(b + 1) * ATT_Q]
            if b == 0:
                blk = jnp.where(behind_ok, blk, NEG_INF)
            if b == nwin // ATT_Q - 1:
                blk = jnp.where(ahead_ok, blk, NEG_INF)
            first_key = s * ATT_Q - WINDOW + b * ATT_Q
            if first_key < 0:
                blk = jnp.where(i > 0, blk, NEG_INF)
            if first_key >= ATT_SUB * bq:
                blk = jnp.where(i < n_steps - 1, blk, NEG_INF)
            blocks.append(blk)
        return blocks

    def softmax_numerators(tile, st):
        s, kvh = tile
        h0 = kvh * GROUP
        parts = window_blocks(s, st) + [st[nwin:]]
        sink = sink_ref[h0 + GROUP - 1]
        for u in range(GROUP - 2, -1, -1):
            sink = jnp.where(lane < (u + 1) * ATT_Q, sink_ref[h0 + u], sink)
        sink = sink * LOG2E
        m = sink
        for p in parts:
            m = jnp.maximum(m, jnp.max(p, axis=0, keepdims=True))
        e = [jnp.exp2(p - m).astype(BF16) for p in parts]
        pad = [zeros_e]
        et = jnp.concatenate((e[:-1] + pad if s % 2 == 0 else pad + e[:-1]) + e[-1:], axis=0)
        return et, jnp.exp2(sink - m)

    def values(tile, et, e_sink):
        s, kvh = tile
        q0 = s * ATT_Q
        w0 = (s // 2) * bq - WINDOW
        vwin = jnp.concatenate(
            value_cols(kvh, w0, w0 + 3 * bq)
            + [vx_ref[kvh * HEAD_DIM:(kvh + 1) * HEAD_DIM, :]], axis=1)
        vaug = jnp.concatenate([vwin, jnp.ones((ONES_ROWS, vwin.shape[1]), BF16)], axis=0)
        ot = _dot(vaug, et)
        denom = ot[HEAD_DIM:HEAD_DIM + 1] + e_sink
        on = ot[0:HEAD_DIM] * (1.0 / denom)
        for u in range(GROUP):
            hd = kvh * GROUP + u
            o_ref[hd * HEAD_DIM:(hd + 1) * HEAD_DIM, q0:q0 + ATT_Q] = (
                on[:, u * ATT_Q:(u + 1) * ATT_Q].astype(BF16))

    scores = [score(t) for t in tiles]
    probs = [softmax_numerators(t, st) for t, st in zip(tiles, scores)]
    for t, (et, e_sink) in zip(tiles, probs):
        values(t, et, e_sink)


def _attn(sink, qt, k, vt, k_ctx, vt_ctx, *, batch, seq, n_ctx):
    bq = WINDOW
    bs = ATT_SUB * bq
    nb = seq // bq
    ns = seq // bs
    k3 = k.reshape(batch, seq, NKV)
    kx3 = k_ctx.reshape(batch, n_ctx, NKV)
    prev = lambda i: jnp.maximum(ATT_SUB * i - 1, 0)
    nxt = lambda i: jnp.minimum(ATT_SUB * (i + 1), nb - 1)
    return pl.pallas_call(
        functools.partial(_attn_kernel, n_steps=ns),
        out_shape=jax.ShapeDtypeStruct((batch, NQ, seq), BF16),
        grid=(batch, ns),
        in_specs=[
            pl.BlockSpec(memory_space=pltpu.SMEM),
            pl.BlockSpec((None, NQ, bs), lambda b, i: (b, 0, i)),
            pl.BlockSpec((None, bq, NKV), lambda b, i: (b, prev(i), 0)),
            pl.BlockSpec((None, bs, NKV), lambda b, i: (b, i, 0)),
            pl.BlockSpec((None, bq, NKV), lambda b, i: (b, nxt(i), 0)),
            pl.BlockSpec((None, n_ctx, NKV), lambda b, i: (b, 0, 0)),
            pl.BlockSpec((None, NKV, bq), lambda b, i: (b, 0, prev(i))),
            pl.BlockSpec((None, NKV, bs), lambda b, i: (b, 0, i)),
            pl.BlockSpec((None, NKV, bq), lambda b, i: (b, 0, nxt(i))),
            pl.BlockSpec((None, NKV, n_ctx), lambda b, i: (b, 0, 0)),
        ],
        out_specs=pl.BlockSpec((None, NQ, bs), lambda b, i: (b, 0, i)),
        compiler_params=pltpu.CompilerParams(dimension_semantics=("arbitrary", "arbitrary"),
                                             vmem_limit_bytes=VMEM_LIMIT),
        name="attn",
    )(sink, qt, k3, k3, k3, kx3, vt, vt, vt, vt_ctx)


GRP_K = 4 * GLA_DK
GRP_V = 4 * GLA_DV
N_GRP = GLA_HEADS // 4
GLA_TB = 1024
GLA_STATE_TB = 2048
CUM_BLK = 256


def _gla_constants():
    t = CUM_BLK
    r = np.arange(t)[:, None]
    c = np.arange(t)[None, :]
    same_chunk = (r // CHUNK) == (c // CHUNK)
    tri_l = (same_chunk & (c <= r)).astype(np.float32)
    tri_u = (same_chunk & (c >= r)).astype(np.float32)
    rk = np.arange(4 * CHUNK)[:, None] // CHUNK
    kmask = (rk == (np.arange(GRP_K)[None, :] // GLA_DK)).astype(np.float32)
    vmask = (rk == (np.arange(GRP_V)[None, :] // GLA_DV)).astype(np.float32)
    smask = ((np.arange(GRP_K)[:, None] // GLA_DK)
             == (np.arange(GRP_V)[None, :] // GLA_DV)).astype(np.float32)
    ai = np.arange(CHUNK)[:, None]
    aj = np.arange(4 * CHUNK)[None, :] % CHUNK
    causal = (aj <= ai).astype(np.float32)
    anti = (aj >= ai).astype(np.float32)
    hmean = ((np.arange(GRP_V)[:, None] // GLA_DV)
             == (np.arange(GRP_V)[None, :] // GLA_DV)).astype(np.float32) / GLA_DV
    return (jnp.asarray(tri_l, BF16), jnp.asarray(tri_u, BF16), jnp.asarray(kmask, BF16),
            jnp.asarray(vmask, BF16), jnp.asarray(smask, BF16), jnp.asarray(causal, BF16),
            jnp.asarray(anti, BF16), jnp.asarray(hmean, BF16))


def _gk(g):
    return slice(g * GRP_K, (g + 1) * GRP_K)


def _gv(g):
    return slice(g * GRP_V, (g + 1) * GRP_V)


def _chunk_slices(t):
    return [slice(ci * CHUNK, (ci + 1) * CHUNK) for ci in range(t // CHUNK)]


def _cum_and_totals(la_hi, la_lo, tri, *, reverse):
    parts = [slice(p, p + CUM_BLK) for p in range(0, la_hi.shape[0], CUM_BLK)]
    cum = jnp.concatenate([_dot(tri, la_hi[p]) + _dot(tri, la_lo[p]) for p in parts], axis=0)
    edge = 0 if reverse else CHUNK - 1
    tots = [cum[sl.start + edge:sl.start + edge + 1] for sl in _chunk_slices(cum.shape[0])]
    return cum, tots


def _bcast_chunks(rows, width):
    return jnp.concatenate([jnp.broadcast_to(r, (CHUNK, width)) for r in rows], axis=0)


def _decay(x, log2_factor):
    return x * jnp.exp2(log2_factor).astype(BF16)


def _state_increments(kd, v):
    return [[_dot_tn(kd[sl, _gk(g)], v[sl, _gv(g)]) for g in range(N_GRP)]
            for sl in _chunk_slices(kd.shape[0])]


def _masked_state(state, smask):
    return state.astype(BF16) * smask


def _decay_columns(tots):
    width = tots[0].shape[1]
    row = lax.broadcasted_iota(jnp.int32, (8, width), 0)
    rows = jnp.zeros((8, width), F32)
    for ci, t in enumerate(tots):
        rows = jnp.where(row == ci, jnp.broadcast_to(t, (8, width)), rows)
    return jnp.exp2(rows).T


def _interleave(stage_gens):
    gens = list(stage_gens)
    done = [False] * len(gens)
    rnd = 0
    while not all(done):
        for gi, gen in enumerate(gens):
            if rnd >= gi and not done[gi]:
                try:
                    next(gen)
                except StopIteration:
                    done[gi] = True
        rnd += 1


def _state_stages(st_ref, k, v, la_hi, la_lo, tri, smask, *, reverse,
                  store_ref=None, store_base=None):
    nch = k.shape[0] // CHUNK
    cum, tots = _cum_and_totals(la_hi, la_lo, tri, reverse=reverse)
    yield
    kd = _decay(k, _bcast_chunks(tots, GQ) - cum)
    dcol = _decay_columns(tots)
    yield
    ds = _state_increments(kd, v)
    yield
    state = [st_ref[g] for g in range(N_GRP)]
    for ci in (range(nch - 1, -1, -1) if reverse else range(nch)):
        for g in range(N_GRP):
            if store_ref is not None:
                store_ref[store_base + ci, g] = _masked_state(state[g], smask)
            state[g] = state[g] * dcol[_gk(g), ci:ci + 1] + ds[ci][g]
    for g in range(N_GRP):
        st_ref[g] = state[g]


def _output_stages(q, k, v, la_f, la_b, gate, st_ref, bwd_state, consts, finish):
    tri_l, tri_u, kmask, vmask, smask, causal, anti, hmean = consts
    chunks = _chunk_slices(q.shape[0])

    cum_f, tots_f = _cum_and_totals(la_f[0], la_f[1], tri_l, reverse=False)
    cum_b, tots_b = _cum_and_totals(la_b[0], la_b[1], tri_u, reverse=True)
    yield

    qd_f = _decay(q, cum_f)
    qd_b = _decay(q, cum_b)
    ki_f = _decay(k, -cum_f)
    ki_b = _decay(k, -cum_b)
    kd_f = _decay(k, _bcast_chunks(tots_f, GQ) - cum_f)
    dcol_f = _decay_columns(tots_f)
    v_bd = [[jnp.concatenate([v[sl, _gv(g)]] * 4, axis=0) * vmask for g in range(N_GRP)]
            for sl in chunks]
    yield

    def scores(qd, ki, sl, g):
        ki_bd = jnp.concatenate([ki[sl, _gk(g)]] * 4, axis=0) * kmask
        return _dot_nt(qd[sl, _gk(g)], ki_bd)

    a_f = [[scores(qd_f, ki_f, sl, g) for g in range(N_GRP)] for sl in chunks]
    a_b = [[scores(qd_b, ki_b, sl, g) for g in range(N_GRP)] for sl in chunks]
    ds = _state_increments(kd_f, v)
    yield

    a = [[af[g].astype(BF16) * causal + ab[g].astype(BF16) * anti for g in range(N_GRP)]
         for af, ab in zip(a_f, a_b)]
    state = [st_ref[g] for g in range(N_GRP)]
    st_in = []
    for ci in range(len(chunks)):
        st_in.append([_masked_state(s, smask) for s in state])
        state = [state[g] * dcol_f[_gk(g), ci:ci + 1] + ds[ci][g] for g in range(N_GRP)]
    for g in range(N_GRP):
        st_ref[g] = state[g]
    yield

    outs = []
    for ci, sl in enumerate(chunks):
        parts = []
        for g in range(N_GRP):
            intra = _dot(a[ci][g], v_bd[ci][g])
            qd2 = jnp.concatenate([qd_f[sl, _gk(g)], qd_b[sl, _gk(g)]], axis=1)
            st2 = jnp.concatenate([st_in[ci][g], bwd_state(ci, g)], axis=0)
            parts.append(intra + _dot(qd2, st2))
        outs.append(jnp.concatenate(parts, axis=1))
    o = jnp.concatenate(outs, axis=0)
    yield

    o2 = (o * o).astype(BF16)
    ms = jnp.concatenate([_dot(o2[:, _gv(g)], hmean) for g in range(N_GRP)], axis=1)
    yield
    finish((o * lax.rsqrt(ms + EPS) * gate.astype(F32)).astype(BF16))


def _gla_kernel(skv_ref, slah_ref, slal_ref,
                kv_ref, qg_ref, lahf_ref, lahb_ref, lalf_ref, lalb_ref,
                ginc_ref, lahc_ref, lalc_ref,
                tril_ref, triu_ref, kmask_ref, vmask_ref, smask_ref, causal_ref, anti_ref,
                hmean_ref, o_ref, st_ref, sb_ref, *, n_state_steps):
    step = pl.program_id(1)
    ch_per_half = CUM_BLK // CHUNK
    fwd = slice(0, GQ)
    bwd = slice(GQ, 2 * GQ)

    def ctx_state(cols, tri_ref, reverse):
        st_ref[...] = jnp.zeros_like(st_ref)
        ctx_blocks = [slice(r0, r0 + CUM_BLK) for r0 in range(0, ginc_ref.shape[0], CUM_BLK)]
        for rs in (ctx_blocks[::-1] if reverse else ctx_blocks):
            for _ in _state_stages(st_ref, ginc_ref[rs, K_COLS], ginc_ref[rs, V_COLS],
                                   lahc_ref[rs, cols], lalc_ref[rs, cols], tri_ref[...],
                                   smask_ref[...], reverse=reverse):
                pass

    @pl.when(step < n_state_steps)
    def _backward_states():
        @pl.when(step == 0)
        def _():
            ctx_state(bwd, triu_ref, True)

        blk = n_state_steps - 1 - step
        parts = [slice(r0, r0 + CUM_BLK) for r0 in range(0, GLA_STATE_TB, CUM_BLK)]
        _interleave([
            _state_stages(st_ref, skv_ref[rs, K_COLS], skv_ref[rs, V_COLS],
                          slah_ref[rs, :], slal_ref[rs, :], triu_ref[...], smask_ref[...],
                          reverse=True, store_ref=sb_ref,
                          store_base=blk * (GLA_STATE_TB // CHUNK) + pi * ch_per_half)
            for pi, rs in reversed(list(enumerate(parts)))])

    @pl.when(step >= n_state_steps)
    def _outputs():
        j = step - n_state_steps

        @pl.when(j == 0)
        def _():
            ctx_state(fwd, tril_ref, False)

        consts = (tril_ref[...], triu_ref[...], kmask_ref[...], vmask_ref[...],
                  smask_ref[...], causal_ref[...], anti_ref[...], hmean_ref[...])
        halves = [slice(r0, r0 + CUM_BLK) for r0 in range(0, GLA_TB, CUM_BLK)]

        def half(hi, rs):
            base = j * (GLA_TB // CHUNK) + hi * ch_per_half

            def finish(val):
                o_ref[rs, :] = val

            return _output_stages(
                qg_ref[rs, 0:GQ], kv_ref[rs, K_COLS], kv_ref[rs, V_COLS],
                (lahf_ref[rs, :], lalf_ref[rs, :]), (lahb_ref[rs, :], lalb_ref[rs, :]),
                qg_ref[rs, GQ:KV_W], st_ref, lambda ci, g: sb_ref[base + ci, g], consts, finish)

        _interleave([half(hi, rs) for hi, rs in enumerate(halves)])


def _gla(gin, la_hi, la_lo, gin_ctx, la_hi_ctx, la_lo_ctx, *, batch, seq, n_ctx):
    assert n_ctx % CUM_BLK == 0 and CUM_BLK // CHUNK <= 8
    tb, stb = GLA_TB, GLA_STATE_TB
    ns, nb = seq // stb, seq // tb
    nch = seq // CHUNK
    r3 = lambda a, t: a.reshape(batch, t, a.shape[-1])
    consts = _gla_constants()

    def state_blk(col):
        return lambda b, s: (b, jnp.maximum(ns - 1 - s, 0), col)

    def out_blk(col):
        return lambda b, s: (b, jnp.maximum(s - ns, 0), col)

    ctx_map = lambda b, s: (b, 0, 0)
    const_map = lambda b, s: (0, 0)
    gin3, lah3, lal3 = r3(gin, seq), r3(la_hi, seq), r3(la_lo, seq)
    return pl.pallas_call(
        functools.partial(_gla_kernel, n_state_steps=ns),
        out_shape=jax.ShapeDtypeStruct((batch, seq, GV), BF16),
        grid=(batch, ns + nb),
        in_specs=[pl.BlockSpec((None, stb, KV_W), state_blk(0)),
                  pl.BlockSpec((None, stb, GQ), state_blk(1)),
                  pl.BlockSpec((None, stb, GQ), state_blk(1)),
                  pl.BlockSpec((None, tb, KV_W), out_blk(0)),
                  pl.BlockSpec((None, tb, KV_W), out_blk(1)),
                  pl.BlockSpec((None, tb, GQ), out_blk(0)),
                  pl.BlockSpec((None, tb, GQ), out_blk(1)),
                  pl.BlockSpec((None, tb, GQ), out_blk(0)),
                  pl.BlockSpec((None, tb, GQ), out_blk(1)),
                  pl.BlockSpec((None, n_ctx, GIN_W), ctx_map),
                  pl.BlockSpec((None, n_ctx, LA_W), ctx_map),
                  pl.BlockSpec((None, n_ctx, LA_W), ctx_map)]
                 + [pl.BlockSpec(cst.shape, const_map) for cst in consts],
        out_specs=pl.BlockSpec((None, tb, GV), out_blk(0)),
        scratch_shapes=[pltpu.VMEM((N_GRP, GRP_K, GRP_V), F32),
                        pltpu.VMEM((nch, N_GRP, GRP_K, GRP_V), BF16)],
        compiler_params=pltpu.CompilerParams(
            dimension_semantics=("arbitrary", "arbitrary"),
            vmem_limit_bytes=VMEM_LIMIT),
        name="gla",
    )(gin3, lah3, lal3, gin3, gin3, lah3, lah3, lal3, lal3, r3(gin_ctx, n_ctx),
      r3(la_hi_ctx, n_ctx), r3(la_lo_ctx, n_ctx), *consts)


FFN_CHUNK = 256
MIX_SLICES = 16
MIX_TM = 512


def _mix_ffn_kernel(att_ref, gla_ref, x_ref, ada_ref, gpm_ref, gpf_ref, gqf_ref,
                    woa_ref, wog_ref, wfi_ref, wfo_ref, o_ref, x1_ref, h_ref, y_ref, acc_ref, *,
                    tiles_per_batch, n_tiles):
    i = pl.program_id(0)
    tm = x_ref.shape[0]
    chunks = [(c0, min(c0 + FFN_CHUNK, FFN_HIDDEN)) for c0 in range(0, FFN_HIDDEN, FFN_CHUNK)]
    n_chunks = len(chunks)
    mix_rows = tm // MIX_SLICES

    def mix_matmul():
        y_ref[...] = _dot_tn(att_ref[...], woa_ref[...]) + _dot(gla_ref[...], wog_ref[...])

    def mix_rows_slice(slot, si, anchor=None):
        rs = slice(si * mix_rows, (si + 1) * mix_rows)
        row = jnp.minimum(i, n_tiles - 1) // tiles_per_batch
        ada = ada_ref[pl.ds(row, 1), :]
        gt1 = ada[:, 2 * D:3 * D]
        if anchor is not None:
            bits = pltpu.bitcast(anchor, jnp.uint32)
            sixteen = jnp.uint32(16)
            gt1 = gt1 + pltpu.bitcast(
                lax.shift_right_logical(lax.shift_right_logical(bits, sixteen), sixteen), F32)
        sh2 = ada[:, 3 * D:4 * D]
        sc2 = ada[:, 4 * D:5 * D]
        x1 = x_ref[rs, :] + gt1 * (_rms(y_ref[rs, :]) * gpm_ref[...])
        x1_ref[slot, rs, :] = x1
        h_ref[slot, rs, :] = ((_rms(x1) * gpf_ref[...]) * (1.0 + sc2) + sh2).astype(BF16)

    def ffn_chunk(slot, ci):
        c0, c1 = chunks[ci]
        h = h_ref[slot]
        g = _dot(h, wfi_ref[:, c0:c1])
        u = _dot(h, wfi_ref[:, FFN_HIDDEN + c0:FFN_HIDDEN + c1])
        a = (_silu(g) * u).astype(BF16)
        part = _dot(a, wfo_ref[c0:c1, :])
        if ci == 0:
            acc_ref[...] = part
        else:
            acc_ref[...] += part
        return part[0:8, :]

    def ffn_finish(slot):
        row = jnp.maximum(i - 1, 0) // tiles_per_batch
        gt2 = ada_ref[pl.ds(row, 1), 5 * D:6 * D]
        o_ref[...] = x1_ref[slot] + gt2 * (_rms(acc_ref[...]) * gqf_ref[...])

    @pl.when(i == 0)
    def _():
        mix_matmul()
        for si in range(MIX_SLICES):
            mix_rows_slice(0, si)

    for parity in range(2):
        @pl.when((i % 2 == parity) & (i > 0))
        def _():
            per_gap = -(-MIX_SLICES // (n_chunks - 1))
            for ci in range(n_chunks):
                anchor = ffn_chunk(1 - parity, ci)
                if ci == 0:
                    mix_matmul()
                else:
                    for si in range((ci - 1) * per_gap, min(ci * per_gap, MIX_SLICES)):
                        mix_rows_slice(parity, si, anchor[0:1, :])
            ffn_finish(1 - parity)


def _mix_ffn(att_t, gla, x2d, ada, g_post_mix, g_pre_ffn, g_post_ffn,
             w_out_a, w_out_g, w_ffn_in, w_ffn_out, *, tm, tiles_per_batch):
    n = x2d.shape[0]
    tpb = tiles_per_batch
    nt = n // tm
    const = lambda i: (0, 0)
    resident = functools.partial(pl.BlockSpec, index_map=const, pipeline_mode=pl.Buffered(1))
    cur = lambda i: jnp.minimum(i, nt - 1)
    done = lambda i: jnp.maximum(i - 1, 0)
    return pl.pallas_call(
        functools.partial(_mix_ffn_kernel, tiles_per_batch=tpb, n_tiles=nt),
        out_shape=jax.ShapeDtypeStruct((n, D), F32),
        grid=(nt + 1,),
        in_specs=[pl.BlockSpec((None, NQ, tm), lambda i: (cur(i) // tpb, 0, cur(i) % tpb)),
                  pl.BlockSpec((tm, GV), lambda i: (cur(i), 0)),
                  pl.BlockSpec((tm, D), lambda i: (cur(i), 0)),
                  pl.BlockSpec((8, 6 * D), const),
                  pl.BlockSpec((1, D), const),
                  pl.BlockSpec((1, D), const),
                  pl.BlockSpec((1, D), const),
                  resident((NQ, D)),
                  resident((GV, D)),
                  resident((D, 2 * FFN_HIDDEN)),
                  resident((FFN_HIDDEN, D))],
        out_specs=pl.BlockSpec((tm, D), lambda i: (done(i), 0)),
        scratch_shapes=[pltpu.VMEM((2, tm, D), F32),
                        pltpu.VMEM((2, tm, D), BF16),
                        pltpu.VMEM((tm, D), F32),
                        pltpu.VMEM((tm, D), F32)],
        compiler_params=pltpu.CompilerParams(dimension_semantics=("arbitrary",),
                                             vmem_limit_bytes=VMEM_LIMIT),
        name="mix_ffn",
    )(att_t, gla, x2d, ada, g_post_mix, g_pre_ffn, g_post_ffn,
      w_out_a, w_out_g, w_ffn_in, w_ffn_out)


def _rope_tables(seq):
    half = HEAD_DIM // 2
    inv_freq = ROPE_BASE ** (-np.arange(0, half, 2, dtype=np.float64) / half)
    dim = np.arange(HEAD_DIM)
    freq = np.tile(inv_freq, HEAD_DIM // ROPE_BLK)[:, None]
    pos = np.arange(seq)[None, :]
    ang = np.where((dim < half)[:, None], pos // GRID_W, pos % GRID_W) * freq
    cos, sin = np.cos(ang), np.sin(ang)
    even = ((dim // ROPE_BLK) % 2 == 0)[:, None]
    sa = np.where(even, -sin, 0.0)
    sb = np.where(even, 0.0, sin)
    return tuple(jnp.asarray(t, F32) for t in (cos, sa, sb))


def kernel(x, c, ctx, c_ctx, w_ada, b_ada, g_pre_mix, g_post_mix, g_pre_ffn, g_post_ffn,
           w_in, attn_sink, w_gate_fwd, b_gate_fwd, w_gate_bwd, b_gate_bwd, g_gla_norm,
           w_out, w_ffn_in, w_ffn_out):
    batch, seq, _ = x.shape
    n_ctx = ctx.shape[1]
    depth = w_ada.shape[0]
    assert depth == 1
    l = 0

    c8 = jnp.zeros((8, D), F32).at[0:batch].set(c).at[4].set(c_ctx)
    ada = _ada(c8, w_ada[l], b_ada[l][None, :])

    wi = w_in[l]
    g0 = NQ + 2 * NKV
    w_qkv_t = wi[:, 0:g0].T.astype(BF16)
    gk0 = g0 + GQ
    gg0 = gk0 + KV_W
    w_rest = jnp.concatenate(
        [wi[:, gk0:gg0], wi[:, g0:gk0], wi[:, gg0:],
         jnp.zeros((D, Z_W - 2 * GATE_RANK), F32)], axis=1).astype(BF16)
    w_gate_p = jnp.zeros((Z_W, LA_W), F32)
    w_gate_p = w_gate_p.at[0:GATE_RANK, 0:GQ].set(w_gate_fwd[l])
    w_gate_p = w_gate_p.at[GATE_RANK:2 * GATE_RANK, GQ:].set(w_gate_bwd[l]).astype(BF16)
    b_gate_p = jnp.concatenate([b_gate_fwd[l], b_gate_bwd[l]])[None, :]

    tables = _rope_tables(seq)
    x2d = x.reshape(batch * seq, D)
    g_pre = g_pre_mix[l][None, :]
    gn_tiled = jnp.tile(g_gla_norm[l], GLA_HEADS)[None, :]
    qt, k, vt, gin, lah, lal = _inproj(x2d, ada, g_pre, w_qkv_t, w_rest, w_gate_p, b_gate_p,
                                      gn_tiled, tables, batch=batch, tokens=seq, tm=INPROJ_TM)
    _, k_c, vt_c, gin_c, lah_c, lal_c = _inproj(
        ctx.reshape(batch * n_ctx, D), ada, g_pre, w_qkv_t, w_rest, w_gate_p, b_gate_p,
        gn_tiled, None, batch=batch, tokens=n_ctx, tm=n_ctx)

    att_t = _attn(attn_sink[l], qt, k, vt, k_c, vt_c, batch=batch, seq=seq, n_ctx=n_ctx)

    gla = _gla(gin, lah, lal, gin_c, lah_c, lal_c, batch=batch, seq=seq, n_ctx=n_ctx)

    wo = w_out[l]
    out = _mix_ffn(att_t, gla.reshape(batch * seq, GV), x2d, ada,
                   g_post_mix[l][None, :], g_pre_ffn[l][None, :], g_post_ffn[l][None, :],
                   wo[:NQ].astype(BF16), wo[NQ:].astype(BF16),
                   w_ffn_in[l].astype(BF16), w_ffn_out[l].astype(BF16),
                   tm=MIX_TM, tiles_per_batch=seq // MIX_TM)
    return out.reshape(batch, seq, D)
```
